```python
import math
import jax, jax.numpy as jnp
from jax import lax
import numpy as np

D_MODEL = 4096
BATCH = 4
SEQ = 4096
DEPTH = 2

HEAD_DIM = 128
N_HEADS_A = D_MODEL // (2 * HEAD_DIM)
N_HEADS_B = D_MODEL // (2 * HEAD_DIM)
KV_LORA = 512
N_IDX_HEADS = 32
IDX_DIM = 64
TOPK_MAX = 256
Q_CHUNK_A = 128
MOBA_BLOCK = 256
MOBA_TOPB = 3
Q_CHUNK_B = 32
MIX_WIDTH = (N_HEADS_A + N_HEADS_B) * HEAD_DIM
D_RNN = D_MODEL * 5 // 4
RG_BLOCKS = 16
RG_BW = D_RNN // RG_BLOCKS
CONV_W = 4
RG_C = 8.0
NUM_BUCKETS = 32
MAX_DISTANCE = 128
EPS = 1e-6
ATTN_SCALE = HEAD_DIM ** -0.5
IDX_SCALE = (N_IDX_HEADS ** -0.5) * (IDX_DIM ** -0.5)
N_ATTN_LAYERS = (DEPTH + 1) // 2
N_REC_LAYERS = DEPTH // 2
PROJ_A_SIZES = (N_HEADS_A * HEAD_DIM, KV_LORA, N_IDX_HEADS * IDX_DIM, IDX_DIM, N_IDX_HEADS,
                N_HEADS_B * HEAD_DIM, N_HEADS_B * HEAD_DIM, N_HEADS_B * HEAD_DIM, MIX_WIDTH)
PROJ_A_WIDTH = sum(PROJ_A_SIZES)

kernel_name = "hybrid_dsa_moba_rglru_trunk"


def rmsnorm(x, g):
    x32 = x.astype(jnp.float32)
    y = x32 * lax.rsqrt(jnp.mean(x32 * x32, axis=-1, keepdims=True) + EPS)
    return (y * g.astype(jnp.float32)).astype(x.dtype)


def layernorm(x, g, b):
    x32 = x.astype(jnp.float32)
    mu = jnp.mean(x32, axis=-1, keepdims=True)
    var = jnp.mean(jnp.square(x32 - mu), axis=-1, keepdims=True)
    y = (x32 - mu) * lax.rsqrt(var + EPS)
    return (y * g.astype(jnp.float32) + b.astype(jnp.float32)).astype(x.dtype)


def t5_bucket(dist):
    d = jnp.maximum(dist, 0)
    max_exact = NUM_BUCKETS // 2
    d_f = jnp.maximum(d, 1).astype(jnp.float32)
    large = max_exact + (jnp.log(d_f / max_exact) / math.log(MAX_DISTANCE / max_exact)
                         * (NUM_BUCKETS - max_exact)).astype(jnp.int32)
    large = jnp.minimum(large, NUM_BUCKETS - 1)
    return jnp.where(d < max_exact, d, large)


def dsa_attention(q_a, ckv, q_idx, k_idx, w_idx, bias_tab, w_uk, w_uv):
    B, S, HA, Dh = q_a.shape
    topk = min(TOPK_MAX, S // 4)
    s_pos = jnp.arange(S)
    b_idx = jnp.arange(B)[:, None, None]

    def chunk(c):
        t0 = c * Q_CHUNK_A
        t_pos = t0 + jnp.arange(Q_CHUNK_A)
        qa = lax.dynamic_slice_in_dim(q_a, t0, Q_CHUNK_A, axis=1)
        qi = lax.dynamic_slice_in_dim(q_idx, t0, Q_CHUNK_A, axis=1)
        wi = lax.dynamic_slice_in_dim(w_idx, t0, Q_CHUNK_A, axis=1)
        rel = jax.nn.relu(jnp.einsum('bthd,bsd->bths', qi, k_idx))
        score = (jnp.einsum('bths,bth->bts', rel, wi) * IDX_SCALE).astype(jnp.float32)
        causal = s_pos[None, :] <= t_pos[:, None]
        score = jnp.where(causal[None], score, -jnp.inf)
        _, idx = lax.top_k(score, topk)
        valid = idx <= t_pos[None, :, None]
        c_sel = ckv[b_idx, idx]
        q_lat = jnp.einsum('bthd,hcd->bthc', qa, w_uk)
        logits = jnp.einsum('bthc,btkc->bthk', q_lat, c_sel) * ATTN_SCALE
        bias = bias_tab[t5_bucket(t_pos[None, :, None] - idx)]
        logits = logits.astype(jnp.float32) + bias.astype(jnp.float32).transpose(0, 1, 3, 2)
        logits = jnp.where(valid[:, :, None, :], logits, -jnp.inf)
        p = jax.nn.softmax(logits, axis=-1).astype(c_sel.dtype)
        o_lat = jnp.einsum('bthk,btkc->bthc', p, c_sel)
        return jnp.einsum('bthc,hcd->bthd', o_lat, w_uv)

    out = lax.map(chunk, jnp.arange(S // Q_CHUNK_A))
    return out.transpose(1, 0, 2, 3, 4).reshape(B, S, HA, Dh)


def moba_attention(q, k, v, bias_tab):
    B, S, H, Dh = q.shape
    nb = -(-S // MOBA_BLOCK)
    pad = nb * MOBA_BLOCK - S
    kp = jnp.pad(k, ((0, 0), (0, pad), (0, 0), (0, 0)))
    vp = jnp.pad(v, ((0, 0), (0, pad), (0, 0), (0, 0)))
    kbh = kp.reshape(B, nb, MOBA_BLOCK, H, Dh).transpose(0, 3, 1, 2, 4)
    vbh = vp.reshape(B, nb, MOBA_BLOCK, H, Dh).transpose(0, 3, 1, 2, 4)
    kmean = jnp.mean(kbh, axis=3)
    topb = min(MOBA_TOPB, nb - 1)
    bi = jnp.arange(B)[:, None, None, None]
    hi = jnp.arange(H)[None, :, None, None]
    head_ids = jnp.arange(H)[None, :, None, None, None]
    blk_ids = jnp.arange(nb)

    def chunk(c):
        t0 = c * Q_CHUNK_B
        t_pos = t0 + jnp.arange(Q_CHUNK_B)
        own = t0 // MOBA_BLOCK
        qh = lax.dynamic_slice_in_dim(q, t0, Q_CHUNK_B, axis=1).transpose(0, 2, 1, 3)
        ko = lax.dynamic_slice_in_dim(kbh, own, 1, axis=2)[:, :, 0]
        vo = lax.dynamic_slice_in_dim(vbh, own, 1, axis=2)[:, :, 0]
        s_own = own * MOBA_BLOCK + jnp.arange(MOBA_BLOCK)
        b_own = bias_tab[t5_bucket(t_pos[:, None] - s_own[None, :])].transpose(2, 0, 1)
        l_own = (jnp.einsum('bhtd,bhsd->bhts', qh, ko) * ATTN_SCALE).astype(jnp.float32) \
            + b_own.astype(jnp.float32)
        l_own = jnp.where(s_own[None, :] <= t_pos[:, None], l_own, -jnp.inf)
        if topb == 0:
            p = jax.nn.softmax(l_own, axis=-1).astype(v.dtype)
            o = jnp.einsum('bhts,bhsd->bhtd', p, vo)
        else:
            gate = jnp.einsum('bhtd,bhnd->bhtn', qh, kmean).astype(jnp.float32)
            gate = jnp.where(blk_ids < own, gate, -jnp.inf)
            _, sel = lax.top_k(gate, topb)
            ksel = kbh[bi, hi, sel]
            vsel = vbh[bi, hi, sel]
            s_sel = sel[..., None] * MOBA_BLOCK + jnp.arange(MOBA_BLOCK)
            b_sel = bias_tab[t5_bucket(t_pos[None, None, :, None, None] - s_sel), head_ids]
            l_sel = (jnp.einsum('bhtd,bhtnsd->bhtns', qh, ksel) * ATTN_SCALE).astype(jnp.float32) \
                + b_sel.astype(jnp.float32)
            l_sel = jnp.where((sel < own)[..., None], l_sel, -jnp.inf)
            l_sel = l_sel.reshape(B, H, Q_CHUNK_B, topb * MOBA_BLOCK)
            p = jax.nn.softmax(jnp.concatenate([l_sel, l_own], axis=-1), axis=-1).astype(v.dtype)
            p_sel = p[..., :topb * MOBA_BLOCK]
            p_own = p[..., topb * MOBA_BLOCK:]
            o = jnp.einsum('bhtm,bhtmd->bhtd', p_sel,
                           vsel.reshape(B, H, Q_CHUNK_B, topb * MOBA_BLOCK, Dh)) \
                + jnp.einsum('bhts,bhsd->bhtd', p_own, vo)
        return o.transpose(0, 2, 1, 3)

    out = lax.map(chunk, jnp.arange(S // Q_CHUNK_B))
    return out.transpose(1, 0, 2, 3, 4).reshape(B, S, H, Dh)


def attention_layer(h, w_in, kv_g, w_uk, w_uv, idx_k_g, idx_k_b, w_out, rel_bias):
    B, S, _ = h.shape
    splits = np.cumsum(PROJ_A_SIZES)[:-1].tolist()
    proj = h @ w_in
    qa, ckv, qi, ki, wi, qb, kb, vb, z = jnp.split(proj, splits, axis=-1)
    qa = qa.reshape(B, S, N_HEADS_A, HEAD_DIM)
    ckv = rmsnorm(ckv, kv_g)
    qi = qi.reshape(B, S, N_IDX_HEADS, IDX_DIM)
    ki = layernorm(ki, idx_k_g, idx_k_b)
    oa = dsa_attention(qa, ckv, qi, ki, wi, rel_bias[:, :N_HEADS_A], w_uk, w_uv)
    ob = moba_attention(qb.reshape(B, S, N_HEADS_B, HEAD_DIM),
                        kb.reshape(B, S, N_HEADS_B, HEAD_DIM),
                        vb.reshape(B, S, N_HEADS_B, HEAD_DIM),
                        rel_bias[:, N_HEADS_A:])
    y = jnp.concatenate([oa.reshape(B, S, -1), ob.reshape(B, S, -1)], axis=-1) * jax.nn.silu(z)
    return y @ w_out


def _lin_combine(left, right):
    a1, b1 = left
    a2, b2 = right
    return a1 * a2, a2 * b1 + b2


def recurrent_layer(h, w_in, conv_w, conv_b, w_a, b_a, w_x, b_x, lam, w_out):
    B, S, _ = h.shape
    proj = h @ w_in
    xr, z = jnp.split(proj, [D_RNN], axis=-1)
    xp = jnp.pad(xr, ((0, 0), (CONV_W - 1, 0), (0, 0)))
    xc = conv_b + xp[:, 0:S] * conv_w[0]
    for j in range(1, CONV_W):
        xc = xc + xp[:, j:j + S] * conv_w[j]
    xb = xc.reshape(B, S, RG_BLOCKS, RG_BW)
    r = jax.nn.sigmoid((jnp.einsum('bsnk,nkj->bsnj', xb, w_a).reshape(B, S, D_RNN) + b_a)
                       .astype(jnp.float32))
    i = jax.nn.sigmoid((jnp.einsum('bsnk,nkj->bsnj', xb, w_x).reshape(B, S, D_RNN) + b_x)
                       .astype(jnp.float32))
    log_a = -RG_C * r * jax.nn.softplus(-lam.astype(jnp.float32))
    a = jnp.exp(log_a)
    mult = jnp.sqrt(-jnp.expm1(2.0 * log_a))
    bt = mult * (i * xc.astype(jnp.float32))
    _, hs = lax.associative_scan(_lin_combine, (a, bt), axis=1)
    y = hs.astype(h.dtype) * jax.nn.silu(z)
    return y @ w_out


def setup_inputs(seed: int = 0) -> dict:
    key = jax.random.key(seed)
    ks = jax.random.split(key, 20)
    f32 = jnp.float32
    NA, NR = N_ATTN_LAYERS, N_REC_LAYERS

    def nrm(k, shape, scale):
        return jax.random.normal(k, shape, f32) * scale

    u = jax.random.uniform(ks[18], (NR, D_RNN), f32, 0.9, 0.999)
    a0 = u ** (1.0 / RG_C)
    rec_lambda = jnp.log(a0) - jnp.log1p(-a0)
    return {
        "x": nrm(ks[0], (BATCH, SEQ, D_MODEL), 1.0),
        "norm_g": 1.0 + nrm(ks[1], (DEPTH, D_MODEL), 0.05),
        "final_g": 1.0 + nrm(ks[2], (D_MODEL,), 0.05),
        "rel_bias": nrm(ks[3], (NUM_BUCKETS, N_HEADS_A + N_HEADS_B), 0.5),
        "attn_w_in": nrm(ks[4], (NA, D_MODEL, PROJ_A_WIDTH), D_MODEL ** -0.5),
        "attn_kv_g": 1.0 + nrm(ks[5], (NA, KV_LORA), 0.05),
        "attn_w_uk": nrm(ks[6], (NA, N_HEADS_A, KV_LORA, HEAD_DIM), KV_LORA ** -0.5),
        "attn_w_uv": nrm(ks[7], (NA, N_HEADS_A, KV_LORA, HEAD_DIM), KV_LORA ** -0.5),
        "idx_k_g": 1.0 + nrm(ks[8], (NA, IDX_DIM), 0.05),
        "idx_k_b": nrm(ks[9], (NA, IDX_DIM), 0.02),
        "attn_w_out": nrm(ks[10], (NA, MIX_WIDTH, D_MODEL), MIX_WIDTH ** -0.5),
        "rec_w_in": nrm(ks[11], (NR, D_MODEL, 2 * D_RNN), D_MODEL ** -0.5),
        "rec_conv_w": nrm(ks[12], (NR, CONV_W, D_RNN), CONV_W ** -0.5),
        "rec_conv_b": nrm(ks[13], (NR, D_RNN), 0.02),
        "rec_w_a": nrm(ks[14], (NR, RG_BLOCKS, RG_BW, RG_BW), RG_BW ** -0.5),
        "rec_b_a": nrm(ks[15], (NR, D_RNN), 0.02),
        "rec_w_x": nrm(ks[16], (NR, RG_BLOCKS, RG_BW, RG_BW), RG_BW ** -0.5),
        "rec_b_x": nrm(ks[17], (NR, D_RNN), 0.02),
        "rec_lambda": rec_lambda,
        "rec_w_out": nrm(ks[19], (NR, D_RNN, D_MODEL), D_RNN ** -0.5),
    }


def reference(x, norm_g, final_g, rel_bias, attn_w_in, attn_kv_g, attn_w_uk, attn_w_uv,
              idx_k_g, idx_k_b, attn_w_out, rec_w_in, rec_conv_w, rec_conv_b, rec_w_a,
              rec_b_a, rec_w_x, rec_b_x, rec_lambda, rec_w_out):
    for layer in range(DEPTH):
        h = rmsnorm(x, norm_g[layer])
        li = layer // 2
        if layer % 2 == 0:
            x = x + attention_layer(h, attn_w_in[li], attn_kv_g[li], attn_w_uk[li], attn_w_uv[li],
                                    idx_k_g[li], idx_k_b[li], attn_w_out[li], rel_bias)
        else:
            x = x + recurrent_layer(h, rec_w_in[li], rec_conv_w[li], rec_conv_b[li], rec_w_a[li],
                                    rec_b_a[li], rec_w_x[li], rec_b_x[li], rec_lambda[li],
                                    rec_w_out[li])
    return rmsnorm(x, final_g)
```

```python
import functools
import math

import numpy as np
import jax
import jax.numpy as jnp
from jax import lax
from jax.experimental import pallas as pl
from jax.experimental.pallas import tpu as pltpu

F32 = jnp.float32
BF16 = jnp.bfloat16
I32 = jnp.int32

HEAD_DIM = 128
KV_LORA = 512
N_IDX_HEADS = 32
IDX_DIM = 64
TOPK_MAX = 256
MOBA_BLOCK = 256
MOBA_TOPB = 3
RG_BLOCKS = 16
CONV_W = 4
RG_C = 8.0
NUM_BUCKETS = 32
MAX_DISTANCE = 128
EPS = 1e-6
ATTN_SCALE = HEAD_DIM ** -0.5
IDX_SCALE = (N_IDX_HEADS ** -0.5) * (IDX_DIM ** -0.5)

LANES = 128
SUBLANES = 8
VMEM_LIMIT_BYTES = 56 * 2 ** 20
MASK_NEG = -1e30
INT_MIN = np.int32(-2 ** 31)

ROW_TILE = 256
MM_BM = 1024
MM_BN = 1024
DSA_TQ = 128
DSA_TK = 256
DSA_HG = 8
MOBA_HB = 4
RG_PAIR = 2
RG_TC = 512


def _cparams(sem):
    return pltpu.CompilerParams(dimension_semantics=sem, vmem_limit_bytes=VMEM_LIMIT_BYTES)


def _tile_lanes(x, n):
    return x if n == 1 else jnp.concatenate([x] * n, axis=1)


def _rmsnorm_kernel(x_ref, g_ref, o_ref):
    x = x_ref[...]
    ms = jnp.mean(x * x, axis=-1, keepdims=True)
    o_ref[...] = ((x * lax.rsqrt(ms + EPS)) * g_ref[...]).astype(o_ref.dtype)


def _rmsnorm(x2d, g, out_dtype, name):
    m, d = x2d.shape
    bm = min(ROW_TILE, m)
    return pl.pallas_call(
        _rmsnorm_kernel,
        grid=(m // bm,),
        in_specs=[pl.BlockSpec((bm, d), lambda i: (i, 0)),
                  pl.BlockSpec((1, d), lambda i: (0, 0))],
        out_specs=pl.BlockSpec((bm, d), lambda i: (i, 0)),
        out_shape=jax.ShapeDtypeStruct((m, d), out_dtype),
        compiler_params=_cparams(("parallel",)),
        name=name,
    )(x2d, g.reshape(1, d).astype(F32))


def _mm_kernel(*refs, n_parts, has_res):
    o_ref = refs[-1]
    acc = jnp.dot(refs[0][...], refs[n_parts][...], preferred_element_type=F32)
    for p in range(1, n_parts):
        acc = acc + jnp.dot(refs[p][...], refs[n_parts + p][...], preferred_element_type=F32)
    if has_res:
        acc = refs[2 * n_parts][...] + acc
    o_ref[...] = acc.astype(o_ref.dtype)


def _matmul(xs, ws, out_dtype, name, res=None, bm=MM_BM, bn=MM_BN):
    m = xs[0].shape[0]
    n = ws[0].shape[1]
    bm = min(bm, m)
    bn = min(bn, n)
    assert m % bm == 0 and n % bn == 0
    in_specs = [pl.BlockSpec((bm, x.shape[1]), lambda i, j: (i, 0)) for x in xs]
    in_specs += [pl.BlockSpec((w.shape[0], bn), lambda i, j: (0, j)) for w in ws]
    args = [*xs, *ws]
    if res is not None:
        in_specs.append(pl.BlockSpec((bm, bn), lambda i, j: (i, j)))
        args.append(res)
    return pl.pallas_call(
        functools.partial(_mm_kernel, n_parts=len(xs), has_res=res is not None),
        grid=(m // bm, n // bn),
        in_specs=in_specs,
        out_specs=pl.BlockSpec((bm, bn), lambda i, j: (i, j)),
        out_shape=jax.ShapeDtypeStruct((m, n), out_dtype),
        compiler_params=_cparams(("parallel", "parallel")),
        name=name,
    )(*args)


def _t5_bucket(dist):
    d = jnp.maximum(dist, 0)
    max_exact = NUM_BUCKETS // 2
    d_f = jnp.maximum(d, 1).astype(F32)
    large = max_exact + (jnp.log(d_f / max_exact) / math.log(MAX_DISTANCE / max_exact)
                         * (NUM_BUCKETS - max_exact)).astype(I32)
    large = jnp.minimum(large, NUM_BUCKETS - 1)
    return jnp.where(d < max_exact, d, large)


def _bias_blocks(tab):
    i = jnp.arange(LANES)
    d0 = i[:, None] - i[None, :]
    out = []
    for off in (0, LANES):
        b = tab[_t5_bucket(d0 + off)]
        out.append(jnp.transpose(b - tab[NUM_BUCKETS - 1], (2, 0, 1)))
    return jnp.stack(out).astype(F32)


def _dsa_prep_kernel(p_ref, kvg_ref, ig_ref, ib_ref, ckv_ref, ckvt_ref, kbd_ref, w_ref):
    p = p_ref[0]
    ts = p.shape[0]
    c = p[:, :KV_LORA]
    cn = (c * lax.rsqrt(jnp.mean(c * c, axis=-1, keepdims=True) + EPS)) * kvg_ref[...]
    ckv_ref[0, 0] = cn.astype(BF16)
    ckvt_ref[0, 0] = cn.T.astype(BF16)
    k = p[:, KV_LORA:KV_LORA + IDX_DIM]
    mu = jnp.mean(k, axis=-1, keepdims=True)
    var = jnp.mean(jnp.square(k - mu), axis=-1, keepdims=True)
    kn = ((k - mu) * lax.rsqrt(var + EPS)) * ig_ref[...] + ib_ref[...]
    zeros = jnp.zeros((ts, LANES - IDX_DIM), F32)
    top = jnp.concatenate([kn, zeros], axis=1).T
    bot = jnp.concatenate([zeros, kn], axis=1).T
    kbd_ref[0, 0] = jnp.concatenate([top, bot], axis=1).astype(BF16)
    w_ref[0] = p[:, KV_LORA + LANES:KV_LORA + LANES + N_IDX_HEADS] * IDX_SCALE


def _dsa_prep(ps, kv_g, idx_g, idx_b):
    b, s, wid = ps.shape
    ts = DSA_TK
    nkt = s // ts
    return pl.pallas_call(
        _dsa_prep_kernel,
        grid=(b, nkt),
        in_specs=[pl.BlockSpec((1, ts, wid), lambda bi, i: (bi, i, 0)),
                  pl.BlockSpec((1, KV_LORA), lambda bi, i: (0, 0)),
                  pl.BlockSpec((1, IDX_DIM), lambda bi, i: (0, 0)),
                  pl.BlockSpec((1, IDX_DIM), lambda bi, i: (0, 0))],
        out_specs=[pl.BlockSpec((1, 1, ts, KV_LORA), lambda bi, i: (bi, i, 0, 0)),
                   pl.BlockSpec((1, 1, KV_LORA, ts), lambda bi, i: (bi, i, 0, 0)),
                   pl.BlockSpec((1, 1, LANES, 2 * ts), lambda bi, i: (bi, i, 0, 0)),
                   pl.BlockSpec((1, ts, N_IDX_HEADS), lambda bi, i: (bi, i, 0))],
        out_shape=[jax.ShapeDtypeStruct((b, nkt, ts, KV_LORA), BF16),
                   jax.ShapeDtypeStruct((b, nkt, KV_LORA, ts), BF16),
                   jax.ShapeDtypeStruct((b, nkt, LANES, 2 * ts), BF16),
                   jax.ShapeDtypeStruct((b, s, N_IDX_HEADS), F32)],
        compiler_params=_cparams(("parallel", "parallel")),
        name="dsa_prep",
    )(ps, kv_g.reshape(1, -1).astype(F32), idx_g.reshape(1, -1).astype(F32),
      idx_b.reshape(1, -1).astype(F32))


def _dsa_kernel(qa_ref, qi_ref, z_ref, w_ref, kbd_ref, ckv_ref, ckvt_ref, wukt_ref, wuv_ref, d_ref,
                y_ref,
                key_ref, mb_ref, wb_ref, t_ref, j_ref, qlat_ref, acc_ref, m_ref, l_ref,
                *, topk, n_heads, seq_bits):
    tq, tk, hg = DSA_TQ, DSA_TK, DSA_HG
    rep = tk // LANES
    i = pl.program_id(1)
    t0 = i * tq
    kt_d = t0 // tk
    nkt = kt_d + 1
    odd = (t0 % tk) // tq

    row = t0 + lax.broadcasted_iota(I32, (tq, tk), 0)
    col = lax.broadcasted_iota(I32, (tq, tk), 1)

    for h in range(N_IDX_HEADS):
        wb_ref[h] = jnp.broadcast_to(w_ref[0, :, h:h + 1], (tq, LANES))

    def idx_body(kt, carry):
        kb = kbd_ref[0, kt]
        acc = jnp.zeros((tq, tk), F32)
        for hp in range(N_IDX_HEADS // 2):
            rel = jnp.dot(qi_ref[0, :, hp * LANES:(hp + 1) * LANES], kb,
                          preferred_element_type=F32)
            acc = acc + jnp.maximum(rel[:, :tk], 0.0) * _tile_lanes(wb_ref[2 * hp], rep)
            acc = acc + jnp.maximum(rel[:, tk:], 0.0) * _tile_lanes(wb_ref[2 * hp + 1], rep)
        score = jnp.where(acc == 0.0, 0.0, acc)
        bits = lax.bitcast_convert_type(score, I32)
        key = bits ^ ((bits >> 31) & np.int32(0x7FFFFFFF))
        key_ref[kt] = jnp.where(kt * tk + col <= row, key, INT_MIN)
        return carry

    lax.fori_loop(0, nkt, idx_body, 0)

    def count(pred_fn):
        def body(kt, c):
            x = pred_fn(key_ref[kt], kt)
            s = x[:, :LANES]
            for r in range(1, rep):
                s = s + x[:, r * LANES:(r + 1) * LANES]
            return c + s
        c = lax.fori_loop(0, nkt, body, jnp.zeros((tq, LANES), F32))
        return jnp.sum(c, axis=1, keepdims=True)

    t_ref[...] = jnp.full((tq, LANES), INT_MIN, I32)

    def bit_body(it, carry):
        cand = t_ref[...] ^ jnp.left_shift(jnp.int32(1), 31 - it)
        cand_t = _tile_lanes(cand, rep)
        n_ge = count(lambda k, kt: jnp.where(k >= cand_t, 1.0, 0.0))
        t_ref[...] = jnp.where(n_ge >= topk, cand, t_ref[...])
        return carry

    lax.fori_loop(0, 32, bit_body, 0)

    thr = t_ref[...]
    thr_t = _tile_lanes(thr, rep)
    n_gt = count(lambda k, kt: jnp.where(k > thr_t, 1.0, 0.0))
    n_ge = count(lambda k, kt: jnp.where(k >= thr_t, 1.0, 0.0))
    need = topk - n_gt
    j_ref[...] = jnp.full((tq, LANES), 2 ** 30, I32)

    @pl.when(jnp.max(n_ge) > topk)
    def _():
        j_ref[...] = jnp.zeros((tq, LANES), I32)

        def jbit_body(it, carry):
            cand = j_ref[...] | jnp.left_shift(jnp.int32(1), seq_bits - 1 - it)
            cand_t = _tile_lanes(cand, rep)
            n_lt = count(lambda k, kt: jnp.where(k == thr_t,
                                                 jnp.where(kt * tk + col < cand_t, 1.0, 0.0), 0.0))
            j_ref[...] = jnp.where(n_lt < need, cand, j_ref[...])
            return carry

        lax.fori_loop(0, seq_bits, jbit_body, 0)

    jmax_t = _tile_lanes(j_ref[...], rep)

    def mask_body(kt, carry):
        k = key_ref[kt]
        s_idx = kt * tk + col
        keep_tie = jnp.where(k == thr_t, jnp.where(s_idx <= jmax_t, 0.0, MASK_NEG), MASK_NEG)
        keep = jnp.where(k > thr_t, 0.0, keep_tie)
        mb_ref[kt] = jnp.where(s_idx <= row, keep, MASK_NEG)
        return carry

    lax.fori_loop(0, nkt, mask_body, 0)

    for h in range(n_heads):
        ql = jnp.dot(qa_ref[0, :, h * HEAD_DIM:(h + 1) * HEAD_DIM], wukt_ref[h],
                     preferred_element_type=F32)
        qlat_ref[h * tq:(h + 1) * tq, :] = (ql * ATTN_SCALE).astype(BF16)
    m_ref[...] = jnp.full(m_ref.shape, MASK_NEG, F32)
    l_ref[...] = jnp.zeros(l_ref.shape, F32)
    acc_ref[...] = jnp.zeros(acc_ref.shape, F32)

    def attend(kt, bias_fn):
        ckt = ckvt_ref[0, kt]
        ck = ckv_ref[0, kt]
        mb = mb_ref[kt]
        for g in range(n_heads // hg):
            r0 = g * hg * tq
            rows = hg * tq
            lg = jnp.dot(qlat_ref[r0:r0 + rows, :], ckt, preferred_element_type=F32)
            ps, alphas = [], []
            for hh in range(hg):
                h = g * hg + hh
                sl = slice(r0 + hh * tq, r0 + (hh + 1) * tq)
                x = lg[hh * tq:(hh + 1) * tq] + mb
                if bias_fn is not None:
                    x = x + bias_fn(h)
                m_prev = m_ref[sl, :]
                m_new = jnp.maximum(m_prev, jnp.max(x, axis=1, keepdims=True))
                alpha = jnp.exp(m_prev - m_new)
                p = jnp.exp(x - _tile_lanes(m_new, rep))
                l_ref[sl, :] = alpha * l_ref[sl, :] + jnp.sum(p, axis=1, keepdims=True)
                m_ref[sl, :] = m_new
                ps.append(p.astype(BF16))
                alphas.append(alpha)
            pv = jnp.dot(jnp.concatenate(ps, axis=0), ck, preferred_element_type=F32)
            alpha_all = _tile_lanes(jnp.concatenate(alphas, axis=0), KV_LORA // LANES)
            acc_ref[r0:r0 + rows, :] = acc_ref[r0:r0 + rows, :] * alpha_all + pv

    def far_body(kt, carry):
        attend(kt, None)
        return carry

    lax.fori_loop(0, jnp.maximum(kt_d - 1, 0), far_body, 0)

    zero_blk = jnp.zeros((tq, LANES), F32)

    @pl.when(kt_d >= 1)
    def _():
        attend(kt_d - 1, lambda h: jnp.concatenate(
            [zero_blk, jnp.where(odd == 1, zero_blk, d_ref[1, h])], axis=1))

    attend(kt_d, lambda h: jnp.concatenate(
        [jnp.where(odd == 1, d_ref[1, h], d_ref[0, h]), d_ref[0, h]], axis=1))

    for h in range(n_heads):
        sl = slice(h * tq, (h + 1) * tq)
        o_lat = acc_ref[sl, :] / _tile_lanes(l_ref[sl, :], KV_LORA // LANES)
        o = jnp.dot(o_lat.astype(BF16), wuv_ref[h], preferred_element_type=F32)
        zz = z_ref[0, :, h * HEAD_DIM:(h + 1) * HEAD_DIM].astype(F32)
        y_ref[0, :, h * HEAD_DIM:(h + 1) * HEAD_DIM] = (o * (zz * jax.nn.sigmoid(zz))).astype(BF16)


def _dsa_attention(pbig, col_qa, col_qi, col_z, ckv, ckvt, kbd, widx, wukt, wuv, dtab, n_heads):
    b, s, _ = pbig.shape
    tq, tk = DSA_TQ, DSA_TK
    nkt = s // tk
    wq = n_heads * HEAD_DIM
    topk = min(TOPK_MAX, s // 4)
    kern = functools.partial(_dsa_kernel, topk=topk, n_heads=n_heads,
                             seq_bits=max(1, (s - 1).bit_length()))
    const4 = lambda bi, i: (bi, 0, 0, 0)
    return pl.pallas_call(
        kern,
        grid=(b, s // tq),
        in_specs=[pl.BlockSpec((1, tq, wq), lambda bi, i: (bi, i, col_qa)),
                  pl.BlockSpec((1, tq, wq), lambda bi, i: (bi, i, col_qi)),
                  pl.BlockSpec((1, tq, wq), lambda bi, i: (bi, i, col_z)),
                  pl.BlockSpec((1, tq, N_IDX_HEADS), lambda bi, i: (bi, i, 0)),
                  pl.BlockSpec((1, nkt, LANES, 2 * tk), const4),
                  pl.BlockSpec((1, nkt, tk, KV_LORA), const4),
                  pl.BlockSpec((1, nkt, KV_LORA, tk), const4),
                  pl.BlockSpec((n_heads, HEAD_DIM, KV_LORA), lambda bi, i: (0, 0, 0)),
                  pl.BlockSpec((n_heads, KV_LORA, HEAD_DIM), lambda bi, i: (0, 0, 0)),
                  pl.BlockSpec((2, n_heads, LANES, LANES), lambda bi, i: (0, 0, 0, 0))],
        out_specs=pl.BlockSpec((1, tq, wq), lambda bi, i: (bi, i, 0)),
        out_shape=jax.ShapeDtypeStruct((b, s, wq), BF16),
        scratch_shapes=[pltpu.VMEM((nkt, tq, tk), I32),
                        pltpu.VMEM((nkt, tq, tk), F32),
                        pltpu.VMEM((N_IDX_HEADS, tq, LANES), F32),
                        pltpu.VMEM((tq, LANES), I32),
                        pltpu.VMEM((tq, LANES), I32),
                        pltpu.VMEM((n_heads * tq, KV_LORA), BF16),
                        pltpu.VMEM((n_heads * tq, KV_LORA), F32),
                        pltpu.VMEM((n_heads * tq, LANES), F32),
                        pltpu.VMEM((n_heads * tq, LANES), F32)],
        compiler_params=_cparams(("parallel", "arbitrary")),
        name="dsa_attention",
    )(pbig, pbig, pbig, widx, kbd, ckv, ckvt, wukt, wuv, dtab)


def _moba_kernel(q_ref, k_ref, v_ref, z_ref, d_ref, y_ref,
                 km_ref, sb_ref, qs_ref, acc_ref, m_ref, l_ref, *, nb, topb):
    tq = MOBA_BLOCK
    hb = MOBA_HB
    rep = tq // LANES
    i = pl.program_id(2)

    @pl.when(i == 0)
    def _():
        km_ref[...] = jnp.zeros(km_ref.shape, F32)
        for h in range(hb):
            for n in range(nb):
                kblk = k_ref[0, n * tq:(n + 1) * tq, h * HEAD_DIM:(h + 1) * HEAD_DIM].astype(F32)
                km_ref[h, n:n + 1, :] = jnp.mean(kblk, axis=0, keepdims=True)

    blk = lax.broadcasted_iota(I32, (tq, LANES), 1).astype(F32)
    row = lax.broadcasted_iota(I32, (tq, tq), 0)
    col = lax.broadcasted_iota(I32, (tq, tq), 1)
    zero_blk = jnp.zeros((LANES, LANES), F32)

    for h in range(hb):
        q = q_ref[0, :, h * HEAD_DIM:(h + 1) * HEAD_DIM]
        gate = lax.dot_general(q, km_ref[h].astype(BF16), (((1,), (1,)), ((), ())),
                               preferred_element_type=F32)
        gate = jnp.where(blk < i.astype(F32), gate, -jnp.inf)
        selb = jnp.full((tq, LANES), MASK_NEG, F32)
        for _ in range(topb):
            gmax = jnp.max(gate, axis=1, keepdims=True)
            first = jnp.min(jnp.where(gate == gmax, blk, float(LANES)), axis=1, keepdims=True)
            hit = jnp.where(blk == first, jnp.where(gmax > -jnp.inf, 1.0, 0.0), 0.0)
            selb = jnp.where(hit > 0.0, 0.0, selb)
            gate = jnp.where(blk == first, -jnp.inf, gate)
        for n in range(nb):
            sb_ref[h, n] = jnp.broadcast_to(selb[:, n:n + 1], (tq, LANES))
        qs_ref[h] = (q.astype(F32) * ATTN_SCALE).astype(BF16)
        m_ref[h] = jnp.full((tq, LANES), MASK_NEG, F32)
        l_ref[h] = jnp.zeros((tq, LANES), F32)
        acc_ref[h] = jnp.zeros((tq, HEAD_DIM), F32)

    def attend(n, extra_fn):
        r0 = pl.multiple_of(n * tq, tq)
        for h in range(hb):
            kblk = k_ref[0, pl.ds(r0, tq), h * HEAD_DIM:(h + 1) * HEAD_DIM]
            vblk = v_ref[0, pl.ds(r0, tq), h * HEAD_DIM:(h + 1) * HEAD_DIM]
            x = lax.dot_general(qs_ref[h], kblk, (((1,), (1,)), ((), ())),
                                preferred_element_type=F32)
            x = extra_fn(x, h, n)
            m_prev = m_ref[h]
            m_new = jnp.maximum(m_prev, jnp.max(x, axis=1, keepdims=True))
            alpha = jnp.exp(m_prev - m_new)
            p = jnp.exp(x - _tile_lanes(m_new, rep))
            l_ref[h] = alpha * l_ref[h] + jnp.sum(p, axis=1, keepdims=True)
            m_ref[h] = m_new
            acc_ref[h] = acc_ref[h] * alpha + jnp.dot(p.astype(BF16), vblk,
                                                      preferred_element_type=F32)

    def far_extra(x, h, n):
        return x + _tile_lanes(sb_ref[h, n], rep)

    def far_body(n, carry):
        attend(n, far_extra)
        return carry

    lax.fori_loop(0, jnp.maximum(i - 1, 0), far_body, 0)

    def near_extra(x, h, n):
        bias = jnp.concatenate([jnp.concatenate([zero_blk, d_ref[1, h]], axis=1),
                                jnp.concatenate([zero_blk, zero_blk], axis=1)], axis=0)
        return x + _tile_lanes(sb_ref[h, n], rep) + bias

    @pl.when(i >= 1)
    def _():
        attend(i - 1, near_extra)

    def own_extra(x, h, n):
        bias = jnp.concatenate([jnp.concatenate([d_ref[0, h], zero_blk], axis=1),
                                jnp.concatenate([d_ref[1, h], d_ref[0, h]], axis=1)], axis=0)
        return jnp.where(col <= row, x + bias, MASK_NEG)

    attend(i, own_extra)

    for h in range(hb):
        o = acc_ref[h] / l_ref[h]
        zz = z_ref[0, :, h * HEAD_DIM:(h + 1) * HEAD_DIM].astype(F32)
        y_ref[0, :, h * HEAD_DIM:(h + 1) * HEAD_DIM] = (o * (zz * jax.nn.sigmoid(zz))).astype(BF16)


def _moba_attention(pbig, col_q, col_k, col_v, col_z, dtab, n_heads):
    b, s, _ = pbig.shape
    tq = MOBA_BLOCK
    assert s % tq == 0
    nb = s // tq
    hb = MOBA_HB
    wg = hb * HEAD_DIM
    topb = min(MOBA_TOPB, nb - 1)
    kern = functools.partial(_moba_kernel, nb=nb, topb=topb)
    return pl.pallas_call(
        kern,
        grid=(b, n_heads // hb, nb),
        in_specs=[pl.BlockSpec((1, tq, wg), lambda bi, g, i: (bi, i, col_q + g)),
                  pl.BlockSpec((1, s, wg), lambda bi, g, i: (bi, 0, col_k + g)),
                  pl.BlockSpec((1, s, wg), lambda bi, g, i: (bi, 0, col_v + g)),
                  pl.BlockSpec((1, tq, wg), lambda bi, g, i: (bi, i, col_z + g)),
                  pl.BlockSpec((2, hb, LANES, LANES), lambda bi, g, i: (0, g, 0, 0))],
        out_specs=pl.BlockSpec((1, tq, wg), lambda bi, g, i: (bi, i, g)),
        out_shape=jax.ShapeDtypeStruct((b, s, n_heads * HEAD_DIM), BF16),
        scratch_shapes=[pltpu.VMEM((hb, LANES, HEAD_DIM), F32),
                        pltpu.VMEM((hb, nb, tq, LANES), F32),
                        pltpu.VMEM((hb, tq, HEAD_DIM), BF16),
                        pltpu.VMEM((hb, tq, HEAD_DIM), F32),
                        pltpu.VMEM((hb, tq, LANES), F32),
                        pltpu.VMEM((hb, tq, LANES), F32)],
        compiler_params=_cparams(("parallel", "parallel", "arbitrary")),
        name="moba_attention",
    )(pbig, pbig, pbig, pbig, dtab)


def _softplus(x):
    return jnp.maximum(x, 0.0) + jnp.log1p(jnp.exp(-jnp.abs(x)))


def _rglru_kernel(xr_ref, z_ref, cw_ref, cb_ref, wa_ref, ba_ref, wx_ref, bx_ref, lam_ref, y_ref,
                  xbuf_ref, a_ref, b_ref, h_ref):
    tc = xr_ref.shape[1]
    c = pl.program_id(2)
    halo = SUBLANES

    @pl.when(c == 0)
    def _():
        xbuf_ref[0:halo, :] = jnp.zeros((halo, xbuf_ref.shape[1]), F32)
        h_ref[...] = jnp.zeros(h_ref.shape, F32)

    @pl.when(c > 0)
    def _():
        xbuf_ref[0:halo, :] = xbuf_ref[tc:tc + halo, :]

    xbuf_ref[halo:halo + tc, :] = xr_ref[0]
    base = halo - (CONV_W - 1)
    xc = cb_ref[...] + xbuf_ref[base:base + tc, :] * cw_ref[0:1, :]
    for j in range(1, CONV_W):
        xc = xc + xbuf_ref[base + j:base + j + tc, :] * cw_ref[j:j + 1, :]

    xcb = xc.astype(BF16)
    r = jax.nn.sigmoid(jnp.dot(xcb, wa_ref[0], preferred_element_type=F32) + ba_ref[...])
    ig = jax.nn.sigmoid(jnp.dot(xcb, wx_ref[0], preferred_element_type=F32) + bx_ref[...])
    log_a = (-RG_C * r) * _softplus(-lam_ref[...])
    a = jnp.exp(log_a)
    a_ref[...] = a
    b_ref[...] = jnp.sqrt(-jnp.tanh(log_a) * (a * a + 1.0)) * (ig * xc)

    sub = lax.broadcasted_iota(I32, (SUBLANES, a_ref.shape[1]), 0)

    def group(gi, h):
        r0 = pl.multiple_of(gi * SUBLANES, SUBLANES)
        a8 = a_ref[pl.ds(r0, SUBLANES), :]
        b8 = b_ref[pl.ds(r0, SUBLANES), :]
        for d in (1, 2, 4):
            a_sh = jnp.where(sub >= d, pltpu.roll(a8, d, 0), 1.0)
            b_sh = jnp.where(sub >= d, pltpu.roll(b8, d, 0), 0.0)
            b8 = a8 * b_sh + b8
            a8 = a8 * a_sh
        h8 = a8 * h + b8
        b_ref[pl.ds(r0, SUBLANES), :] = h8
        return h8[SUBLANES - 1:SUBLANES, :]

    h_ref[...] = lax.fori_loop(0, tc // SUBLANES, group, h_ref[...])

    zz = z_ref[0].astype(F32)
    y_ref[0] = (b_ref[...] * (zz * jax.nn.sigmoid(zz))).astype(BF16)


def _rglru(xr, z, conv_w, conv_b, wa_g, b_a, wx_g, b_x, lam):
    b, s, d = xr.shape
    g = wa_g.shape[0]
    cg = d // g
    tc = min(RG_TC, s)
    row = lambda v: v.reshape(1, d).astype(F32)
    vec_spec = pl.BlockSpec((1, cg), lambda bi, gi, c: (0, gi))
    return pl.pallas_call(
        _rglru_kernel,
        grid=(b, g, s // tc),
        in_specs=[pl.BlockSpec((1, tc, cg), lambda bi, gi, c: (bi, c, gi)),
                  pl.BlockSpec((1, tc, cg), lambda bi, gi, c: (bi, c, gi)),
                  pl.BlockSpec((CONV_W, cg), lambda bi, gi, c: (0, gi)),
                  vec_spec,
                  pl.BlockSpec((1, cg, cg), lambda bi, gi, c: (gi, 0, 0)),
                  vec_spec,
                  pl.BlockSpec((1, cg, cg), lambda bi, gi, c: (gi, 0, 0)),
                  vec_spec,
                  vec_spec],
        out_specs=pl.BlockSpec((1, tc, cg), lambda bi, gi, c: (bi, c, gi)),
        out_shape=jax.ShapeDtypeStruct((b, s, d), BF16),
        scratch_shapes=[pltpu.VMEM((tc + SUBLANES, cg), F32),
                        pltpu.VMEM((tc, cg), F32),
                        pltpu.VMEM((tc, cg), F32),
                        pltpu.VMEM((1, cg), F32)],
        compiler_params=_cparams(("parallel", "parallel", "arbitrary")),
        name="rglru",
    )(xr, z, conv_w.astype(F32), row(conv_b), wa_g, row(b_a), wx_g, row(b_x), row(lam))


def _group_block_diag(w, pair):
    nb, k, _ = w.shape
    wg = w.reshape(nb // pair, pair, k, k)
    eye = jnp.eye(pair, dtype=w.dtype)
    out = jnp.einsum('gpkj,pq->gpkqj', wg, eye)
    return out.reshape(nb // pair, pair * k, pair * k)


def _attention_layer(x2d, b, s, norm_g, w_in, kv_g, w_uk, w_uv, idx_g, idx_b, w_out, rel_bias):
    d_model = x2d.shape[1]
    n_heads = w_uk.shape[0]
    wq = n_heads * HEAD_DIM
    wi_w = N_IDX_HEADS * IDX_DIM
    sizes = (wq, KV_LORA, wi_w, IDX_DIM, N_IDX_HEADS, wq, wq, wq, 2 * wq)
    offs = np.concatenate([[0], np.cumsum(sizes)])
    seg = lambda k: w_in[:, offs[k]:offs[k + 1]]
    qa_w, ckv_w, qi_w, ki_w, wi_w_, qb_w, kb_w, vb_w, z_w = [seg(k) for k in range(9)]
    w_big = jnp.concatenate([qa_w, qi_w, qb_w, kb_w, vb_w, z_w], axis=1).astype(BF16)
    zpad = lambda n: jnp.zeros((d_model, n), w_in.dtype)
    w_small = jnp.concatenate([ckv_w, ki_w, zpad(LANES - IDX_DIM), wi_w_, zpad(LANES - N_IDX_HEADS)],
                              axis=1).astype(BF16)

    h = _rmsnorm(x2d, norm_g, BF16, "rmsnorm0")
    pbig = _matmul([h], [w_big], BF16, "proj_attn_big")
    psmall = _matmul([h], [w_small], F32, "proj_attn_small")
    pbig = pbig.reshape(b, s, -1)
    psmall = psmall.reshape(b, s, -1)

    ckv, ckvt, kbd, widx = _dsa_prep(psmall, kv_g, idx_g, idx_b)
    wukt = jnp.transpose(w_uk, (0, 2, 1)).astype(BF16)
    dtab_a = _bias_blocks(rel_bias[:, :n_heads])
    dtab_b = _bias_blocks(rel_bias[:, n_heads:])
    ya = _dsa_attention(pbig, 0, 1, 5, ckv, ckvt, kbd, widx, wukt, w_uv.astype(BF16), dtab_a, n_heads)
    gpw = n_heads // MOBA_HB
    yb = _moba_attention(pbig, 2 * gpw, 3 * gpw, 4 * gpw, 6 * gpw, dtab_b, n_heads)

    w_out_b = w_out.astype(BF16)
    return _matmul([ya.reshape(b * s, wq), yb.reshape(b * s, wq)], [w_out_b[:wq], w_out_b[wq:]],
                   F32, "proj_attn_out", res=x2d, bn=MM_BN // 2)


def _recurrent_layer(x2d, b, s, norm_g, w_in, conv_w, conv_b, w_a, b_a, w_x, b_x, lam, w_out):
    d_rnn = conv_b.shape[0]
    h = _rmsnorm(x2d, norm_g, BF16, "rmsnorm1")
    w_in_b = w_in.astype(BF16)
    xr = _matmul([h], [w_in_b[:, :d_rnn]], F32, "proj_rec_x")
    z = _matmul([h], [w_in_b[:, d_rnn:]], BF16, "proj_rec_z")
    y = _rglru(xr.reshape(b, s, d_rnn), z.reshape(b, s, d_rnn), conv_w, conv_b,
               _group_block_diag(w_a, RG_PAIR).astype(BF16), b_a,
               _group_block_diag(w_x, RG_PAIR).astype(BF16), b_x, lam)
    return _matmul([y.reshape(b * s, d_rnn)], [w_out.astype(BF16)], F32, "proj_rec_out",
                   res=x2d, bn=MM_BN // 2)


def kernel(x, norm_g, final_g, rel_bias, attn_w_in, attn_kv_g, attn_w_uk, attn_w_uv, idx_k_g, idx_k_b,
           attn_w_out, rec_w_in, rec_conv_w, rec_conv_b, rec_w_a, rec_b_a, rec_w_x, rec_b_x,
           rec_lambda, rec_w_out):
    b, s, d = x.shape
    depth = norm_g.shape[0]
    x2d = x.reshape(b * s, d)
    for layer in range(depth):
        li = layer // 2
        if layer % 2 == 0:
            x2d = _attention_layer(x2d, b, s, norm_g[layer], attn_w_in[li], attn_kv_g[li],
                                   attn_w_uk[li], attn_w_uv[li], idx_k_g[li], idx_k_b[li],
                                   attn_w_out[li], rel_bias)
        else:
            x2d = _recurrent_layer(x2d, b, s, norm_g[layer], rec_w_in[li], rec_conv_w[li],
                                   rec_conv_b[li], rec_w_a[li], rec_b_a[li], rec_w_x[li],
                                   rec_b_x[li], rec_lambda[li], rec_w_out[li])
    return _rmsnorm(x2d, final_g, x.dtype, "rmsnorm_final").reshape(b, s, d)
```

```python
import functools
import math

import numpy as np
import jax
import jax.numpy as jnp
from jax import lax
from jax.experimental import pallas as pl
from jax.experimental.pallas import tpu as pltpu

F32 = jnp.float32
BF16 = jnp.bfloat16
I32 = jnp.int32

HEAD_DIM = 128
KV_LORA = 512
N_IDX_HEADS = 32
IDX_DIM = 64
TOPK_MAX = 256
MOBA_BLOCK = 256
MOBA_TOPB = 3
RG_BLOCKS = 16
CONV_W = 4
RG_C = 8.0
NUM_BUCKETS = 32
MAX_DISTANCE = 128
EPS = 1e-6
ATTN_SCALE = HEAD_DIM ** -0.5
IDX_SCALE = (N_IDX_HEADS ** -0.5) * (IDX_DIM ** -0.5)
LOG2E = math.log2(math.e)

LANES = 128
SUBLANES = 8
VMEM_LIMIT_BYTES = 56 * 2 ** 20
MASK_NEG = -1e30
INT_MIN = np.int32(-2 ** 31)

ROW_TILE = 256
MM_BM = 1024
MM_BN = 1024
ATT_T = 256
ATT_HB = 4
IDX_RB = 128
RG_PAIR = 2
RG_TC = 512
assert ATT_T == MOBA_BLOCK and ATT_T == 2 * LANES


def _cparams(sem):
    return pltpu.CompilerParams(dimension_semantics=sem, vmem_limit_bytes=VMEM_LIMIT_BYTES)


def _tile_lanes(x, n):
    return x if n == 1 else jnp.concatenate([x] * n, axis=1)


def _rmsnorm_kernel(x_ref, g_ref, o_ref):
    x = x_ref[...]
    ms = jnp.mean(x * x, axis=-1, keepdims=True)
    o_ref[...] = ((x * lax.rsqrt(ms + EPS)) * g_ref[...]).astype(o_ref.dtype)


def _rmsnorm(x2d, g, out_dtype, name):
    m, d = x2d.shape
    bm = min(ROW_TILE, m)
    return pl.pallas_call(
        _rmsnorm_kernel,
        grid=(m // bm,),
        in_specs=[pl.BlockSpec((bm, d), lambda i: (i, 0)),
                  pl.BlockSpec((1, d), lambda i: (0, 0))],
        out_specs=pl.BlockSpec((bm, d), lambda i: (i, 0)),
        out_shape=jax.ShapeDtypeStruct((m, d), out_dtype),
        compiler_params=_cparams(("parallel",)),
        name=name,
    )(x2d, g.reshape(1, d).astype(F32))


def _mm_kernel(*refs, n_parts, has_res):
    o_ref = refs[-1]
    acc = jnp.dot(refs[0][...], refs[n_parts][...], preferred_element_type=F32)
    for p in range(1, n_parts):
        acc = acc + jnp.dot(refs[p][...], refs[n_parts + p][...], preferred_element_type=F32)
    if has_res:
        acc = refs[2 * n_parts][...] + acc
    o_ref[...] = acc.astype(o_ref.dtype)


def _matmul(xs, ws, out_dtype, name, res=None, bm=MM_BM, bn=MM_BN):
    m = xs[0].shape[0]
    n = ws[0].shape[1]
    bm = min(bm, m)
    bn = min(bn, n)
    assert m % bm == 0 and n % bn == 0
    in_specs = [pl.BlockSpec((bm, x.shape[1]), lambda i, j: (i, 0)) for x in xs]
    in_specs += [pl.BlockSpec((w.shape[0], bn), lambda i, j: (0, j)) for w in ws]
    args = [*xs, *ws]
    if res is not None:
        in_specs.append(pl.BlockSpec((bm, bn), lambda i, j: (i, j)))
        args.append(res)
    return pl.pallas_call(
        functools.partial(_mm_kernel, n_parts=len(xs), has_res=res is not None),
        grid=(m // bm, n // bn),
        in_specs=in_specs,
        out_specs=pl.BlockSpec((bm, bn), lambda i, j: (i, j)),
        out_shape=jax.ShapeDtypeStruct((m, n), out_dtype),
        compiler_params=_cparams(("parallel", "parallel")),
        name=name,
    )(*args)


def _t5_bucket_np(dist):
    d = np.maximum(dist, 0)
    max_exact = NUM_BUCKETS // 2
    d_f = np.maximum(d, 1).astype(np.float32)
    ratio = np.log(d_f / np.float32(max_exact)) / np.float32(math.log(MAX_DISTANCE / max_exact))
    large = max_exact + (ratio * np.float32(NUM_BUCKETS - max_exact)).astype(np.int32)
    large = np.minimum(large, NUM_BUCKETS - 1)
    return np.where(d < max_exact, d, large)


def _bias_blocks(tab):
    n = LANES
    assert np.all(_t5_bucket_np(np.arange(n, 64 * n)) == NUM_BUCKETS - 1)
    h = tab.shape[1]
    rel = (tab - tab[NUM_BUCKETS - 1]).astype(F32) * LOG2E
    out = []
    for off in (0, n):
        buckets = _t5_bucket_np(off + n - 1 - np.arange(2 * n - 1))
        onehot = np.zeros((2 * n, NUM_BUCKETS), np.float32)
        onehot[np.arange(2 * n - 1), buckets] = 1.0
        w = jnp.dot(jnp.asarray(onehot), rel, precision=lax.Precision.HIGHEST).T
        t = jnp.tile(w, (1, n))[:, :n * (2 * n - 1)].reshape(h, n, 2 * n - 1)
        out.append(t[:, :, n - 1:])
    return jnp.stack(out)


def _dsa_prep_kernel(p_ref, kvg_ref, ig_ref, ib_ref, ckv_ref, kbd_ref, w_ref):
    p = p_ref[0]
    ts = p.shape[0]
    c = p[:, :KV_LORA]
    cn = (c * lax.rsqrt(jnp.mean(c * c, axis=-1, keepdims=True) + EPS)) * kvg_ref[...]
    ckv_ref[0] = cn.astype(BF16)
    k = p[:, KV_LORA:KV_LORA + IDX_DIM]
    mu = jnp.mean(k, axis=-1, keepdims=True)
    var = jnp.mean(jnp.square(k - mu), axis=-1, keepdims=True)
    kn = ((k - mu) * lax.rsqrt(var + EPS)) * ig_ref[...] + ib_ref[...]
    zeros = jnp.zeros((ts, LANES - IDX_DIM), F32)
    top = jnp.concatenate([kn, zeros], axis=1).T
    bot = jnp.concatenate([zeros, kn], axis=1).T
    kbd_ref[0, 0] = jnp.concatenate([top, bot], axis=1).astype(BF16)
    w_ref[0] = p[:, KV_LORA + LANES:KV_LORA + LANES + N_IDX_HEADS] * IDX_SCALE


def _dsa_prep(ps, kv_g, idx_g, idx_b):
    b, s, wid = ps.shape
    ts = ATT_T
    nkt = s // ts
    return pl.pallas_call(
        _dsa_prep_kernel,
        grid=(b, nkt),
        in_specs=[pl.BlockSpec((1, ts, wid), lambda bi, i: (bi, i, 0)),
                  pl.BlockSpec((1, KV_LORA), lambda bi, i: (0, 0)),
                  pl.BlockSpec((1, IDX_DIM), lambda bi, i: (0, 0)),
                  pl.BlockSpec((1, IDX_DIM), lambda bi, i: (0, 0))],
        out_specs=[pl.BlockSpec((1, ts, KV_LORA), lambda bi, i: (bi, i, 0)),
                   pl.BlockSpec((1, 1, LANES, 2 * ts), lambda bi, i: (bi, i, 0, 0)),
                   pl.BlockSpec((1, ts, N_IDX_HEADS), lambda bi, i: (bi, i, 0))],
        out_shape=[jax.ShapeDtypeStruct((b, s, KV_LORA), BF16),
                   jax.ShapeDtypeStruct((b, nkt, LANES, 2 * ts), BF16),
                   jax.ShapeDtypeStruct((b, s, N_IDX_HEADS), F32)],
        compiler_params=_cparams(("parallel", "parallel")),
        name="dsa_prep",
    )(ps, kv_g.reshape(1, -1).astype(F32), idx_g.reshape(1, -1).astype(F32),
      idx_b.reshape(1, -1).astype(F32))


def _dsa_index_kernel(qi_ref, w_ref, kbd_ref, mask_ref, key_ref, wb_ref, t_ref, j_ref,
                      *, topk, nkt_total, seq_bits):
    tq = tk = ATT_T
    rep = tk // LANES
    rb = IDX_RB
    i = pl.program_id(1)
    t0 = i * tq
    nkt = i + 1

    row = t0 + lax.broadcasted_iota(I32, (tq, tk), 0)
    col = lax.broadcasted_iota(I32, (tq, tk), 1)
    col_rb = lax.broadcasted_iota(I32, (rb, tk), 1)

    for h in range(N_IDX_HEADS):
        wb_ref[h] = jnp.broadcast_to(w_ref[0, :, h:h + 1], (tq, LANES))

    def idx_body(kt, carry):
        kb = kbd_ref[0, kt]
        acc = jnp.zeros((tq, tk), F32)
        for hp in range(N_IDX_HEADS // 2):
            rel = jnp.dot(qi_ref[0, :, hp * LANES:(hp + 1) * LANES], kb,
                          preferred_element_type=F32)
            acc = acc + jnp.maximum(rel[:, :tk], 0.0) * _tile_lanes(wb_ref[2 * hp], rep)
            acc = acc + jnp.maximum(rel[:, tk:], 0.0) * _tile_lanes(wb_ref[2 * hp + 1], rep)
        score = jnp.where(acc == 0.0, 0.0, acc)
        bits = lax.bitcast_convert_type(score, I32)
        key = bits ^ ((bits >> 31) & np.int32(0x7FFFFFFF))
        key_ref[kt] = jnp.where(kt * tk + col <= row, key, INT_MIN)
        return carry

    lax.fori_loop(0, nkt, idx_body, 0)

    def count(pred_fn):
        parts = []
        for r0 in range(0, tq, rb):
            def body(kt, c, r0=r0):
                x = pred_fn(key_ref[kt, r0:r0 + rb, :], kt, r0)
                s = x[:, :LANES]
                for r in range(1, rep):
                    s = s + x[:, r * LANES:(r + 1) * LANES]
                return c + s
            c = lax.fori_loop(0, nkt, body, jnp.zeros((rb, LANES), F32))
            parts.append(jnp.broadcast_to(jnp.sum(c, axis=1, keepdims=True), (rb, LANES)))
        return jnp.concatenate(parts, axis=0)

    t_ref[...] = jnp.full((tq, LANES), INT_MIN, I32)

    def bit_body(it, carry):
        cand = t_ref[...] ^ jnp.left_shift(jnp.int32(1), 31 - it)
        n_ge = count(lambda k, kt, r0: jnp.where(k >= _tile_lanes(cand[r0:r0 + rb], rep), 1.0, 0.0))
        t_ref[...] = jnp.where(n_ge >= topk, cand, t_ref[...])
        return carry

    lax.fori_loop(0, 32, bit_body, 0)

    thr = t_ref[...]
    thr_t = _tile_lanes(thr, rep)
    n_gt = count(lambda k, kt, r0: jnp.where(k > thr_t[r0:r0 + rb], 1.0, 0.0))
    n_ge = count(lambda k, kt, r0: jnp.where(k >= thr_t[r0:r0 + rb], 1.0, 0.0))
    need = topk - n_gt
    j_ref[...] = jnp.full((tq, LANES), 2 ** 30, I32)

    @pl.when(jnp.max(n_ge) > topk)
    def _():
        j_ref[...] = jnp.zeros((tq, LANES), I32)

        def jbit_body(it, carry):
            cand = j_ref[...] | jnp.left_shift(jnp.int32(1), seq_bits - 1 - it)
            cand_t = _tile_lanes(cand, rep)
            n_lt = count(lambda k, kt, r0: jnp.where(
                k == thr_t[r0:r0 + rb],
                jnp.where(kt * tk + col_rb < cand_t[r0:r0 + rb], 1.0, 0.0), 0.0))
            j_ref[...] = jnp.where(n_lt < need, cand, j_ref[...])
            return carry

        lax.fori_loop(0, seq_bits, jbit_body, 0)

    jmax_t = _tile_lanes(j_ref[...], rep)

    def mask_body(kt, carry):
        k = key_ref[kt]
        s_idx = kt * tk + col
        keep_tie = jnp.where(k == thr_t, jnp.where(s_idx <= jmax_t, 0.0, MASK_NEG), MASK_NEG)
        keep = jnp.where(k > thr_t, 0.0, keep_tie)
        mask_ref[0, 0, kt] = jnp.where(s_idx <= row, keep, MASK_NEG).astype(BF16)
        return carry

    lax.fori_loop(0, nkt, mask_body, 0)

    def fill_body(kt, carry):
        mask_ref[0, 0, kt] = jnp.full((tq, tk), MASK_NEG, BF16)
        return carry

    lax.fori_loop(nkt, nkt_total, fill_body, 0)


def _dsa_index(pbig, col_qi, kbd, widx):
    b, s, _ = pbig.shape
    tq = tk = ATT_T
    nkt = s // tk
    wq = N_IDX_HEADS * IDX_DIM
    topk = min(TOPK_MAX, s // 4)
    kern = functools.partial(_dsa_index_kernel, topk=topk, nkt_total=nkt,
                             seq_bits=max(1, (s - 1).bit_length()))
    return pl.pallas_call(
        kern,
        grid=(b, s // tq),
        in_specs=[pl.BlockSpec((1, tq, wq), lambda bi, i: (bi, i, col_qi)),
                  pl.BlockSpec((1, tq, N_IDX_HEADS), lambda bi, i: (bi, i, 0)),
                  pl.BlockSpec((1, nkt, LANES, 2 * tk), lambda bi, i: (bi, 0, 0, 0))],
        out_specs=pl.BlockSpec((1, 1, nkt, tq, tk), lambda bi, i: (bi, i, 0, 0, 0)),
        out_shape=jax.ShapeDtypeStruct((b, s // tq, nkt, tq, tk), BF16),
        scratch_shapes=[pltpu.VMEM((nkt, tq, tk), I32),
                        pltpu.VMEM((N_IDX_HEADS, tq, LANES), F32),
                        pltpu.VMEM((tq, LANES), I32),
                        pltpu.VMEM((tq, LANES), I32)],
        compiler_params=_cparams(("parallel", "parallel")),
        name="dsa_index",
    )(pbig, widx, kbd)


def _flash_kernel(*refs, mode, nb, topb):
    if mode == "mask":
        q_ref, k_ref, v_ref, z_ref, d_ref, mask_ref, y_ref, qs_ref, acc_ref, m_ref = refs
    else:
        q_ref, k_ref, v_ref, z_ref, d_ref, y_ref, km_ref, qs_ref, acc_ref, m_ref = refs
    tq = ATT_T
    hb = q_ref.shape[2] // HEAD_DIM
    rep = tq // LANES
    i = pl.program_id(2)
    lane = lax.broadcasted_iota(I32, (tq, LANES), 1)
    zero_blk = jnp.zeros((LANES, LANES), F32)
    ones_blk = jnp.ones((tq, LANES), BF16)

    if mode == "sel":
        @pl.when(i == 0)
        def _():
            km_ref[...] = jnp.zeros(km_ref.shape, F32)
            for h in range(hb):
                for n in range(nb):
                    kblk = k_ref[0, n * tq:(n + 1) * tq, h * HEAD_DIM:(h + 1) * HEAD_DIM].astype(F32)
                    km_ref[h, n:n + 1, :] = jnp.mean(kblk, axis=0, keepdims=True)
        blk = lane.astype(F32)

    for h in range(hb):
        q = q_ref[0, :, h * HEAD_DIM:(h + 1) * HEAD_DIM]
        qsc = (q.astype(F32) * (ATTN_SCALE * LOG2E)).astype(BF16)
        if mode == "sel":
            gate = lax.dot_general(q, km_ref[h].astype(BF16), (((1,), (1,)), ((), ())),
                                   preferred_element_type=F32)
            gate = jnp.where(blk < i.astype(F32), gate, -jnp.inf)
            selb = jnp.full((tq, LANES), MASK_NEG, F32)
            for _ in range(topb):
                gmax = jnp.max(gate, axis=1, keepdims=True)
                first = jnp.min(jnp.where(gate == gmax, blk, float(LANES)), axis=1, keepdims=True)
                hit = jnp.where(blk == first, jnp.where(gmax > -jnp.inf, 1.0, 0.0), 0.0)
                selb = jnp.where(hit > 0.0, 0.0, selb)
                gate = jnp.where(blk == first, -jnp.inf, gate)
            qs_ref[h] = jnp.concatenate([qsc, selb.astype(BF16)], axis=1)
        else:
            qs_ref[h] = qsc
        m_ref[h] = jnp.full((tq, LANES), MASK_NEG, F32)
        acc_ref[h] = jnp.zeros((tq, 2 * HEAD_DIM), F32)

    def attend(n, kind):
        r0 = pl.multiple_of(n * tq, tq)
        if mode == "mask":
            mask_tile = mask_ref[0, 0, n].astype(F32)
        else:
            sel_lane = n if kind != "own" else -1
            onehot = jnp.where(lane == sel_lane, 1.0, 0.0).astype(BF16)
            row = lax.broadcasted_iota(I32, (tq, tq), 0)
            col = lax.broadcasted_iota(I32, (tq, tq), 1)
        for h in range(hb):
            kblk = k_ref[0, pl.ds(r0, tq), h * HEAD_DIM:(h + 1) * HEAD_DIM]
            vblk = v_ref[0, pl.ds(r0, tq), h * HEAD_DIM:(h + 1) * HEAD_DIM]
            if mode == "sel":
                kblk = jnp.concatenate([kblk, onehot], axis=1)
            x = lax.dot_general(qs_ref[h], kblk, (((1,), (1,)), ((), ())),
                                preferred_element_type=F32)
            if mode == "mask":
                x = x + mask_tile
            if kind == "near":
                x = x + jnp.concatenate([jnp.concatenate([zero_blk, d_ref[1, h]], axis=1),
                                         jnp.concatenate([zero_blk, zero_blk], axis=1)], axis=0)
            elif kind == "own":
                x = x + jnp.concatenate([jnp.concatenate([d_ref[0, h], zero_blk], axis=1),
                                         jnp.concatenate([d_ref[1, h], d_ref[0, h]], axis=1)], axis=0)
                if mode == "sel":
                    x = jnp.where(col <= row, x, MASK_NEG)
            m_prev = m_ref[h]
            m_new = jnp.maximum(m_prev, jnp.max(x, axis=1, keepdims=True))
            alpha = jnp.exp2(m_prev - m_new)
            p = jnp.exp2(x - _tile_lanes(m_new, rep)).astype(BF16)
            pv = jnp.dot(p, jnp.concatenate([vblk, ones_blk], axis=1), preferred_element_type=F32)
            acc_ref[h] = acc_ref[h] * _tile_lanes(alpha, 2) + pv
            m_ref[h] = m_new

    def far_body(n, carry):
        attend(n, "far")
        return carry

    lax.fori_loop(0, jnp.maximum(i - 1, 0), far_body, 0)

    @pl.when(i >= 1)
    def _():
        attend(i - 1, "near")

    attend(i, "own")

    for h in range(hb):
        acc = acc_ref[h]
        o = acc[:, :HEAD_DIM] / acc[:, HEAD_DIM:]
        zz = z_ref[0, :, h * HEAD_DIM:(h + 1) * HEAD_DIM].astype(F32)
        y_ref[0, :, h * HEAD_DIM:(h + 1) * HEAD_DIM] = (o * (zz * jax.nn.sigmoid(zz))).astype(BF16)


def _flash_attention(q, k, v, z, dtab, n_heads, name, mask=None):
    (qa, cq), (ka, ck), (va, cv), (za, cz) = q, k, v, z
    b, s, _ = qa.shape
    tq = ATT_T
    assert s % tq == 0
    nb = s // tq
    hb = ATT_HB
    wg = hb * HEAD_DIM
    mode = "sel" if mask is None else "mask"
    kern = functools.partial(_flash_kernel, mode=mode, nb=nb, topb=min(MOBA_TOPB, nb - 1))
    in_specs = [pl.BlockSpec((1, tq, wg), lambda bi, g, i: (bi, i, cq + g)),
                pl.BlockSpec((1, s, wg), lambda bi, g, i: (bi, 0, ck + g)),
                pl.BlockSpec((1, s, wg), lambda bi, g, i: (bi, 0, cv + g)),
                pl.BlockSpec((1, tq, wg), lambda bi, g, i: (bi, i, cz + g)),
                pl.BlockSpec((2, hb, LANES, LANES), lambda bi, g, i: (0, g, 0, 0))]
    args = [qa, ka, va, za, dtab]
    scratch = []
    if mode == "mask":
        in_specs.append(pl.BlockSpec((1, 1, nb, tq, tq), lambda bi, g, i: (bi, i, 0, 0, 0)))
        args.append(mask)
        kq = HEAD_DIM
    else:
        scratch.append(pltpu.VMEM((hb, LANES, HEAD_DIM), F32))
        kq = 2 * HEAD_DIM
    scratch += [pltpu.VMEM((hb, tq, kq), BF16),
                pltpu.VMEM((hb, tq, 2 * HEAD_DIM), F32),
                pltpu.VMEM((hb, tq, LANES), F32)]
    return pl.pallas_call(
        kern,
        grid=(b, n_heads // hb, nb),
        in_specs=in_specs,
        out_specs=pl.BlockSpec((1, tq, wg), lambda bi, g, i: (bi, i, g)),
        out_shape=jax.ShapeDtypeStruct((b, s, n_heads * HEAD_DIM), BF16),
        scratch_shapes=scratch,
        compiler_params=_cparams(("parallel", "parallel", "arbitrary")),
        name=name,
    )(*args)


def _softplus(x):
    return jnp.maximum(x, 0.0) + jnp.log1p(jnp.exp(-jnp.abs(x)))


def _rglru_kernel(xr_ref, z_ref, cw_ref, cb_ref, wa_ref, ba_ref, wx_ref, bx_ref, lam_ref, y_ref,
                  xbuf_ref, a_ref, b_ref, h_ref):
    tc = xr_ref.shape[1]
    c = pl.program_id(2)
    halo = SUBLANES

    @pl.when(c == 0)
    def _():
        xbuf_ref[0:halo, :] = jnp.zeros((halo, xbuf_ref.shape[1]), F32)
        h_ref[...] = jnp.zeros(h_ref.shape, F32)

    @pl.when(c > 0)
    def _():
        xbuf_ref[0:halo, :] = xbuf_ref[tc:tc + halo, :]

    xbuf_ref[halo:halo + tc, :] = xr_ref[0]
    base = halo - (CONV_W - 1)
    xc = cb_ref[...] + xbuf_ref[base:base + tc, :] * cw_ref[0:1, :]
    for j in range(1, CONV_W):
        xc = xc + xbuf_ref[base + j:base + j + tc, :] * cw_ref[j:j + 1, :]

    xcb = xc.astype(BF16)
    r = jax.nn.sigmoid(jnp.dot(xcb, wa_ref[0], preferred_element_type=F32) + ba_ref[...])
    ig = jax.nn.sigmoid(jnp.dot(xcb, wx_ref[0], preferred_element_type=F32) + bx_ref[...])
    log_a = (-RG_C * r) * _softplus(-lam_ref[...])
    a = jnp.exp(log_a)
    a_ref[...] = a
    b_ref[...] = jnp.sqrt(-jnp.tanh(log_a) * (a * a + 1.0)) * (ig * xc)

    sub = lax.broadcasted_iota(I32, (SUBLANES, a_ref.shape[1]), 0)

    def group(gi, h):
        r0 = pl.multiple_of(gi * SUBLANES, SUBLANES)
        a8 = a_ref[pl.ds(r0, SUBLANES), :]
        b8 = b_ref[pl.ds(r0, SUBLANES), :]
        for d in (1, 2, 4):
            a_sh = jnp.where(sub >= d, pltpu.roll(a8, d, 0), 1.0)
            b_sh = jnp.where(sub >= d, pltpu.roll(b8, d, 0), 0.0)
            b8 = a8 * b_sh + b8
            a8 = a8 * a_sh
        h8 = a8 * h + b8
        b_ref[pl.ds(r0, SUBLANES), :] = h8
        return h8[SUBLANES - 1:SUBLANES, :]

    h_ref[...] = lax.fori_loop(0, tc // SUBLANES, group, h_ref[...])

    zz = z_ref[0].astype(F32)
    y_ref[0] = (b_ref[...] * (zz * jax.nn.sigmoid(zz))).astype(BF16)


def _rglru(xr, z, conv_w, conv_b, wa_g, b_a, wx_g, b_x, lam):
    b, s, d = xr.shape
    g = wa_g.shape[0]
    cg = d // g
    tc = min(RG_TC, s)
    row = lambda v: v.reshape(1, d).astype(F32)
    vec_spec = pl.BlockSpec((1, cg), lambda bi, gi, c: (0, gi))
    return pl.pallas_call(
        _rglru_kernel,
        grid=(b, g, s // tc),
        in_specs=[pl.BlockSpec((1, tc, cg), lambda bi, gi, c: (bi, c, gi)),
                  pl.BlockSpec((1, tc, cg), lambda bi, gi, c: (bi, c, gi)),
                  pl.BlockSpec((CONV_W, cg), lambda bi, gi, c: (0, gi)),
                  vec_spec,
                  pl.BlockSpec((1, cg, cg), lambda bi, gi, c: (gi, 0, 0)),
                  vec_spec,
                  pl.BlockSpec((1, cg, cg), lambda bi, gi, c: (gi, 0, 0)),
                  vec_spec,
                  vec_spec],
        out_specs=pl.BlockSpec((1, tc, cg), lambda bi, gi, c: (bi, c, gi)),
        out_shape=jax.ShapeDtypeStruct((b, s, d), BF16),
        scratch_shapes=[pltpu.VMEM((tc + SUBLANES, cg), F32),
                        pltpu.VMEM((tc, cg), F32),
                        pltpu.VMEM((tc, cg), F32),
                        pltpu.VMEM((1, cg), F32)],
        compiler_params=_cparams(("parallel", "parallel", "arbitrary")),
        name="rglru",
    )(xr, z, conv_w.astype(F32), row(conv_b), wa_g, row(b_a), wx_g, row(b_x), row(lam))


def _group_block_diag(w, pair):
    nb, k, _ = w.shape
    wg = w.reshape(nb // pair, pair, k, k)
    eye = jnp.eye(pair, dtype=w.dtype)
    out = jnp.einsum('gpkj,pq->gpkqj', wg, eye)
    return out.reshape(nb // pair, pair * k, pair * k)


def _attention_layer(x2d, b, s, norm_g, w_in, kv_g, w_uk, w_uv, idx_g, idx_b, w_out, rel_bias):
    d_model = x2d.shape[1]
    n_heads = w_uk.shape[0]
    wq = n_heads * HEAD_DIM
    assert N_IDX_HEADS * IDX_DIM == wq
    sizes = (wq, KV_LORA, wq, IDX_DIM, N_IDX_HEADS, wq, wq, wq, 2 * wq)
    offs = np.concatenate([[0], np.cumsum(sizes)])
    seg = lambda k0, k1: w_in[:, offs[k0]:offs[k1]]
    w_big = jnp.concatenate([seg(0, 1), seg(2, 3), seg(5, 9)], axis=1).astype(BF16)
    zpad = lambda n: jnp.zeros((d_model, n), w_in.dtype)
    w_small = jnp.concatenate([seg(1, 2), seg(3, 4), zpad(LANES - IDX_DIM), seg(4, 5),
                               zpad(LANES - N_IDX_HEADS)], axis=1).astype(BF16)

    h = _rmsnorm(x2d, norm_g, BF16, "rmsnorm0")
    pbig = _matmul([h], [w_big], BF16, "proj_attn_big").reshape(b, s, -1)
    psmall = _matmul([h], [w_small], F32, "proj_attn_small").reshape(b, s, -1)

    ckv, kbd, widx = _dsa_prep(psmall, kv_g, idx_g, idx_b)
    mask = _dsa_index(pbig, 1, kbd, widx)
    w_kv = jnp.concatenate([jnp.transpose(w_uk, (1, 0, 2)).reshape(KV_LORA, wq),
                            jnp.transpose(w_uv, (1, 0, 2)).reshape(KV_LORA, wq)], axis=1).astype(BF16)
    kv = _matmul([ckv.reshape(b * s, KV_LORA)], [w_kv], BF16, "dsa_kv_up").reshape(b, s, 2 * wq)

    dtab_a = _bias_blocks(rel_bias[:, :n_heads])
    dtab_b = _bias_blocks(rel_bias[:, n_heads:])
    gpw = n_heads // ATT_HB
    ya = _flash_attention((pbig, 0), (kv, 0), (kv, gpw), (pbig, 5 * gpw), dtab_a, n_heads,
                          "dsa_attention", mask=mask)
    yb = _flash_attention((pbig, 2 * gpw), (pbig, 3 * gpw), (pbig, 4 * gpw), (pbig, 6 * gpw), dtab_b,
                          n_heads, "moba_attention")

    w_out_b = w_out.astype(BF16)
    return _matmul([ya.reshape(b * s, wq), yb.reshape(b * s, wq)], [w_out_b[:wq], w_out_b[wq:]],
                   F32, "proj_attn_out", res=x2d, bn=MM_BN // 2)


def _recurrent_layer(x2d, b, s, norm_g, w_in, conv_w, conv_b, w_a, b_a, w_x, b_x, lam, w_out):
    d_rnn = conv_b.shape[0]
    h = _rmsnorm(x2d, norm_g, BF16, "rmsnorm1")
    w_in_b = w_in.astype(BF16)
    xr = _matmul([h], [w_in_b[:, :d_rnn]], F32, "proj_rec_x")
    z = _matmul([h], [w_in_b[:, d_rnn:]], BF16, "proj_rec_z")
    y = _rglru(xr.reshape(b, s, d_rnn), z.reshape(b, s, d_rnn), conv_w, conv_b,
               _group_block_diag(w_a, RG_PAIR).astype(BF16), b_a,
               _group_block_diag(w_x, RG_PAIR).astype(BF16), b_x, lam)
    return _matmul([y.reshape(b * s, d_rnn)], [w_out.astype(BF16)], F32, "proj_rec_out",
                   res=x2d, bn=MM_BN // 2)


def kernel(x, norm_g, final_g, rel_bias, attn_w_in, attn_kv_g, attn_w_uk, attn_w_uv, idx_k_g, idx_k_b,
           attn_w_out, rec_w_in, rec_conv_w, rec_conv_b, rec_w_a, rec_b_a, rec_w_x, rec_b_x,
           rec_lambda, rec_w_out):
    b, s, d = x.shape
    depth = norm_g.shape[0]
    x2d = x.reshape(b * s, d)
    for layer in range(depth):
        li = layer // 2
        if layer % 2 == 0:
            x2d = _attention_layer(x2d, b, s, norm_g[layer], attn_w_in[li], attn_kv_g[li],
                                   attn_w_uk[li], attn_w_uv[li], idx_k_g[li], idx_k_b[li],
                                   attn_w_out[li], rel_bias)
        else:
            x2d = _recurrent_layer(x2d, b, s, norm_g[layer], rec_w_in[li], rec_conv_w[li],
                                   rec_conv_b[li], rec_w_a[li], rec_b_a[li], rec_w_x[li],
                                   rec_b_x[li], rec_lambda[li], rec_w_out[li])
    return _rmsnorm(x2d, final_g, x.dtype, "rmsnorm_final").reshape(b, s, d)
```

```python
import functools
import math

import numpy as np
import jax
import jax.numpy as jnp
from jax import lax
from jax.experimental import pallas as pl
from jax.experimental.pallas import tpu as pltpu

F32 = jnp.float32
BF16 = jnp.bfloat16
I32 = jnp.int32

HEAD_DIM = 128
KV_LORA = 512
N_IDX_HEADS = 32
IDX_DIM = 64
TOPK_MAX = 256
MOBA_BLOCK = 256
MOBA_TOPB = 3
RG_BLOCKS = 16
CONV_W = 4
RG_C = 8.0
NUM_BUCKETS = 32
MAX_DISTANCE = 128
EPS = 1e-6
ATTN_SCALE = HEAD_DIM ** -0.5
IDX_SCALE = (N_IDX_HEADS ** -0.5) * (IDX_DIM ** -0.5)
LOG2E = math.log2(math.e)

LANES = 128
SUBLANES = 8
VMEM_LIMIT_BYTES = 56 * 2 ** 20
MASK_NEG = -1e30
INT_MIN = np.int32(-2 ** 31)

ROW_TILE = 256
MM_BM = 1024
MM_BN = 1024
ATT_T = 256
ATT_HB = 4
IDX_RB = 128
RG_PAIR = 2
RG_TC = 512
assert ATT_T == MOBA_BLOCK and ATT_T == 2 * LANES


def _cparams(sem):
    return pltpu.CompilerParams(dimension_semantics=sem, vmem_limit_bytes=VMEM_LIMIT_BYTES)


def _tile_lanes(x, n):
    return x if n == 1 else jnp.concatenate([x] * n, axis=1)


def _rmsnorm_kernel(x_ref, g_ref, o_ref):
    x = x_ref[...]
    ms = jnp.mean(x * x, axis=-1, keepdims=True)
    o_ref[...] = ((x * lax.rsqrt(ms + EPS)) * g_ref[...]).astype(o_ref.dtype)


def _rmsnorm(x2d, g, out_dtype, name):
    m, d = x2d.shape
    bm = min(ROW_TILE, m)
    return pl.pallas_call(
        _rmsnorm_kernel,
        grid=(m // bm,),
        in_specs=[pl.BlockSpec((bm, d), lambda i: (i, 0)),
                  pl.BlockSpec((1, d), lambda i: (0, 0))],
        out_specs=pl.BlockSpec((bm, d), lambda i: (i, 0)),
        out_shape=jax.ShapeDtypeStruct((m, d), out_dtype),
        compiler_params=_cparams(("parallel",)),
        name=name,
    )(x2d, g.reshape(1, d).astype(F32))


def _mm_kernel(*refs, n_parts, has_res):
    o_ref = refs[-1]
    acc = jnp.dot(refs[0][...], refs[n_parts][...], preferred_element_type=F32)
    for p in range(1, n_parts):
        acc = acc + jnp.dot(refs[p][...], refs[n_parts + p][...], preferred_element_type=F32)
    if has_res:
        acc = refs[2 * n_parts][...] + acc
    o_ref[...] = acc.astype(o_ref.dtype)


def _matmul(xs, ws, out_dtype, name, res=None, bm=MM_BM, bn=MM_BN):
    m = xs[0].shape[0]
    n = ws[0].shape[1]
    bm = min(bm, m)
    bn = min(bn, n)
    assert m % bm == 0 and n % bn == 0
    in_specs = [pl.BlockSpec((bm, x.shape[1]), lambda i, j: (i, 0)) for x in xs]
    in_specs += [pl.BlockSpec((w.shape[0], bn), lambda i, j: (0, j)) for w in ws]
    args = [*xs, *ws]
    if res is not None:
        in_specs.append(pl.BlockSpec((bm, bn), lambda i, j: (i, j)))
        args.append(res)
    return pl.pallas_call(
        functools.partial(_mm_kernel, n_parts=len(xs), has_res=res is not None),
        grid=(m // bm, n // bn),
        in_specs=in_specs,
        out_specs=pl.BlockSpec((bm, bn), lambda i, j: (i, j)),
        out_shape=jax.ShapeDtypeStruct((m, n), out_dtype),
        compiler_params=_cparams(("parallel", "parallel")),
        name=name,
    )(*args)


def _t5_bucket_np(dist):
    d = np.maximum(dist, 0)
    max_exact = NUM_BUCKETS // 2
    d_f = np.maximum(d, 1).astype(np.float32)
    ratio = np.log(d_f / np.float32(max_exact)) / np.float32(math.log(MAX_DISTANCE / max_exact))
    large = max_exact + (ratio * np.float32(NUM_BUCKETS - max_exact)).astype(np.int32)
    large = np.minimum(large, NUM_BUCKETS - 1)
    return np.where(d < max_exact, d, large)


def _bias_blocks(tab):
    n = LANES
    assert np.all(_t5_bucket_np(np.arange(n, 64 * n)) == NUM_BUCKETS - 1)
    h = tab.shape[1]
    rel = (tab - tab[NUM_BUCKETS - 1]).astype(F32) * LOG2E
    out = []
    for off in (0, n):
        buckets = _t5_bucket_np(off + n - 1 - np.arange(2 * n - 1))
        onehot = np.zeros((2 * n, NUM_BUCKETS), np.float32)
        onehot[np.arange(2 * n - 1), buckets] = 1.0
        w = jnp.dot(jnp.asarray(onehot), rel, precision=lax.Precision.HIGHEST).T
        t = jnp.tile(w, (1, n))[:, :n * (2 * n - 1)].reshape(h, n, 2 * n - 1)
        out.append(t[:, :, n - 1:])
    return jnp.stack(out)


def _dsa_prep_kernel(p_ref, kvg_ref, ig_ref, ib_ref, ckv_ref, kbd_ref, w_ref):
    p = p_ref[0]
    ts = p.shape[0]
    c = p[:, :KV_LORA]
    cn = (c * lax.rsqrt(jnp.mean(c * c, axis=-1, keepdims=True) + EPS)) * kvg_ref[...]
    ckv_ref[0] = cn.astype(BF16)
    k = p[:, KV_LORA:KV_LORA + IDX_DIM]
    mu = jnp.mean(k, axis=-1, keepdims=True)
    var = jnp.mean(jnp.square(k - mu), axis=-1, keepdims=True)
    kn = ((k - mu) * lax.rsqrt(var + EPS)) * ig_ref[...] + ib_ref[...]
    zeros = jnp.zeros((ts, LANES - IDX_DIM), F32)
    top = jnp.concatenate([kn, zeros], axis=1).T
    bot = jnp.concatenate([zeros, kn], axis=1).T
    kbd_ref[0, 0] = jnp.concatenate([top, bot], axis=1).astype(BF16)
    w_ref[0] = p[:, KV_LORA + LANES:KV_LORA + LANES + N_IDX_HEADS] * IDX_SCALE


def _dsa_prep(ps, kv_g, idx_g, idx_b):
    b, s, wid = ps.shape
    ts = ATT_T
    nkt = s // ts
    return pl.pallas_call(
        _dsa_prep_kernel,
        grid=(b, nkt),
        in_specs=[pl.BlockSpec((1, ts, wid), lambda bi, i: (bi, i, 0)),
                  pl.BlockSpec((1, KV_LORA), lambda bi, i: (0, 0)),
                  pl.BlockSpec((1, IDX_DIM), lambda bi, i: (0, 0)),
                  pl.BlockSpec((1, IDX_DIM), lambda bi, i: (0, 0))],
        out_specs=[pl.BlockSpec((1, ts, KV_LORA), lambda bi, i: (bi, i, 0)),
                   pl.BlockSpec((1, 1, LANES, 2 * ts), lambda bi, i: (bi, i, 0, 0)),
                   pl.BlockSpec((1, ts, N_IDX_HEADS), lambda bi, i: (bi, i, 0))],
        out_shape=[jax.ShapeDtypeStruct((b, s, KV_LORA), BF16),
                   jax.ShapeDtypeStruct((b, nkt, LANES, 2 * ts), BF16),
                   jax.ShapeDtypeStruct((b, s, N_IDX_HEADS), F32)],
        compiler_params=_cparams(("parallel", "parallel")),
        name="dsa_prep",
    )(ps, kv_g.reshape(1, -1).astype(F32), idx_g.reshape(1, -1).astype(F32),
      idx_b.reshape(1, -1).astype(F32))


def _dsa_index_kernel(qi_ref, w_ref, kbd_ref, mask_ref, key_ref, wb_ref, t_ref, j_ref,
                      *, topk, nkt_total, seq_bits):
    tq = tk = ATT_T
    rep = tk // LANES
    rb = IDX_RB
    i = pl.program_id(1)
    t0 = i * tq
    nkt = i + 1

    row = t0 + lax.broadcasted_iota(I32, (tq, tk), 0)
    col = lax.broadcasted_iota(I32, (tq, tk), 1)
    col_rb = lax.broadcasted_iota(I32, (rb, tk), 1)

    for h in range(N_IDX_HEADS):
        wb_ref[h] = jnp.broadcast_to(w_ref[0, :, h:h + 1], (tq, LANES))

    def idx_body(kt, carry):
        kb = kbd_ref[0, kt]
        acc = jnp.zeros((tq, tk), F32)
        for hp in range(N_IDX_HEADS // 2):
            rel = jnp.dot(qi_ref[0, :, hp * LANES:(hp + 1) * LANES], kb,
                          preferred_element_type=F32)
            acc = acc + jnp.maximum(rel[:, :tk], 0.0) * _tile_lanes(wb_ref[2 * hp], rep)
            acc = acc + jnp.maximum(rel[:, tk:], 0.0) * _tile_lanes(wb_ref[2 * hp + 1], rep)
        score = jnp.where(acc == 0.0, 0.0, acc)
        bits = lax.bitcast_convert_type(score, I32)
        key = bits ^ ((bits >> 31) & np.int32(0x7FFFFFFF))
        key_ref[kt] = jnp.where(kt * tk + col <= row, key, INT_MIN)
        return carry

    lax.fori_loop(0, nkt, idx_body, 0)

    def count(pred_fn):
        parts = []
        for r0 in range(0, tq, rb):
            def body(kt, c, r0=r0):
                x = pred_fn(key_ref[kt, r0:r0 + rb, :], kt, r0)
                s = x[:, :LANES]
                for r in range(1, rep):
                    s = s + x[:, r * LANES:(r + 1) * LANES]
                return c + s
            c = lax.fori_loop(0, nkt, body, jnp.zeros((rb, LANES), F32))
            parts.append(jnp.broadcast_to(jnp.sum(c, axis=1, keepdims=True), (rb, LANES)))
        return jnp.concatenate(parts, axis=0)

    t_ref[...] = jnp.full((tq, LANES), INT_MIN, I32)

    def bit_body(it, carry):
        cand = t_ref[...] ^ jnp.left_shift(jnp.int32(1), 31 - it)
        n_ge = count(lambda k, kt, r0: jnp.where(k >= _tile_lanes(cand[r0:r0 + rb], rep), 1.0, 0.0))
        t_ref[...] = jnp.where(n_ge >= topk, cand, t_ref[...])
        return carry

    lax.fori_loop(0, 32, bit_body, 0)

    thr = t_ref[...]
    thr_t = _tile_lanes(thr, rep)
    n_gt = count(lambda k, kt, r0: jnp.where(k > thr_t[r0:r0 + rb], 1.0, 0.0))
    n_ge = count(lambda k, kt, r0: jnp.where(k >= thr_t[r0:r0 + rb], 1.0, 0.0))
    need = topk - n_gt
    j_ref[...] = jnp.full((tq, LANES), 2 ** 30, I32)

    @pl.when(jnp.max(n_ge) > topk)
    def _():
        j_ref[...] = jnp.zeros((tq, LANES), I32)

        def jbit_body(it, carry):
            cand = j_ref[...] | jnp.left_shift(jnp.int32(1), seq_bits - 1 - it)
            cand_t = _tile_lanes(cand, rep)
            n_lt = count(lambda k, kt, r0: jnp.where(
                k == thr_t[r0:r0 + rb],
                jnp.where(kt * tk + col_rb < cand_t[r0:r0 + rb], 1.0, 0.0), 0.0))
            j_ref[...] = jnp.where(n_lt < need, cand, j_ref[...])
            return carry

        lax.fori_loop(0, seq_bits, jbit_body, 0)

    jmax_t = _tile_lanes(j_ref[...], rep)

    def mask_body(kt, carry):
        k = key_ref[kt]
        s_idx = kt * tk + col
        keep_tie = jnp.where(k == thr_t, jnp.where(s_idx <= jmax_t, 0.0, MASK_NEG), MASK_NEG)
        keep = jnp.where(k > thr_t, 0.0, keep_tie)
        mask_ref[0, 0, kt] = jnp.where(s_idx <= row, keep, MASK_NEG).astype(BF16)
        return carry

    lax.fori_loop(0, nkt, mask_body, 0)

    def fill_body(kt, carry):
        mask_ref[0, 0, kt] = jnp.full((tq, tk), MASK_NEG, BF16)
        return carry

    lax.fori_loop(nkt, nkt_total, fill_body, 0)


def _dsa_index(pbig, col_qi, kbd, widx):
    b, s, _ = pbig.shape
    tq = tk = ATT_T
    nkt = s // tk
    wq = N_IDX_HEADS * IDX_DIM
    topk = min(TOPK_MAX, s // 4)
    kern = functools.partial(_dsa_index_kernel, topk=topk, nkt_total=nkt,
                             seq_bits=max(1, (s - 1).bit_length()))
    return pl.pallas_call(
        kern,
        grid=(b, s // tq),
        in_specs=[pl.BlockSpec((1, tq, wq), lambda bi, i: (bi, i, col_qi)),
                  pl.BlockSpec((1, tq, N_IDX_HEADS), lambda bi, i: (bi, i, 0)),
                  pl.BlockSpec((1, nkt, LANES, 2 * tk), lambda bi, i: (bi, 0, 0, 0))],
        out_specs=pl.BlockSpec((1, 1, nkt, tq, tk), lambda bi, i: (bi, i, 0, 0, 0)),
        out_shape=jax.ShapeDtypeStruct((b, s // tq, nkt, tq, tk), BF16),
        scratch_shapes=[pltpu.VMEM((nkt, tq, tk), I32),
                        pltpu.VMEM((N_IDX_HEADS, tq, LANES), F32),
                        pltpu.VMEM((tq, LANES), I32),
                        pltpu.VMEM((tq, LANES), I32)],
        compiler_params=_cparams(("parallel", "parallel")),
        name="dsa_index",
    )(pbig, widx, kbd)


def _flash_kernel(*refs, mode, nb, topb):
    if mode == "mask":
        q_ref, k_ref, v_ref, z_ref, d_ref, mask_ref, y_ref = refs[:7]
    else:
        q_ref, k_ref, v_ref, z_ref, d_ref, y_ref, km_ref = refs[:7]
    qs_ref, acc_ref, m_ref, p_ref, alpha_ref = refs[7:]
    tq = ATT_T
    hb = q_ref.shape[2] // HEAD_DIM
    rep = tq // LANES
    i = pl.program_id(2)
    lane = lax.broadcasted_iota(I32, (tq, LANES), 1)
    zero_blk = jnp.zeros((LANES, LANES), F32)
    ones_blk = jnp.ones((tq, LANES), BF16)

    if mode == "sel":
        @pl.when(i == 0)
        def _():
            km_ref[...] = jnp.zeros(km_ref.shape, F32)
            for h in range(hb):
                for n in range(nb):
                    kblk = k_ref[0, n * tq:(n + 1) * tq, h * HEAD_DIM:(h + 1) * HEAD_DIM].astype(F32)
                    km_ref[h, n:n + 1, :] = jnp.mean(kblk, axis=0, keepdims=True)
        blk = lane.astype(F32)

    for h in range(hb):
        q = q_ref[0, :, h * HEAD_DIM:(h + 1) * HEAD_DIM]
        qsc = (q.astype(F32) * (ATTN_SCALE * LOG2E)).astype(BF16)
        if mode == "sel":
            gate = lax.dot_general(q, km_ref[h].astype(BF16), (((1,), (1,)), ((), ())),
                                   preferred_element_type=F32)
            gate = jnp.where(blk < i.astype(F32), gate, -jnp.inf)
            selb = jnp.full((tq, LANES), MASK_NEG, F32)
            for _ in range(topb):
                gmax = jnp.max(gate, axis=1, keepdims=True)
                first = jnp.min(jnp.where(gate == gmax, blk, float(LANES)), axis=1, keepdims=True)
                hit = jnp.where(blk == first, jnp.where(gmax > -jnp.inf, 1.0, 0.0), 0.0)
                selb = jnp.where(hit > 0.0, 0.0, selb)
                gate = jnp.where(blk == first, -jnp.inf, gate)
            qs_ref[h] = jnp.concatenate([qsc, selb.astype(BF16)], axis=1)
        else:
            qs_ref[h] = qsc
        m_ref[h] = jnp.full((tq, LANES), MASK_NEG, F32)
        acc_ref[h] = jnp.zeros((tq, 2 * HEAD_DIM), F32)
        p_ref[h] = jnp.zeros((tq, tq), BF16)
        alpha_ref[h] = jnp.ones((tq, LANES), F32)

    def attend(n, kind):
        r0 = pl.multiple_of(n * tq, tq)
        if mode == "mask":
            mask_tile = mask_ref[0, 0, n].astype(F32)
        else:
            sel_lane = n if kind != "own" else -1
            onehot = jnp.where(lane == sel_lane, 1.0, 0.0).astype(BF16)
            row = lax.broadcasted_iota(I32, (tq, tq), 0)
            col = lax.broadcasted_iota(I32, (tq, tq), 1)
        for h in range(hb):
            kblk = k_ref[0, pl.ds(r0, tq), h * HEAD_DIM:(h + 1) * HEAD_DIM]
            if mode == "sel":
                kblk = jnp.concatenate([kblk, onehot], axis=1)
            x = lax.dot_general(qs_ref[h], kblk, (((1,), (1,)), ((), ())),
                                preferred_element_type=F32)
            if mode == "mask":
                x = x + mask_tile
            if kind == "near":
                x = x + jnp.concatenate([jnp.concatenate([zero_blk, d_ref[1, h]], axis=1),
                                         jnp.concatenate([zero_blk, zero_blk], axis=1)], axis=0)
            elif kind == "own":
                x = x + jnp.concatenate([jnp.concatenate([d_ref[0, h], zero_blk], axis=1),
                                         jnp.concatenate([d_ref[1, h], d_ref[0, h]], axis=1)], axis=0)
                if mode == "sel":
                    x = jnp.where(col <= row, x, MASK_NEG)
            m_prev = m_ref[h]
            m_new = jnp.maximum(m_prev, jnp.max(x, axis=1, keepdims=True))
            alpha_ref[h] = jnp.exp2(m_prev - m_new)
            p_ref[h] = jnp.exp2(x - _tile_lanes(m_new, rep)).astype(BF16)
            m_ref[h] = m_new

    def accumulate(n):
        r0 = pl.multiple_of(n * tq, tq)
        for h in range(hb):
            vblk = v_ref[0, pl.ds(r0, tq), h * HEAD_DIM:(h + 1) * HEAD_DIM]
            pv = jnp.dot(p_ref[h], jnp.concatenate([vblk, ones_blk], axis=1),
                         preferred_element_type=F32)
            acc_ref[h] = acc_ref[h] * _tile_lanes(alpha_ref[h], 2) + pv

    def far_body(n, carry):
        accumulate(jnp.maximum(n - 1, 0))
        attend(n, "far")
        return carry

    lax.fori_loop(0, jnp.maximum(i - 1, 0), far_body, 0)

    @pl.when(i >= 1)
    def _():
        accumulate(jnp.maximum(i - 2, 0))
        attend(i - 1, "near")

    accumulate(jnp.maximum(i - 1, 0))
    attend(i, "own")
    accumulate(i)

    for h in range(hb):
        acc = acc_ref[h]
        o = acc[:, :HEAD_DIM] / acc[:, HEAD_DIM:]
        zz = z_ref[0, :, h * HEAD_DIM:(h + 1) * HEAD_DIM].astype(F32)
        y_ref[0, :, h * HEAD_DIM:(h + 1) * HEAD_DIM] = (o * (zz * jax.nn.sigmoid(zz))).astype(BF16)


def _flash_attention(q, k, v, z, dtab, n_heads, name, mask=None):
    (qa, cq), (ka, ck), (va, cv), (za, cz) = q, k, v, z
    b, s, _ = qa.shape
    tq = ATT_T
    assert s % tq == 0
    nb = s // tq
    hb = ATT_HB
    wg = hb * HEAD_DIM
    mode = "sel" if mask is None else "mask"
    kern = functools.partial(_flash_kernel, mode=mode, nb=nb, topb=min(MOBA_TOPB, nb - 1))
    in_specs = [pl.BlockSpec((1, tq, wg), lambda bi, g, i: (bi, i, cq + g)),
                pl.BlockSpec((1, s, wg), lambda bi, g, i: (bi, 0, ck + g)),
                pl.BlockSpec((1, s, wg), lambda bi, g, i: (bi, 0, cv + g)),
                pl.BlockSpec((1, tq, wg), lambda bi, g, i: (bi, i, cz + g)),
                pl.BlockSpec((2, hb, LANES, LANES), lambda bi, g, i: (0, g, 0, 0))]
    args = [qa, ka, va, za, dtab]
    scratch = []
    if mode == "mask":
        in_specs.append(pl.BlockSpec((1, 1, nb, tq, tq), lambda bi, g, i: (bi, i, 0, 0, 0)))
        args.append(mask)
        kq = HEAD_DIM
    else:
        scratch.append(pltpu.VMEM((hb, LANES, HEAD_DIM), F32))
        kq = 2 * HEAD_DIM
    scratch += [pltpu.VMEM((hb, tq, kq), BF16),
                pltpu.VMEM((hb, tq, 2 * HEAD_DIM), F32),
                pltpu.VMEM((hb, tq, LANES), F32),
                pltpu.VMEM((hb, tq, tq), BF16),
                pltpu.VMEM((hb, tq, LANES), F32)]
    return pl.pallas_call(
        kern,
        grid=(b, n_heads // hb, nb),
        in_specs=in_specs,
        out_specs=pl.BlockSpec((1, tq, wg), lambda bi, g, i: (bi, i, g)),
        out_shape=jax.ShapeDtypeStruct((b, s, n_heads * HEAD_DIM), BF16),
        scratch_shapes=scratch,
        compiler_params=_cparams(("parallel", "parallel", "arbitrary")),
        name=name,
    )(*args)


def _softplus(x):
    return jnp.maximum(x, 0.0) + jnp.log1p(jnp.exp(-jnp.abs(x)))


def _rglru_kernel(xr_ref, z_ref, cw_ref, cb_ref, wa_ref, ba_ref, wx_ref, bx_ref, lam_ref, y_ref,
                  xbuf_ref, a_ref, b_ref, h_ref):
    tc = xr_ref.shape[1]
    c = pl.program_id(2)
    halo = SUBLANES

    @pl.when(c == 0)
    def _():
        xbuf_ref[0:halo, :] = jnp.zeros((halo, xbuf_ref.shape[1]), F32)
        h_ref[...] = jnp.zeros(h_ref.shape, F32)

    @pl.when(c > 0)
    def _():
        xbuf_ref[0:halo, :] = xbuf_ref[tc:tc + halo, :]

    xbuf_ref[halo:halo + tc, :] = xr_ref[0]
    base = halo - (CONV_W - 1)
    xc = cb_ref[...] + xbuf_ref[base:base + tc, :] * cw_ref[0:1, :]
    for j in range(1, CONV_W):
        xc = xc + xbuf_ref[base + j:base + j + tc, :] * cw_ref[j:j + 1, :]

    xcb = xc.astype(BF16)
    r = jax.nn.sigmoid(jnp.dot(xcb, wa_ref[0], preferred_element_type=F32) + ba_ref[...])
    ig = jax.nn.sigmoid(jnp.dot(xcb, wx_ref[0], preferred_element_type=F32) + bx_ref[...])
    log_a = (-RG_C * r) * _softplus(-lam_ref[...])
    a = jnp.exp(log_a)
    a_ref[...] = a
    b_ref[...] = jnp.sqrt(-jnp.tanh(log_a) * (a * a + 1.0)) * (ig * xc)

    sub = lax.broadcasted_iota(I32, (SUBLANES, a_ref.shape[1]), 0)

    def group(gi, h):
        r0 = pl.multiple_of(gi * SUBLANES, SUBLANES)
        a8 = a_ref[pl.ds(r0, SUBLANES), :]
        b8 = b_ref[pl.ds(r0, SUBLANES), :]
        for d in (1, 2, 4):
            a_sh = jnp.where(sub >= d, pltpu.roll(a8, d, 0), 1.0)
            b_sh = jnp.where(sub >= d, pltpu.roll(b8, d, 0), 0.0)
            b8 = a8 * b_sh + b8
            a8 = a8 * a_sh
        h8 = a8 * h + b8
        b_ref[pl.ds(r0, SUBLANES), :] = h8
        return h8[SUBLANES - 1:SUBLANES, :]

    h_ref[...] = lax.fori_loop(0, tc // SUBLANES, group, h_ref[...])

    zz = z_ref[0].astype(F32)
    y_ref[0] = (b_ref[...] * (zz * jax.nn.sigmoid(zz))).astype(BF16)


def _rglru(xr, z, conv_w, conv_b, wa_g, b_a, wx_g, b_x, lam):
    b, s, d = xr.shape
    g = wa_g.shape[0]
    cg = d // g
    tc = min(RG_TC, s)
    row = lambda v: v.reshape(1, d).astype(F32)
    vec_spec = pl.BlockSpec((1, cg), lambda bi, gi, c: (0, gi))
    return pl.pallas_call(
        _rglru_kernel,
        grid=(b, g, s // tc),
        in_specs=[pl.BlockSpec((1, tc, cg), lambda bi, gi, c: (bi, c, gi)),
                  pl.BlockSpec((1, tc, cg), lambda bi, gi, c: (bi, c, gi)),
                  pl.BlockSpec((CONV_W, cg), lambda bi, gi, c: (0, gi)),
                  vec_spec,
                  pl.BlockSpec((1, cg, cg), lambda bi, gi, c: (gi, 0, 0)),
                  vec_spec,
                  pl.BlockSpec((1, cg, cg), lambda bi, gi, c: (gi, 0, 0)),
                  vec_spec,
                  vec_spec],
        out_specs=pl.BlockSpec((1, tc, cg), lambda bi, gi, c: (bi, c, gi)),
        out_shape=jax.ShapeDtypeStruct((b, s, d), BF16),
        scratch_shapes=[pltpu.VMEM((tc + SUBLANES, cg), F32),
                        pltpu.VMEM((tc, cg), F32),
                        pltpu.VMEM((tc, cg), F32),
                        pltpu.VMEM((1, cg), F32)],
        compiler_params=_cparams(("parallel", "parallel", "arbitrary")),
        name="rglru",
    )(xr, z, conv_w.astype(F32), row(conv_b), wa_g, row(b_a), wx_g, row(b_x), row(lam))


def _group_block_diag(w, pair):
    nb, k, _ = w.shape
    wg = w.reshape(nb // pair, pair, k, k)
    eye = jnp.eye(pair, dtype=w.dtype)
    out = jnp.einsum('gpkj,pq->gpkqj', wg, eye)
    return out.reshape(nb // pair, pair * k, pair * k)


def _attention_layer(x2d, b, s, norm_g, w_in, kv_g, w_uk, w_uv, idx_g, idx_b, w_out, rel_bias):
    d_model = x2d.shape[1]
    n_heads = w_uk.shape[0]
    wq = n_heads * HEAD_DIM
    assert N_IDX_HEADS * IDX_DIM == wq
    sizes = (wq, KV_LORA, wq, IDX_DIM, N_IDX_HEADS, wq, wq, wq, 2 * wq)
    offs = np.concatenate([[0], np.cumsum(sizes)])
    seg = lambda k0, k1: w_in[:, offs[k0]:offs[k1]]
    w_big = jnp.concatenate([seg(0, 1), seg(2, 3), seg(5, 9)], axis=1).astype(BF16)
    zpad = lambda n: jnp.zeros((d_model, n), w_in.dtype)
    w_small = jnp.concatenate([seg(1, 2), seg(3, 4), zpad(LANES - IDX_DIM), seg(4, 5),
                               zpad(LANES - N_IDX_HEADS)], axis=1).astype(BF16)

    h = _rmsnorm(x2d, norm_g, BF16, "rmsnorm0")
    pbig = _matmul([h], [w_big], BF16, "proj_attn_big").reshape(b, s, -1)
    psmall = _matmul([h], [w_small], F32, "proj_attn_small").reshape(b, s, -1)

    ckv, kbd, widx = _dsa_prep(psmall, kv_g, idx_g, idx_b)
    mask = _dsa_index(pbig, 1, kbd, widx)
    w_kv = jnp.concatenate([jnp.transpose(w_uk, (1, 0, 2)).reshape(KV_LORA, wq),
                            jnp.transpose(w_uv, (1, 0, 2)).reshape(KV_LORA, wq)], axis=1).astype(BF16)
    kv = _matmul([ckv.reshape(b * s, KV_LORA)], [w_kv], BF16, "dsa_kv_up").reshape(b, s, 2 * wq)

    dtab_a = _bias_blocks(rel_bias[:, :n_heads])
    dtab_b = _bias_blocks(rel_bias[:, n_heads:])
    gpw = n_heads // ATT_HB
    ya = _flash_attention((pbig, 0), (kv, 0), (kv, gpw), (pbig, 5 * gpw), dtab_a, n_heads,
                          "dsa_attention", mask=mask)
    yb = _flash_attention((pbig, 2 * gpw), (pbig, 3 * gpw), (pbig, 4 * gpw), (pbig, 6 * gpw), dtab_b,
                          n_heads, "moba_attention")

    w_out_b = w_out.astype(BF16)
    return _matmul([ya.reshape(b * s, wq), yb.reshape(b * s, wq)], [w_out_b[:wq], w_out_b[wq:]],
                   F32, "proj_attn_out", res=x2d, bn=MM_BN // 2)


def _recurrent_layer(x2d, b, s, norm_g, w_in, conv_w, conv_b, w_a, b_a, w_x, b_x, lam, w_out):
    d_rnn = conv_b.shape[0]
    h = _rmsnorm(x2d, norm_g, BF16, "rmsnorm1")
    w_in_b = w_in.astype(BF16)
    xr = _matmul([h], [w_in_b[:, :d_rnn]], F32, "proj_rec_x")
    z = _matmul([h], [w_in_b[:, d_rnn:]], BF16, "proj_rec_z")
    y = _rglru(xr.reshape(b, s, d_rnn), z.reshape(b, s, d_rnn), conv_w, conv_b,
               _group_block_diag(w_a, RG_PAIR).astype(BF16), b_a,
               _group_block_diag(w_x, RG_PAIR).astype(BF16), b_x, lam)
    return _matmul([y.reshape(b * s, d_rnn)], [w_out.astype(BF16)], F32, "proj_rec_out",
                   res=x2d, bn=MM_BN // 2)


def kernel(x, norm_g, final_g, rel_bias, attn_w_in, attn_kv_g, attn_w_uk, attn_w_uv, idx_k_g, idx_k_b,
           attn_w_out, rec_w_in, rec_conv_w, rec_conv_b, rec_w_a, rec_b_a, rec_w_x, rec_b_x,
           rec_lambda, rec_w_out):
    b, s, d = x.shape
    depth = norm_g.shape[0]
    x2d = x.reshape(b * s, d)
    for layer in range(depth):
        li = layer // 2
        if layer % 2 == 0:
            x2d = _attention_layer(x2d, b, s, norm_g[layer], attn_w_in[li], attn_kv_g[li],
                                   attn_w_uk[li], attn_w_uv[li], idx_k_g[li], idx_k_b[li],
                                   attn_w_out[li], rel_bias)
        else:
            x2d = _recurrent_layer(x2d, b, s, norm_g[layer], rec_w_in[li], rec_conv_w[li],
                                   rec_conv_b[li], rec_w_a[li], rec_b_a[li], rec_w_x[li],
                                   rec_b_x[li], rec_lambda[li], rec_w_out[li])
    return _rmsnorm(x2d, final_g, x.dtype, "rmsnorm_final").reshape(b, s, d)
```

```python
import functools
import math

import numpy as np
import jax
import jax.numpy as jnp
from jax import lax
from jax.experimental import pallas as pl
from jax.experimental.pallas import tpu as pltpu

F32 = jnp.float32
BF16 = jnp.bfloat16
I32 = jnp.int32
I16 = jnp.int16

HEAD_DIM = 128
KV_LORA = 512
N_IDX_HEADS = 32
IDX_DIM = 64
TOPK_MAX = 256
MOBA_BLOCK = 256
MOBA_TOPB = 3
RG_BLOCKS = 16
CONV_W = 4
RG_C = 8.0
NUM_BUCKETS = 32
MAX_DISTANCE = 128
EPS = 1e-6
ATTN_SCALE = HEAD_DIM ** -0.5
IDX_SCALE = (N_IDX_HEADS ** -0.5) * (IDX_DIM ** -0.5)
LOG2E = math.log2(math.e)

LANES = 128
SUBLANES = 8
VMEM_LIMIT_BYTES = 56 * 2 ** 20
MASK_NEG = -1e30
INT_MIN = np.int32(-2 ** 31)
I16_MIN = -2 ** 15

ROW_TILE = 256
MM_BM = 1024
MM_BN = 1024
ATT_T = 256
ATT_HB = 4
IDX_RB = 128
RG_PAIR = 2
RG_TC = 512
RG_SEG_PAD = 8
RG_SCAN_UNROLL = 4
assert ATT_T == MOBA_BLOCK and ATT_T == 2 * LANES


def _cparams(sem):
    return pltpu.CompilerParams(dimension_semantics=sem, vmem_limit_bytes=VMEM_LIMIT_BYTES)


def _sigmoid(x):
    return 0.5 * jnp.tanh(0.5 * x) + 0.5


def _tile_lanes(x, n):
    return x if n == 1 else jnp.concatenate([x] * n, axis=1)


def _rmsnorm_kernel(x_ref, g_ref, o_ref):
    x = x_ref[...]
    ms = jnp.mean(x * x, axis=-1, keepdims=True)
    o_ref[...] = ((x * lax.rsqrt(ms + EPS)) * g_ref[...]).astype(o_ref.dtype)


def _rmsnorm(x2d, g, out_dtype, name):
    m, d = x2d.shape
    bm = min(ROW_TILE, m)
    return pl.pallas_call(
        _rmsnorm_kernel,
        grid=(m // bm,),
        in_specs=[pl.BlockSpec((bm, d), lambda i: (i, 0)),
                  pl.BlockSpec((1, d), lambda i: (0, 0))],
        out_specs=pl.BlockSpec((bm, d), lambda i: (i, 0)),
        out_shape=jax.ShapeDtypeStruct((m, d), out_dtype),
        compiler_params=_cparams(("parallel",)),
        name=name,
    )(x2d, g.reshape(1, d).astype(F32))


def _mm_kernel(*refs, n_parts, has_res):
    o_ref = refs[-1]
    acc = jnp.dot(refs[0][...], refs[n_parts][...], preferred_element_type=F32)
    for p in range(1, n_parts):
        acc = acc + jnp.dot(refs[p][...], refs[n_parts + p][...], preferred_element_type=F32)
    if has_res:
        acc = refs[2 * n_parts][...] + acc
    o_ref[...] = acc.astype(o_ref.dtype)


def _matmul(xs, ws, out_dtype, name, res=None, n=None, bm=MM_BM, bn=MM_BN):
    ws = [w if isinstance(w, tuple) else (w, 0, 0) for w in ws]
    m = xs[0].shape[0]
    n = ws[0][0].shape[1] if n is None else n
    bm = min(bm, m)
    bn = min(bn, n)
    assert m % bm == 0 and n % bn == 0
    in_specs = [pl.BlockSpec((bm, x.shape[1]), lambda i, j: (i, 0)) for x in xs]
    for x, (w, rb, cb) in zip(xs, ws):
        assert w.shape[0] % x.shape[1] == 0 and (cb + n // bn) * bn <= w.shape[1]
        in_specs.append(pl.BlockSpec((x.shape[1], bn), lambda i, j, rb=rb, cb=cb: (rb, cb + j)))
    args = [*xs, *[w for w, _, _ in ws]]
    if res is not None:
        in_specs.append(pl.BlockSpec((bm, bn), lambda i, j: (i, j)))
        args.append(res)
    return pl.pallas_call(
        functools.partial(_mm_kernel, n_parts=len(xs), has_res=res is not None),
        grid=(m // bm, n // bn),
        in_specs=in_specs,
        out_specs=pl.BlockSpec((bm, bn), lambda i, j: (i, j)),
        out_shape=jax.ShapeDtypeStruct((m, n), out_dtype),
        compiler_params=_cparams(("parallel", "parallel")),
        name=name,
    )(*args)


def _t5_bucket_np(dist):
    d = np.maximum(dist, 0)
    max_exact = NUM_BUCKETS // 2
    d_f = np.maximum(d, 1).astype(np.float32)
    ratio = np.log(d_f / np.float32(max_exact)) / np.float32(math.log(MAX_DISTANCE / max_exact))
    large = max_exact + (ratio * np.float32(NUM_BUCKETS - max_exact)).astype(np.int32)
    large = np.minimum(large, NUM_BUCKETS - 1)
    return np.where(d < max_exact, d, large)


def _bias_blocks(tab):
    n = LANES
    assert np.all(_t5_bucket_np(np.arange(n, 64 * n)) == NUM_BUCKETS - 1)
    h = tab.shape[1]
    rel = (tab - tab[NUM_BUCKETS - 1]).astype(F32) * LOG2E
    out = []
    for off in (0, n):
        buckets = _t5_bucket_np(off + n - 1 - np.arange(2 * n - 1))
        onehot = np.zeros((2 * n, NUM_BUCKETS), np.float32)
        onehot[np.arange(2 * n - 1), buckets] = 1.0
        w = jnp.dot(jnp.asarray(onehot), rel, precision=lax.Precision.HIGHEST).T
        t = jnp.tile(w, (1, n))[:, :n * (2 * n - 1)].reshape(h, n, 2 * n - 1)
        out.append(t[:, :, n - 1:])
    return jnp.stack(out)


def _dsa_prep_kernel(p_ref, kvg_ref, ig_ref, ib_ref, ckv_ref, kbd_ref, w_ref):
    p = p_ref[0]
    ts = p.shape[0]
    c = p[:, :KV_LORA]
    cn = (c * lax.rsqrt(jnp.mean(c * c, axis=-1, keepdims=True) + EPS)) * kvg_ref[...]
    ckv_ref[0] = cn.astype(BF16)
    k = p[:, KV_LORA:KV_LORA + IDX_DIM]
    mu = jnp.mean(k, axis=-1, keepdims=True)
    var = jnp.mean(jnp.square(k - mu), axis=-1, keepdims=True)
    kn = ((k - mu) * lax.rsqrt(var + EPS)) * ig_ref[...] + ib_ref[...]
    zeros = jnp.zeros((ts, LANES - IDX_DIM), F32)
    top = jnp.concatenate([kn, zeros], axis=1).T
    bot = jnp.concatenate([zeros, kn], axis=1).T
    kbd_ref[0, 0] = jnp.concatenate([top, bot], axis=1).astype(BF16)
    w_ref[0] = p[:, KV_LORA + LANES:KV_LORA + LANES + N_IDX_HEADS] * IDX_SCALE


def _dsa_prep(ps, kv_g, idx_g, idx_b):
    b, s, wid = ps.shape
    ts = ATT_T
    nkt = s // ts
    return pl.pallas_call(
        _dsa_prep_kernel,
        grid=(b, nkt),
        in_specs=[pl.BlockSpec((1, ts, wid), lambda bi, i: (bi, i, 0)),
                  pl.BlockSpec((1, KV_LORA), lambda bi, i: (0, 0)),
                  pl.BlockSpec((1, IDX_DIM), lambda bi, i: (0, 0)),
                  pl.BlockSpec((1, IDX_DIM), lambda bi, i: (0, 0))],
        out_specs=[pl.BlockSpec((1, ts, KV_LORA), lambda bi, i: (bi, i, 0)),
                   pl.BlockSpec((1, 1, LANES, 2 * ts), lambda bi, i: (bi, i, 0, 0)),
                   pl.BlockSpec((1, ts, N_IDX_HEADS), lambda bi, i: (bi, i, 0))],
        out_shape=[jax.ShapeDtypeStruct((b, s, KV_LORA), BF16),
                   jax.ShapeDtypeStruct((b, nkt, LANES, 2 * ts), BF16),
                   jax.ShapeDtypeStruct((b, s, N_IDX_HEADS), F32)],
        compiler_params=_cparams(("parallel", "parallel")),
        name="dsa_prep",
    )(ps, kv_g.reshape(1, -1).astype(F32), idx_g.reshape(1, -1).astype(F32),
      idx_b.reshape(1, -1).astype(F32))


def _dsa_index_kernel(qi_ref, w_ref, kbd_ref, mask_ref, hi_ref, lo_ref, wb_ref, u_ref, j_ref,
                      *, topk, nkt_total, seq_bits):
    tq = tk = ATT_T
    rep = tk // LANES
    rb = IDX_RB
    i = pl.program_id(1)
    t0 = i * tq
    nkt = i + 1

    row = t0 + lax.broadcasted_iota(I32, (tq, tk), 0)
    col = lax.broadcasted_iota(I32, (tq, tk), 1)
    col_rb = lax.broadcasted_iota(I32, (rb, tk), 1)
    one, zero = jnp.int16(1), jnp.int16(0)

    for h in range(N_IDX_HEADS):
        wb_ref[h] = jnp.broadcast_to(w_ref[0, :, h:h + 1], (tq, LANES))

    def idx_body(kt, carry):
        kb = kbd_ref[0, kt]
        acc = jnp.zeros((tq, tk), F32)
        for hp in range(N_IDX_HEADS // 2):
            rel = jnp.dot(qi_ref[0, :, hp * LANES:(hp + 1) * LANES], kb,
                          preferred_element_type=F32)
            acc = acc + jnp.maximum(rel[:, :tk], 0.0) * _tile_lanes(wb_ref[2 * hp], rep)
            acc = acc + jnp.maximum(rel[:, tk:], 0.0) * _tile_lanes(wb_ref[2 * hp + 1], rep)
        score = jnp.where(acc == 0.0, 0.0, acc)
        bits = lax.bitcast_convert_type(score, I32)
        key = bits ^ ((bits >> 31) & np.int32(0x7FFFFFFF))
        key = jnp.where(kt * tk + col <= row, key, INT_MIN)
        hi_ref[kt] = (key >> 16).astype(I16)
        lo_ref[kt] = ((key & np.int32(0xFFFF)) + I16_MIN).astype(I16)
        return carry

    lax.fori_loop(0, nkt, idx_body, 0)

    def count(pred_fn):
        parts = []
        for r0 in range(0, tq, rb):
            def body(kt, c, r0=r0):
                x = pred_fn(kt, r0)
                s = x[:, :LANES]
                for r in range(1, rep):
                    s = s + x[:, r * LANES:(r + 1) * LANES]
                return c + s
            c = lax.fori_loop(0, nkt, body, jnp.zeros((rb, LANES), I16))
            parts.append(jnp.broadcast_to(jnp.sum(c.astype(F32), axis=1, keepdims=True), (rb, LANES)))
        return jnp.concatenate(parts, axis=0)

    def tile16(v, r0=None):
        v = v if r0 is None else v[r0:r0 + rb]
        return _tile_lanes(v.astype(I16), rep)

    def search16(ref, target):
        u_ref[...] = jnp.zeros((tq, LANES), I32)

        def bit_body(it, carry):
            cand_u = u_ref[...] | jnp.left_shift(jnp.int32(1), 15 - it)
            cand = cand_u + I16_MIN
            n_ge = count(lambda kt, r0: jnp.where(ref[kt, r0:r0 + rb, :] >= tile16(cand, r0), one, zero))
            u_ref[...] = jnp.where(n_ge >= target, cand_u, u_ref[...])
            return carry

        lax.fori_loop(0, 16, bit_body, 0)
        return u_ref[...] + I16_MIN

    t_hi = search16(hi_ref, topk)
    thi_t = tile16(t_hi)
    n_hi_gt = count(lambda kt, r0: jnp.where(hi_ref[kt, r0:r0 + rb, :] > thi_t[r0:r0 + rb], one, zero))

    def lo_body(kt, carry):
        lo_ref[kt] = jnp.where(hi_ref[kt] == thi_t, lo_ref[kt], jnp.int16(I16_MIN))
        return carry

    lax.fori_loop(0, nkt, lo_body, 0)
    t_lo = search16(lo_ref, topk - n_hi_gt)
    tlo_t = tile16(t_lo)

    def at_thr(kt, r0, inner_fn):
        rows = slice(r0, r0 + rb)
        return jnp.where(hi_ref[kt, rows, :] == thi_t[rows], inner_fn(lo_ref[kt, rows, :], tlo_t[rows]), zero)

    n_gt = n_hi_gt + count(lambda kt, r0: at_thr(kt, r0, lambda lo, t: jnp.where(lo > t, one, zero)))
    n_ge = n_hi_gt + count(lambda kt, r0: at_thr(kt, r0, lambda lo, t: jnp.where(lo >= t, one, zero)))
    need = topk - n_gt
    j_ref[...] = jnp.full((tq, LANES), 2 ** 15 - 1, I32)

    @pl.when(jnp.max(n_ge) > topk)
    def _():
        j_ref[...] = jnp.zeros((tq, LANES), I32)

        def jbit_body(it, carry):
            cand = j_ref[...] | jnp.left_shift(jnp.int32(1), seq_bits - 1 - it)
            n_lt = count(lambda kt, r0: at_thr(kt, r0, lambda lo, t: jnp.where(
                lo == t, jnp.where((kt * tk + col_rb).astype(I16) < tile16(cand, r0), one, zero), zero)))
            j_ref[...] = jnp.where(n_lt < need, cand, j_ref[...])
            return carry

        lax.fori_loop(0, seq_bits, jbit_body, 0)

    jmax_t = tile16(j_ref[...])
    row16 = row.astype(I16)
    keep_b = jnp.zeros((tq, tk), BF16)
    drop_b = jnp.full((tq, tk), MASK_NEG, BF16)

    def mask_body(kt, carry):
        hi, lo = hi_ref[kt], lo_ref[kt]
        s16 = (kt * tk + col).astype(I16)
        keep = jnp.where(lo == tlo_t, jnp.where(s16 <= jmax_t, keep_b, drop_b), drop_b)
        keep = jnp.where(lo > tlo_t, keep_b, keep)
        keep = jnp.where(hi == thi_t, keep, drop_b)
        keep = jnp.where(hi > thi_t, keep_b, keep)
        mask_ref[0, 0, kt] = jnp.where(s16 <= row16, keep, drop_b)
        return carry

    lax.fori_loop(0, nkt, mask_body, 0)

    def fill_body(kt, carry):
        mask_ref[0, 0, kt] = drop_b
        return carry

    lax.fori_loop(nkt, nkt_total, fill_body, 0)


def _dsa_index(pqi, col_qi, kbd, widx):
    b, s, _ = pqi.shape
    tq = tk = ATT_T
    nkt = s // tk
    wq = N_IDX_HEADS * IDX_DIM
    topk = min(TOPK_MAX, s // 4)
    assert s < 2 ** 15
    kern = functools.partial(_dsa_index_kernel, topk=topk, nkt_total=nkt,
                             seq_bits=max(1, (s - 1).bit_length()))
    return pl.pallas_call(
        kern,
        grid=(b, s // tq),
        in_specs=[pl.BlockSpec((1, tq, wq), lambda bi, i: (bi, i, col_qi)),
                  pl.BlockSpec((1, tq, N_IDX_HEADS), lambda bi, i: (bi, i, 0)),
                  pl.BlockSpec((1, nkt, LANES, 2 * tk), lambda bi, i: (bi, 0, 0, 0))],
        out_specs=pl.BlockSpec((1, 1, nkt, tq, tk), lambda bi, i: (bi, i, 0, 0, 0)),
        out_shape=jax.ShapeDtypeStruct((b, s // tq, nkt, tq, tk), BF16),
        scratch_shapes=[pltpu.VMEM((nkt, tq, tk), I16),
                        pltpu.VMEM((nkt, tq, tk), I16),
                        pltpu.VMEM((N_IDX_HEADS, tq, LANES), F32),
                        pltpu.VMEM((tq, LANES), I32),
                        pltpu.VMEM((tq, LANES), I32)],
        compiler_params=_cparams(("parallel", "parallel")),
        name="dsa_index",
    )(pqi, widx, kbd)


def _flash_kernel(*refs, mode, nb, topb):
    if mode == "mask":
        q_ref, k_ref, v_ref, z_ref, d_ref, mask_ref, y_ref = refs[:7]
    else:
        q_ref, k_ref, v_ref, z_ref, d_ref, y_ref, km_ref = refs[:7]
    qs_ref, acc_ref, m_ref, p_ref, alpha_ref, x_ref = refs[7:]
    tq = ATT_T
    hb = q_ref.shape[2] // HEAD_DIM
    rep = tq // LANES
    i = pl.program_id(2)
    lane = lax.broadcasted_iota(I32, (tq, LANES), 1)
    zero_blk = jnp.zeros((LANES, LANES), F32)
    ones_blk = jnp.ones((tq, LANES), BF16)

    if mode == "sel":
        @pl.when(i == 0)
        def _():
            km_ref[...] = jnp.zeros(km_ref.shape, F32)
            for h in range(hb):
                for n in range(nb):
                    kblk = k_ref[0, n * tq:(n + 1) * tq, h * HEAD_DIM:(h + 1) * HEAD_DIM].astype(F32)
                    km_ref[h, n:n + 1, :] = jnp.mean(kblk, axis=0, keepdims=True)
        nbp = -(-nb // SUBLANES) * SUBLANES
        blk = lax.broadcasted_iota(I32, (nbp, tq), 0).astype(F32)

    for h in range(hb):
        q = q_ref[0, :, h * HEAD_DIM:(h + 1) * HEAD_DIM]
        qsc = (q.astype(F32) * (ATTN_SCALE * LOG2E)).astype(BF16)
        if mode == "sel":
            gate = lax.dot_general(km_ref[h, 0:nbp, :].astype(BF16), q, (((1,), (1,)), ((), ())),
                                   preferred_element_type=F32)
            gate = jnp.where(blk < i.astype(F32), gate, -jnp.inf)
            sel = jnp.full((nbp, tq), MASK_NEG, F32)
            for _ in range(topb):
                gmax = jnp.max(gate, axis=0, keepdims=True)
                first = jnp.min(jnp.where(gate == gmax, blk, float(nbp)), axis=0, keepdims=True)
                hit = jnp.where(blk == first, jnp.where(gmax > -jnp.inf, 1.0, 0.0), 0.0)
                sel = jnp.where(hit > 0.0, 0.0, sel)
                gate = jnp.where(blk == first, -jnp.inf, gate)
            selb = jnp.concatenate([sel, jnp.full((LANES - nbp, tq), MASK_NEG, F32)], axis=0).T
            qs_ref[h] = jnp.concatenate([qsc, selb.astype(BF16)], axis=1)
        else:
            qs_ref[h] = qsc
        m_ref[h] = jnp.full((tq, LANES), MASK_NEG, F32)
        acc_ref[h] = jnp.zeros((tq, 2 * HEAD_DIM), F32)
        p_ref[h] = jnp.zeros((tq, tq), BF16)
        alpha_ref[h] = jnp.ones((tq, LANES), F32)

    def qk_stage(n, is_own):
        r0 = pl.multiple_of(n * tq, tq)
        if mode == "sel":
            sel_lane = jnp.where(is_own, -1, n)
            onehot = jnp.where(lane == sel_lane, 1.0, 0.0).astype(BF16)
        for h in range(hb):
            kblk = k_ref[0, pl.ds(r0, tq), h * HEAD_DIM:(h + 1) * HEAD_DIM]
            if mode == "sel":
                kblk = jnp.concatenate([kblk, onehot], axis=1)
            x_ref[h] = lax.dot_general(qs_ref[h], kblk, (((1,), (1,)), ((), ())),
                                       preferred_element_type=F32)

    def softmax_stage(n, kind):
        if mode == "mask":
            mask_tile = mask_ref[0, 0, n].astype(F32)
        elif kind == "own":
            row = lax.broadcasted_iota(I32, (tq, tq), 0)
            col = lax.broadcasted_iota(I32, (tq, tq), 1)
        for h in range(hb):
            x = x_ref[h]
            if mode == "mask":
                x = x + mask_tile
            if kind == "near":
                x = x + jnp.concatenate([jnp.concatenate([zero_blk, d_ref[1, h]], axis=1),
                                         jnp.concatenate([zero_blk, zero_blk], axis=1)], axis=0)
            elif kind == "own":
                x = x + jnp.concatenate([jnp.concatenate([d_ref[0, h], zero_blk], axis=1),
                                         jnp.concatenate([d_ref[1, h], d_ref[0, h]], axis=1)], axis=0)
                if mode == "sel":
                    x = jnp.where(col <= row, x, MASK_NEG)
            m_prev = m_ref[h]
            m_new = jnp.maximum(m_prev, jnp.max(x, axis=1, keepdims=True))
            alpha_ref[h] = jnp.exp2(m_prev - m_new)
            p_ref[h] = jnp.exp2(x - _tile_lanes(m_new, rep)).astype(BF16)
            m_ref[h] = m_new

    def pv_stage(n):
        r0 = pl.multiple_of(n * tq, tq)
        for h in range(hb):
            vblk = v_ref[0, pl.ds(r0, tq), h * HEAD_DIM:(h + 1) * HEAD_DIM]
            pv = jnp.dot(p_ref[h], jnp.concatenate([vblk, ones_blk], axis=1),
                         preferred_element_type=F32)
            acc_ref[h] = acc_ref[h] * _tile_lanes(alpha_ref[h], 2) + pv

    def step(n, kind):
        pv_stage(jnp.maximum(n - 1, 0))
        softmax_stage(n, kind)
        if kind != "own":
            qk_stage(n + 1, n + 1 == i)

    qk_stage(0, i == 0)

    def far_body(n, carry):
        step(n, "far")
        return carry

    lax.fori_loop(0, jnp.maximum(i - 1, 0), far_body, 0)

    @pl.when(i >= 1)
    def _():
        step(i - 1, "near")

    step(i, "own")
    pv_stage(i)

    for h in range(hb):
        acc = acc_ref[h]
        o = acc[:, :HEAD_DIM] / acc[:, HEAD_DIM:]
        zz = z_ref[0, :, h * HEAD_DIM:(h + 1) * HEAD_DIM].astype(F32)
        y_ref[0, :, h * HEAD_DIM:(h + 1) * HEAD_DIM] = (o * (zz * _sigmoid(zz))).astype(BF16)


def _flash_attention(q, k, v, z, dtab, n_heads, name, mask=None):
    (qa, cq), (ka, ck), (va, cv), (za, cz) = q, k, v, z
    b, s, _ = qa.shape
    tq = ATT_T
    assert s % tq == 0
    nb = s // tq
    hb = ATT_HB
    wg = hb * HEAD_DIM
    mode = "sel" if mask is None else "mask"
    kern = functools.partial(_flash_kernel, mode=mode, nb=nb, topb=min(MOBA_TOPB, nb - 1))
    in_specs = [pl.BlockSpec((1, tq, wg), lambda bi, g, i: (bi, i, cq + g)),
                pl.BlockSpec((1, s, wg), lambda bi, g, i: (bi, 0, ck + g)),
                pl.BlockSpec((1, s, wg), lambda bi, g, i: (bi, 0, cv + g)),
                pl.BlockSpec((1, tq, wg), lambda bi, g, i: (bi, i, cz + g)),
                pl.BlockSpec((2, hb, LANES, LANES), lambda bi, g, i: (0, g, 0, 0))]
    args = [qa, ka, va, za, dtab]
    scratch = []
    if mode == "mask":
        in_specs.append(pl.BlockSpec((1, 1, nb, tq, tq), lambda bi, g, i: (bi, i, 0, 0, 0)))
        args.append(mask)
        kq = HEAD_DIM
    else:
        scratch.append(pltpu.VMEM((hb, LANES, HEAD_DIM), F32))
        kq = 2 * HEAD_DIM
    scratch += [pltpu.VMEM((hb, tq, kq), BF16),
                pltpu.VMEM((hb, tq, 2 * HEAD_DIM), F32),
                pltpu.VMEM((hb, tq, LANES), F32),
                pltpu.VMEM((hb, tq, tq), BF16),
                pltpu.VMEM((hb, tq, LANES), F32),
                pltpu.VMEM((hb, tq, tq), F32)]
    return pl.pallas_call(
        kern,
        grid=(b, n_heads // hb, nb),
        in_specs=in_specs,
        out_specs=pl.BlockSpec((1, tq, wg), lambda bi, g, i: (bi, i, g)),
        out_shape=jax.ShapeDtypeStruct((b, s, n_heads * HEAD_DIM), BF16),
        scratch_shapes=scratch,
        compiler_params=_cparams(("parallel", "parallel", "arbitrary")),
        name=name,
    )(*args)


def _softplus(x):
    return jnp.maximum(x, 0.0) + jnp.log1p(jnp.exp(-jnp.abs(x)))


def _rglru_kernel(xr_ref, z_ref, cw_ref, cb_ref, wa_ref, ba_ref, wx_ref, bx_ref, lam_ref, y_ref,
                  xbuf_ref, a_ref, b_ref, h_ref):
    tc = xr_ref.shape[1]
    c = pl.program_id(2)
    halo = SUBLANES

    @pl.when(c == 0)
    def _():
        xbuf_ref[0:halo, :] = jnp.zeros((halo, xbuf_ref.shape[1]), F32)
        h_ref[...] = jnp.zeros(h_ref.shape, F32)

    @pl.when(c > 0)
    def _():
        xbuf_ref[0:halo, :] = xbuf_ref[tc:tc + halo, :]

    xbuf_ref[halo:halo + tc, :] = xr_ref[0]
    xfull = xbuf_ref[...]
    u = xfull * cw_ref[0:1, :]
    for j in range(1, CONV_W):
        u = xfull * cw_ref[j:j + 1, :] + pltpu.roll(u, 1, 0)
    xc = (u + cb_ref[...])[halo:, :]

    xcb = xc.astype(BF16)
    tr = jnp.tanh(jnp.dot(xcb, wa_ref[0], preferred_element_type=F32) + ba_ref[...])
    ti = jnp.tanh(jnp.dot(xcb, wx_ref[0], preferred_element_type=F32) + bx_ref[...])
    half_c = (-0.5 * RG_C) * _softplus(-lam_ref[...])
    a = jnp.exp(tr * half_c + half_c)
    one_m_a2 = 1.0 - a * a
    mult = jnp.where(one_m_a2 > 0.0, one_m_a2 * lax.rsqrt(one_m_a2), 0.0)
    bt = (mult * xc) * (0.5 * ti + 0.5)

    nseg = SUBLANES
    seg = tc // nseg
    pitch = seg + RG_SEG_PAD
    nslab = a_ref.shape[0]
    for sl in range(nslab):
        for sg in range(nseg):
            a_ref[sl, sg * pitch:sg * pitch + seg, :] = a[sg * seg:(sg + 1) * seg, sl * LANES:(sl + 1) * LANES]
            b_ref[sl, sg * pitch:sg * pitch + seg, :] = bt[sg * seg:(sg + 1) * seg, sl * LANES:(sl + 1) * LANES]

    def seg_step(j, carry):
        out = []
        for sl in range(nslab):
            hl, pr = carry[2 * sl], carry[2 * sl + 1]
            rows = pl.ds(j, nseg, stride=pitch)
            a8 = a_ref[sl, rows, :]
            hl = a8 * hl + b_ref[sl, rows, :]
            pr = a8 * pr
            b_ref[sl, rows, :] = hl
            a_ref[sl, rows, :] = pr
            out += [hl, pr]
        return tuple(out)

    init = tuple(v for _ in range(nslab)
                 for v in (jnp.zeros((nseg, LANES), F32), jnp.ones((nseg, LANES), F32)))
    ends = lax.fori_loop(0, seg, seg_step, init, unroll=RG_SCAN_UNROLL)

    zz = z_ref[0].astype(F32)
    gate = zz * _sigmoid(zz)
    for sl in range(nslab):
        h_end, p_end = ends[2 * sl], ends[2 * sl + 1]
        h_in = h_ref[:, sl * LANES:(sl + 1) * LANES]
        for sg in range(nseg):
            rows = slice(sg * pitch, sg * pitch + seg)
            hs = b_ref[sl, rows, :] + a_ref[sl, rows, :] * h_in
            y_ref[0, sg * seg:(sg + 1) * seg, sl * LANES:(sl + 1) * LANES] = (
                hs * gate[sg * seg:(sg + 1) * seg, sl * LANES:(sl + 1) * LANES]).astype(BF16)
            h_in = p_end[sg:sg + 1, :] * h_in + h_end[sg:sg + 1, :]
        h_ref[:, sl * LANES:(sl + 1) * LANES] = h_in


def _rglru(xr, z, conv_w, conv_b, wa_g, b_a, wx_g, b_x, lam):
    b, s, d = xr.shape
    g = wa_g.shape[0]
    cg = d // g
    tc = min(RG_TC, s)
    assert ((tc // SUBLANES + RG_SEG_PAD) // SUBLANES) % 2 == 1 and cg % LANES == 0
    row = lambda v: v.reshape(1, d).astype(F32)
    vec_spec = pl.BlockSpec((1, cg), lambda bi, gi, c: (0, gi))
    return pl.pallas_call(
        _rglru_kernel,
        grid=(b, g, s // tc),
        in_specs=[pl.BlockSpec((1, tc, cg), lambda bi, gi, c: (bi, c, gi)),
                  pl.BlockSpec((1, tc, cg), lambda bi, gi, c: (bi, c, gi)),
                  pl.BlockSpec((CONV_W, cg), lambda bi, gi, c: (0, gi)),
                  vec_spec,
                  pl.BlockSpec((1, cg, cg), lambda bi, gi, c: (gi, 0, 0)),
                  vec_spec,
                  pl.BlockSpec((1, cg, cg), lambda bi, gi, c: (gi, 0, 0)),
                  vec_spec,
                  vec_spec],
        out_specs=pl.BlockSpec((1, tc, cg), lambda bi, gi, c: (bi, c, gi)),
        out_shape=jax.ShapeDtypeStruct((b, s, d), BF16),
        scratch_shapes=[pltpu.VMEM((tc + SUBLANES, cg), F32),
                        pltpu.VMEM((cg // LANES, tc + SUBLANES * RG_SEG_PAD, LANES), F32),
                        pltpu.VMEM((cg // LANES, tc + SUBLANES * RG_SEG_PAD, LANES), F32),
                        pltpu.VMEM((1, cg), F32)],
        compiler_params=_cparams(("parallel", "parallel", "arbitrary")),
        name="rglru",
    )(xr, z, conv_w.astype(F32), row(conv_b), wa_g * 0.5, row(b_a) * 0.5, wx_g * 0.5, row(b_x) * 0.5,
      row(lam))


def _group_block_diag(w, pair):
    nb, k, _ = w.shape
    out = jnp.zeros((nb // pair, pair * k, pair * k), w.dtype)
    for p in range(pair):
        out = out.at[:, p * k:(p + 1) * k, p * k:(p + 1) * k].set(w[p::pair])
    return out


def _attention_layer(x2d, b, s, norm_g, w_in, kv_g, w_uk, w_uv, idx_g, idx_b, w_out, rel_bias):
    d_model = x2d.shape[1]
    n_heads = w_uk.shape[0]
    wq = n_heads * HEAD_DIM
    assert N_IDX_HEADS * IDX_DIM == wq
    sizes = (wq, KV_LORA, wq, IDX_DIM, N_IDX_HEADS, wq, wq, wq, 2 * wq)
    offs = [int(o) for o in np.concatenate([[0], np.cumsum(sizes)])]
    half = MM_BN // 2
    assert offs[2] % half == 0
    w_head = w_in.astype(BF16)
    w_tail = w_head[:, offs[5]:]
    zpad = lambda n: jnp.zeros((d_model, n), BF16)
    w_small = jnp.concatenate([w_head[:, offs[1]:offs[2]], w_head[:, offs[3]:offs[4]],
                               zpad(LANES - IDX_DIM), w_head[:, offs[4]:offs[5]],
                               zpad(LANES - N_IDX_HEADS)], axis=1)

    h = _rmsnorm(x2d, norm_g, BF16, "rmsnorm0")
    pqa = _matmul([h], [(w_head, 0, 0)], BF16, "proj_attn_qa", n=wq).reshape(b, s, wq)
    pqi = _matmul([h], [(w_head, 0, offs[2] // half)], BF16, "proj_attn_qi", n=wq, bn=half
                  ).reshape(b, s, wq)
    ptail = _matmul([h], [w_tail], BF16, "proj_attn_tail").reshape(b, s, -1)
    psmall = _matmul([h], [w_small], F32, "proj_attn_small").reshape(b, s, -1)

    ckv, kbd, widx = _dsa_prep(psmall, kv_g, idx_g, idx_b)
    mask = _dsa_index(pqi, 0, kbd, widx)
    w_kv = jnp.concatenate([jnp.transpose(w_uk, (1, 0, 2)).reshape(KV_LORA, wq),
                            jnp.transpose(w_uv, (1, 0, 2)).reshape(KV_LORA, wq)], axis=1).astype(BF16)
    kv = _matmul([ckv.reshape(b * s, KV_LORA)], [w_kv], BF16, "dsa_kv_up").reshape(b, s, 2 * wq)

    dtab_a = _bias_blocks(rel_bias[:, :n_heads])
    dtab_b = _bias_blocks(rel_bias[:, n_heads:])
    gpw = n_heads // ATT_HB
    ya = _flash_attention((pqa, 0), (kv, 0), (kv, gpw), (ptail, 3 * gpw), dtab_a, n_heads,
                          "dsa_attention", mask=mask)
    yb = _flash_attention((ptail, 0), (ptail, gpw), (ptail, 2 * gpw), (ptail, 4 * gpw), dtab_b,
                          n_heads, "moba_attention")

    w_out_b = w_out.astype(BF16)
    return _matmul([ya.reshape(b * s, wq), yb.reshape(b * s, wq)], [(w_out_b, 0, 0), (w_out_b, 1, 0)],
                   F32, "proj_attn_out", res=x2d, bn=half)


def _recurrent_layer(x2d, b, s, norm_g, w_in, conv_w, conv_b, w_a, b_a, w_x, b_x, lam, w_out):
    d_rnn = conv_b.shape[0]
    h = _rmsnorm(x2d, norm_g, BF16, "rmsnorm1")
    w_in_b = w_in.astype(BF16)
    assert d_rnn % MM_BN == 0
    xr = _matmul([h], [(w_in_b, 0, 0)], F32, "proj_rec_x", n=d_rnn)
    z = _matmul([h], [(w_in_b, 0, d_rnn // MM_BN)], BF16, "proj_rec_z", n=d_rnn)
    y = _rglru(xr.reshape(b, s, d_rnn), z.reshape(b, s, d_rnn), conv_w, conv_b,
               _group_block_diag(w_a.astype(BF16), RG_PAIR), b_a,
               _group_block_diag(w_x.astype(BF16), RG_PAIR), b_x, lam)
    return _matmul([y.reshape(b * s, d_rnn)], [w_out.astype(BF16)], F32, "proj_rec_out",
                   res=x2d, bn=MM_BN // 2)


def kernel(x, norm_g, final_g, rel_bias, attn_w_in, attn_kv_g, attn_w_uk, attn_w_uv, idx_k_g, idx_k_b,
           attn_w_out, rec_w_in, rec_conv_w, rec_conv_b, rec_w_a, rec_b_a, rec_w_x, rec_b_x,
           rec_lambda, rec_w_out):
    b, s, d = x.shape
    depth = norm_g.shape[0]
    x2d = x.reshape(b * s, d)
    for layer in range(depth):
        li = layer // 2
        if layer % 2 == 0:
            x2d = _attention_layer(x2d, b, s, norm_g[layer], attn_w_in[li], attn_kv_g[li],
                                   attn_w_uk[li], attn_w_uv[li], idx_k_g[li], idx_k_b[li],
                                   attn_w_out[li], rel_bias)
        else:
            x2d = _recurrent_layer(x2d, b, s, norm_g[layer], rec_w_in[li], rec_conv_w[li],
                                   rec_conv_b[li], rec_w_a[li], rec_b_a[li], rec_w_x[li],
                                   rec_b_x[li], rec_lambda[li], rec_w_out[li])
    return _rmsnorm(x2d, final_g, x.dtype, "rmsnorm_final").reshape(b, s, d)
```

```python
import functools
import math

import numpy as np
import jax
import jax.numpy as jnp
from jax import lax
from jax.experimental import pallas as pl
from jax.experimental.pallas import tpu as pltpu

F32 = jnp.float32
BF16 = jnp.bfloat16
I32 = jnp.int32

HEAD_DIM = 128
KV_LORA = 512
N_IDX_HEADS = 32
IDX_DIM = 64
TOPK_MAX = 256
MOBA_BLOCK = 256
MOBA_TOPB = 3
RG_BLOCKS = 16
CONV_W = 4
RG_C = 8.0
NUM_BUCKETS = 32
MAX_DISTANCE = 128
EPS = 1e-6
ATTN_SCALE = HEAD_DIM ** -0.5
IDX_SCALE = (N_IDX_HEADS ** -0.5) * (IDX_DIM ** -0.5)
LOG2E = math.log2(math.e)

LANES = 128
SUBLANES = 8
VMEM_LIMIT_BYTES = 56 * 2 ** 20
MASK_NEG = -1e30
INT_MIN = np.int32(-2 ** 31)

ROW_TILE = 256
MM_BM = 1024
MM_BN = 1024
ATT_T = 256
ATT_HB = 4
RG_PAIR = 2
RG_TC = 512
RG_SEG_PAD = 8
RG_SCAN_UNROLL = 4
assert ATT_T == MOBA_BLOCK and ATT_T == 2 * LANES


def _cparams(sem):
    return pltpu.CompilerParams(dimension_semantics=sem, vmem_limit_bytes=VMEM_LIMIT_BYTES)


def _sigmoid(x):
    return 0.5 * jnp.tanh(0.5 * x) + 0.5


def _tile_lanes(x, n):
    return x if n == 1 else jnp.concatenate([x] * n, axis=1)


def _rmsnorm_kernel(x_ref, g_ref, o_ref):
    x = x_ref[...]
    ms = jnp.mean(x * x, axis=-1, keepdims=True)
    o_ref[...] = ((x * lax.rsqrt(ms + EPS)) * g_ref[...]).astype(o_ref.dtype)


def _rmsnorm(x2d, g, out_dtype, name):
    m, d = x2d.shape
    bm = min(ROW_TILE, m)
    return pl.pallas_call(
        _rmsnorm_kernel,
        grid=(m // bm,),
        in_specs=[pl.BlockSpec((bm, d), lambda i: (i, 0)),
                  pl.BlockSpec((1, d), lambda i: (0, 0))],
        out_specs=pl.BlockSpec((bm, d), lambda i: (i, 0)),
        out_shape=jax.ShapeDtypeStruct((m, d), out_dtype),
        compiler_params=_cparams(("parallel",)),
        name=name,
    )(x2d, g.reshape(1, d).astype(F32))


def _mm_kernel(*refs, n_parts, has_res):
    o_ref = refs[-1]
    acc = jnp.dot(refs[0][...], refs[n_parts][...], preferred_element_type=F32)
    for p in range(1, n_parts):
        acc = acc + jnp.dot(refs[p][...], refs[n_parts + p][...], preferred_element_type=F32)
    if has_res:
        acc = refs[2 * n_parts][...] + acc
    o_ref[...] = acc.astype(o_ref.dtype)


def _matmul(xs, ws, out_dtype, name, res=None, n=None, bm=MM_BM, bn=MM_BN):
    ws = [w if isinstance(w, tuple) else (w, 0, 0) for w in ws]
    m = xs[0].shape[0]
    n = ws[0][0].shape[1] if n is None else n
    bm = min(bm, m)
    bn = min(bn, n)
    assert m % bm == 0 and n % bn == 0
    in_specs = [pl.BlockSpec((bm, x.shape[1]), lambda i, j: (i, 0)) for x in xs]
    for x, (w, rb, cb) in zip(xs, ws):
        assert w.shape[0] % x.shape[1] == 0 and (cb + n // bn) * bn <= w.shape[1]
        in_specs.append(pl.BlockSpec((x.shape[1], bn), lambda i, j, rb=rb, cb=cb: (rb, cb + j)))
    args = [*xs, *[w for w, _, _ in ws]]
    if res is not None:
        in_specs.append(pl.BlockSpec((bm, bn), lambda i, j: (i, j)))
        args.append(res)
    return pl.pallas_call(
        functools.partial(_mm_kernel, n_parts=len(xs), has_res=res is not None),
        grid=(m // bm, n // bn),
        in_specs=in_specs,
        out_specs=pl.BlockSpec((bm, bn), lambda i, j: (i, j)),
        out_shape=jax.ShapeDtypeStruct((m, n), out_dtype),
        compiler_params=_cparams(("parallel", "parallel")),
        name=name,
    )(*args)


def _t5_bucket_np(dist):
    d = np.maximum(dist, 0)
    max_exact = NUM_BUCKETS // 2
    d_f = np.maximum(d, 1).astype(np.float32)
    ratio = np.log(d_f / np.float32(max_exact)) / np.float32(math.log(MAX_DISTANCE / max_exact))
    large = max_exact + (ratio * np.float32(NUM_BUCKETS - max_exact)).astype(np.int32)
    large = np.minimum(large, NUM_BUCKETS - 1)
    return np.where(d < max_exact, d, large)


def _bias_blocks(tab):
    n = LANES
    assert np.all(_t5_bucket_np(np.arange(n, 64 * n)) == NUM_BUCKETS - 1)
    h = tab.shape[1]
    rel = (tab - tab[NUM_BUCKETS - 1]).astype(F32) * LOG2E
    out = []
    for off in (0, n):
        buckets = _t5_bucket_np(off + n - 1 - np.arange(2 * n - 1))
        onehot = np.zeros((2 * n, NUM_BUCKETS), np.float32)
        onehot[np.arange(2 * n - 1), buckets] = 1.0
        w = jnp.dot(jnp.asarray(onehot), rel, precision=lax.Precision.HIGHEST).T
        t = jnp.tile(w, (1, n))[:, :n * (2 * n - 1)].reshape(h, n, 2 * n - 1)
        out.append(t[:, :, n - 1:])
    return jnp.stack(out)


def _dsa_prep_kernel(p_ref, kvg_ref, ig_ref, ib_ref, ckv_ref, kbd_ref, w_ref):
    p = p_ref[0]
    ts = p.shape[0]
    c = p[:, :KV_LORA]
    cn = (c * lax.rsqrt(jnp.mean(c * c, axis=-1, keepdims=True) + EPS)) * kvg_ref[...]
    ckv_ref[0] = cn.astype(BF16)
    k = p[:, KV_LORA:KV_LORA + IDX_DIM]
    mu = jnp.mean(k, axis=-1, keepdims=True)
    var = jnp.mean(jnp.square(k - mu), axis=-1, keepdims=True)
    kn = ((k - mu) * lax.rsqrt(var + EPS)) * ig_ref[...] + ib_ref[...]
    zeros = jnp.zeros((ts, LANES - IDX_DIM), F32)
    top = jnp.concatenate([kn, zeros], axis=1).T
    bot = jnp.concatenate([zeros, kn], axis=1).T
    kbd_ref[0, 0] = jnp.concatenate([top, bot], axis=1).astype(BF16)
    w_ref[0] = p[:, KV_LORA + LANES:KV_LORA + LANES + N_IDX_HEADS] * IDX_SCALE


def _dsa_prep(ps, kv_g, idx_g, idx_b):
    b, s, wid = ps.shape
    ts = ATT_T
    nkt = s // ts
    return pl.pallas_call(
        _dsa_prep_kernel,
        grid=(b, nkt),
        in_specs=[pl.BlockSpec((1, ts, wid), lambda bi, i: (bi, i, 0)),
                  pl.BlockSpec((1, KV_LORA), lambda bi, i: (0, 0)),
                  pl.BlockSpec((1, IDX_DIM), lambda bi, i: (0, 0)),
                  pl.BlockSpec((1, IDX_DIM), lambda bi, i: (0, 0))],
        out_specs=[pl.BlockSpec((1, ts, KV_LORA), lambda bi, i: (bi, i, 0)),
                   pl.BlockSpec((1, 1, LANES, 2 * ts), lambda bi, i: (bi, i, 0, 0)),
                   pl.BlockSpec((1, ts, N_IDX_HEADS), lambda bi, i: (bi, i, 0))],
        out_shape=[jax.ShapeDtypeStruct((b, s, KV_LORA), BF16),
                   jax.ShapeDtypeStruct((b, nkt, LANES, 2 * ts), BF16),
                   jax.ShapeDtypeStruct((b, s, N_IDX_HEADS), F32)],
        compiler_params=_cparams(("parallel", "parallel")),
        name="dsa_prep",
    )(ps, kv_g.reshape(1, -1).astype(F32), idx_g.reshape(1, -1).astype(F32),
      idx_b.reshape(1, -1).astype(F32))


def _dsa_index_kernel(qi_ref, w_ref, kbd_ref, mask_ref, key_ref, keyt_ref, wb_ref, t_ref, j_ref,
                      *, topk, nkt_total, seq_bits):
    tq = tk = ATT_T
    rep = tk // LANES
    sl = SUBLANES
    i = pl.program_id(1)
    t0 = i * tq
    nkt = i + 1

    row = t0 + lax.broadcasted_iota(I32, (tq, tk), 0)
    col = lax.broadcasted_iota(I32, (tq, tk), 1)
    row_t = lax.broadcasted_iota(I32, (tk, tq), 0)
    col_t = t0 + lax.broadcasted_iota(I32, (tk, tq), 1)
    sub8 = lax.broadcasted_iota(I32, (sl, tq), 0)

    def to_key(score):
        score = jnp.where(score == 0.0, 0.0, score)
        bits = lax.bitcast_convert_type(score, I32)
        return bits ^ ((bits >> 31) & np.int32(0x7FFFFFFF))

    for h in range(N_IDX_HEADS):
        wb_ref[h] = jnp.broadcast_to(w_ref[0, :, h:h + 1], (tq, LANES))

    def idx_body(kt, carry):
        kb = kbd_ref[0, kt]
        acc = jnp.zeros((tq, tk), F32)
        for hp in range(N_IDX_HEADS // 2):
            rel = jnp.dot(qi_ref[0, :, hp * LANES:(hp + 1) * LANES], kb,
                          preferred_element_type=F32)
            acc = acc + jnp.maximum(rel[:, :tk], 0.0) * _tile_lanes(wb_ref[2 * hp], rep)
            acc = acc + jnp.maximum(rel[:, tk:], 0.0) * _tile_lanes(wb_ref[2 * hp + 1], rep)
        key_ref[kt] = jnp.where(kt * tk + col <= row, to_key(acc), INT_MIN)
        keyt_ref[kt] = jnp.where(kt * tk + row_t <= col_t, to_key(acc.T), INT_MIN)
        return carry

    lax.fori_loop(0, nkt, idx_body, 0)

    def count(pred_fn):
        def body(kt, c):
            xs = [pred_fn(keyt_ref[kt, r * sl:(r + 1) * sl, :], kt * tk + r * sl) for r in range(tk // sl)]
            while len(xs) > 1:
                xs = [xs[a] + xs[a + 1] for a in range(0, len(xs), 2)]
            return c + xs[0]
        c = lax.fori_loop(0, nkt, body, jnp.zeros((sl, tq), F32))
        return jnp.broadcast_to(jnp.sum(c, axis=0, keepdims=True), (sl, tq))

    t_ref[...] = jnp.full((sl, tq), INT_MIN, I32)

    def bit_body(it, carry):
        cand = t_ref[...] ^ jnp.left_shift(jnp.int32(1), 31 - it)
        n_ge = count(lambda k, s0: jnp.where(k >= cand, 1.0, 0.0))
        t_ref[...] = jnp.where(n_ge >= topk, cand, t_ref[...])
        return carry

    lax.fori_loop(0, 32, bit_body, 0)

    thr = t_ref[...]
    n_gt = count(lambda k, s0: jnp.where(k > thr, 1.0, 0.0))
    n_ge = count(lambda k, s0: jnp.where(k >= thr, 1.0, 0.0))
    need = topk - n_gt
    j_ref[...] = jnp.full((sl, tq), 2 ** 30, I32)

    @pl.when(jnp.max(n_ge) > topk)
    def _():
        j_ref[...] = jnp.zeros((sl, tq), I32)

        def jbit_body(it, carry):
            cand = j_ref[...] | jnp.left_shift(jnp.int32(1), seq_bits - 1 - it)
            n_lt = count(lambda k, s0: jnp.where(k == thr, jnp.where(s0 + sub8 < cand, 1.0, 0.0), 0.0))
            j_ref[...] = jnp.where(n_lt < need, cand, j_ref[...])
            return carry

        lax.fori_loop(0, seq_bits, jbit_body, 0)

    def per_row(v8):
        wide = jnp.broadcast_to(lax.bitcast_convert_type(v8[0:1], F32), (LANES, tq))
        return _tile_lanes(lax.bitcast_convert_type(wide.T, I32), rep)

    thr_t = per_row(thr)
    jmax_t = per_row(j_ref[...])

    def mask_body(kt, carry):
        k = key_ref[kt]
        s_idx = kt * tk + col
        keep_tie = jnp.where(k == thr_t, jnp.where(s_idx <= jmax_t, 0.0, MASK_NEG), MASK_NEG)
        keep = jnp.where(k > thr_t, 0.0, keep_tie)
        mask_ref[0, 0, kt] = jnp.where(s_idx <= row, keep, MASK_NEG).astype(BF16)
        return carry

    lax.fori_loop(0, nkt, mask_body, 0)

    def fill_body(kt, carry):
        mask_ref[0, 0, kt] = jnp.full((tq, tk), MASK_NEG, BF16)
        return carry

    lax.fori_loop(nkt, nkt_total, fill_body, 0)


def _dsa_index(pqi, col_qi, kbd, widx):
    b, s, _ = pqi.shape
    tq = tk = ATT_T
    nkt = s // tk
    wq = N_IDX_HEADS * IDX_DIM
    topk = min(TOPK_MAX, s // 4)
    kern = functools.partial(_dsa_index_kernel, topk=topk, nkt_total=nkt,
                             seq_bits=max(1, (s - 1).bit_length()))
    return pl.pallas_call(
        kern,
        grid=(b, s // tq),
        in_specs=[pl.BlockSpec((1, tq, wq), lambda bi, i: (bi, i, col_qi)),
                  pl.BlockSpec((1, tq, N_IDX_HEADS), lambda bi, i: (bi, i, 0)),
                  pl.BlockSpec((1, nkt, LANES, 2 * tk), lambda bi, i: (bi, 0, 0, 0))],
        out_specs=pl.BlockSpec((1, 1, nkt, tq, tk), lambda bi, i: (bi, i, 0, 0, 0)),
        out_shape=jax.ShapeDtypeStruct((b, s // tq, nkt, tq, tk), BF16),
        scratch_shapes=[pltpu.VMEM((nkt, tq, tk), I32),
                        pltpu.VMEM((nkt, tk, tq), I32),
                        pltpu.VMEM((N_IDX_HEADS, tq, LANES), F32),
                        pltpu.VMEM((SUBLANES, tq), I32),
                        pltpu.VMEM((SUBLANES, tq), I32)],
        compiler_params=_cparams(("parallel", "parallel")),
        name="dsa_index",
    )(pqi, widx, kbd)


def _flash_kernel(*refs, mode, nb, topb):
    if mode == "mask":
        q_ref, k_ref, v_ref, z_ref, d_ref, mask_ref, y_ref = refs[:7]
    else:
        q_ref, k_ref, v_ref, z_ref, d_ref, y_ref, km_ref = refs[:7]
    qs_ref, acc_ref, m_ref, p_ref, alpha_ref, x_ref = refs[7:]
    tq = ATT_T
    hb = q_ref.shape[2] // HEAD_DIM
    rep = tq // LANES
    i = pl.program_id(2)
    lane = lax.broadcasted_iota(I32, (tq, LANES), 1)
    zero_blk = jnp.zeros((LANES, LANES), F32)
    ones_blk = jnp.ones((tq, LANES), BF16)

    if mode == "sel":
        @pl.when(i == 0)
        def _():
            km_ref[...] = jnp.zeros(km_ref.shape, F32)
            for h in range(hb):
                for n in range(nb):
                    kblk = k_ref[0, n * tq:(n + 1) * tq, h * HEAD_DIM:(h + 1) * HEAD_DIM].astype(F32)
                    km_ref[h, n:n + 1, :] = jnp.mean(kblk, axis=0, keepdims=True)
        nbp = -(-nb // SUBLANES) * SUBLANES
        blk = lax.broadcasted_iota(I32, (nbp, tq), 0).astype(F32)

    for h in range(hb):
        q = q_ref[0, :, h * HEAD_DIM:(h + 1) * HEAD_DIM]
        qsc = (q.astype(F32) * (ATTN_SCALE * LOG2E)).astype(BF16)
        if mode == "sel":
            gate = lax.dot_general(km_ref[h, 0:nbp, :].astype(BF16), q, (((1,), (1,)), ((), ())),
                                   preferred_element_type=F32)
            gate = jnp.where(blk < i.astype(F32), gate, -jnp.inf)
            sel = jnp.full((nbp, tq), MASK_NEG, F32)
            for _ in range(topb):
                gmax = jnp.max(gate, axis=0, keepdims=True)
                first = jnp.min(jnp.where(gate == gmax, blk, float(nbp)), axis=0, keepdims=True)
                hit = jnp.where(blk == first, jnp.where(gmax > -jnp.inf, 1.0, 0.0), 0.0)
                sel = jnp.where(hit > 0.0, 0.0, sel)
                gate = jnp.where(blk == first, -jnp.inf, gate)
            selb = jnp.concatenate([sel, jnp.full((LANES - nbp, tq), MASK_NEG, F32)], axis=0).T
            qs_ref[h] = jnp.concatenate([qsc, selb.astype(BF16)], axis=1)
        else:
            qs_ref[h] = qsc
        m_ref[h] = jnp.full((tq, LANES), MASK_NEG, F32)
        acc_ref[h] = jnp.zeros((tq, 2 * HEAD_DIM), F32)
        p_ref[h] = jnp.zeros((tq, tq), BF16)
        alpha_ref[h] = jnp.ones((tq, LANES), F32)

    def qk_stage(n, is_own):
        r0 = pl.multiple_of(n * tq, tq)
        if mode == "sel":
            sel_lane = jnp.where(is_own, -1, n)
            onehot = jnp.where(lane == sel_lane, 1.0, 0.0).astype(BF16)
        for h in range(hb):
            kblk = k_ref[0, pl.ds(r0, tq), h * HEAD_DIM:(h + 1) * HEAD_DIM]
            if mode == "sel":
                kblk = jnp.concatenate([kblk, onehot], axis=1)
            x_ref[h] = lax.dot_general(qs_ref[h], kblk, (((1,), (1,)), ((), ())),
                                       preferred_element_type=F32)

    def softmax_stage(n, kind):
        if mode == "mask":
            mask_tile = mask_ref[0, 0, n].astype(F32)
        elif kind == "own":
            row = lax.broadcasted_iota(I32, (tq, tq), 0)
            col = lax.broadcasted_iota(I32, (tq, tq), 1)
        for h in range(hb):
            x = x_ref[h]
            if mode == "mask":
                x = x + mask_tile
            if kind == "near":
                x = x + jnp.concatenate([jnp.concatenate([zero_blk, d_ref[1, h]], axis=1),
                                         jnp.concatenate([zero_blk, zero_blk], axis=1)], axis=0)
            elif kind == "own":
                x = x + jnp.concatenate([jnp.concatenate([d_ref[0, h], zero_blk], axis=1),
                                         jnp.concatenate([d_ref[1, h], d_ref[0, h]], axis=1)], axis=0)
                if mode == "sel":
                    x = jnp.where(col <= row, x, MASK_NEG)
            m_prev = m_ref[h]
            m_new = jnp.maximum(m_prev, jnp.max(x, axis=1, keepdims=True))
            alpha_ref[h] = jnp.exp2(m_prev - m_new)
            p_ref[h] = jnp.exp2(x - _tile_lanes(m_new, rep)).astype(BF16)
            m_ref[h] = m_new

    def pv_stage(n):
        r0 = pl.multiple_of(n * tq, tq)
        for h in range(hb):
            vblk = v_ref[0, pl.ds(r0, tq), h * HEAD_DIM:(h + 1) * HEAD_DIM]
            pv = jnp.dot(p_ref[h], jnp.concatenate([vblk, ones_blk], axis=1),
                         preferred_element_type=F32)
            acc_ref[h] = acc_ref[h] * _tile_lanes(alpha_ref[h], 2) + pv

    def step(n, kind):
        pv_stage(jnp.maximum(n - 1, 0))
        softmax_stage(n, kind)
        if kind != "own":
            qk_stage(n + 1, n + 1 == i)

    qk_stage(0, i == 0)

    def far_body(n, carry):
        step(n, "far")
        return carry

    lax.fori_loop(0, jnp.maximum(i - 1, 0), far_body, 0)

    @pl.when(i >= 1)
    def _():
        step(i - 1, "near")

    step(i, "own")
    pv_stage(i)

    for h in range(hb):
        acc = acc_ref[h]
        o = acc[:, :HEAD_DIM] / acc[:, HEAD_DIM:]
        zz = z_ref[0, :, h * HEAD_DIM:(h + 1) * HEAD_DIM].astype(F32)
        y_ref[0, :, h * HEAD_DIM:(h + 1) * HEAD_DIM] = (o * (zz * _sigmoid(zz))).astype(BF16)


def _flash_attention(q, k, v, z, dtab, n_heads, name, mask=None):
    (qa, cq), (ka, ck), (va, cv), (za, cz) = q, k, v, z
    b, s, _ = qa.shape
    tq = ATT_T
    assert s % tq == 0
    nb = s // tq
    hb = ATT_HB
    wg = hb * HEAD_DIM
    mode = "sel" if mask is None else "mask"
    kern = functools.partial(_flash_kernel, mode=mode, nb=nb, topb=min(MOBA_TOPB, nb - 1))
    in_specs = [pl.BlockSpec((1, tq, wg), lambda bi, g, i: (bi, i, cq + g)),
                pl.BlockSpec((1, s, wg), lambda bi, g, i: (bi, 0, ck + g)),
                pl.BlockSpec((1, s, wg), lambda bi, g, i: (bi, 0, cv + g)),
                pl.BlockSpec((1, tq, wg), lambda bi, g, i: (bi, i, cz + g)),
                pl.BlockSpec((2, hb, LANES, LANES), lambda bi, g, i: (0, g, 0, 0))]
    args = [qa, ka, va, za, dtab]
    scratch = []
    if mode == "mask":
        in_specs.append(pl.BlockSpec((1, 1, nb, tq, tq), lambda bi, g, i: (bi, i, 0, 0, 0)))
        args.append(mask)
        kq = HEAD_DIM
    else:
        scratch.append(pltpu.VMEM((hb, LANES, HEAD_DIM), F32))
        kq = 2 * HEAD_DIM
    scratch += [pltpu.VMEM((hb, tq, kq), BF16),
                pltpu.VMEM((hb, tq, 2 * HEAD_DIM), F32),
                pltpu.VMEM((hb, tq, LANES), F32),
                pltpu.VMEM((hb, tq, tq), BF16),
                pltpu.VMEM((hb, tq, LANES), F32),
                pltpu.VMEM((hb, tq, tq), F32)]
    return pl.pallas_call(
        kern,
        grid=(b, n_heads // hb, nb),
        in_specs=in_specs,
        out_specs=pl.BlockSpec((1, tq, wg), lambda bi, g, i: (bi, i, g)),
        out_shape=jax.ShapeDtypeStruct((b, s, n_heads * HEAD_DIM), BF16),
        scratch_shapes=scratch,
        compiler_params=_cparams(("parallel", "parallel", "arbitrary")),
        name=name,
    )(*args)


def _softplus(x):
    return jnp.maximum(x, 0.0) + jnp.log1p(jnp.exp(-jnp.abs(x)))


def _rglru_kernel(xr_ref, z_ref, cw_ref, cb_ref, wa_ref, ba_ref, wx_ref, bx_ref, lam_ref, y_ref,
                  xbuf_ref, a_ref, b_ref, h_ref):
    tc = xr_ref.shape[1]
    c = pl.program_id(2)
    halo = SUBLANES

    @pl.when(c == 0)
    def _():
        xbuf_ref[0:halo, :] = jnp.zeros((halo, xbuf_ref.shape[1]), F32)
        h_ref[...] = jnp.zeros(h_ref.shape, F32)

    @pl.when(c > 0)
    def _():
        xbuf_ref[0:halo, :] = xbuf_ref[tc:tc + halo, :]

    xbuf_ref[halo:halo + tc, :] = xr_ref[0]
    xfull = xbuf_ref[...]
    u = xfull * cw_ref[0:1, :]
    for j in range(1, CONV_W):
        u = xfull * cw_ref[j:j + 1, :] + pltpu.roll(u, 1, 0)
    xc = (u + cb_ref[...])[halo:, :]

    xcb = xc.astype(BF16)
    tr = jnp.tanh(jnp.dot(xcb, wa_ref[0], preferred_element_type=F32) + ba_ref[...])
    ti = jnp.tanh(jnp.dot(xcb, wx_ref[0], preferred_element_type=F32) + bx_ref[...])
    half_c = (-0.5 * RG_C) * _softplus(-lam_ref[...])
    a = jnp.exp(tr * half_c + half_c)
    one_m_a2 = 1.0 - a * a
    mult = jnp.where(one_m_a2 > 0.0, one_m_a2 * lax.rsqrt(one_m_a2), 0.0)
    bt = (mult * xc) * (0.5 * ti + 0.5)

    nseg = SUBLANES
    seg = tc // nseg
    pitch = seg + RG_SEG_PAD
    nslab = a_ref.shape[0]
    for sl in range(nslab):
        for sg in range(nseg):
            a_ref[sl, sg * pitch:sg * pitch + seg, :] = a[sg * seg:(sg + 1) * seg, sl * LANES:(sl + 1) * LANES]
            b_ref[sl, sg * pitch:sg * pitch + seg, :] = bt[sg * seg:(sg + 1) * seg, sl * LANES:(sl + 1) * LANES]

    def seg_step(j, carry):
        out = []
        for sl in range(nslab):
            hl, pr = carry[2 * sl], carry[2 * sl + 1]
            rows = pl.ds(j, nseg, stride=pitch)
            a8 = a_ref[sl, rows, :]
            hl = a8 * hl + b_ref[sl, rows, :]
            pr = a8 * pr
            b_ref[sl, rows, :] = hl
            a_ref[sl, rows, :] = pr
            out += [hl, pr]
        return tuple(out)

    init = tuple(v for _ in range(nslab)
                 for v in (jnp.zeros((nseg, LANES), F32), jnp.ones((nseg, LANES), F32)))
    ends = lax.fori_loop(0, seg, seg_step, init, unroll=RG_SCAN_UNROLL)

    zz = z_ref[0].astype(F32)
    gate = zz * _sigmoid(zz)
    for sl in range(nslab):
        h_end, p_end = ends[2 * sl], ends[2 * sl + 1]
        h_in = h_ref[:, sl * LANES:(sl + 1) * LANES]
        for sg in range(nseg):
            rows = slice(sg * pitch, sg * pitch + seg)
            hs = b_ref[sl, rows, :] + a_ref[sl, rows, :] * h_in
            y_ref[0, sg * seg:(sg + 1) * seg, sl * LANES:(sl + 1) * LANES] = (
                hs * gate[sg * seg:(sg + 1) * seg, sl * LANES:(sl + 1) * LANES]).astype(BF16)
            h_in = p_end[sg:sg + 1, :] * h_in + h_end[sg:sg + 1, :]
        h_ref[:, sl * LANES:(sl + 1) * LANES] = h_in


def _rglru(xr, z, conv_w, conv_b, wa_g, b_a, wx_g, b_x, lam):
    b, s, d = xr.shape
    g = wa_g.shape[0]
    cg = d // g
    tc = min(RG_TC, s)
    assert ((tc // SUBLANES + RG_SEG_PAD) // SUBLANES) % 2 == 1 and cg % LANES == 0
    row = lambda v: v.reshape(1, d).astype(F32)
    vec_spec = pl.BlockSpec((1, cg), lambda bi, gi, c: (0, gi))
    return pl.pallas_call(
        _rglru_kernel,
        grid=(b, g, s // tc),
        in_specs=[pl.BlockSpec((1, tc, cg), lambda bi, gi, c: (bi, c, gi)),
                  pl.BlockSpec((1, tc, cg), lambda bi, gi, c: (bi, c, gi)),
                  pl.BlockSpec((CONV_W, cg), lambda bi, gi, c: (0, gi)),
                  vec_spec,
                  pl.BlockSpec((1, cg, cg), lambda bi, gi, c: (gi, 0, 0)),
                  vec_spec,
                  pl.BlockSpec((1, cg, cg), lambda bi, gi, c: (gi, 0, 0)),
                  vec_spec,
                  vec_spec],
        out_specs=pl.BlockSpec((1, tc, cg), lambda bi, gi, c: (bi, c, gi)),
        out_shape=jax.ShapeDtypeStruct((b, s, d), BF16),
        scratch_shapes=[pltpu.VMEM((tc + SUBLANES, cg), F32),
                        pltpu.VMEM((cg // LANES, tc + SUBLANES * RG_SEG_PAD, LANES), F32),
                        pltpu.VMEM((cg // LANES, tc + SUBLANES * RG_SEG_PAD, LANES), F32),
                        pltpu.VMEM((1, cg), F32)],
        compiler_params=_cparams(("parallel", "parallel", "arbitrary")),
        name="rglru",
    )(xr, z, conv_w.astype(F32), row(conv_b), wa_g * 0.5, row(b_a) * 0.5, wx_g * 0.5, row(b_x) * 0.5,
      row(lam))


def _group_block_diag(w, pair):
    nb, k, _ = w.shape
    out = jnp.zeros((nb // pair, pair * k, pair * k), w.dtype)
    for p in range(pair):
        out = out.at[:, p * k:(p + 1) * k, p * k:(p + 1) * k].set(w[p::pair])
    return out


def _attention_layer(x2d, b, s, norm_g, w_in, kv_g, w_uk, w_uv, idx_g, idx_b, w_out, rel_bias):
    d_model = x2d.shape[1]
    n_heads = w_uk.shape[0]
    wq = n_heads * HEAD_DIM
    assert N_IDX_HEADS * IDX_DIM == wq
    sizes = (wq, KV_LORA, wq, IDX_DIM, N_IDX_HEADS, wq, wq, wq, 2 * wq)
    offs = [int(o) for o in np.concatenate([[0], np.cumsum(sizes)])]
    half = MM_BN // 2
    assert offs[2] % half == 0
    w_head = w_in.astype(BF16)
    w_tail = w_head[:, offs[5]:]
    zpad = lambda n: jnp.zeros((d_model, n), BF16)
    w_small = jnp.concatenate([w_head[:, offs[1]:offs[2]], w_head[:, offs[3]:offs[4]],
                               zpad(LANES - IDX_DIM), w_head[:, offs[4]:offs[5]],
                               zpad(LANES - N_IDX_HEADS)], axis=1)

    h = _rmsnorm(x2d, norm_g, BF16, "rmsnorm0")
    pqa = _matmul([h], [(w_head, 0, 0)], BF16, "proj_attn_qa", n=wq).reshape(b, s, wq)
    pqi = _matmul([h], [(w_head, 0, offs[2] // half)], BF16, "proj_attn_qi", n=wq, bn=half
                  ).reshape(b, s, wq)
    ptail = _matmul([h], [w_tail], BF16, "proj_attn_tail").reshape(b, s, -1)
    psmall = _matmul([h], [w_small], F32, "proj_attn_small").reshape(b, s, -1)

    ckv, kbd, widx = _dsa_prep(psmall, kv_g, idx_g, idx_b)
    mask = _dsa_index(pqi, 0, kbd, widx)
    w_kv = jnp.concatenate([jnp.transpose(w_uk, (1, 0, 2)).reshape(KV_LORA, wq),
                            jnp.transpose(w_uv, (1, 0, 2)).reshape(KV_LORA, wq)], axis=1).astype(BF16)
    kv = _matmul([ckv.reshape(b * s, KV_LORA)], [w_kv], BF16, "dsa_kv_up").reshape(b, s, 2 * wq)

    dtab_a = _bias_blocks(rel_bias[:, :n_heads])
    dtab_b = _bias_blocks(rel_bias[:, n_heads:])
    gpw = n_heads // ATT_HB
    ya = _flash_attention((pqa, 0), (kv, 0), (kv, gpw), (ptail, 3 * gpw), dtab_a, n_heads,
                          "dsa_attention", mask=mask)
    yb = _flash_attention((ptail, 0), (ptail, gpw), (ptail, 2 * gpw), (ptail, 4 * gpw), dtab_b,
                          n_heads, "moba_attention")

    w_out_b = w_out.astype(BF16)
    return _matmul([ya.reshape(b * s, wq), yb.reshape(b * s, wq)], [(w_out_b, 0, 0), (w_out_b, 1, 0)],
                   F32, "proj_attn_out", res=x2d, bn=half)


def _recurrent_layer(x2d, b, s, norm_g, w_in, conv_w, conv_b, w_a, b_a, w_x, b_x, lam, w_out):
    d_rnn = conv_b.shape[0]
    h = _rmsnorm(x2d, norm_g, BF16, "rmsnorm1")
    w_in_b = w_in.astype(BF16)
    assert d_rnn % MM_BN == 0
    xr = _matmul([h], [(w_in_b, 0, 0)], F32, "proj_rec_x", n=d_rnn)
    z = _matmul([h], [(w_in_b, 0, d_rnn // MM_BN)], BF16, "proj_rec_z", n=d_rnn)
    y = _rglru(xr.reshape(b, s, d_rnn), z.reshape(b, s, d_rnn), conv_w, conv_b,
               _group_block_diag(w_a.astype(BF16), RG_PAIR), b_a,
               _group_block_diag(w_x.astype(BF16), RG_PAIR), b_x, lam)
    return _matmul([y.reshape(b * s, d_rnn)], [w_out.astype(BF16)], F32, "proj_rec_out",
                   res=x2d, bn=MM_BN // 2)


def kernel(x, norm_g, final_g, rel_bias, attn_w_in, attn_kv_g, attn_w_uk, attn_w_uv, idx_k_g, idx_k_b,
           attn_w_out, rec_w_in, rec_conv_w, rec_conv_b, rec_w_a, rec_b_a, rec_w_x, rec_b_x,
           rec_lambda, rec_w_out):
    b, s, d = x.shape
    depth = norm_g.shape[0]
    x2d = x.reshape(b * s, d)
    for layer in range(depth):
        li = layer // 2
        if layer % 2 == 0:
            x2d = _attention_layer(x2d, b, s, norm_g[layer], attn_w_in[li], attn_kv_g[li],
                                   attn_w_uk[li], attn_w_uv[li], idx_k_g[li], idx_k_b[li],
                                   attn_w_out[li], rel_bias)
        else:
            x2d = _recurrent_layer(x2d, b, s, norm_g[layer], rec_w_in[li], rec_conv_w[li],
                                   rec_conv_b[li], rec_w_a[li], rec_b_a[li], rec_w_x[li],
                                   rec_b_x[li], rec_lambda[li], rec_w_out[li])
    return _rmsnorm(x2d, final_g, x.dtype, "rmsnorm_final").reshape(b, s, d)
```

```python
import functools
import math

import numpy as np
import jax
import jax.numpy as jnp
from jax import lax
from jax.experimental import pallas as pl
from jax.experimental.pallas import tpu as pltpu

F32 = jnp.float32
BF16 = jnp.bfloat16
I32 = jnp.int32

HEAD_DIM = 128
KV_LORA = 512
N_IDX_HEADS = 32
IDX_DIM = 64
TOPK_MAX = 256
MOBA_BLOCK = 256
MOBA_TOPB = 3
RG_BLOCKS = 16
CONV_W = 4
RG_C = 8.0
NUM_BUCKETS = 32
MAX_DISTANCE = 128
EPS = 1e-6
ATTN_SCALE = HEAD_DIM ** -0.5
IDX_SCALE = (N_IDX_HEADS ** -0.5) * (IDX_DIM ** -0.5)
LOG2E = math.log2(math.e)

LANES = 128
SUBLANES = 8
VMEM_LIMIT_BYTES = 56 * 2 ** 20
MASK_NEG = -1e30
INT_MIN = np.int32(-2 ** 31)

ROW_TILE = 256
MM_BM = 1024
MM_BN = 1024
ATT_T = 256
ATT_TK = 2 * ATT_T
ATT_HB = 4
RG_PAIR = 2
RG_TC = 512
RG_SEG_PAD = 8
RG_SCAN_UNROLL = 4
assert ATT_T == MOBA_BLOCK and ATT_T == 2 * LANES


def _cparams(sem):
    return pltpu.CompilerParams(dimension_semantics=sem, vmem_limit_bytes=VMEM_LIMIT_BYTES)


def _sigmoid(x):
    return 0.5 * jnp.tanh(0.5 * x) + 0.5


def _tile_lanes(x, n):
    return x if n == 1 else jnp.concatenate([x] * n, axis=1)


def _rmsnorm_kernel(x_ref, g_ref, o_ref):
    x = x_ref[...]
    ms = jnp.mean(x * x, axis=-1, keepdims=True)
    o_ref[...] = ((x * lax.rsqrt(ms + EPS)) * g_ref[...]).astype(o_ref.dtype)


def _rmsnorm(x2d, g, out_dtype, name):
    m, d = x2d.shape
    bm = min(ROW_TILE, m)
    return pl.pallas_call(
        _rmsnorm_kernel,
        grid=(m // bm,),
        in_specs=[pl.BlockSpec((bm, d), lambda i: (i, 0)),
                  pl.BlockSpec((1, d), lambda i: (0, 0))],
        out_specs=pl.BlockSpec((bm, d), lambda i: (i, 0)),
        out_shape=jax.ShapeDtypeStruct((m, d), out_dtype),
        compiler_params=_cparams(("parallel",)),
        name=name,
    )(x2d, g.reshape(1, d).astype(F32))


def _mm_kernel(*refs, n_parts, has_res):
    o_ref = refs[-1]
    acc = jnp.dot(refs[0][...], refs[n_parts][...], preferred_element_type=F32)
    for p in range(1, n_parts):
        acc = acc + jnp.dot(refs[p][...], refs[n_parts + p][...], preferred_element_type=F32)
    if has_res:
        acc = refs[2 * n_parts][...] + acc
    o_ref[...] = acc.astype(o_ref.dtype)


def _matmul(xs, ws, out_dtype, name, res=None, n=None, bm=MM_BM, bn=MM_BN):
    ws = [w if isinstance(w, tuple) else (w, 0, 0) for w in ws]
    m = xs[0].shape[0]
    n = ws[0][0].shape[1] if n is None else n
    bm = min(bm, m)
    bn = min(bn, n)
    assert m % bm == 0 and n % bn == 0
    in_specs = [pl.BlockSpec((bm, x.shape[1]), lambda i, j: (i, 0)) for x in xs]
    for x, (w, rb, cb) in zip(xs, ws):
        assert w.shape[0] % x.shape[1] == 0 and (cb + n // bn) * bn <= w.shape[1]
        in_specs.append(pl.BlockSpec((x.shape[1], bn), lambda i, j, rb=rb, cb=cb: (rb, cb + j)))
    args = [*xs, *[w for w, _, _ in ws]]
    if res is not None:
        in_specs.append(pl.BlockSpec((bm, bn), lambda i, j: (i, j)))
        args.append(res)
    return pl.pallas_call(
        functools.partial(_mm_kernel, n_parts=len(xs), has_res=res is not None),
        grid=(m // bm, n // bn),
        in_specs=in_specs,
        out_specs=pl.BlockSpec((bm, bn), lambda i, j: (i, j)),
        out_shape=jax.ShapeDtypeStruct((m, n), out_dtype),
        compiler_params=_cparams(("parallel", "parallel")),
        name=name,
    )(*args)


def _t5_bucket_np(dist):
    d = np.maximum(dist, 0)
    max_exact = NUM_BUCKETS // 2
    d_f = np.maximum(d, 1).astype(np.float32)
    ratio = np.log(d_f / np.float32(max_exact)) / np.float32(math.log(MAX_DISTANCE / max_exact))
    large = max_exact + (ratio * np.float32(NUM_BUCKETS - max_exact)).astype(np.int32)
    large = np.minimum(large, NUM_BUCKETS - 1)
    return np.where(d < max_exact, d, large)


def _bias_blocks(tab):
    n = LANES
    assert np.all(_t5_bucket_np(np.arange(n, 64 * n)) == NUM_BUCKETS - 1)
    h = tab.shape[1]
    rel = (tab - tab[NUM_BUCKETS - 1]).astype(F32) * LOG2E
    out = []
    for off in (0, n):
        buckets = _t5_bucket_np(off + n - 1 - np.arange(2 * n - 1))
        onehot = np.zeros((2 * n, NUM_BUCKETS), np.float32)
        onehot[np.arange(2 * n - 1), buckets] = 1.0
        w = jnp.dot(jnp.asarray(onehot), rel, precision=lax.Precision.HIGHEST).T
        t = jnp.tile(w, (1, n))[:, :n * (2 * n - 1)].reshape(h, n, 2 * n - 1)
        out.append(t[:, :, n - 1:])
    return jnp.stack(out)


def _dsa_prep_kernel(p_ref, kvg_ref, ig_ref, ib_ref, ckv_ref, kbd_ref, w_ref):
    p = p_ref[0]
    ts = p.shape[0]
    c = p[:, :KV_LORA]
    cn = (c * lax.rsqrt(jnp.mean(c * c, axis=-1, keepdims=True) + EPS)) * kvg_ref[...]
    ckv_ref[0] = cn.astype(BF16)
    k = p[:, KV_LORA:KV_LORA + IDX_DIM]
    mu = jnp.mean(k, axis=-1, keepdims=True)
    var = jnp.mean(jnp.square(k - mu), axis=-1, keepdims=True)
    kn = ((k - mu) * lax.rsqrt(var + EPS)) * ig_ref[...] + ib_ref[...]
    zeros = jnp.zeros((ts, LANES - IDX_DIM), F32)
    top = jnp.concatenate([kn, zeros], axis=1).T
    bot = jnp.concatenate([zeros, kn], axis=1).T
    kbd_ref[0, 0] = jnp.concatenate([top, bot], axis=1).astype(BF16)
    w_ref[0] = p[:, KV_LORA + LANES:KV_LORA + LANES + N_IDX_HEADS] * IDX_SCALE


def _dsa_prep(ps, kv_g, idx_g, idx_b):
    b, s, wid = ps.shape
    ts = ATT_T
    nkt = s // ts
    return pl.pallas_call(
        _dsa_prep_kernel,
        grid=(b, nkt),
        in_specs=[pl.BlockSpec((1, ts, wid), lambda bi, i: (bi, i, 0)),
                  pl.BlockSpec((1, KV_LORA), lambda bi, i: (0, 0)),
                  pl.BlockSpec((1, IDX_DIM), lambda bi, i: (0, 0)),
                  pl.BlockSpec((1, IDX_DIM), lambda bi, i: (0, 0))],
        out_specs=[pl.BlockSpec((1, ts, KV_LORA), lambda bi, i: (bi, i, 0)),
                   pl.BlockSpec((1, 1, LANES, 2 * ts), lambda bi, i: (bi, i, 0, 0)),
                   pl.BlockSpec((1, ts, N_IDX_HEADS), lambda bi, i: (bi, i, 0))],
        out_shape=[jax.ShapeDtypeStruct((b, s, KV_LORA), BF16),
                   jax.ShapeDtypeStruct((b, nkt, LANES, 2 * ts), BF16),
                   jax.ShapeDtypeStruct((b, s, N_IDX_HEADS), F32)],
        compiler_params=_cparams(("parallel", "parallel")),
        name="dsa_prep",
    )(ps, kv_g.reshape(1, -1).astype(F32), idx_g.reshape(1, -1).astype(F32),
      idx_b.reshape(1, -1).astype(F32))


def _dsa_index_kernel(qi_ref, w_ref, kbd_ref, mask_ref, key_ref, keyt_ref, wb_ref, t_ref, j_ref,
                      *, topk, nkt_total, seq_bits):
    tq = tk = ATT_T
    rep = tk // LANES
    sl = SUBLANES
    i = pl.program_id(1)
    t0 = i * tq
    nkt = i + 1

    row = t0 + lax.broadcasted_iota(I32, (tq, tk), 0)
    col = lax.broadcasted_iota(I32, (tq, tk), 1)
    row_t = lax.broadcasted_iota(I32, (tk, tq), 0)
    col_t = t0 + lax.broadcasted_iota(I32, (tk, tq), 1)
    sub8 = lax.broadcasted_iota(I32, (sl, tq), 0)

    def to_key(score):
        score = jnp.where(score == 0.0, 0.0, score)
        bits = lax.bitcast_convert_type(score, I32)
        return bits ^ ((bits >> 31) & np.int32(0x7FFFFFFF))

    for h in range(N_IDX_HEADS):
        wb_ref[h] = jnp.broadcast_to(w_ref[0, :, h:h + 1], (tq, LANES))

    def idx_body(kt, carry):
        kb = kbd_ref[0, kt]
        acc = jnp.zeros((tq, tk), F32)
        for hp in range(N_IDX_HEADS // 2):
            rel = jnp.dot(qi_ref[0, :, hp * LANES:(hp + 1) * LANES], kb,
                          preferred_element_type=F32)
            acc = acc + jnp.maximum(rel[:, :tk], 0.0) * _tile_lanes(wb_ref[2 * hp], rep)
            acc = acc + jnp.maximum(rel[:, tk:], 0.0) * _tile_lanes(wb_ref[2 * hp + 1], rep)
        key_ref[kt] = jnp.where(kt * tk + col <= row, to_key(acc), INT_MIN)
        keyt_ref[kt] = jnp.where(kt * tk + row_t <= col_t, to_key(acc.T), INT_MIN)
        return carry

    lax.fori_loop(0, nkt, idx_body, 0)

    def count(pred_fn):
        def body(kt, c):
            xs = [pred_fn(keyt_ref[kt, r * sl:(r + 1) * sl, :], kt * tk + r * sl) for r in range(tk // sl)]
            while len(xs) > 1:
                xs = [xs[a] + xs[a + 1] for a in range(0, len(xs), 2)]
            return c + xs[0]
        c = lax.fori_loop(0, nkt, body, jnp.zeros((sl, tq), F32))
        return jnp.broadcast_to(jnp.sum(c, axis=0, keepdims=True), (sl, tq))

    t_ref[...] = jnp.full((sl, tq), INT_MIN, I32)

    def bit_body(it, carry):
        cand = t_ref[...] ^ jnp.left_shift(jnp.int32(1), 31 - it)
        n_ge = count(lambda k, s0: jnp.where(k >= cand, 1.0, 0.0))
        t_ref[...] = jnp.where(n_ge >= topk, cand, t_ref[...])
        return carry

    lax.fori_loop(0, 32, bit_body, 0)

    thr = t_ref[...]
    n_gt = count(lambda k, s0: jnp.where(k > thr, 1.0, 0.0))
    n_ge = count(lambda k, s0: jnp.where(k >= thr, 1.0, 0.0))
    need = topk - n_gt
    j_ref[...] = jnp.full((sl, tq), 2 ** 30, I32)

    @pl.when(jnp.max(n_ge) > topk)
    def _():
        j_ref[...] = jnp.zeros((sl, tq), I32)

        def jbit_body(it, carry):
            cand = j_ref[...] | jnp.left_shift(jnp.int32(1), seq_bits - 1 - it)
            n_lt = count(lambda k, s0: jnp.where(k == thr, jnp.where(s0 + sub8 < cand, 1.0, 0.0), 0.0))
            j_ref[...] = jnp.where(n_lt < need, cand, j_ref[...])
            return carry

        lax.fori_loop(0, seq_bits, jbit_body, 0)

    def per_row(v8):
        wide = jnp.broadcast_to(lax.bitcast_convert_type(v8[0:1], F32), (LANES, tq))
        return _tile_lanes(lax.bitcast_convert_type(wide.T, I32), rep)

    thr_t = per_row(thr)
    jmax_t = per_row(j_ref[...])

    def mask_body(kt, carry):
        k = key_ref[kt]
        s_idx = kt * tk + col
        keep_tie = jnp.where(k == thr_t, jnp.where(s_idx <= jmax_t, 0.0, MASK_NEG), MASK_NEG)
        keep = jnp.where(k > thr_t, 0.0, keep_tie)
        mask_ref[0, 0, kt] = jnp.where(s_idx <= row, keep, MASK_NEG).astype(BF16)
        return carry

    lax.fori_loop(0, nkt, mask_body, 0)

    def fill_body(kt, carry):
        mask_ref[0, 0, kt] = jnp.full((tq, tk), MASK_NEG, BF16)
        return carry

    lax.fori_loop(nkt, nkt_total, fill_body, 0)


def _dsa_index(pqi, col_qi, kbd, widx):
    b, s, _ = pqi.shape
    tq = tk = ATT_T
    nkt = s // tk
    wq = N_IDX_HEADS * IDX_DIM
    topk = min(TOPK_MAX, s // 4)
    kern = functools.partial(_dsa_index_kernel, topk=topk, nkt_total=nkt,
                             seq_bits=max(1, (s - 1).bit_length()))
    return pl.pallas_call(
        kern,
        grid=(b, s // tq),
        in_specs=[pl.BlockSpec((1, tq, wq), lambda bi, i: (bi, i, col_qi)),
                  pl.BlockSpec((1, tq, N_IDX_HEADS), lambda bi, i: (bi, i, 0)),
                  pl.BlockSpec((1, nkt, LANES, 2 * tk), lambda bi, i: (bi, 0, 0, 0))],
        out_specs=pl.BlockSpec((1, 1, nkt, tq, tk), lambda bi, i: (bi, i, 0, 0, 0)),
        out_shape=jax.ShapeDtypeStruct((b, s // tq, nkt, tq, tk), BF16),
        scratch_shapes=[pltpu.VMEM((nkt, tq, tk), I32),
                        pltpu.VMEM((nkt, tk, tq), I32),
                        pltpu.VMEM((N_IDX_HEADS, tq, LANES), F32),
                        pltpu.VMEM((SUBLANES, tq), I32),
                        pltpu.VMEM((SUBLANES, tq), I32)],
        compiler_params=_cparams(("parallel", "parallel")),
        name="dsa_index",
    )(pqi, widx, kbd)


def _flash_kernel(*refs, mode, nb, topb):
    if mode == "mask":
        q_ref, k_ref, v_ref, z_ref, d_ref, mask_ref, y_ref = refs[:7]
    else:
        q_ref, k_ref, v_ref, z_ref, d_ref, y_ref, km_ref = refs[:7]
    qs_ref, acc_ref, m_ref, p_ref, alpha_ref, x_ref = refs[7:]
    tq = ATT_T
    tk = ATT_TK
    hb = q_ref.shape[2] // HEAD_DIM
    rep = tk // LANES
    i = pl.program_id(2)
    jd = i // 2
    par = i % 2
    pm = par.astype(F32)
    zero_blk = jnp.zeros((LANES, LANES), F32)
    ones_blk = jnp.ones((tk, LANES), BF16)

    if mode == "sel":
        @pl.when(i == 0)
        def _():
            km_ref[...] = jnp.zeros(km_ref.shape, F32)
            for h in range(hb):
                for n in range(nb):
                    kblk = k_ref[0, n * tq:(n + 1) * tq, h * HEAD_DIM:(h + 1) * HEAD_DIM].astype(F32)
                    km_ref[h, n:n + 1, :] = jnp.mean(kblk, axis=0, keepdims=True)
        nbp = -(-nb // SUBLANES) * SUBLANES
        blk = lax.broadcasted_iota(I32, (nbp, tq), 0).astype(F32)
        lane = lax.broadcasted_iota(I32, (tk, LANES), 1)
        second_half = lax.broadcasted_iota(I32, (tk, LANES), 0) >= tq

    for h in range(hb):
        q = q_ref[0, :, h * HEAD_DIM:(h + 1) * HEAD_DIM]
        qsc = (q.astype(F32) * (ATTN_SCALE * LOG2E)).astype(BF16)
        if mode == "sel":
            gate = lax.dot_general(km_ref[h, 0:nbp, :].astype(BF16), q, (((1,), (1,)), ((), ())),
                                   preferred_element_type=F32)
            gate = jnp.where(blk < i.astype(F32), gate, -jnp.inf)
            sel = jnp.full((nbp, tq), MASK_NEG, F32)
            for _ in range(topb):
                gmax = jnp.max(gate, axis=0, keepdims=True)
                first = jnp.min(jnp.where(gate == gmax, blk, float(nbp)), axis=0, keepdims=True)
                hit = jnp.where(blk == first, jnp.where(gmax > -jnp.inf, 1.0, 0.0), 0.0)
                sel = jnp.where(hit > 0.0, 0.0, sel)
                gate = jnp.where(blk == first, -jnp.inf, gate)
            selb = jnp.concatenate([sel, jnp.full((LANES - nbp, tq), MASK_NEG, F32)], axis=0).T
            qs_ref[h] = jnp.concatenate([qsc, selb.astype(BF16)], axis=1)
        else:
            qs_ref[h] = qsc
        m_ref[h] = jnp.full((tq, LANES), MASK_NEG, F32)
        acc_ref[h] = jnp.zeros((tq, 2 * HEAD_DIM), F32)
        p_ref[h] = jnp.zeros((tq, tk), BF16)
        alpha_ref[h] = jnp.ones((tq, LANES), F32)

    def qk_stage(j, is_diag):
        r0 = pl.multiple_of(j * tk, tk)
        if mode == "sel":
            first = jnp.where(is_diag, jnp.where(par == 1, 2 * jd, -1), 2 * j)
            second = jnp.where(is_diag, -1, 2 * j + 1)
            onehot = jnp.where(lane == jnp.where(second_half, second, first), 1.0, 0.0).astype(BF16)
        for h in range(hb):
            kblk = k_ref[0, pl.ds(r0, tk), h * HEAD_DIM:(h + 1) * HEAD_DIM]
            if mode == "sel":
                kblk = jnp.concatenate([kblk, onehot], axis=1)
            x_ref[h] = lax.dot_general(qs_ref[h], kblk, (((1,), (1,)), ((), ())),
                                       preferred_element_type=F32)

    def bias_tile(h, kind):
        d0, d1 = d_ref[0, h], d_ref[1, h]
        if kind == "near":
            top = [zero_blk, zero_blk, zero_blk, d1 * (1.0 - pm)]
            bot = [zero_blk] * 4
        else:
            top = [d0 * (1.0 - pm), d1 * pm, d0 * pm, zero_blk]
            bot = [d1 * (1.0 - pm), d0 * (1.0 - pm), d1 * pm, d0 * pm]
        return jnp.concatenate([jnp.concatenate(top, axis=1), jnp.concatenate(bot, axis=1)], axis=0)

    def softmax_stage(j, kind):
        if mode == "mask":
            mask_tile = jnp.concatenate([mask_ref[0, 0, 2 * j], mask_ref[0, 0, 2 * j + 1]],
                                        axis=1).astype(F32)
        elif kind == "diag":
            row = lax.broadcasted_iota(I32, (tq, tk), 0)
            col = lax.broadcasted_iota(I32, (tq, tk), 1)
            causal = col <= row + tq * par
        for h in range(hb):
            x = x_ref[h]
            if mode == "mask":
                x = x + mask_tile
            if kind != "far":
                x = x + bias_tile(h, kind)
            if kind == "diag" and mode == "sel":
                x = jnp.where(causal, x, MASK_NEG)
            m_prev = m_ref[h]
            m_new = jnp.maximum(m_prev, jnp.max(x, axis=1, keepdims=True))
            alpha_ref[h] = jnp.exp2(m_prev - m_new)
            p_ref[h] = jnp.exp2(x - _tile_lanes(m_new, rep)).astype(BF16)
            m_ref[h] = m_new

    def pv_stage(j):
        r0 = pl.multiple_of(j * tk, tk)
        for h in range(hb):
            vblk = v_ref[0, pl.ds(r0, tk), h * HEAD_DIM:(h + 1) * HEAD_DIM]
            pv = jnp.dot(p_ref[h], jnp.concatenate([vblk, ones_blk], axis=1),
                         preferred_element_type=F32)
            acc_ref[h] = acc_ref[h] * _tile_lanes(alpha_ref[h], 2) + pv

    def step(j, kind):
        pv_stage(jnp.maximum(j - 1, 0))
        softmax_stage(j, kind)
        if kind != "diag":
            qk_stage(j + 1, j + 1 == jd)

    qk_stage(0, jd == 0)

    def far_body(j, carry):
        step(j, "far")
        return carry

    lax.fori_loop(0, jnp.maximum(jd - 1, 0), far_body, 0)

    @pl.when(jd >= 1)
    def _():
        step(jd - 1, "near")

    step(jd, "diag")
    pv_stage(jd)

    for h in range(hb):
        acc = acc_ref[h]
        o = acc[:, :HEAD_DIM] / acc[:, HEAD_DIM:]
        zz = z_ref[0, :, h * HEAD_DIM:(h + 1) * HEAD_DIM].astype(F32)
        y_ref[0, :, h * HEAD_DIM:(h + 1) * HEAD_DIM] = (o * (zz * _sigmoid(zz))).astype(BF16)


def _flash_attention(q, k, v, z, dtab, n_heads, name, mask=None):
    (qa, cq), (ka, ck), (va, cv), (za, cz) = q, k, v, z
    b, s, _ = qa.shape
    tq = ATT_T
    tk = ATT_TK
    assert s % tk == 0
    nb = s // tq
    hb = ATT_HB
    wg = hb * HEAD_DIM
    mode = "sel" if mask is None else "mask"
    kern = functools.partial(_flash_kernel, mode=mode, nb=nb, topb=min(MOBA_TOPB, nb - 1))
    in_specs = [pl.BlockSpec((1, tq, wg), lambda bi, g, i: (bi, i, cq + g)),
                pl.BlockSpec((1, s, wg), lambda bi, g, i: (bi, 0, ck + g)),
                pl.BlockSpec((1, s, wg), lambda bi, g, i: (bi, 0, cv + g)),
                pl.BlockSpec((1, tq, wg), lambda bi, g, i: (bi, i, cz + g)),
                pl.BlockSpec((2, hb, LANES, LANES), lambda bi, g, i: (0, g, 0, 0))]
    args = [qa, ka, va, za, dtab]
    scratch = []
    if mode == "mask":
        in_specs.append(pl.BlockSpec((1, 1, nb, tq, tq), lambda bi, g, i: (bi, i, 0, 0, 0)))
        args.append(mask)
        kq = HEAD_DIM
    else:
        scratch.append(pltpu.VMEM((hb, LANES, HEAD_DIM), F32))
        kq = 2 * HEAD_DIM
    scratch += [pltpu.VMEM((hb, tq, kq), BF16),
                pltpu.VMEM((hb, tq, 2 * HEAD_DIM), F32),
                pltpu.VMEM((hb, tq, LANES), F32),
                pltpu.VMEM((hb, tq, tk), BF16),
                pltpu.VMEM((hb, tq, LANES), F32),
                pltpu.VMEM((hb, tq, tk), F32)]
    return pl.pallas_call(
        kern,
        grid=(b, n_heads // hb, nb),
        in_specs=in_specs,
        out_specs=pl.BlockSpec((1, tq, wg), lambda bi, g, i: (bi, i, g)),
        out_shape=jax.ShapeDtypeStruct((b, s, n_heads * HEAD_DIM), BF16),
        scratch_shapes=scratch,
        compiler_params=_cparams(("parallel", "parallel", "arbitrary")),
        name=name,
    )(*args)


def _softplus(x):
    return jnp.maximum(x, 0.0) + jnp.log1p(jnp.exp(-jnp.abs(x)))


def _rglru_kernel(xr_ref, z_ref, cw_ref, cb_ref, wa_ref, ba_ref, wx_ref, bx_ref, lam_ref, y_ref,
                  xbuf_ref, a_ref, b_ref, h_ref):
    tc = xr_ref.shape[1]
    c = pl.program_id(2)
    halo = SUBLANES

    @pl.when(c == 0)
    def _():
        xbuf_ref[0:halo, :] = jnp.zeros((halo, xbuf_ref.shape[1]), F32)
        h_ref[...] = jnp.zeros(h_ref.shape, F32)

    @pl.when(c > 0)
    def _():
        xbuf_ref[0:halo, :] = xbuf_ref[tc:tc + halo, :]

    xbuf_ref[halo:halo + tc, :] = xr_ref[0]
    xfull = xbuf_ref[...]
    u = xfull * cw_ref[0:1, :]
    for j in range(1, CONV_W):
        u = xfull * cw_ref[j:j + 1, :] + pltpu.roll(u, 1, 0)
    xc = (u + cb_ref[...])[halo:, :]

    xcb = xc.astype(BF16)
    tr = jnp.tanh(jnp.dot(xcb, wa_ref[0], preferred_element_type=F32) + ba_ref[...])
    ti = jnp.tanh(jnp.dot(xcb, wx_ref[0], preferred_element_type=F32) + bx_ref[...])
    half_c = (-0.5 * RG_C) * _softplus(-lam_ref[...])
    a = jnp.exp(tr * half_c + half_c)
    one_m_a2 = 1.0 - a * a
    mult = jnp.where(one_m_a2 > 0.0, one_m_a2 * lax.rsqrt(one_m_a2), 0.0)
    bt = (mult * xc) * (0.5 * ti + 0.5)

    nseg = SUBLANES
    seg = tc // nseg
    pitch = seg + RG_SEG_PAD
    nslab = a_ref.shape[0]
    for sl in range(nslab):
        for sg in range(nseg):
            a_ref[sl, sg * pitch:sg * pitch + seg, :] = a[sg * seg:(sg + 1) * seg, sl * LANES:(sl + 1) * LANES]
            b_ref[sl, sg * pitch:sg * pitch + seg, :] = bt[sg * seg:(sg + 1) * seg, sl * LANES:(sl + 1) * LANES]

    def seg_step(j, carry):
        out = []
        for sl in range(nslab):
            hl, pr = carry[2 * sl], carry[2 * sl + 1]
            rows = pl.ds(j, nseg, stride=pitch)
            a8 = a_ref[sl, rows, :]
            hl = a8 * hl + b_ref[sl, rows, :]
            pr = a8 * pr
            b_ref[sl, rows, :] = hl
            a_ref[sl, rows, :] = pr
            out += [hl, pr]
        return tuple(out)

    init = tuple(v for _ in range(nslab)
                 for v in (jnp.zeros((nseg, LANES), F32), jnp.ones((nseg, LANES), F32)))
    ends = lax.fori_loop(0, seg, seg_step, init, unroll=RG_SCAN_UNROLL)

    zz = z_ref[0].astype(F32)
    gate = zz * _sigmoid(zz)
    for sl in range(nslab):
        h_end, p_end = ends[2 * sl], ends[2 * sl + 1]
        h_in = h_ref[:, sl * LANES:(sl + 1) * LANES]
        for sg in range(nseg):
            rows = slice(sg * pitch, sg * pitch + seg)
            hs = b_ref[sl, rows, :] + a_ref[sl, rows, :] * h_in
            y_ref[0, sg * seg:(sg + 1) * seg, sl * LANES:(sl + 1) * LANES] = (
                hs * gate[sg * seg:(sg + 1) * seg, sl * LANES:(sl + 1) * LANES]).astype(BF16)
            h_in = p_end[sg:sg + 1, :] * h_in + h_end[sg:sg + 1, :]
        h_ref[:, sl * LANES:(sl + 1) * LANES] = h_in


def _rglru(xr, z, conv_w, conv_b, wa_g, b_a, wx_g, b_x, lam):
    b, s, d = xr.shape
    g = wa_g.shape[0]
    cg = d // g
    tc = min(RG_TC, s)
    assert ((tc // SUBLANES + RG_SEG_PAD) // SUBLANES) % 2 == 1 and cg % LANES == 0
    row = lambda v: v.reshape(1, d).astype(F32)
    vec_spec = pl.BlockSpec((1, cg), lambda bi, gi, c: (0, gi))
    return pl.pallas_call(
        _rglru_kernel,
        grid=(b, g, s // tc),
        in_specs=[pl.BlockSpec((1, tc, cg), lambda bi, gi, c: (bi, c, gi)),
                  pl.BlockSpec((1, tc, cg), lambda bi, gi, c: (bi, c, gi)),
                  pl.BlockSpec((CONV_W, cg), lambda bi, gi, c: (0, gi)),
                  vec_spec,
                  pl.BlockSpec((1, cg, cg), lambda bi, gi, c: (gi, 0, 0)),
                  vec_spec,
                  pl.BlockSpec((1, cg, cg), lambda bi, gi, c: (gi, 0, 0)),
                  vec_spec,
                  vec_spec],
        out_specs=pl.BlockSpec((1, tc, cg), lambda bi, gi, c: (bi, c, gi)),
        out_shape=jax.ShapeDtypeStruct((b, s, d), BF16),
        scratch_shapes=[pltpu.VMEM((tc + SUBLANES, cg), F32),
                        pltpu.VMEM((cg // LANES, tc + SUBLANES * RG_SEG_PAD, LANES), F32),
                        pltpu.VMEM((cg // LANES, tc + SUBLANES * RG_SEG_PAD, LANES), F32),
                        pltpu.VMEM((1, cg), F32)],
        compiler_params=_cparams(("parallel", "parallel", "arbitrary")),
        name="rglru",
    )(xr, z, conv_w.astype(F32), row(conv_b), wa_g * 0.5, row(b_a) * 0.5, wx_g * 0.5, row(b_x) * 0.5,
      row(lam))


def _group_block_diag(w, pair):
    nb, k, _ = w.shape
    out = jnp.zeros((nb // pair, pair * k, pair * k), w.dtype)
    for p in range(pair):
        out = out.at[:, p * k:(p + 1) * k, p * k:(p + 1) * k].set(w[p::pair])
    return out


def _attention_layer(x2d, b, s, norm_g, w_in, kv_g, w_uk, w_uv, idx_g, idx_b, w_out, rel_bias):
    d_model = x2d.shape[1]
    n_heads = w_uk.shape[0]
    wq = n_heads * HEAD_DIM
    assert N_IDX_HEADS * IDX_DIM == wq
    sizes = (wq, KV_LORA, wq, IDX_DIM, N_IDX_HEADS, wq, wq, wq, 2 * wq)
    offs = [int(o) for o in np.concatenate([[0], np.cumsum(sizes)])]
    half = MM_BN // 2
    assert offs[2] % half == 0
    w_head = w_in.astype(BF16)
    w_tail = w_head[:, offs[5]:]
    zpad = lambda n: jnp.zeros((d_model, n), BF16)
    w_small = jnp.concatenate([w_head[:, offs[1]:offs[2]], w_head[:, offs[3]:offs[4]],
                               zpad(LANES - IDX_DIM), w_head[:, offs[4]:offs[5]],
                               zpad(LANES - N_IDX_HEADS)], axis=1)

    h = _rmsnorm(x2d, norm_g, BF16, "rmsnorm0")
    pqa = _matmul([h], [(w_head, 0, 0)], BF16, "proj_attn_qa", n=wq).reshape(b, s, wq)
    pqi = _matmul([h], [(w_head, 0, offs[2] // half)], BF16, "proj_attn_qi", n=wq, bn=half
                  ).reshape(b, s, wq)
    ptail = _matmul([h], [w_tail], BF16, "proj_attn_tail").reshape(b, s, -1)
    psmall = _matmul([h], [w_small], F32, "proj_attn_small").reshape(b, s, -1)

    ckv, kbd, widx = _dsa_prep(psmall, kv_g, idx_g, idx_b)
    mask = _dsa_index(pqi, 0, kbd, widx)
    w_kv = jnp.concatenate([jnp.transpose(w_uk, (1, 0, 2)).reshape(KV_LORA, wq),
                            jnp.transpose(w_uv, (1, 0, 2)).reshape(KV_LORA, wq)], axis=1).astype(BF16)
    kv = _matmul([ckv.reshape(b * s, KV_LORA)], [w_kv], BF16, "dsa_kv_up").reshape(b, s, 2 * wq)

    dtab_a = _bias_blocks(rel_bias[:, :n_heads])
    dtab_b = _bias_blocks(rel_bias[:, n_heads:])
    gpw = n_heads // ATT_HB
    ya = _flash_attention((pqa, 0), (kv, 0), (kv, gpw), (ptail, 3 * gpw), dtab_a, n_heads,
                          "dsa_attention", mask=mask)
    yb = _flash_attention((ptail, 0), (ptail, gpw), (ptail, 2 * gpw), (ptail, 4 * gpw), dtab_b,
                          n_heads, "moba_attention")

    w_out_b = w_out.astype(BF16)
    return _matmul([ya.reshape(b * s, wq), yb.reshape(b * s, wq)], [(w_out_b, 0, 0), (w_out_b, 1, 0)],
                   F32, "proj_attn_out", res=x2d, bn=half)


def _recurrent_layer(x2d, b, s, norm_g, w_in, conv_w, conv_b, w_a, b_a, w_x, b_x, lam, w_out):
    d_rnn = conv_b.shape[0]
    h = _rmsnorm(x2d, norm_g, BF16, "rmsnorm1")
    w_in_b = w_in.astype(BF16)
    assert d_rnn % MM_BN == 0
    xr = _matmul([h], [(w_in_b, 0, 0)], F32, "proj_rec_x", n=d_rnn)
    z = _matmul([h], [(w_in_b, 0, d_rnn // MM_BN)], BF16, "proj_rec_z", n=d_rnn)
    y = _rglru(xr.reshape(b, s, d_rnn), z.reshape(b, s, d_rnn), conv_w, conv_b,
               _group_block_diag(w_a.astype(BF16), RG_PAIR), b_a,
               _group_block_diag(w_x.astype(BF16), RG_PAIR), b_x, lam)
    return _matmul([y.reshape(b * s, d_rnn)], [w_out.astype(BF16)], F32, "proj_rec_out",
                   res=x2d, bn=MM_BN // 2)


def kernel(x, norm_g, final_g, rel_bias, attn_w_in, attn_kv_g, attn_w_uk, attn_w_uv, idx_k_g, idx_k_b,
           attn_w_out, rec_w_in, rec_conv_w, rec_conv_b, rec_w_a, rec_b_a, rec_w_x, rec_b_x,
           rec_lambda, rec_w_out):
    b, s, d = x.shape
    depth = norm_g.shape[0]
    x2d = x.reshape(b * s, d)
    for layer in range(depth):
        li = layer // 2
        if layer % 2 == 0:
            x2d = _attention_layer(x2d, b, s, norm_g[layer], attn_w_in[li], attn_kv_g[li],
                                   attn_w_uk[li], attn_w_uv[li], idx_k_g[li], idx_k_b[li],
                                   attn_w_out[li], rel_bias)
        else:
            x2d = _recurrent_layer(x2d, b, s, norm_g[layer], rec_w_in[li], rec_conv_w[li],
                                   rec_conv_b[li], rec_w_a[li], rec_b_a[li], rec_w_x[li],
                                   rec_b_x[li], rec_lambda[li], rec_w_out[li])
    return _rmsnorm(x2d, final_g, x.dtype, "rmsnorm_final").reshape(b, s, d)
```

```python
import functools
import math

import numpy as np
import jax
import jax.numpy as jnp
from jax import lax
from jax.experimental import pallas as pl
from jax.experimental.pallas import tpu as pltpu

F32 = jnp.float32
BF16 = jnp.bfloat16
I32 = jnp.int32

HEAD_DIM = 128
KV_LORA = 512
N_IDX_HEADS = 32
IDX_DIM = 64
TOPK_MAX = 256
MOBA_BLOCK = 256
MOBA_TOPB = 3
RG_BLOCKS = 16
CONV_W = 4
RG_C = 8.0
NUM_BUCKETS = 32
MAX_DISTANCE = 128
EPS = 1e-6
ATTN_SCALE = HEAD_DIM ** -0.5
IDX_SCALE = (N_IDX_HEADS ** -0.5) * (IDX_DIM ** -0.5)
LOG2E = math.log2(math.e)

LANES = 128
SUBLANES = 8
VMEM_LIMIT_BYTES = 56 * 2 ** 20
MASK_NEG = -1e30
INT_MIN = np.int32(-2 ** 31)

ROW_TILE = 256
MM_BM = 1024
MM_BN = 1024
ATT_T = 256
ATT_TK = 2 * ATT_T
ATT_HB = 4
RG_PAIR = 2
RG_TC = 1024
RG_SEG_PAD = 8
RG_SCAN_UNROLL = 4
assert ATT_T == MOBA_BLOCK and ATT_T == 2 * LANES


def _cparams(sem):
    return pltpu.CompilerParams(dimension_semantics=sem, vmem_limit_bytes=VMEM_LIMIT_BYTES)


def _sigmoid(x):
    return 0.5 * jnp.tanh(0.5 * x) + 0.5


def _tile_lanes(x, n):
    return x if n == 1 else jnp.concatenate([x] * n, axis=1)


def _rmsnorm_kernel(x_ref, g_ref, o_ref):
    x = x_ref[...]
    ms = jnp.mean(x * x, axis=-1, keepdims=True)
    o_ref[...] = ((x * lax.rsqrt(ms + EPS)) * g_ref[...]).astype(o_ref.dtype)


def _rmsnorm(x2d, g, out_dtype, name):
    m, d = x2d.shape
    bm = min(ROW_TILE, m)
    return pl.pallas_call(
        _rmsnorm_kernel,
        grid=(m // bm,),
        in_specs=[pl.BlockSpec((bm, d), lambda i: (i, 0)),
                  pl.BlockSpec((1, d), lambda i: (0, 0))],
        out_specs=pl.BlockSpec((bm, d), lambda i: (i, 0)),
        out_shape=jax.ShapeDtypeStruct((m, d), out_dtype),
        compiler_params=_cparams(("parallel",)),
        name=name,
    )(x2d, g.reshape(1, d).astype(F32))


def _mm_kernel(*refs, n_parts, has_res, epilogue):
    o_ref = refs[-1]
    acc = jnp.dot(refs[0][...], refs[n_parts][...], preferred_element_type=F32)
    for p in range(1, n_parts):
        acc = acc + jnp.dot(refs[p][...], refs[n_parts + p][...], preferred_element_type=F32)
    if has_res:
        acc = refs[2 * n_parts][...] + acc
    if epilogue is not None:
        acc = epilogue(acc)
    o_ref[...] = acc.astype(o_ref.dtype)


def _matmul(xs, ws, out_dtype, name, res=None, n=None, bm=MM_BM, bn=MM_BN, epilogue=None):
    ws = [w if isinstance(w, tuple) else (w, 0, 0) for w in ws]
    m = xs[0].shape[0]
    n = ws[0][0].shape[1] if n is None else n
    bm = min(bm, m)
    bn = min(bn, n)
    assert m % bm == 0 and n % bn == 0
    in_specs = [pl.BlockSpec((bm, x.shape[1]), lambda i, j: (i, 0)) for x in xs]
    for x, (w, rb, cb) in zip(xs, ws):
        assert w.shape[0] % x.shape[1] == 0 and (cb + n // bn) * bn <= w.shape[1]
        in_specs.append(pl.BlockSpec((x.shape[1], bn), lambda i, j, rb=rb, cb=cb: (rb, cb + j)))
    args = [*xs, *[w for w, _, _ in ws]]
    if res is not None:
        in_specs.append(pl.BlockSpec((bm, bn), lambda i, j: (i, j)))
        args.append(res)
    return pl.pallas_call(
        functools.partial(_mm_kernel, n_parts=len(xs), has_res=res is not None, epilogue=epilogue),
        grid=(m // bm, n // bn),
        in_specs=in_specs,
        out_specs=pl.BlockSpec((bm, bn), lambda i, j: (i, j)),
        out_shape=jax.ShapeDtypeStruct((m, n), out_dtype),
        compiler_params=_cparams(("parallel", "parallel")),
        name=name,
    )(*args)


def _silu(x):
    return x * _sigmoid(x)


def _t5_bucket_np(dist):
    d = np.maximum(dist, 0)
    max_exact = NUM_BUCKETS // 2
    d_f = np.maximum(d, 1).astype(np.float32)
    ratio = np.log(d_f / np.float32(max_exact)) / np.float32(math.log(MAX_DISTANCE / max_exact))
    large = max_exact + (ratio * np.float32(NUM_BUCKETS - max_exact)).astype(np.int32)
    large = np.minimum(large, NUM_BUCKETS - 1)
    return np.where(d < max_exact, d, large)


def _bias_blocks(tab):
    n = LANES
    assert np.all(_t5_bucket_np(np.arange(n, 64 * n)) == NUM_BUCKETS - 1)
    h = tab.shape[1]
    rel = (tab - tab[NUM_BUCKETS - 1]).astype(F32) * LOG2E
    out = []
    for off in (0, n):
        buckets = _t5_bucket_np(off + n - 1 - np.arange(2 * n - 1))
        onehot = np.zeros((2 * n, NUM_BUCKETS), np.float32)
        onehot[np.arange(2 * n - 1), buckets] = 1.0
        w = jnp.dot(jnp.asarray(onehot), rel, precision=lax.Precision.HIGHEST).T
        t = jnp.tile(w, (1, n))[:, :n * (2 * n - 1)].reshape(h, n, 2 * n - 1)
        out.append(t[:, :, n - 1:])
    return jnp.stack(out)


def _dsa_prep_kernel(p_ref, kvg_ref, ig_ref, ib_ref, ckv_ref, kbd_ref, w_ref):
    p = p_ref[0]
    ts = p.shape[0]
    c = p[:, :KV_LORA]
    cn = (c * lax.rsqrt(jnp.mean(c * c, axis=-1, keepdims=True) + EPS)) * kvg_ref[...]
    ckv_ref[0] = cn.astype(BF16)
    k = p[:, KV_LORA:KV_LORA + IDX_DIM]
    mu = jnp.mean(k, axis=-1, keepdims=True)
    var = jnp.mean(jnp.square(k - mu), axis=-1, keepdims=True)
    kn = ((k - mu) * lax.rsqrt(var + EPS)) * ig_ref[...] + ib_ref[...]
    zeros = jnp.zeros((ts, LANES - IDX_DIM), F32)
    top = jnp.concatenate([kn, zeros], axis=1).T
    bot = jnp.concatenate([zeros, kn], axis=1).T
    kbd_ref[0, 0] = jnp.concatenate([top, bot], axis=1).astype(BF16)
    w_ref[0] = p[:, KV_LORA + LANES:KV_LORA + LANES + N_IDX_HEADS] * IDX_SCALE


def _dsa_prep(ps, kv_g, idx_g, idx_b):
    b, s, wid = ps.shape
    ts = ATT_T
    nkt = s // ts
    return pl.pallas_call(
        _dsa_prep_kernel,
        grid=(b, nkt),
        in_specs=[pl.BlockSpec((1, ts, wid), lambda bi, i: (bi, i, 0)),
                  pl.BlockSpec((1, KV_LORA), lambda bi, i: (0, 0)),
                  pl.BlockSpec((1, IDX_DIM), lambda bi, i: (0, 0)),
                  pl.BlockSpec((1, IDX_DIM), lambda bi, i: (0, 0))],
        out_specs=[pl.BlockSpec((1, ts, KV_LORA), lambda bi, i: (bi, i, 0)),
                   pl.BlockSpec((1, 1, LANES, 2 * ts), lambda bi, i: (bi, i, 0, 0)),
                   pl.BlockSpec((1, ts, N_IDX_HEADS), lambda bi, i: (bi, i, 0))],
        out_shape=[jax.ShapeDtypeStruct((b, s, KV_LORA), BF16),
                   jax.ShapeDtypeStruct((b, nkt, LANES, 2 * ts), BF16),
                   jax.ShapeDtypeStruct((b, s, N_IDX_HEADS), F32)],
        compiler_params=_cparams(("parallel", "parallel")),
        name="dsa_prep",
    )(ps, kv_g.reshape(1, -1).astype(F32), idx_g.reshape(1, -1).astype(F32),
      idx_b.reshape(1, -1).astype(F32))


def _dsa_index_kernel(qi_ref, w_ref, kbd_ref, mask_ref, key_ref, keyt_ref, wb_ref, t_ref, j_ref,
                      *, topk, nkt_total, seq_bits):
    tq = tk = ATT_T
    rep = tk // LANES
    sl = SUBLANES
    i = pl.program_id(1)
    t0 = i * tq
    nkt = i + 1

    row = t0 + lax.broadcasted_iota(I32, (tq, tk), 0)
    col = lax.broadcasted_iota(I32, (tq, tk), 1)
    row_t = lax.broadcasted_iota(I32, (tk, tq), 0)
    col_t = t0 + lax.broadcasted_iota(I32, (tk, tq), 1)
    sub8 = lax.broadcasted_iota(I32, (sl, tq), 0)

    def to_key(score):
        score = jnp.where(score == 0.0, 0.0, score)
        bits = lax.bitcast_convert_type(score, I32)
        return bits ^ ((bits >> 31) & np.int32(0x7FFFFFFF))

    for h in range(N_IDX_HEADS):
        wb_ref[h] = jnp.broadcast_to(w_ref[0, :, h:h + 1], (tq, LANES))

    def idx_body(kt, carry):
        kb = kbd_ref[0, kt]
        acc = jnp.zeros((tq, tk), F32)
        for hp in range(N_IDX_HEADS // 2):
            rel = jnp.dot(qi_ref[0, :, hp * LANES:(hp + 1) * LANES], kb,
                          preferred_element_type=F32)
            acc = acc + jnp.maximum(rel[:, :tk], 0.0) * _tile_lanes(wb_ref[2 * hp], rep)
            acc = acc + jnp.maximum(rel[:, tk:], 0.0) * _tile_lanes(wb_ref[2 * hp + 1], rep)
        key_ref[kt] = jnp.where(kt * tk + col <= row, to_key(acc), INT_MIN)
        keyt_ref[kt] = jnp.where(kt * tk + row_t <= col_t, to_key(acc.T), INT_MIN)
        return carry

    lax.fori_loop(0, nkt, idx_body, 0)

    def count(pred_fn):
        def body(kt, c):
            xs = [pred_fn(keyt_ref[kt, r * sl:(r + 1) * sl, :], kt * tk + r * sl) for r in range(tk // sl)]
            while len(xs) > 1:
                xs = [xs[a] + xs[a + 1] for a in range(0, len(xs), 2)]
            return c + xs[0]
        c = lax.fori_loop(0, nkt, body, jnp.zeros((sl, tq), F32))
        return jnp.broadcast_to(jnp.sum(c, axis=0, keepdims=True), (sl, tq))

    t_ref[...] = jnp.full((sl, tq), INT_MIN, I32)

    def bit_body(it, carry):
        cand = t_ref[...] ^ jnp.left_shift(jnp.int32(1), 31 - it)
        n_ge = count(lambda k, s0: jnp.where(k >= cand, 1.0, 0.0))
        t_ref[...] = jnp.where(n_ge >= topk, cand, t_ref[...])
        return carry

    lax.fori_loop(0, 32, bit_body, 0)

    thr = t_ref[...]
    n_gt = count(lambda k, s0: jnp.where(k > thr, 1.0, 0.0))
    n_ge = count(lambda k, s0: jnp.where(k >= thr, 1.0, 0.0))
    need = topk - n_gt
    j_ref[...] = jnp.full((sl, tq), 2 ** 30, I32)

    @pl.when(jnp.max(n_ge) > topk)
    def _():
        j_ref[...] = jnp.zeros((sl, tq), I32)

        def jbit_body(it, carry):
            cand = j_ref[...] | jnp.left_shift(jnp.int32(1), seq_bits - 1 - it)
            n_lt = count(lambda k, s0: jnp.where(k == thr, jnp.where(s0 + sub8 < cand, 1.0, 0.0), 0.0))
            j_ref[...] = jnp.where(n_lt < need, cand, j_ref[...])
            return carry

        lax.fori_loop(0, seq_bits, jbit_body, 0)

    def per_row(v8):
        wide = jnp.broadcast_to(lax.bitcast_convert_type(v8[0:1], F32), (LANES, tq))
        return _tile_lanes(lax.bitcast_convert_type(wide.T, I32), rep)

    thr_t = per_row(thr)
    jmax_t = per_row(j_ref[...])

    def mask_body(kt, carry):
        k = key_ref[kt]
        s_idx = kt * tk + col
        keep_tie = jnp.where(k == thr_t, jnp.where(s_idx <= jmax_t, 0.0, MASK_NEG), MASK_NEG)
        keep = jnp.where(k > thr_t, 0.0, keep_tie)
        mask_ref[0, 0, kt] = jnp.where(s_idx <= row, keep, MASK_NEG).astype(BF16)
        return carry

    lax.fori_loop(0, nkt, mask_body, 0)

    def fill_body(kt, carry):
        mask_ref[0, 0, kt] = jnp.full((tq, tk), MASK_NEG, BF16)
        return carry

    lax.fori_loop(nkt, nkt_total, fill_body, 0)


def _dsa_index(pqi, col_qi, kbd, widx):
    b, s, _ = pqi.shape
    tq = tk = ATT_T
    nkt = s // tk
    wq = N_IDX_HEADS * IDX_DIM
    topk = min(TOPK_MAX, s // 4)
    kern = functools.partial(_dsa_index_kernel, topk=topk, nkt_total=nkt,
                             seq_bits=max(1, (s - 1).bit_length()))
    return pl.pallas_call(
        kern,
        grid=(b, s // tq),
        in_specs=[pl.BlockSpec((1, tq, wq), lambda bi, i: (bi, i, col_qi)),
                  pl.BlockSpec((1, tq, N_IDX_HEADS), lambda bi, i: (bi, i, 0)),
                  pl.BlockSpec((1, nkt, LANES, 2 * tk), lambda bi, i: (bi, 0, 0, 0))],
        out_specs=pl.BlockSpec((1, 1, nkt, tq, tk), lambda bi, i: (bi, i, 0, 0, 0)),
        out_shape=jax.ShapeDtypeStruct((b, s // tq, nkt, tq, tk), BF16),
        scratch_shapes=[pltpu.VMEM((nkt, tq, tk), I32),
                        pltpu.VMEM((nkt, tk, tq), I32),
                        pltpu.VMEM((N_IDX_HEADS, tq, LANES), F32),
                        pltpu.VMEM((SUBLANES, tq), I32),
                        pltpu.VMEM((SUBLANES, tq), I32)],
        compiler_params=_cparams(("parallel", "parallel")),
        name="dsa_index",
    )(pqi, widx, kbd)


def _flash_kernel(*refs, mode, nb, topb):
    if mode == "mask":
        q_ref, k_ref, v_ref, z_ref, d_ref, mask_ref, y_ref = refs[:7]
    else:
        q_ref, k_ref, v_ref, z_ref, d_ref, y_ref, km_ref = refs[:7]
    qs_ref, acc_ref, m_ref, p_ref, alpha_ref, x_ref = refs[7:]
    tq = ATT_T
    tk = ATT_TK
    hb = q_ref.shape[2] // HEAD_DIM
    rep = tk // LANES
    i = pl.program_id(2)
    jd = i // 2
    par = i % 2
    pm = par.astype(F32)
    zero_blk = jnp.zeros((LANES, LANES), F32)
    ones_blk = jnp.ones((tk, LANES), BF16)

    if mode == "sel":
        @pl.when(i == 0)
        def _():
            km_ref[...] = jnp.zeros(km_ref.shape, F32)
            for h in range(hb):
                for n in range(nb):
                    kblk = k_ref[0, n * tq:(n + 1) * tq, h * HEAD_DIM:(h + 1) * HEAD_DIM].astype(F32)
                    km_ref[h, n:n + 1, :] = jnp.mean(kblk, axis=0, keepdims=True)
        nbp = -(-nb // SUBLANES) * SUBLANES
        blk = lax.broadcasted_iota(I32, (nbp, tq), 0).astype(F32)
        lane = lax.broadcasted_iota(I32, (tk, LANES), 1)
        second_half = lax.broadcasted_iota(I32, (tk, LANES), 0) >= tq

    for h in range(hb):
        q = q_ref[0, :, h * HEAD_DIM:(h + 1) * HEAD_DIM]
        qsc = (q.astype(F32) * (ATTN_SCALE * LOG2E)).astype(BF16)
        if mode == "sel":
            gate = lax.dot_general(km_ref[h, 0:nbp, :].astype(BF16), q, (((1,), (1,)), ((), ())),
                                   preferred_element_type=F32)
            gate = jnp.where(blk < i.astype(F32), gate, -jnp.inf)
            sel = jnp.full((nbp, tq), MASK_NEG, F32)
            for _ in range(topb):
                gmax = jnp.max(gate, axis=0, keepdims=True)
                first = jnp.min(jnp.where(gate == gmax, blk, float(nbp)), axis=0, keepdims=True)
                hit = jnp.where(blk == first, jnp.where(gmax > -jnp.inf, 1.0, 0.0), 0.0)
                sel = jnp.where(hit > 0.0, 0.0, sel)
                gate = jnp.where(blk == first, -jnp.inf, gate)
            selb = jnp.concatenate([sel, jnp.full((LANES - nbp, tq), MASK_NEG, F32)], axis=0).T
            qs_ref[h] = jnp.concatenate([qsc, selb.astype(BF16)], axis=1)
        else:
            qs_ref[h] = qsc
        m_ref[h] = jnp.full((tq, LANES), MASK_NEG, F32)
        acc_ref[h] = jnp.zeros((tq, 2 * HEAD_DIM), F32)
        p_ref[h] = jnp.zeros((tq, tk), BF16)
        alpha_ref[h] = jnp.ones((tq, LANES), F32)

    def qk_stage(j, is_diag):
        r0 = pl.multiple_of(j * tk, tk)
        if mode == "sel":
            first = jnp.where(is_diag, jnp.where(par == 1, 2 * jd, -1), 2 * j)
            second = jnp.where(is_diag, -1, 2 * j + 1)
            onehot = jnp.where(lane == jnp.where(second_half, second, first), 1.0, 0.0).astype(BF16)
        for h in range(hb):
            kblk = k_ref[0, pl.ds(r0, tk), h * HEAD_DIM:(h + 1) * HEAD_DIM]
            if mode == "sel":
                kblk = jnp.concatenate([kblk, onehot], axis=1)
            x_ref[h] = lax.dot_general(qs_ref[h], kblk, (((1,), (1,)), ((), ())),
                                       preferred_element_type=F32)

    def bias_tile(h, kind):
        d0, d1 = d_ref[0, h], d_ref[1, h]
        if kind == "near":
            top = [zero_blk, zero_blk, zero_blk, d1 * (1.0 - pm)]
            bot = [zero_blk] * 4
        else:
            top = [d0 * (1.0 - pm), d1 * pm, d0 * pm, zero_blk]
            bot = [d1 * (1.0 - pm), d0 * (1.0 - pm), d1 * pm, d0 * pm]
        return jnp.concatenate([jnp.concatenate(top, axis=1), jnp.concatenate(bot, axis=1)], axis=0)

    def softmax_stage(j, kind):
        if mode == "mask":
            mask_tile = jnp.concatenate([mask_ref[0, 0, 2 * j], mask_ref[0, 0, 2 * j + 1]],
                                        axis=1).astype(F32)
        elif kind == "diag":
            row = lax.broadcasted_iota(I32, (tq, tk), 0)
            col = lax.broadcasted_iota(I32, (tq, tk), 1)
            causal = col <= row + tq * par
        for h in range(hb):
            x = x_ref[h]
            if mode == "mask":
                x = x + mask_tile
            if kind != "far":
                x = x + bias_tile(h, kind)
            if kind == "diag" and mode == "sel":
                x = jnp.where(causal, x, MASK_NEG)
            m_prev = m_ref[h]
            m_new = jnp.maximum(m_prev, jnp.max(x, axis=1, keepdims=True))
            alpha_ref[h] = jnp.exp2(m_prev - m_new)
            p_ref[h] = jnp.exp2(x - _tile_lanes(m_new, rep)).astype(BF16)
            m_ref[h] = m_new

    def pv_stage(j):
        r0 = pl.multiple_of(j * tk, tk)
        for h in range(hb):
            vblk = v_ref[0, pl.ds(r0, tk), h * HEAD_DIM:(h + 1) * HEAD_DIM]
            pv = jnp.dot(p_ref[h], jnp.concatenate([vblk, ones_blk], axis=1),
                         preferred_element_type=F32)
            acc_ref[h] = acc_ref[h] * _tile_lanes(alpha_ref[h], 2) + pv

    def step(j, kind):
        pv_stage(jnp.maximum(j - 1, 0))
        softmax_stage(j, kind)
        if kind != "diag":
            qk_stage(j + 1, j + 1 == jd)

    qk_stage(0, jd == 0)

    def far_body(j, carry):
        step(j, "far")
        return carry

    lax.fori_loop(0, jnp.maximum(jd - 1, 0), far_body, 0)

    @pl.when(jd >= 1)
    def _():
        step(jd - 1, "near")

    step(jd, "diag")
    pv_stage(jd)

    for h in range(hb):
        acc = acc_ref[h]
        o = acc[:, :HEAD_DIM] / acc[:, HEAD_DIM:]
        zz = z_ref[0, :, h * HEAD_DIM:(h + 1) * HEAD_DIM].astype(F32)
        y_ref[0, :, h * HEAD_DIM:(h + 1) * HEAD_DIM] = (o * (zz * _sigmoid(zz))).astype(BF16)


def _flash_attention(q, k, v, z, dtab, n_heads, name, mask=None):
    (qa, cq), (ka, ck), (va, cv), (za, cz) = q, k, v, z
    b, s, _ = qa.shape
    tq = ATT_T
    tk = ATT_TK
    assert s % tk == 0
    nb = s // tq
    hb = ATT_HB
    wg = hb * HEAD_DIM
    mode = "sel" if mask is None else "mask"
    kern = functools.partial(_flash_kernel, mode=mode, nb=nb, topb=min(MOBA_TOPB, nb - 1))
    in_specs = [pl.BlockSpec((1, tq, wg), lambda bi, g, i: (bi, i, cq + g)),
                pl.BlockSpec((1, s, wg), lambda bi, g, i: (bi, 0, ck + g)),
                pl.BlockSpec((1, s, wg), lambda bi, g, i: (bi, 0, cv + g)),
                pl.BlockSpec((1, tq, wg), lambda bi, g, i: (bi, i, cz + g)),
                pl.BlockSpec((2, hb, LANES, LANES), lambda bi, g, i: (0, g, 0, 0))]
    args = [qa, ka, va, za, dtab]
    scratch = []
    if mode == "mask":
        in_specs.append(pl.BlockSpec((1, 1, nb, tq, tq), lambda bi, g, i: (bi, i, 0, 0, 0)))
        args.append(mask)
        kq = HEAD_DIM
    else:
        scratch.append(pltpu.VMEM((hb, LANES, HEAD_DIM), F32))
        kq = 2 * HEAD_DIM
    scratch += [pltpu.VMEM((hb, tq, kq), BF16),
                pltpu.VMEM((hb, tq, 2 * HEAD_DIM), F32),
                pltpu.VMEM((hb, tq, LANES), F32),
                pltpu.VMEM((hb, tq, tk), BF16),
                pltpu.VMEM((hb, tq, LANES), F32),
                pltpu.VMEM((hb, tq, tk), F32)]
    return pl.pallas_call(
        kern,
        grid=(b, n_heads // hb, nb),
        in_specs=in_specs,
        out_specs=pl.BlockSpec((1, tq, wg), lambda bi, g, i: (bi, i, g)),
        out_shape=jax.ShapeDtypeStruct((b, s, n_heads * HEAD_DIM), BF16),
        scratch_shapes=scratch,
        compiler_params=_cparams(("parallel", "parallel", "arbitrary")),
        name=name,
    )(*args)


def _softplus(x):
    return jnp.maximum(x, 0.0) + jnp.log1p(jnp.exp(-jnp.abs(x)))


def _rglru_kernel(xr_ref, gz_ref, cw_ref, cb_ref, wa_ref, ba_ref, wx_ref, bx_ref, lam_ref, y_ref,
                  xbuf_ref, a_ref, b_ref, h_ref):
    tc = xr_ref.shape[1]
    c = pl.program_id(2)
    halo = SUBLANES

    @pl.when(c == 0)
    def _():
        xbuf_ref[0:halo, :] = jnp.zeros((halo, xbuf_ref.shape[1]), F32)
        h_ref[...] = jnp.zeros(h_ref.shape, F32)

    @pl.when(c > 0)
    def _():
        xbuf_ref[0:halo, :] = xbuf_ref[tc:tc + halo, :]

    xbuf_ref[halo:halo + tc, :] = xr_ref[0]
    xfull = xbuf_ref[...]
    u = xfull * cw_ref[0:1, :]
    for j in range(1, CONV_W):
        u = xfull * cw_ref[j:j + 1, :] + pltpu.roll(u, 1, 0)
    xc = (u + cb_ref[...])[halo:, :]

    xcb = xc.astype(BF16)
    tr = jnp.tanh(jnp.dot(xcb, wa_ref[0], preferred_element_type=F32) + ba_ref[...])
    ti = jnp.tanh(jnp.dot(xcb, wx_ref[0], preferred_element_type=F32) + bx_ref[...])
    half_c = (-0.5 * RG_C * LOG2E) * _softplus(-lam_ref[...])
    a = jnp.exp2(tr * half_c + half_c)
    one_m_a2 = 1.0 - a * a
    mult = jnp.where(one_m_a2 > 0.0, one_m_a2 * lax.rsqrt(one_m_a2), 0.0)
    bt = (mult * xc) * (0.5 * ti + 0.5)

    nseg = SUBLANES
    seg = tc // nseg
    pitch = seg + RG_SEG_PAD
    nslab = a_ref.shape[0]
    for sl in range(nslab):
        for sg in range(nseg):
            a_ref[sl, sg * pitch:sg * pitch + seg, :] = a[sg * seg:(sg + 1) * seg, sl * LANES:(sl + 1) * LANES]
            b_ref[sl, sg * pitch:sg * pitch + seg, :] = bt[sg * seg:(sg + 1) * seg, sl * LANES:(sl + 1) * LANES]

    def seg_step(j, carry):
        out = []
        for sl in range(nslab):
            hl, pr = carry[2 * sl], carry[2 * sl + 1]
            rows = pl.ds(j, nseg, stride=pitch)
            a8 = a_ref[sl, rows, :]
            hl = a8 * hl + b_ref[sl, rows, :]
            pr = a8 * pr
            b_ref[sl, rows, :] = hl
            a_ref[sl, rows, :] = pr
            out += [hl, pr]
        return tuple(out)

    init = tuple(v for _ in range(nslab)
                 for v in (jnp.zeros((nseg, LANES), F32), jnp.ones((nseg, LANES), F32)))
    ends = lax.fori_loop(0, seg, seg_step, init, unroll=RG_SCAN_UNROLL)

    gate = gz_ref[0].astype(F32)
    for sl in range(nslab):
        h_end, p_end = ends[2 * sl], ends[2 * sl + 1]
        h_in = h_ref[:, sl * LANES:(sl + 1) * LANES]
        for sg in range(nseg):
            rows = slice(sg * pitch, sg * pitch + seg)
            hs = b_ref[sl, rows, :] + a_ref[sl, rows, :] * h_in
            y_ref[0, sg * seg:(sg + 1) * seg, sl * LANES:(sl + 1) * LANES] = (
                hs * gate[sg * seg:(sg + 1) * seg, sl * LANES:(sl + 1) * LANES]).astype(BF16)
            h_in = p_end[sg:sg + 1, :] * h_in + h_end[sg:sg + 1, :]
        h_ref[:, sl * LANES:(sl + 1) * LANES] = h_in


def _rglru(xr, gz, conv_w, conv_b, wa_g, b_a, wx_g, b_x, lam):
    b, s, d = xr.shape
    g = wa_g.shape[0]
    cg = d // g
    tc = min(RG_TC, s)
    assert ((tc // SUBLANES + RG_SEG_PAD) // SUBLANES) % 2 == 1 and cg % LANES == 0
    row = lambda v: v.reshape(1, d).astype(F32)
    vec_spec = pl.BlockSpec((1, cg), lambda bi, gi, c: (0, gi))
    return pl.pallas_call(
        _rglru_kernel,
        grid=(b, g, s // tc),
        in_specs=[pl.BlockSpec((1, tc, cg), lambda bi, gi, c: (bi, c, gi)),
                  pl.BlockSpec((1, tc, cg), lambda bi, gi, c: (bi, c, gi)),
                  pl.BlockSpec((CONV_W, cg), lambda bi, gi, c: (0, gi)),
                  vec_spec,
                  pl.BlockSpec((1, cg, cg), lambda bi, gi, c: (gi, 0, 0)),
                  vec_spec,
                  pl.BlockSpec((1, cg, cg), lambda bi, gi, c: (gi, 0, 0)),
                  vec_spec,
                  vec_spec],
        out_specs=pl.BlockSpec((1, tc, cg), lambda bi, gi, c: (bi, c, gi)),
        out_shape=jax.ShapeDtypeStruct((b, s, d), BF16),
        scratch_shapes=[pltpu.VMEM((tc + SUBLANES, cg), F32),
                        pltpu.VMEM((cg // LANES, tc + SUBLANES * RG_SEG_PAD, LANES), F32),
                        pltpu.VMEM((cg // LANES, tc + SUBLANES * RG_SEG_PAD, LANES), F32),
                        pltpu.VMEM((1, cg), F32)],
        compiler_params=_cparams(("parallel", "parallel", "arbitrary")),
        name="rglru",
    )(xr, gz, conv_w.astype(F32), row(conv_b), wa_g * 0.5, row(b_a) * 0.5, wx_g * 0.5, row(b_x) * 0.5,
      row(lam))


def _group_block_diag(w, pair):
    nb, k, _ = w.shape
    out = None
    for p in range(pair):
        lo, hi = p * k, (pair - 1 - p) * k
        blk = jnp.pad(w[p::pair], ((0, 0), (lo, hi), (lo, hi)))
        out = blk if out is None else out + blk
    return out


def _attention_layer(x2d, b, s, norm_g, w_in, kv_g, w_uk, w_uv, idx_g, idx_b, w_out, rel_bias):
    d_model = x2d.shape[1]
    n_heads = w_uk.shape[0]
    wq = n_heads * HEAD_DIM
    assert N_IDX_HEADS * IDX_DIM == wq
    sizes = (wq, KV_LORA, wq, IDX_DIM, N_IDX_HEADS, wq, wq, wq, 2 * wq)
    offs = [int(o) for o in np.concatenate([[0], np.cumsum(sizes)])]
    half = MM_BN // 2
    assert offs[2] % half == 0
    w_head = w_in.astype(BF16)
    w_tail = w_head[:, offs[5]:]
    zpad = lambda n: jnp.zeros((d_model, n), BF16)
    w_small = jnp.concatenate([w_head[:, offs[1]:offs[2]], w_head[:, offs[3]:offs[4]],
                               zpad(LANES - IDX_DIM), w_head[:, offs[4]:offs[5]],
                               zpad(LANES - N_IDX_HEADS)], axis=1)

    h = _rmsnorm(x2d, norm_g, BF16, "rmsnorm0")
    pqa = _matmul([h], [(w_head, 0, 0)], BF16, "proj_attn_qa", n=wq).reshape(b, s, wq)
    pqi = _matmul([h], [(w_head, 0, offs[2] // half)], BF16, "proj_attn_qi", n=wq, bn=half
                  ).reshape(b, s, wq)
    ptail = _matmul([h], [w_tail], BF16, "proj_attn_tail").reshape(b, s, -1)
    psmall = _matmul([h], [w_small], F32, "proj_attn_small").reshape(b, s, -1)

    ckv, kbd, widx = _dsa_prep(psmall, kv_g, idx_g, idx_b)
    mask = _dsa_index(pqi, 0, kbd, widx)
    w_kv = jnp.concatenate([jnp.transpose(w_uk, (1, 0, 2)).reshape(KV_LORA, wq),
                            jnp.transpose(w_uv, (1, 0, 2)).reshape(KV_LORA, wq)], axis=1).astype(BF16)
    kv = _matmul([ckv.reshape(b * s, KV_LORA)], [w_kv], BF16, "dsa_kv_up", bm=2 * MM_BM).reshape(b, s, 2 * wq)

    dtab_a = _bias_blocks(rel_bias[:, :n_heads])
    dtab_b = _bias_blocks(rel_bias[:, n_heads:])
    gpw = n_heads // ATT_HB
    ya = _flash_attention((pqa, 0), (kv, 0), (kv, gpw), (ptail, 3 * gpw), dtab_a, n_heads,
                          "dsa_attention", mask=mask)
    yb = _flash_attention((ptail, 0), (ptail, gpw), (ptail, 2 * gpw), (ptail, 4 * gpw), dtab_b,
                          n_heads, "moba_attention")

    w_out_b = w_out.astype(BF16)
    return _matmul([ya.reshape(b * s, wq), yb.reshape(b * s, wq)], [(w_out_b, 0, 0), (w_out_b, 1, 0)],
                   F32, "proj_attn_out", res=x2d, bn=half)


def _recurrent_layer(x2d, b, s, norm_g, w_in, conv_w, conv_b, w_a, b_a, w_x, b_x, lam, w_out):
    d_rnn = conv_b.shape[0]
    h = _rmsnorm(x2d, norm_g, BF16, "rmsnorm1")
    w_in_b = w_in.astype(BF16)
    assert d_rnn % MM_BN == 0
    xr = _matmul([h], [(w_in_b, 0, 0)], F32, "proj_rec_x", n=d_rnn)
    gz = _matmul([h], [(w_in_b, 0, d_rnn // MM_BN)], BF16, "proj_rec_z", n=d_rnn, epilogue=_silu)
    y = _rglru(xr.reshape(b, s, d_rnn), gz.reshape(b, s, d_rnn), conv_w, conv_b,
               _group_block_diag(w_a.astype(BF16), RG_PAIR), b_a,
               _group_block_diag(w_x.astype(BF16), RG_PAIR), b_x, lam)
    return _matmul([y.reshape(b * s, d_rnn)], [w_out.astype(BF16)], F32, "proj_rec_out",
                   res=x2d, bn=MM_BN // 2)


def kernel(x, norm_g, final_g, rel_bias, attn_w_in, attn_kv_g, attn_w_uk, attn_w_uv, idx_k_g, idx_k_b,
           attn_w_out, rec_w_in, rec_conv_w, rec_conv_b, rec_w_a, rec_b_a, rec_w_x, rec_b_x,
           rec_lambda, rec_w_out):
    b, s, d = x.shape
    depth = norm_g.shape[0]
    x2d = x.reshape(b * s, d)
    for layer in range(depth):
        li = layer // 2
        if layer % 2 == 0:
            x2d = _attention_layer(x2d, b, s, norm_g[layer], attn_w_in[li], attn_kv_g[li],
                                   attn_w_uk[li], attn_w_uv[li], idx_k_g[li], idx_k_b[li],
                                   attn_w_out[li], rel_bias)
        else:
            x2d = _recurrent_layer(x2d, b, s, norm_g[layer], rec_w_in[li], rec_conv_w[li],
                                   rec_conv_b[li], rec_w_a[li], rec_b_a[li], rec_w_x[li],
                                   rec_b_x[li], rec_lambda[li], rec_w_out[li])
    return _rmsnorm(x2d, final_g, x.dtype, "rmsnorm_final").reshape(b, s, d)
```

```python
import functools
import math

import numpy as np
import jax
import jax.numpy as jnp
from jax import lax
from jax.experimental import pallas as pl
from jax.experimental.pallas import tpu as pltpu

F32 = jnp.float32
BF16 = jnp.bfloat16
I32 = jnp.int32

HEAD_DIM = 128
KV_LORA = 512
N_IDX_HEADS = 32
IDX_DIM = 64
TOPK_MAX = 256
MOBA_BLOCK = 256
MOBA_TOPB = 3
RG_BLOCKS = 16
CONV_W = 4
RG_C = 8.0
NUM_BUCKETS = 32
MAX_DISTANCE = 128
EPS = 1e-6
ATTN_SCALE = HEAD_DIM ** -0.5
IDX_SCALE = (N_IDX_HEADS ** -0.5) * (IDX_DIM ** -0.5)
LOG2E = math.log2(math.e)

LANES = 128
SUBLANES = 8
WT_ROW_ALIGN = 16
VMEM_LIMIT_BYTES = 56 * 2 ** 20
MASK_NEG = -1e30
INT_MIN = np.int32(-2 ** 31)

ROW_TILE = 256
MM_BM = 1024
MM_BN = 1024
ATT_T = 256
ATT_TK = 2 * ATT_T
ATT_HB = 4
RG_PAIR = 2
RG_TC = 1024
RG_SEG_PAD = 8
RG_SCAN_UNROLL = 4
assert ATT_T == MOBA_BLOCK and ATT_T == 2 * LANES


def _cparams(sem):
    return pltpu.CompilerParams(dimension_semantics=sem, vmem_limit_bytes=VMEM_LIMIT_BYTES)


def _sigmoid(x):
    return 0.5 * jnp.tanh(0.5 * x) + 0.5


def _tile_lanes(x, n):
    return x if n == 1 else jnp.concatenate([x] * n, axis=1)


def _rmsnorm_kernel(x_ref, g_ref, o_ref):
    x = x_ref[...]
    ms = jnp.mean(x * x, axis=-1, keepdims=True)
    o_ref[...] = ((x * lax.rsqrt(ms + EPS)) * g_ref[...]).astype(o_ref.dtype)


def _rmsnorm(x2d, g, out_dtype, name):
    m, d = x2d.shape
    bm = min(ROW_TILE, m)
    return pl.pallas_call(
        _rmsnorm_kernel,
        grid=(m // bm,),
        in_specs=[pl.BlockSpec((bm, d), lambda i: (i, 0)),
                  pl.BlockSpec((1, d), lambda i: (0, 0))],
        out_specs=pl.BlockSpec((bm, d), lambda i: (i, 0)),
        out_shape=jax.ShapeDtypeStruct((m, d), out_dtype),
        compiler_params=_cparams(("parallel",)),
        name=name,
    )(x2d, g.reshape(1, d).astype(F32))


def _mm_kernel(*refs, n_parts, has_res, epilogue):
    o_ref = refs[-1]
    acc = jnp.dot(refs[0][...], refs[n_parts][...], preferred_element_type=F32)
    for p in range(1, n_parts):
        acc = acc + jnp.dot(refs[p][...], refs[n_parts + p][...], preferred_element_type=F32)
    if has_res:
        acc = refs[2 * n_parts][...] + acc
    if epilogue is not None:
        acc = epilogue(acc)
    o_ref[...] = acc.astype(o_ref.dtype)


def _matmul(xs, ws, out_dtype, name, res=None, n=None, bm=MM_BM, bn=MM_BN, epilogue=None):
    ws = [w if isinstance(w, tuple) else (w, 0, 0) for w in ws]
    m = xs[0].shape[0]
    n = ws[0][0].shape[1] if n is None else n
    bm = min(bm, m)
    bn = min(bn, n)
    assert m % bm == 0 and n % bn == 0
    in_specs = [pl.BlockSpec((bm, x.shape[1]), lambda i, j: (i, 0)) for x in xs]
    for x, (w, rb, cb) in zip(xs, ws):
        assert w.shape[0] % x.shape[1] == 0 and (cb + n // bn) * bn <= w.shape[1]
        in_specs.append(pl.BlockSpec((x.shape[1], bn), lambda i, j, rb=rb, cb=cb: (rb, cb + j)))
    args = [*xs, *[w for w, _, _ in ws]]
    if res is not None:
        in_specs.append(pl.BlockSpec((bm, bn), lambda i, j: (i, j)))
        args.append(res)
    return pl.pallas_call(
        functools.partial(_mm_kernel, n_parts=len(xs), has_res=res is not None, epilogue=epilogue),
        grid=(m // bm, n // bn),
        in_specs=in_specs,
        out_specs=pl.BlockSpec((bm, bn), lambda i, j: (i, j)),
        out_shape=jax.ShapeDtypeStruct((m, n), out_dtype),
        compiler_params=_cparams(("parallel", "parallel")),
        name=name,
    )(*args)


def _silu(x):
    return x * _sigmoid(x)


def _mm_nt_kernel(x_ref, wt_ref, o_ref, *, epilogue):
    acc = lax.dot_general(x_ref[...], wt_ref[...], (((1,), (1,)), ((), ())), preferred_element_type=F32)
    if epilogue is not None:
        acc = epilogue(acc)
    o_ref[...] = acc.astype(o_ref.dtype)


def _matmul_nt(x, wt, row0, n, out_dtype, name, bn=MM_BN, epilogue=None):
    m, k = x.shape
    bm = min(MM_BM, m)
    bn = min(bn, n)
    assert m % bm == 0 and n % bn == 0 and row0 % WT_ROW_ALIGN == 0 and bn % WT_ROW_ALIGN == 0
    assert row0 + n <= wt.shape[0] and wt.shape[1] == k
    return pl.pallas_call(
        functools.partial(_mm_nt_kernel, epilogue=epilogue),
        grid=(m // bm, n // bn),
        in_specs=[pl.BlockSpec((bm, k), lambda i, j: (i, 0)),
                  pl.BlockSpec((pl.Element(bn), pl.Element(k)),
                               lambda i, j: (pl.multiple_of(row0 + j * bn, WT_ROW_ALIGN), 0))],
        out_specs=pl.BlockSpec((bm, bn), lambda i, j: (i, j)),
        out_shape=jax.ShapeDtypeStruct((m, n), out_dtype),
        compiler_params=_cparams(("parallel", "parallel")),
        name=name,
    )(x, wt)


def _t5_bucket_np(dist):
    d = np.maximum(dist, 0)
    max_exact = NUM_BUCKETS // 2
    d_f = np.maximum(d, 1).astype(np.float32)
    ratio = np.log(d_f / np.float32(max_exact)) / np.float32(math.log(MAX_DISTANCE / max_exact))
    large = max_exact + (ratio * np.float32(NUM_BUCKETS - max_exact)).astype(np.int32)
    large = np.minimum(large, NUM_BUCKETS - 1)
    return np.where(d < max_exact, d, large)


def _bias_blocks(tab):
    n = LANES
    assert np.all(_t5_bucket_np(np.arange(n, 64 * n)) == NUM_BUCKETS - 1)
    h = tab.shape[1]
    rel = (tab - tab[NUM_BUCKETS - 1]).astype(F32) * LOG2E
    out = []
    for off in (0, n):
        buckets = _t5_bucket_np(off + n - 1 - np.arange(2 * n - 1))
        onehot = np.zeros((2 * n, NUM_BUCKETS), np.float32)
        onehot[np.arange(2 * n - 1), buckets] = 1.0
        w = jnp.dot(jnp.asarray(onehot), rel, precision=lax.Precision.HIGHEST).T
        t = jnp.tile(w, (1, n))[:, :n * (2 * n - 1)].reshape(h, n, 2 * n - 1)
        out.append(t[:, :, n - 1:])
    return jnp.stack(out)


def _dsa_prep_kernel(p_ref, kvg_ref, ig_ref, ib_ref, ckv_ref, kbd_ref, w_ref):
    p = p_ref[0]
    ts = p.shape[0]
    c = p[:, :KV_LORA]
    cn = (c * lax.rsqrt(jnp.mean(c * c, axis=-1, keepdims=True) + EPS)) * kvg_ref[...]
    ckv_ref[0] = cn.astype(BF16)
    k = p[:, KV_LORA:KV_LORA + IDX_DIM]
    mu = jnp.mean(k, axis=-1, keepdims=True)
    var = jnp.mean(jnp.square(k - mu), axis=-1, keepdims=True)
    kn = ((k - mu) * lax.rsqrt(var + EPS)) * ig_ref[...] + ib_ref[...]
    zeros = jnp.zeros((ts, LANES - IDX_DIM), F32)
    top = jnp.concatenate([kn, zeros], axis=1).T
    bot = jnp.concatenate([zeros, kn], axis=1).T
    kbd_ref[0, 0] = jnp.concatenate([top, bot], axis=1).astype(BF16)
    w_ref[0] = p[:, KV_LORA + LANES:KV_LORA + LANES + N_IDX_HEADS] * IDX_SCALE


def _dsa_prep(ps, kv_g, idx_g, idx_b):
    b, s, wid = ps.shape
    ts = ATT_T
    nkt = s // ts
    return pl.pallas_call(
        _dsa_prep_kernel,
        grid=(b, nkt),
        in_specs=[pl.BlockSpec((1, ts, wid), lambda bi, i: (bi, i, 0)),
                  pl.BlockSpec((1, KV_LORA), lambda bi, i: (0, 0)),
                  pl.BlockSpec((1, IDX_DIM), lambda bi, i: (0, 0)),
                  pl.BlockSpec((1, IDX_DIM), lambda bi, i: (0, 0))],
        out_specs=[pl.BlockSpec((1, ts, KV_LORA), lambda bi, i: (bi, i, 0)),
                   pl.BlockSpec((1, 1, LANES, 2 * ts), lambda bi, i: (bi, i, 0, 0)),
                   pl.BlockSpec((1, ts, N_IDX_HEADS), lambda bi, i: (bi, i, 0))],
        out_shape=[jax.ShapeDtypeStruct((b, s, KV_LORA), BF16),
                   jax.ShapeDtypeStruct((b, nkt, LANES, 2 * ts), BF16),
                   jax.ShapeDtypeStruct((b, s, N_IDX_HEADS), F32)],
        compiler_params=_cparams(("parallel", "parallel")),
        name="dsa_prep",
    )(ps, kv_g.reshape(1, -1).astype(F32), idx_g.reshape(1, -1).astype(F32),
      idx_b.reshape(1, -1).astype(F32))


def _dsa_index_kernel(qi_ref, w_ref, kbd_ref, mask_ref, key_ref, keyt_ref, wb_ref, t_ref, j_ref,
                      *, topk, nkt_total, seq_bits):
    tq = tk = ATT_T
    rep = tk // LANES
    sl = SUBLANES
    i = pl.program_id(1)
    t0 = i * tq
    nkt = i + 1

    row = t0 + lax.broadcasted_iota(I32, (tq, tk), 0)
    col = lax.broadcasted_iota(I32, (tq, tk), 1)
    row_t = lax.broadcasted_iota(I32, (tk, tq), 0)
    col_t = t0 + lax.broadcasted_iota(I32, (tk, tq), 1)
    sub8 = lax.broadcasted_iota(I32, (sl, tq), 0)

    def to_key(score):
        score = jnp.where(score == 0.0, 0.0, score)
        bits = lax.bitcast_convert_type(score, I32)
        return bits ^ ((bits >> 31) & np.int32(0x7FFFFFFF))

    for h in range(N_IDX_HEADS):
        wb_ref[h] = jnp.broadcast_to(w_ref[0, :, h:h + 1], (tq, LANES))

    def idx_body(kt, carry):
        kb = kbd_ref[0, kt]
        acc = jnp.zeros((tq, tk), F32)
        for hp in range(N_IDX_HEADS // 2):
            rel = jnp.dot(qi_ref[0, :, hp * LANES:(hp + 1) * LANES], kb,
                          preferred_element_type=F32)
            acc = acc + jnp.maximum(rel[:, :tk], 0.0) * _tile_lanes(wb_ref[2 * hp], rep)
            acc = acc + jnp.maximum(rel[:, tk:], 0.0) * _tile_lanes(wb_ref[2 * hp + 1], rep)
        key_ref[kt] = jnp.where(kt * tk + col <= row, to_key(acc), INT_MIN)
        keyt_ref[kt] = jnp.where(kt * tk + row_t <= col_t, to_key(acc.T), INT_MIN)
        return carry

    lax.fori_loop(0, nkt, idx_body, 0)

    def count(pred_fn):
        def body(kt, c):
            xs = [pred_fn(keyt_ref[kt, r * sl:(r + 1) * sl, :], kt * tk + r * sl) for r in range(tk // sl)]
            while len(xs) > 1:
                xs = [xs[a] + xs[a + 1] for a in range(0, len(xs), 2)]
            return c + xs[0]
        c = lax.fori_loop(0, nkt, body, jnp.zeros((sl, tq), F32))
        return jnp.broadcast_to(jnp.sum(c, axis=0, keepdims=True), (sl, tq))

    t_ref[...] = jnp.full((sl, tq), INT_MIN, I32)

    def bit_body(it, carry):
        cand = t_ref[...] ^ jnp.left_shift(jnp.int32(1), 31 - it)
        n_ge = count(lambda k, s0: jnp.where(k >= cand, 1.0, 0.0))
        t_ref[...] = jnp.where(n_ge >= topk, cand, t_ref[...])
        return carry

    lax.fori_loop(0, 32, bit_body, 0)

    thr = t_ref[...]
    n_gt = count(lambda k, s0: jnp.where(k > thr, 1.0, 0.0))
    n_ge = count(lambda k, s0: jnp.where(k >= thr, 1.0, 0.0))
    need = topk - n_gt
    j_ref[...] = jnp.full((sl, tq), 2 ** 30, I32)

    @pl.when(jnp.max(n_ge) > topk)
    def _():
        j_ref[...] = jnp.zeros((sl, tq), I32)

        def jbit_body(it, carry):
            cand = j_ref[...] | jnp.left_shift(jnp.int32(1), seq_bits - 1 - it)
            n_lt = count(lambda k, s0: jnp.where(k == thr, jnp.where(s0 + sub8 < cand, 1.0, 0.0), 0.0))
            j_ref[...] = jnp.where(n_lt < need, cand, j_ref[...])
            return carry

        lax.fori_loop(0, seq_bits, jbit_body, 0)

    def per_row(v8):
        wide = jnp.broadcast_to(lax.bitcast_convert_type(v8[0:1], F32), (LANES, tq))
        return _tile_lanes(lax.bitcast_convert_type(wide.T, I32), rep)

    thr_t = per_row(thr)
    jmax_t = per_row(j_ref[...])

    def mask_body(kt, carry):
        k = key_ref[kt]
        s_idx = kt * tk + col
        keep_tie = jnp.where(k == thr_t, jnp.where(s_idx <= jmax_t, 0.0, MASK_NEG), MASK_NEG)
        keep = jnp.where(k > thr_t, 0.0, keep_tie)
        mask_ref[0, 0, kt] = jnp.where(s_idx <= row, keep, MASK_NEG).astype(BF16)
        return carry

    lax.fori_loop(0, nkt, mask_body, 0)

    def fill_body(kt, carry):
        mask_ref[0, 0, kt] = jnp.full((tq, tk), MASK_NEG, BF16)
        return carry

    lax.fori_loop(nkt, nkt_total, fill_body, 0)


def _dsa_index(pqi, col_qi, kbd, widx):
    b, s, _ = pqi.shape
    tq = tk = ATT_T
    nkt = s // tk
    wq = N_IDX_HEADS * IDX_DIM
    topk = min(TOPK_MAX, s // 4)
    kern = functools.partial(_dsa_index_kernel, topk=topk, nkt_total=nkt,
                             seq_bits=max(1, (s - 1).bit_length()))
    return pl.pallas_call(
        kern,
        grid=(b, s // tq),
        in_specs=[pl.BlockSpec((1, tq, wq), lambda bi, i: (bi, i, col_qi)),
                  pl.BlockSpec((1, tq, N_IDX_HEADS), lambda bi, i: (bi, i, 0)),
                  pl.BlockSpec((1, nkt, LANES, 2 * tk), lambda bi, i: (bi, 0, 0, 0))],
        out_specs=pl.BlockSpec((1, 1, nkt, tq, tk), lambda bi, i: (bi, i, 0, 0, 0)),
        out_shape=jax.ShapeDtypeStruct((b, s // tq, nkt, tq, tk), BF16),
        scratch_shapes=[pltpu.VMEM((nkt, tq, tk), I32),
                        pltpu.VMEM((nkt, tk, tq), I32),
                        pltpu.VMEM((N_IDX_HEADS, tq, LANES), F32),
                        pltpu.VMEM((SUBLANES, tq), I32),
                        pltpu.VMEM((SUBLANES, tq), I32)],
        compiler_params=_cparams(("parallel", "parallel")),
        name="dsa_index",
    )(pqi, widx, kbd)


def _flash_kernel(*refs, mode, nb, topb):
    if mode == "mask":
        q_ref, k_ref, v_ref, z_ref, d_ref, mask_ref, y_ref = refs[:7]
    else:
        q_ref, k_ref, v_ref, z_ref, d_ref, y_ref, km_ref = refs[:7]
    qs_ref, acc_ref, m_ref, p_ref, alpha_ref, x_ref = refs[7:]
    tq = ATT_T
    tk = ATT_TK
    hb = q_ref.shape[2] // HEAD_DIM
    rep = tk // LANES
    i = pl.program_id(2)
    jd = i // 2
    par = i % 2
    pm = par.astype(F32)
    zero_blk = jnp.zeros((LANES, LANES), F32)
    ones_blk = jnp.ones((tk, LANES), BF16)

    if mode == "sel":
        @pl.when(i == 0)
        def _():
            km_ref[...] = jnp.zeros(km_ref.shape, F32)
            for h in range(hb):
                for n in range(nb):
                    kblk = k_ref[0, n * tq:(n + 1) * tq, h * HEAD_DIM:(h + 1) * HEAD_DIM].astype(F32)
                    km_ref[h, n:n + 1, :] = jnp.mean(kblk, axis=0, keepdims=True)
        nbp = -(-nb // SUBLANES) * SUBLANES
        blk = lax.broadcasted_iota(I32, (nbp, tq), 0).astype(F32)
        lane = lax.broadcasted_iota(I32, (tk, LANES), 1)
        second_half = lax.broadcasted_iota(I32, (tk, LANES), 0) >= tq

    for h in range(hb):
        q = q_ref[0, :, h * HEAD_DIM:(h + 1) * HEAD_DIM]
        qsc = (q.astype(F32) * (ATTN_SCALE * LOG2E)).astype(BF16)
        if mode == "sel":
            gate = lax.dot_general(km_ref[h, 0:nbp, :].astype(BF16), q, (((1,), (1,)), ((), ())),
                                   preferred_element_type=F32)
            gate = jnp.where(blk < i.astype(F32), gate, -jnp.inf)
            sel = jnp.full((nbp, tq), MASK_NEG, F32)
            for _ in range(topb):
                gmax = jnp.max(gate, axis=0, keepdims=True)
                first = jnp.min(jnp.where(gate == gmax, blk, float(nbp)), axis=0, keepdims=True)
                hit = jnp.where(blk == first, jnp.where(gmax > -jnp.inf, 1.0, 0.0), 0.0)
                sel = jnp.where(hit > 0.0, 0.0, sel)
                gate = jnp.where(blk == first, -jnp.inf, gate)
            selb = jnp.concatenate([sel, jnp.full((LANES - nbp, tq), MASK_NEG, F32)], axis=0).T
            qs_ref[h] = jnp.concatenate([qsc, selb.astype(BF16)], axis=1)
        else:
            qs_ref[h] = qsc
        m_ref[h] = jnp.full((tq, LANES), MASK_NEG, F32)
        acc_ref[h] = jnp.zeros((tq, 2 * HEAD_DIM), F32)
        p_ref[h] = jnp.zeros((tq, tk), BF16)
        alpha_ref[h] = jnp.ones((tq, LANES), F32)

    def qk_stage(j, is_diag):
        r0 = pl.multiple_of(j * tk, tk)
        if mode == "sel":
            first = jnp.where(is_diag, jnp.where(par == 1, 2 * jd, -1), 2 * j)
            second = jnp.where(is_diag, -1, 2 * j + 1)
            onehot = jnp.where(lane == jnp.where(second_half, second, first), 1.0, 0.0).astype(BF16)
        for h in range(hb):
            kblk = k_ref[0, pl.ds(r0, tk), h * HEAD_DIM:(h + 1) * HEAD_DIM]
            if mode == "sel":
                kblk = jnp.concatenate([kblk, onehot], axis=1)
            x_ref[h] = lax.dot_general(qs_ref[h], kblk, (((1,), (1,)), ((), ())),
                                       preferred_element_type=F32)

    def bias_tile(h, kind, j=None):
        d0, d1 = d_ref[0, h], d_ref[1, h]
        if kind == "past":
            near = jnp.where(j == jd - 1, 1.0 - pm, 0.0)
            top = [zero_blk, zero_blk, zero_blk, d1 * near]
            bot = [zero_blk] * 4
        else:
            top = [d0 * (1.0 - pm), d1 * pm, d0 * pm, zero_blk]
            bot = [d1 * (1.0 - pm), d0 * (1.0 - pm), d1 * pm, d0 * pm]
        return jnp.concatenate([jnp.concatenate(top, axis=1), jnp.concatenate(bot, axis=1)], axis=0)

    def softmax_stage(j, kind):
        if mode == "mask":
            mask_tile = jnp.concatenate([mask_ref[0, 0, 2 * j], mask_ref[0, 0, 2 * j + 1]],
                                        axis=1).astype(F32)
        elif kind == "diag":
            row = lax.broadcasted_iota(I32, (tq, tk), 0)
            col = lax.broadcasted_iota(I32, (tq, tk), 1)
            causal = col <= row + tq * par
        for h in range(hb):
            x = x_ref[h]
            if mode == "mask":
                x = x + mask_tile
            x = x + bias_tile(h, kind, j)
            if kind == "diag" and mode == "sel":
                x = jnp.where(causal, x, MASK_NEG)
            m_prev = m_ref[h]
            m_new = jnp.maximum(m_prev, jnp.max(x, axis=1, keepdims=True))
            alpha_ref[h] = jnp.exp2(m_prev - m_new)
            p_ref[h] = jnp.exp2(x - _tile_lanes(m_new, rep)).astype(BF16)
            m_ref[h] = m_new

    def pv_stage(j):
        r0 = pl.multiple_of(j * tk, tk)
        for h in range(hb):
            vblk = v_ref[0, pl.ds(r0, tk), h * HEAD_DIM:(h + 1) * HEAD_DIM]
            pv = jnp.dot(p_ref[h], jnp.concatenate([vblk, ones_blk], axis=1),
                         preferred_element_type=F32)
            acc_ref[h] = acc_ref[h] * _tile_lanes(alpha_ref[h], 2) + pv

    def step(j, kind):
        pv_stage(jnp.maximum(j - 1, 0))
        softmax_stage(j, kind)
        if kind != "diag":
            qk_stage(j + 1, j + 1 == jd)

    qk_stage(0, jd == 0)

    def past_body(j, carry):
        step(j, "past")
        return carry

    lax.fori_loop(0, jd, past_body, 0)
    step(jd, "diag")
    pv_stage(jd)

    for h in range(hb):
        acc = acc_ref[h]
        o = acc[:, :HEAD_DIM] / acc[:, HEAD_DIM:]
        zz = z_ref[0, :, h * HEAD_DIM:(h + 1) * HEAD_DIM].astype(F32)
        y_ref[0, :, h * HEAD_DIM:(h + 1) * HEAD_DIM] = (o * (zz * _sigmoid(zz))).astype(BF16)


def _flash_attention(q, k, v, z, dtab, n_heads, name, mask=None):
    (qa, cq), (ka, ck), (va, cv), (za, cz) = q, k, v, z
    b, s, _ = qa.shape
    tq = ATT_T
    tk = ATT_TK
    assert s % tk == 0
    nb = s // tq
    hb = ATT_HB
    wg = hb * HEAD_DIM
    mode = "sel" if mask is None else "mask"
    kern = functools.partial(_flash_kernel, mode=mode, nb=nb, topb=min(MOBA_TOPB, nb - 1))
    in_specs = [pl.BlockSpec((1, tq, wg), lambda bi, g, i: (bi, i, cq + g)),
                pl.BlockSpec((1, s, wg), lambda bi, g, i: (bi, 0, ck + g)),
                pl.BlockSpec((1, s, wg), lambda bi, g, i: (bi, 0, cv + g)),
                pl.BlockSpec((1, tq, wg), lambda bi, g, i: (bi, i, cz + g)),
                pl.BlockSpec((2, hb, LANES, LANES), lambda bi, g, i: (0, g, 0, 0))]
    args = [qa, ka, va, za, dtab]
    scratch = []
    if mode == "mask":
        in_specs.append(pl.BlockSpec((1, 1, nb, tq, tq), lambda bi, g, i: (bi, i, 0, 0, 0)))
        args.append(mask)
        kq = HEAD_DIM
    else:
        scratch.append(pltpu.VMEM((hb, LANES, HEAD_DIM), F32))
        kq = 2 * HEAD_DIM
    scratch += [pltpu.VMEM((hb, tq, kq), BF16),
                pltpu.VMEM((hb, tq, 2 * HEAD_DIM), F32),
                pltpu.VMEM((hb, tq, LANES), F32),
                pltpu.VMEM((hb, tq, tk), BF16),
                pltpu.VMEM((hb, tq, LANES), F32),
                pltpu.VMEM((hb, tq, tk), F32)]
    return pl.pallas_call(
        kern,
        grid=(b, n_heads // hb, nb),
        in_specs=in_specs,
        out_specs=pl.BlockSpec((1, tq, wg), lambda bi, g, i: (bi, i, g)),
        out_shape=jax.ShapeDtypeStruct((b, s, n_heads * HEAD_DIM), BF16),
        scratch_shapes=scratch,
        compiler_params=_cparams(("parallel", "parallel", "arbitrary")),
        name=name,
    )(*args)


def _softplus(x):
    return jnp.maximum(x, 0.0) + jnp.log1p(jnp.exp(-jnp.abs(x)))


def _rglru_kernel(xr_ref, gz_ref, cw_ref, cb_ref, wa_ref, ba_ref, wx_ref, bx_ref, lam_ref, y_ref,
                  xbuf_ref, a_ref, b_ref, h_ref):
    tc = xr_ref.shape[1]
    c = pl.program_id(2)
    halo = SUBLANES

    @pl.when(c == 0)
    def _():
        xbuf_ref[0:halo, :] = jnp.zeros((halo, xbuf_ref.shape[1]), F32)
        h_ref[...] = jnp.zeros(h_ref.shape, F32)

    @pl.when(c > 0)
    def _():
        xbuf_ref[0:halo, :] = xbuf_ref[tc:tc + halo, :]

    xbuf_ref[halo:halo + tc, :] = xr_ref[0]
    xfull = xbuf_ref[...]
    u = xfull * cw_ref[0:1, :]
    for j in range(1, CONV_W):
        u = xfull * cw_ref[j:j + 1, :] + pltpu.roll(u, 1, 0)
    xc = (u + cb_ref[...])[halo:, :]

    xcb = xc.astype(BF16)
    tr = jnp.tanh(jnp.dot(xcb, wa_ref[0], preferred_element_type=F32) + ba_ref[...])
    ti = jnp.tanh(jnp.dot(xcb, wx_ref[0], preferred_element_type=F32) + bx_ref[...])
    half_c = (-0.5 * RG_C * LOG2E) * _softplus(-lam_ref[...])
    a = jnp.exp2(tr * half_c + half_c)
    one_m_a2 = 1.0 - a * a
    mult = jnp.where(one_m_a2 > 0.0, one_m_a2 * lax.rsqrt(one_m_a2), 0.0)
    bt = (mult * xc) * (0.5 * ti + 0.5)

    nseg = SUBLANES
    seg = tc // nseg
    pitch = seg + RG_SEG_PAD
    nslab = a_ref.shape[0]
    for sl in range(nslab):
        for sg in range(nseg):
            a_ref[sl, sg * pitch:sg * pitch + seg, :] = a[sg * seg:(sg + 1) * seg, sl * LANES:(sl + 1) * LANES]
            b_ref[sl, sg * pitch:sg * pitch + seg, :] = bt[sg * seg:(sg + 1) * seg, sl * LANES:(sl + 1) * LANES]

    def seg_step(j, carry):
        out = []
        for sl in range(nslab):
            hl, pr = carry[2 * sl], carry[2 * sl + 1]
            rows = pl.ds(j, nseg, stride=pitch)
            a8 = a_ref[sl, rows, :]
            hl = a8 * hl + b_ref[sl, rows, :]
            pr = a8 * pr
            b_ref[sl, rows, :] = hl
            a_ref[sl, rows, :] = pr
            out += [hl, pr]
        return tuple(out)

    init = tuple(v for _ in range(nslab)
                 for v in (jnp.zeros((nseg, LANES), F32), jnp.ones((nseg, LANES), F32)))
    ends = lax.fori_loop(0, seg, seg_step, init, unroll=RG_SCAN_UNROLL)

    gate = gz_ref[0].astype(F32)
    for sl in range(nslab):
        h_end, p_end = ends[2 * sl], ends[2 * sl + 1]
        h_in = h_ref[:, sl * LANES:(sl + 1) * LANES]
        for sg in range(nseg):
            rows = slice(sg * pitch, sg * pitch + seg)
            hs = b_ref[sl, rows, :] + a_ref[sl, rows, :] * h_in
            y_ref[0, sg * seg:(sg + 1) * seg, sl * LANES:(sl + 1) * LANES] = (
                hs * gate[sg * seg:(sg + 1) * seg, sl * LANES:(sl + 1) * LANES]).astype(BF16)
            h_in = p_end[sg:sg + 1, :] * h_in + h_end[sg:sg + 1, :]
        h_ref[:, sl * LANES:(sl + 1) * LANES] = h_in


def _rglru(xr, gz, conv_w, conv_b, wa_g, b_a, wx_g, b_x, lam):
    b, s, d = xr.shape
    g = wa_g.shape[0]
    cg = d // g
    tc = min(RG_TC, s)
    assert ((tc // SUBLANES + RG_SEG_PAD) // SUBLANES) % 2 == 1 and cg % LANES == 0
    row = lambda v: v.reshape(1, d).astype(F32)
    vec_spec = pl.BlockSpec((1, cg), lambda bi, gi, c: (0, gi))
    return pl.pallas_call(
        _rglru_kernel,
        grid=(b, g, s // tc),
        in_specs=[pl.BlockSpec((1, tc, cg), lambda bi, gi, c: (bi, c, gi)),
                  pl.BlockSpec((1, tc, cg), lambda bi, gi, c: (bi, c, gi)),
                  pl.BlockSpec((CONV_W, cg), lambda bi, gi, c: (0, gi)),
                  vec_spec,
                  pl.BlockSpec((1, cg, cg), lambda bi, gi, c: (gi, 0, 0)),
                  vec_spec,
                  pl.BlockSpec((1, cg, cg), lambda bi, gi, c: (gi, 0, 0)),
                  vec_spec,
                  vec_spec],
        out_specs=pl.BlockSpec((1, tc, cg), lambda bi, gi, c: (bi, c, gi)),
        out_shape=jax.ShapeDtypeStruct((b, s, d), BF16),
        scratch_shapes=[pltpu.VMEM((tc + SUBLANES, cg), F32),
                        pltpu.VMEM((cg // LANES, tc + SUBLANES * RG_SEG_PAD, LANES), F32),
                        pltpu.VMEM((cg // LANES, tc + SUBLANES * RG_SEG_PAD, LANES), F32),
                        pltpu.VMEM((1, cg), F32)],
        compiler_params=_cparams(("parallel", "parallel", "arbitrary")),
        name="rglru",
    )(xr, gz, conv_w.astype(F32), row(conv_b), wa_g * 0.5, row(b_a) * 0.5, wx_g * 0.5, row(b_x) * 0.5,
      row(lam))


def _group_block_diag(w, pair):
    nb, k, _ = w.shape
    out = None
    for p in range(pair):
        lo, hi = p * k, (pair - 1 - p) * k
        blk = jnp.pad(w[p::pair], ((0, 0), (lo, hi), (lo, hi)))
        out = blk if out is None else out + blk
    return out


def _attention_layer(x2d, b, s, norm_g, w_in, kv_g, w_uk, w_uv, idx_g, idx_b, w_out, rel_bias):
    d_model = x2d.shape[1]
    n_heads = w_uk.shape[0]
    wq = n_heads * HEAD_DIM
    assert N_IDX_HEADS * IDX_DIM == wq
    sizes = (wq, KV_LORA, wq, IDX_DIM, N_IDX_HEADS, wq, wq, wq, 2 * wq)
    offs = [int(o) for o in np.concatenate([[0], np.cumsum(sizes)])]
    wt = w_in.T.astype(BF16)
    zrows = lambda n: jnp.zeros((n, d_model), BF16)
    wt_small = jnp.concatenate([wt[offs[1]:offs[2]], wt[offs[3]:offs[4]], zrows(LANES - IDX_DIM),
                                wt[offs[4]:offs[5]], zrows(LANES - N_IDX_HEADS)], axis=0)

    h = _rmsnorm(x2d, norm_g, BF16, "rmsnorm0")
    pqa = _matmul_nt(h, wt, offs[0], wq, BF16, "proj_attn_qa").reshape(b, s, wq)
    pqi = _matmul_nt(h, wt, offs[2], wq, BF16, "proj_attn_qi").reshape(b, s, wq)
    ptail = _matmul_nt(h, wt, offs[5], 5 * wq, BF16, "proj_attn_tail").reshape(b, s, 5 * wq)
    psmall = _matmul_nt(h, wt_small, 0, wt_small.shape[0], F32, "proj_attn_small").reshape(b, s, -1)

    ckv, kbd, widx = _dsa_prep(psmall, kv_g, idx_g, idx_b)
    mask = _dsa_index(pqi, 0, kbd, widx)
    w_kv = jnp.concatenate([jnp.transpose(w_uk, (1, 0, 2)).reshape(KV_LORA, wq),
                            jnp.transpose(w_uv, (1, 0, 2)).reshape(KV_LORA, wq)], axis=1).astype(BF16)
    kv = _matmul([ckv.reshape(b * s, KV_LORA)], [w_kv], BF16, "dsa_kv_up", bm=2 * MM_BM).reshape(b, s, 2 * wq)

    dtab_a = _bias_blocks(rel_bias[:, :n_heads])
    dtab_b = _bias_blocks(rel_bias[:, n_heads:])
    gpw = n_heads // ATT_HB
    ya = _flash_attention((pqa, 0), (kv, 0), (kv, gpw), (ptail, 3 * gpw), dtab_a, n_heads,
                          "dsa_attention", mask=mask)
    yb = _flash_attention((ptail, 0), (ptail, gpw), (ptail, 2 * gpw), (ptail, 4 * gpw), dtab_b,
                          n_heads, "moba_attention")

    w_out_b = w_out.astype(BF16)
    return _matmul([ya.reshape(b * s, wq), yb.reshape(b * s, wq)], [(w_out_b, 0, 0), (w_out_b, 1, 0)],
                   F32, "proj_attn_out", res=x2d, bn=MM_BN // 2)


def _recurrent_layer(x2d, b, s, norm_g, w_in, conv_w, conv_b, w_a, b_a, w_x, b_x, lam, w_out):
    d_rnn = conv_b.shape[0]
    h = _rmsnorm(x2d, norm_g, BF16, "rmsnorm1")
    w_in_b = w_in.astype(BF16)
    assert d_rnn % MM_BN == 0
    xr = _matmul([h], [(w_in_b, 0, 0)], F32, "proj_rec_x", n=d_rnn)
    gz = _matmul([h], [(w_in_b, 0, d_rnn // MM_BN)], BF16, "proj_rec_z", n=d_rnn, epilogue=_silu)
    y = _rglru(xr.reshape(b, s, d_rnn), gz.reshape(b, s, d_rnn), conv_w, conv_b,
               _group_block_diag(w_a.astype(BF16), RG_PAIR), b_a,
               _group_block_diag(w_x.astype(BF16), RG_PAIR), b_x, lam)
    return _matmul([y.reshape(b * s, d_rnn)], [w_out.astype(BF16)], F32, "proj_rec_out",
                   res=x2d, bn=MM_BN // 2)


def kernel(x, norm_g, final_g, rel_bias, attn_w_in, attn_kv_g, attn_w_uk, attn_w_uv, idx_k_g, idx_k_b,
           attn_w_out, rec_w_in, rec_conv_w, rec_conv_b, rec_w_a, rec_b_a, rec_w_x, rec_b_x,
           rec_lambda, rec_w_out):
    b, s, d = x.shape
    depth = norm_g.shape[0]
    x2d = x.reshape(b * s, d)
    for layer in range(depth):
        li = layer // 2
        if layer % 2 == 0:
            x2d = _attention_layer(x2d, b, s, norm_g[layer], attn_w_in[li], attn_kv_g[li],
                                   attn_w_uk[li], attn_w_uv[li], idx_k_g[li], idx_k_b[li],
                                   attn_w_out[li], rel_bias)
        else:
            x2d = _recurrent_layer(x2d, b, s, norm_g[layer], rec_w_in[li], rec_conv_w[li],
                                   rec_conv_b[li], rec_w_a[li], rec_b_a[li], rec_w_x[li],
                                   rec_b_x[li], rec_lambda[li], rec_w_out[li])
    return _rmsnorm(x2d, final_g, x.dtype, "rmsnorm_final").reshape(b, s, d)
```

```python
import functools
import math

import numpy as np
import jax
import jax.numpy as jnp
from jax import lax
from jax.experimental import pallas as pl
from jax.experimental.pallas import tpu as pltpu

F32 = jnp.float32
BF16 = jnp.bfloat16
I32 = jnp.int32

HEAD_DIM = 128
KV_LORA = 512
N_IDX_HEADS = 32
IDX_DIM = 64
TOPK_MAX = 256
MOBA_BLOCK = 256
MOBA_TOPB = 3
RG_BLOCKS = 16
CONV_W = 4
RG_C = 8.0
NUM_BUCKETS = 32
MAX_DISTANCE = 128
EPS = 1e-6
ATTN_SCALE = HEAD_DIM ** -0.5
IDX_SCALE = (N_IDX_HEADS ** -0.5) * (IDX_DIM ** -0.5)
LOG2E = math.log2(math.e)

LANES = 128
SUBLANES = 8
WT_ROW_ALIGN = 16
VMEM_LIMIT_BYTES = 56 * 2 ** 20
MASK_NEG = -1e30
INT_MIN = np.int32(-2 ** 31)

ROW_TILE = 256
MM_BM = 1024
MM_BN = 1024
ATT_T = 256
ATT_TK = 2 * ATT_T
ATT_HB = 4
RG_PAIR = 2
RG_TC = 1024
RG_SEG_PAD = 8
RG_SCAN_UNROLL = 4
assert ATT_T == MOBA_BLOCK and ATT_T == 2 * LANES


def _cparams(sem):
    return pltpu.CompilerParams(dimension_semantics=sem, vmem_limit_bytes=VMEM_LIMIT_BYTES)


def _sigmoid(x):
    return 0.5 * jnp.tanh(0.5 * x) + 0.5


def _tile_lanes(x, n):
    return x if n == 1 else jnp.concatenate([x] * n, axis=1)


def _rmsnorm_kernel(x_ref, g_ref, o_ref):
    x = x_ref[...]
    ms = jnp.mean(x * x, axis=-1, keepdims=True)
    o_ref[...] = ((x * lax.rsqrt(ms + EPS)) * g_ref[...]).astype(o_ref.dtype)


def _rmsnorm(x2d, g, out_dtype, name):
    m, d = x2d.shape
    bm = min(ROW_TILE, m)
    return pl.pallas_call(
        _rmsnorm_kernel,
        grid=(m // bm,),
        in_specs=[pl.BlockSpec((bm, d), lambda i: (i, 0)),
                  pl.BlockSpec((1, d), lambda i: (0, 0))],
        out_specs=pl.BlockSpec((bm, d), lambda i: (i, 0)),
        out_shape=jax.ShapeDtypeStruct((m, d), out_dtype),
        compiler_params=_cparams(("parallel",)),
        name=name,
    )(x2d, g.reshape(1, d).astype(F32))


def _mm_kernel(*refs, n_parts, has_res, epilogue, copy_out, row_norm):
    n_in = 2 * n_parts + int(has_res)
    o_ref = refs[n_in]
    acc = jnp.dot(refs[0][...], refs[n_parts][...], preferred_element_type=F32)
    for p in range(1, n_parts):
        acc = acc + jnp.dot(refs[p][...], refs[n_parts + p][...], preferred_element_type=F32)
    if row_norm:
        r_ref = refs[-1]

        @pl.when(pl.program_id(1) == 0)
        def _():
            x = refs[0][...].astype(F32)
            r_ref[...] = jnp.broadcast_to(lax.rsqrt(jnp.mean(x * x, axis=-1, keepdims=True) + EPS),
                                          r_ref.shape)

        acc = acc * _tile_lanes(r_ref[...], acc.shape[1] // LANES)
    if has_res:
        acc = refs[2 * n_parts][...] + acc
    if epilogue is not None:
        acc = epilogue(acc)
    o_ref[...] = acc.astype(o_ref.dtype)
    if copy_out:
        refs[n_in + 1][...] = acc.astype(refs[n_in + 1].dtype)


def _matmul(xs, ws, out_dtype, name, res=None, n=None, bm=MM_BM, bn=MM_BN, epilogue=None,
            copy_dtype=None, row_norm=False):
    ws = [w if isinstance(w, tuple) else (w, 0, 0) for w in ws]
    m = xs[0].shape[0]
    n = ws[0][0].shape[1] if n is None else n
    bm = min(bm, m)
    bn = min(bn, n)
    assert m % bm == 0 and n % bn == 0 and not (row_norm and len(xs) > 1)
    in_specs = [pl.BlockSpec((bm, x.shape[1]), lambda i, j: (i, 0)) for x in xs]
    for x, (w, rb, cb) in zip(xs, ws):
        assert w.shape[0] % x.shape[1] == 0 and (cb + n // bn) * bn <= w.shape[1]
        in_specs.append(pl.BlockSpec((x.shape[1], bn), lambda i, j, rb=rb, cb=cb: (rb, cb + j)))
    args = [*xs, *[w for w, _, _ in ws]]
    if res is not None:
        in_specs.append(pl.BlockSpec((bm, bn), lambda i, j: (i, j)))
        args.append(res)
    out_spec = pl.BlockSpec((bm, bn), lambda i, j: (i, j))
    out_shape = jax.ShapeDtypeStruct((m, n), out_dtype)
    if copy_dtype is not None:
        out_spec = [out_spec, pl.BlockSpec((bm, bn), lambda i, j: (i, j))]
        out_shape = [out_shape, jax.ShapeDtypeStruct((m, n), copy_dtype)]
    return pl.pallas_call(
        functools.partial(_mm_kernel, n_parts=len(xs), has_res=res is not None, epilogue=epilogue,
                          copy_out=copy_dtype is not None, row_norm=row_norm),
        grid=(m // bm, n // bn),
        in_specs=in_specs,
        out_specs=out_spec,
        out_shape=out_shape,
        scratch_shapes=[pltpu.VMEM((bm, LANES), F32)] if row_norm else [],
        compiler_params=_cparams(("parallel", "arbitrary" if row_norm else "parallel")),
        name=name,
    )(*args)


def _silu(x):
    return x * _sigmoid(x)


def _mm_nt_kernel(x_ref, wt_ref, o_ref, *, epilogue):
    acc = lax.dot_general(x_ref[...], wt_ref[...], (((1,), (1,)), ((), ())), preferred_element_type=F32)
    if epilogue is not None:
        acc = epilogue(acc)
    o_ref[...] = acc.astype(o_ref.dtype)


def _matmul_nt(x, wt, row0, n, out_dtype, name, bn=MM_BN, epilogue=None):
    m, k = x.shape
    bm = min(MM_BM, m)
    bn = min(bn, n)
    assert m % bm == 0 and n % bn == 0 and row0 % WT_ROW_ALIGN == 0 and bn % WT_ROW_ALIGN == 0
    assert row0 + n <= wt.shape[0] and wt.shape[1] == k
    return pl.pallas_call(
        functools.partial(_mm_nt_kernel, epilogue=epilogue),
        grid=(m // bm, n // bn),
        in_specs=[pl.BlockSpec((bm, k), lambda i, j: (i, 0)),
                  pl.BlockSpec((pl.Element(bn), pl.Element(k)),
                               lambda i, j: (pl.multiple_of(row0 + j * bn, WT_ROW_ALIGN), 0))],
        out_specs=pl.BlockSpec((bm, bn), lambda i, j: (i, j)),
        out_shape=jax.ShapeDtypeStruct((m, n), out_dtype),
        compiler_params=_cparams(("parallel", "parallel")),
        name=name,
    )(x, wt)


def _t5_bucket_np(dist):
    d = np.maximum(dist, 0)
    max_exact = NUM_BUCKETS // 2
    d_f = np.maximum(d, 1).astype(np.float32)
    ratio = np.log(d_f / np.float32(max_exact)) / np.float32(math.log(MAX_DISTANCE / max_exact))
    large = max_exact + (ratio * np.float32(NUM_BUCKETS - max_exact)).astype(np.int32)
    large = np.minimum(large, NUM_BUCKETS - 1)
    return np.where(d < max_exact, d, large)


def _bias_blocks(tab):
    n = LANES
    assert np.all(_t5_bucket_np(np.arange(n, 64 * n)) == NUM_BUCKETS - 1)
    h = tab.shape[1]
    rel = (tab - tab[NUM_BUCKETS - 1]).astype(F32) * LOG2E
    out = []
    for off in (0, n):
        buckets = _t5_bucket_np(off + n - 1 - np.arange(2 * n - 1))
        onehot = np.zeros((2 * n, NUM_BUCKETS), np.float32)
        onehot[np.arange(2 * n - 1), buckets] = 1.0
        w = jnp.dot(jnp.asarray(onehot), rel, precision=lax.Precision.HIGHEST).T
        t = jnp.tile(w, (1, n))[:, :n * (2 * n - 1)].reshape(h, n, 2 * n - 1)
        out.append(t[:, :, n - 1:])
    return jnp.stack(out)


def _dsa_prep_kernel(p_ref, kvg_ref, ig_ref, ib_ref, ckv_ref, kbd_ref, w_ref):
    p = p_ref[0]
    ts = p.shape[0]
    c = p[:, :KV_LORA]
    cn = (c * lax.rsqrt(jnp.mean(c * c, axis=-1, keepdims=True) + EPS)) * kvg_ref[...]
    ckv_ref[0] = cn.astype(BF16)
    k = p[:, KV_LORA:KV_LORA + IDX_DIM]
    mu = jnp.mean(k, axis=-1, keepdims=True)
    var = jnp.mean(jnp.square(k - mu), axis=-1, keepdims=True)
    kn = ((k - mu) * lax.rsqrt(var + EPS)) * ig_ref[...] + ib_ref[...]
    zeros = jnp.zeros((ts, LANES - IDX_DIM), F32)
    top = jnp.concatenate([kn, zeros], axis=1).T
    bot = jnp.concatenate([zeros, kn], axis=1).T
    kbd_ref[0, 0] = jnp.concatenate([top, bot], axis=1).astype(BF16)
    w_ref[0] = p[:, KV_LORA + LANES:KV_LORA + LANES + N_IDX_HEADS] * IDX_SCALE


def _dsa_prep(ps, kv_g, idx_g, idx_b):
    b, s, wid = ps.shape
    ts = ATT_T
    nkt = s // ts
    return pl.pallas_call(
        _dsa_prep_kernel,
        grid=(b, nkt),
        in_specs=[pl.BlockSpec((1, ts, wid), lambda bi, i: (bi, i, 0)),
                  pl.BlockSpec((1, KV_LORA), lambda bi, i: (0, 0)),
                  pl.BlockSpec((1, IDX_DIM), lambda bi, i: (0, 0)),
                  pl.BlockSpec((1, IDX_DIM), lambda bi, i: (0, 0))],
        out_specs=[pl.BlockSpec((1, ts, KV_LORA), lambda bi, i: (bi, i, 0)),
                   pl.BlockSpec((1, 1, LANES, 2 * ts), lambda bi, i: (bi, i, 0, 0)),
                   pl.BlockSpec((1, ts, N_IDX_HEADS), lambda bi, i: (bi, i, 0))],
        out_shape=[jax.ShapeDtypeStruct((b, s, KV_LORA), BF16),
                   jax.ShapeDtypeStruct((b, nkt, LANES, 2 * ts), BF16),
                   jax.ShapeDtypeStruct((b, s, N_IDX_HEADS), F32)],
        compiler_params=_cparams(("parallel", "parallel")),
        name="dsa_prep",
    )(ps, kv_g.reshape(1, -1).astype(F32), idx_g.reshape(1, -1).astype(F32),
      idx_b.reshape(1, -1).astype(F32))


def _dsa_index_kernel(qi_ref, w_ref, kbd_ref, mask_ref, key_ref, keyt_ref, wb_ref, t_ref, j_ref,
                      *, topk, nkt_total, seq_bits):
    tq = tk = ATT_T
    rep = tk // LANES
    sl = SUBLANES
    i = pl.program_id(1)
    t0 = i * tq
    nkt = i + 1

    row = t0 + lax.broadcasted_iota(I32, (tq, tk), 0)
    col = lax.broadcasted_iota(I32, (tq, tk), 1)
    row_t = lax.broadcasted_iota(I32, (tk, tq), 0)
    col_t = t0 + lax.broadcasted_iota(I32, (tk, tq), 1)
    sub8 = lax.broadcasted_iota(I32, (sl, tq), 0)

    def to_key(score):
        score = jnp.where(score == 0.0, 0.0, score)
        bits = lax.bitcast_convert_type(score, I32)
        return bits ^ ((bits >> 31) & np.int32(0x7FFFFFFF))

    for h in range(N_IDX_HEADS):
        wb_ref[h] = jnp.broadcast_to(w_ref[0, :, h:h + 1], (tq, LANES))

    def idx_body(kt, carry):
        kb = kbd_ref[0, kt]
        acc = jnp.zeros((tq, tk), F32)
        for hp in range(N_IDX_HEADS // 2):
            rel = jnp.dot(qi_ref[0, :, hp * LANES:(hp + 1) * LANES], kb,
                          preferred_element_type=F32)
            acc = acc + jnp.maximum(rel[:, :tk], 0.0) * _tile_lanes(wb_ref[2 * hp], rep)
            acc = acc + jnp.maximum(rel[:, tk:], 0.0) * _tile_lanes(wb_ref[2 * hp + 1], rep)
        key_ref[kt] = jnp.where(kt * tk + col <= row, to_key(acc), INT_MIN)
        keyt_ref[kt] = jnp.where(kt * tk + row_t <= col_t, to_key(acc.T), INT_MIN)
        return carry

    lax.fori_loop(0, nkt, idx_body, 0)

    def count(pred_fn):
        def body(kt, c):
            xs = [pred_fn(keyt_ref[kt, r * sl:(r + 1) * sl, :], kt * tk + r * sl) for r in range(tk // sl)]
            while len(xs) > 1:
                xs = [xs[a] + xs[a + 1] for a in range(0, len(xs), 2)]
            return c + xs[0]
        c = lax.fori_loop(0, nkt, body, jnp.zeros((sl, tq), F32))
        return jnp.broadcast_to(jnp.sum(c, axis=0, keepdims=True), (sl, tq))

    t_ref[...] = jnp.full((sl, tq), INT_MIN, I32)

    def bit_body(it, carry):
        cand = t_ref[...] ^ jnp.left_shift(jnp.int32(1), 31 - it)
        n_ge = count(lambda k, s0: jnp.where(k >= cand, 1.0, 0.0))
        t_ref[...] = jnp.where(n_ge >= topk, cand, t_ref[...])
        return carry

    lax.fori_loop(0, 32, bit_body, 0)

    thr = t_ref[...]
    n_gt = count(lambda k, s0: jnp.where(k > thr, 1.0, 0.0))
    n_ge = count(lambda k, s0: jnp.where(k >= thr, 1.0, 0.0))
    need = topk - n_gt
    j_ref[...] = jnp.full((sl, tq), 2 ** 30, I32)

    @pl.when(jnp.max(n_ge) > topk)
    def _():
        j_ref[...] = jnp.zeros((sl, tq), I32)

        def jbit_body(it, carry):
            cand = j_ref[...] | jnp.left_shift(jnp.int32(1), seq_bits - 1 - it)
            n_lt = count(lambda k, s0: jnp.where(k == thr, jnp.where(s0 + sub8 < cand, 1.0, 0.0), 0.0))
            j_ref[...] = jnp.where(n_lt < need, cand, j_ref[...])
            return carry

        lax.fori_loop(0, seq_bits, jbit_body, 0)

    def per_row(v8):
        wide = jnp.broadcast_to(lax.bitcast_convert_type(v8[0:1], F32), (LANES, tq))
        return _tile_lanes(lax.bitcast_convert_type(wide.T, I32), rep)

    thr_t = per_row(thr)
    jmax_t = per_row(j_ref[...])

    def mask_body(kt, carry):
        k = key_ref[kt]
        s_idx = kt * tk + col
        keep_tie = jnp.where(k == thr_t, jnp.where(s_idx <= jmax_t, 0.0, MASK_NEG), MASK_NEG)
        keep = jnp.where(k > thr_t, 0.0, keep_tie)
        mask_ref[0, 0, kt] = jnp.where(s_idx <= row, keep, MASK_NEG).astype(BF16)
        return carry

    lax.fori_loop(0, nkt, mask_body, 0)

    def fill_body(kt, carry):
        mask_ref[0, 0, kt] = jnp.full((tq, tk), MASK_NEG, BF16)
        return carry

    lax.fori_loop(nkt, nkt_total, fill_body, 0)


def _dsa_index(pqi, col_qi, kbd, widx):
    b, s, _ = pqi.shape
    tq = tk = ATT_T
    nkt = s // tk
    wq = N_IDX_HEADS * IDX_DIM
    topk = min(TOPK_MAX, s // 4)
    kern = functools.partial(_dsa_index_kernel, topk=topk, nkt_total=nkt,
                             seq_bits=max(1, (s - 1).bit_length()))
    return pl.pallas_call(
        kern,
        grid=(b, s // tq),
        in_specs=[pl.BlockSpec((1, tq, wq), lambda bi, i: (bi, i, col_qi)),
                  pl.BlockSpec((1, tq, N_IDX_HEADS), lambda bi, i: (bi, i, 0)),
                  pl.BlockSpec((1, nkt, LANES, 2 * tk), lambda bi, i: (bi, 0, 0, 0))],
        out_specs=pl.BlockSpec((1, 1, nkt, tq, tk), lambda bi, i: (bi, i, 0, 0, 0)),
        out_shape=jax.ShapeDtypeStruct((b, s // tq, nkt, tq, tk), BF16),
        scratch_shapes=[pltpu.VMEM((nkt, tq, tk), I32),
                        pltpu.VMEM((nkt, tk, tq), I32),
                        pltpu.VMEM((N_IDX_HEADS, tq, LANES), F32),
                        pltpu.VMEM((SUBLANES, tq), I32),
                        pltpu.VMEM((SUBLANES, tq), I32)],
        compiler_params=_cparams(("parallel", "parallel")),
        name="dsa_index",
    )(pqi, widx, kbd)


def _flash_kernel(*refs, mode, nb, topb):
    if mode == "mask":
        q_ref, k_ref, v_ref, z_ref, d_ref, mask_ref, y_ref = refs[:7]
    else:
        q_ref, k_ref, v_ref, z_ref, d_ref, y_ref, km_ref = refs[:7]
    qs_ref, acc_ref, m_ref, p_ref, alpha_ref, x_ref = refs[7:]
    tq = ATT_T
    tk = ATT_TK
    hb = q_ref.shape[2] // HEAD_DIM
    rep = tk // LANES
    i = pl.program_id(2)
    jd = i // 2
    par = i % 2
    pm = par.astype(F32)
    zero_blk = jnp.zeros((LANES, LANES), F32)
    ones_blk = jnp.ones((tk, LANES), BF16)

    if mode == "sel":
        @pl.when(i == 0)
        def _():
            km_ref[...] = jnp.zeros(km_ref.shape, F32)
            for h in range(hb):
                for n in range(nb):
                    kblk = k_ref[0, n * tq:(n + 1) * tq, h * HEAD_DIM:(h + 1) * HEAD_DIM].astype(F32)
                    km_ref[h, n:n + 1, :] = jnp.mean(kblk, axis=0, keepdims=True)
        nbp = -(-nb // SUBLANES) * SUBLANES
        blk = lax.broadcasted_iota(I32, (nbp, tq), 0).astype(F32)
        lane = lax.broadcasted_iota(I32, (tk, LANES), 1)
        second_half = lax.broadcasted_iota(I32, (tk, LANES), 0) >= tq

    for h in range(hb):
        q = q_ref[0, :, h * HEAD_DIM:(h + 1) * HEAD_DIM]
        qsc = (q.astype(F32) * (ATTN_SCALE * LOG2E)).astype(BF16)
        if mode == "sel":
            gate = lax.dot_general(km_ref[h, 0:nbp, :].astype(BF16), q, (((1,), (1,)), ((), ())),
                                   preferred_element_type=F32)
            gate = jnp.where(blk < i.astype(F32), gate, -jnp.inf)
            sel = jnp.full((nbp, tq), MASK_NEG, F32)
            for _ in range(topb):
                gmax = jnp.max(gate, axis=0, keepdims=True)
                first = jnp.min(jnp.where(gate == gmax, blk, float(nbp)), axis=0, keepdims=True)
                hit = jnp.where(blk == first, jnp.where(gmax > -jnp.inf, 1.0, 0.0), 0.0)
                sel = jnp.where(hit > 0.0, 0.0, sel)
                gate = jnp.where(blk == first, -jnp.inf, gate)
            selb = jnp.concatenate([sel, jnp.full((LANES - nbp, tq), MASK_NEG, F32)], axis=0).T
            qs_ref[h] = jnp.concatenate([qsc, selb.astype(BF16)], axis=1)
        else:
            qs_ref[h] = qsc
        m_ref[h] = jnp.full((tq, LANES), MASK_NEG, F32)
        acc_ref[h] = jnp.zeros((tq, 2 * HEAD_DIM), F32)
        p_ref[h] = jnp.zeros((tq, tk), BF16)
        alpha_ref[h] = jnp.ones((tq, LANES), F32)

    def qk_stage(j, is_diag):
        r0 = pl.multiple_of(j * tk, tk)
        if mode == "sel":
            first = jnp.where(is_diag, jnp.where(par == 1, 2 * jd, -1), 2 * j)
            second = jnp.where(is_diag, -1, 2 * j + 1)
            onehot = jnp.where(lane == jnp.where(second_half, second, first), 1.0, 0.0).astype(BF16)
        for h in range(hb):
            kblk = k_ref[0, pl.ds(r0, tk), h * HEAD_DIM:(h + 1) * HEAD_DIM]
            if mode == "sel":
                kblk = jnp.concatenate([kblk, onehot], axis=1)
            x_ref[h] = lax.dot_general(qs_ref[h], kblk, (((1,), (1,)), ((), ())),
                                       preferred_element_type=F32)

    def bias_tile(h, kind, j=None):
        d0, d1 = d_ref[0, h], d_ref[1, h]
        if kind == "past":
            near = jnp.where(j == jd - 1, 1.0 - pm, 0.0)
            top = [zero_blk, zero_blk, zero_blk, d1 * near]
            bot = [zero_blk] * 4
        else:
            top = [d0 * (1.0 - pm), d1 * pm, d0 * pm, zero_blk]
            bot = [d1 * (1.0 - pm), d0 * (1.0 - pm), d1 * pm, d0 * pm]
        return jnp.concatenate([jnp.concatenate(top, axis=1), jnp.concatenate(bot, axis=1)], axis=0)

    def softmax_stage(j, kind):
        if mode == "mask":
            mask_tile = jnp.concatenate([mask_ref[0, 0, 2 * j], mask_ref[0, 0, 2 * j + 1]],
                                        axis=1).astype(F32)
        elif kind == "diag":
            row = lax.broadcasted_iota(I32, (tq, tk), 0)
            col = lax.broadcasted_iota(I32, (tq, tk), 1)
            causal = col <= row + tq * par
        for h in range(hb):
            x = x_ref[h]
            if mode == "mask":
                x = x + mask_tile
            x = x + bias_tile(h, kind, j)
            if kind == "diag" and mode == "sel":
                x = jnp.where(causal, x, MASK_NEG)
            m_prev = m_ref[h]
            m_new = jnp.maximum(m_prev, jnp.max(x, axis=1, keepdims=True))
            alpha_ref[h] = jnp.exp2(m_prev - m_new)
            p_ref[h] = jnp.exp2(x - _tile_lanes(m_new, rep)).astype(BF16)
            m_ref[h] = m_new

    def pv_stage(j):
        r0 = pl.multiple_of(j * tk, tk)
        for h in range(hb):
            vblk = v_ref[0, pl.ds(r0, tk), h * HEAD_DIM:(h + 1) * HEAD_DIM]
            pv = jnp.dot(p_ref[h], jnp.concatenate([vblk, ones_blk], axis=1),
                         preferred_element_type=F32)
            acc_ref[h] = acc_ref[h] * _tile_lanes(alpha_ref[h], 2) + pv

    def step(j, kind):
        pv_stage(jnp.maximum(j - 1, 0))
        softmax_stage(j, kind)
        if kind != "diag":
            qk_stage(j + 1, j + 1 == jd)

    qk_stage(0, jd == 0)

    def past_body(j, carry):
        step(j, "past")
        return carry

    lax.fori_loop(0, jd, past_body, 0)
    step(jd, "diag")
    pv_stage(jd)

    for h in range(hb):
        acc = acc_ref[h]
        o = acc[:, :HEAD_DIM] / acc[:, HEAD_DIM:]
        zz = z_ref[0, :, h * HEAD_DIM:(h + 1) * HEAD_DIM].astype(F32)
        y_ref[0, :, h * HEAD_DIM:(h + 1) * HEAD_DIM] = (o * (zz * _sigmoid(zz))).astype(BF16)


def _flash_attention(q, k, v, z, dtab, n_heads, name, mask=None):
    (qa, cq), (ka, ck), (va, cv), (za, cz) = q, k, v, z
    b, s, _ = qa.shape
    tq = ATT_T
    tk = ATT_TK
    assert s % tk == 0
    nb = s // tq
    hb = ATT_HB
    wg = hb * HEAD_DIM
    mode = "sel" if mask is None else "mask"
    kern = functools.partial(_flash_kernel, mode=mode, nb=nb, topb=min(MOBA_TOPB, nb - 1))
    in_specs = [pl.BlockSpec((1, tq, wg), lambda bi, g, i: (bi, i, cq + g)),
                pl.BlockSpec((1, s, wg), lambda bi, g, i: (bi, 0, ck + g)),
                pl.BlockSpec((1, s, wg), lambda bi, g, i: (bi, 0, cv + g)),
                pl.BlockSpec((1, tq, wg), lambda bi, g, i: (bi, i, cz + g)),
                pl.BlockSpec((2, hb, LANES, LANES), lambda bi, g, i: (0, g, 0, 0))]
    args = [qa, ka, va, za, dtab]
    scratch = []
    if mode == "mask":
        in_specs.append(pl.BlockSpec((1, 1, nb, tq, tq), lambda bi, g, i: (bi, i, 0, 0, 0)))
        args.append(mask)
        kq = HEAD_DIM
    else:
        scratch.append(pltpu.VMEM((hb, LANES, HEAD_DIM), F32))
        kq = 2 * HEAD_DIM
    scratch += [pltpu.VMEM((hb, tq, kq), BF16),
                pltpu.VMEM((hb, tq, 2 * HEAD_DIM), F32),
                pltpu.VMEM((hb, tq, LANES), F32),
                pltpu.VMEM((hb, tq, tk), BF16),
                pltpu.VMEM((hb, tq, LANES), F32),
                pltpu.VMEM((hb, tq, tk), F32)]
    return pl.pallas_call(
        kern,
        grid=(b, n_heads // hb, nb),
        in_specs=in_specs,
        out_specs=pl.BlockSpec((1, tq, wg), lambda bi, g, i: (bi, i, g)),
        out_shape=jax.ShapeDtypeStruct((b, s, n_heads * HEAD_DIM), BF16),
        scratch_shapes=scratch,
        compiler_params=_cparams(("parallel", "parallel", "arbitrary")),
        name=name,
    )(*args)


def _softplus(x):
    return jnp.maximum(x, 0.0) + jnp.log1p(jnp.exp(-jnp.abs(x)))


def _rglru_kernel(xr_ref, gz_ref, cw_ref, cb_ref, wa_ref, ba_ref, wx_ref, bx_ref, lam_ref, y_ref,
                  xbuf_ref, a_ref, b_ref, h_ref):
    tc = xr_ref.shape[1]
    c = pl.program_id(2)
    halo = SUBLANES

    @pl.when(c == 0)
    def _():
        xbuf_ref[0:halo, :] = jnp.zeros((halo, xbuf_ref.shape[1]), F32)
        h_ref[...] = jnp.zeros(h_ref.shape, F32)

    @pl.when(c > 0)
    def _():
        xbuf_ref[0:halo, :] = xbuf_ref[tc:tc + halo, :]

    xbuf_ref[halo:halo + tc, :] = xr_ref[0]
    xfull = xbuf_ref[...]
    u = xfull * cw_ref[0:1, :]
    for j in range(1, CONV_W):
        u = xfull * cw_ref[j:j + 1, :] + pltpu.roll(u, 1, 0)
    xc = (u + cb_ref[...])[halo:, :]

    xcb = xc.astype(BF16)
    tr = jnp.tanh(jnp.dot(xcb, wa_ref[0], preferred_element_type=F32) + ba_ref[...])
    ti = jnp.tanh(jnp.dot(xcb, wx_ref[0], preferred_element_type=F32) + bx_ref[...])
    half_c = (-0.5 * RG_C * LOG2E) * _softplus(-lam_ref[...])
    a = jnp.exp2(tr * half_c + half_c)
    one_m_a2 = 1.0 - a * a
    mult = jnp.where(one_m_a2 > 0.0, one_m_a2 * lax.rsqrt(one_m_a2), 0.0)
    bt = (mult * xc) * (0.5 * ti + 0.5)

    nseg = SUBLANES
    seg = tc // nseg
    pitch = seg + RG_SEG_PAD
    nslab = a_ref.shape[0]
    for sl in range(nslab):
        for sg in range(nseg):
            a_ref[sl, sg * pitch:sg * pitch + seg, :] = a[sg * seg:(sg + 1) * seg, sl * LANES:(sl + 1) * LANES]
            b_ref[sl, sg * pitch:sg * pitch + seg, :] = bt[sg * seg:(sg + 1) * seg, sl * LANES:(sl + 1) * LANES]

    def seg_step(j, carry):
        out = []
        for sl in range(nslab):
            hl, pr = carry[2 * sl], carry[2 * sl + 1]
            rows = pl.ds(j, nseg, stride=pitch)
            a8 = a_ref[sl, rows, :]
            hl = a8 * hl + b_ref[sl, rows, :]
            pr = a8 * pr
            b_ref[sl, rows, :] = hl
            a_ref[sl, rows, :] = pr
            out += [hl, pr]
        return tuple(out)

    init = tuple(v for _ in range(nslab)
                 for v in (jnp.zeros((nseg, LANES), F32), jnp.ones((nseg, LANES), F32)))
    ends = lax.fori_loop(0, seg, seg_step, init, unroll=RG_SCAN_UNROLL)

    gate = gz_ref[0].astype(F32)
    for sl in range(nslab):
        h_end, p_end = ends[2 * sl], ends[2 * sl + 1]
        h_in = h_ref[:, sl * LANES:(sl + 1) * LANES]
        for sg in range(nseg):
            rows = slice(sg * pitch, sg * pitch + seg)
            hs = b_ref[sl, rows, :] + a_ref[sl, rows, :] * h_in
            y_ref[0, sg * seg:(sg + 1) * seg, sl * LANES:(sl + 1) * LANES] = (
                hs * gate[sg * seg:(sg + 1) * seg, sl * LANES:(sl + 1) * LANES]).astype(BF16)
            h_in = p_end[sg:sg + 1, :] * h_in + h_end[sg:sg + 1, :]
        h_ref[:, sl * LANES:(sl + 1) * LANES] = h_in


def _rglru(xr, gz, conv_w, conv_b, wa_g, b_a, wx_g, b_x, lam):
    b, s, d = xr.shape
    g = wa_g.shape[0]
    cg = d // g
    tc = min(RG_TC, s)
    assert ((tc // SUBLANES + RG_SEG_PAD) // SUBLANES) % 2 == 1 and cg % LANES == 0
    row = lambda v: v.reshape(1, d).astype(F32)
    vec_spec = pl.BlockSpec((1, cg), lambda bi, gi, c: (0, gi))
    return pl.pallas_call(
        _rglru_kernel,
        grid=(b, g, s // tc),
        in_specs=[pl.BlockSpec((1, tc, cg), lambda bi, gi, c: (bi, c, gi)),
                  pl.BlockSpec((1, tc, cg), lambda bi, gi, c: (bi, c, gi)),
                  pl.BlockSpec((CONV_W, cg), lambda bi, gi, c: (0, gi)),
                  vec_spec,
                  pl.BlockSpec((1, cg, cg), lambda bi, gi, c: (gi, 0, 0)),
                  vec_spec,
                  pl.BlockSpec((1, cg, cg), lambda bi, gi, c: (gi, 0, 0)),
                  vec_spec,
                  vec_spec],
        out_specs=pl.BlockSpec((1, tc, cg), lambda bi, gi, c: (bi, c, gi)),
        out_shape=jax.ShapeDtypeStruct((b, s, d), BF16),
        scratch_shapes=[pltpu.VMEM((tc + SUBLANES, cg), F32),
                        pltpu.VMEM((cg // LANES, tc + SUBLANES * RG_SEG_PAD, LANES), F32),
                        pltpu.VMEM((cg // LANES, tc + SUBLANES * RG_SEG_PAD, LANES), F32),
                        pltpu.VMEM((1, cg), F32)],
        compiler_params=_cparams(("parallel", "parallel", "arbitrary")),
        name="rglru",
    )(xr, gz, conv_w.astype(F32), row(conv_b), wa_g * 0.5, row(b_a) * 0.5, wx_g * 0.5, row(b_x) * 0.5,
      row(lam))


def _group_block_diag(w, pair):
    nb, k, _ = w.shape
    out = None
    for p in range(pair):
        lo, hi = p * k, (pair - 1 - p) * k
        blk = jnp.pad(w[p::pair], ((0, 0), (lo, hi), (lo, hi)))
        out = blk if out is None else out + blk
    return out


def _attention_layer(x2d, b, s, norm_g, w_in, kv_g, w_uk, w_uv, idx_g, idx_b, w_out, rel_bias,
                     copy_dtype=None):
    d_model = x2d.shape[1]
    n_heads = w_uk.shape[0]
    wq = n_heads * HEAD_DIM
    assert N_IDX_HEADS * IDX_DIM == wq
    sizes = (wq, KV_LORA, wq, IDX_DIM, N_IDX_HEADS, wq, wq, wq, 2 * wq)
    offs = [int(o) for o in np.concatenate([[0], np.cumsum(sizes)])]
    wt = w_in.T.astype(BF16)
    zrows = lambda n: jnp.zeros((n, d_model), BF16)
    wt_small = jnp.concatenate([wt[offs[1]:offs[2]], wt[offs[3]:offs[4]], zrows(LANES - IDX_DIM),
                                wt[offs[4]:offs[5]], zrows(LANES - N_IDX_HEADS)], axis=0)

    h = _rmsnorm(x2d, norm_g, BF16, "rmsnorm0")
    pqa = _matmul_nt(h, wt, offs[0], wq, BF16, "proj_attn_qa").reshape(b, s, wq)
    pqi = _matmul_nt(h, wt, offs[2], wq, BF16, "proj_attn_qi").reshape(b, s, wq)
    ptail = _matmul_nt(h, wt, offs[5], 5 * wq, BF16, "proj_attn_tail").reshape(b, s, 5 * wq)
    psmall = _matmul_nt(h, wt_small, 0, wt_small.shape[0], F32, "proj_attn_small").reshape(b, s, -1)

    ckv, kbd, widx = _dsa_prep(psmall, kv_g, idx_g, idx_b)
    mask = _dsa_index(pqi, 0, kbd, widx)
    w_kv = jnp.concatenate([jnp.transpose(w_uk, (1, 0, 2)).reshape(KV_LORA, wq),
                            jnp.transpose(w_uv, (1, 0, 2)).reshape(KV_LORA, wq)], axis=1).astype(BF16)
    kv = _matmul([ckv.reshape(b * s, KV_LORA)], [w_kv], BF16, "dsa_kv_up", bm=2 * MM_BM).reshape(b, s, 2 * wq)

    dtab_a = _bias_blocks(rel_bias[:, :n_heads])
    dtab_b = _bias_blocks(rel_bias[:, n_heads:])
    gpw = n_heads // ATT_HB
    ya = _flash_attention((pqa, 0), (kv, 0), (kv, gpw), (ptail, 3 * gpw), dtab_a, n_heads,
                          "dsa_attention", mask=mask)
    yb = _flash_attention((ptail, 0), (ptail, gpw), (ptail, 2 * gpw), (ptail, 4 * gpw), dtab_b,
                          n_heads, "moba_attention")

    w_out_b = w_out.astype(BF16)
    return _matmul([ya.reshape(b * s, wq), yb.reshape(b * s, wq)], [(w_out_b, 0, 0), (w_out_b, 1, 0)],
                   F32, "proj_attn_out", res=x2d, bn=MM_BN // 2, copy_dtype=copy_dtype)


def _recurrent_layer(x2d, xb, b, s, norm_g, w_in, conv_w, conv_b, w_a, b_a, w_x, b_x, lam, w_out):
    d_rnn = conv_b.shape[0]
    assert d_rnn % MM_BN == 0
    w_in_b = (w_in * norm_g.astype(F32)[:, None]).astype(BF16)
    xr = _matmul([xb], [(w_in_b, 0, 0)], F32, "proj_rec_x", n=d_rnn, row_norm=True)
    gz = _matmul([xb], [(w_in_b, 0, d_rnn // MM_BN)], BF16, "proj_rec_z", n=d_rnn, epilogue=_silu,
                 row_norm=True)
    y = _rglru(xr.reshape(b, s, d_rnn), gz.reshape(b, s, d_rnn), conv_w, conv_b,
               _group_block_diag(w_a.astype(BF16), RG_PAIR), b_a,
               _group_block_diag(w_x.astype(BF16), RG_PAIR), b_x, lam)
    return _matmul([y.reshape(b * s, d_rnn)], [w_out.astype(BF16)], F32, "proj_rec_out",
                   res=x2d, bn=MM_BN // 2)


def kernel(x, norm_g, final_g, rel_bias, attn_w_in, attn_kv_g, attn_w_uk, attn_w_uv, idx_k_g, idx_k_b,
           attn_w_out, rec_w_in, rec_conv_w, rec_conv_b, rec_w_a, rec_b_a, rec_w_x, rec_b_x,
           rec_lambda, rec_w_out):
    b, s, d = x.shape
    depth = norm_g.shape[0]
    x2d = x.reshape(b * s, d)
    for layer in range(depth):
        li = layer // 2
        if layer % 2 == 0:
            feeds_recurrent = layer + 1 < depth
            out = _attention_layer(x2d, b, s, norm_g[layer], attn_w_in[li], attn_kv_g[li],
                                   attn_w_uk[li], attn_w_uv[li], idx_k_g[li], idx_k_b[li],
                                   attn_w_out[li], rel_bias,
                                   copy_dtype=BF16 if feeds_recurrent else None)
            x2d, xb = out if feeds_recurrent else (out, None)
        else:
            x2d = _recurrent_layer(x2d, xb, b, s, norm_g[layer], rec_w_in[li], rec_conv_w[li],
                                   rec_conv_b[li], rec_w_a[li], rec_b_a[li], rec_w_x[li],
                                   rec_b_x[li], rec_lambda[li], rec_w_out[li])
    return _rmsnorm(x2d, final_g, x.dtype, "rmsnorm_final").reshape(b, s, d)
```

```python
import functools
import math

import numpy as np
import jax
import jax.numpy as jnp
from jax import lax
from jax.experimental import pallas as pl
from jax.experimental.pallas import tpu as pltpu

F32 = jnp.float32
BF16 = jnp.bfloat16
I32 = jnp.int32

HEAD_DIM = 128
KV_LORA = 512
N_IDX_HEADS = 32
IDX_DIM = 64
TOPK_MAX = 256
MOBA_BLOCK = 256
MOBA_TOPB = 3
RG_BLOCKS = 16
CONV_W = 4
RG_C = 8.0
NUM_BUCKETS = 32
MAX_DISTANCE = 128
EPS = 1e-6
ATTN_SCALE = HEAD_DIM ** -0.5
IDX_SCALE = (N_IDX_HEADS ** -0.5) * (IDX_DIM ** -0.5)
LOG2E = math.log2(math.e)

LANES = 128
SUBLANES = 8
WT_ROW_ALIGN = 16
VMEM_LIMIT_BYTES = 56 * 2 ** 20
MASK_NEG = -1e30
INT_MIN = np.int32(-2 ** 31)

ROW_TILE = 512
MM_BM = 1024
MM_BN = 1024
ATT_T = 256
ATT_TK = 2 * ATT_T
ATT_HB = 4
RG_PAIR = 2
RG_TC = 1024
RG_SEG_PAD = 8
RG_SCAN_UNROLL = 4
assert ATT_T == MOBA_BLOCK and ATT_T == 2 * LANES


def _cparams(sem):
    return pltpu.CompilerParams(dimension_semantics=sem, vmem_limit_bytes=VMEM_LIMIT_BYTES)


def _sigmoid(x):
    return 0.5 * jnp.tanh(0.5 * x) + 0.5


def _tile_lanes(x, n):
    return x if n == 1 else jnp.concatenate([x] * n, axis=1)


def _rmsnorm_kernel(x_ref, g_ref, o_ref):
    x = x_ref[...]
    ms = jnp.mean(x * x, axis=-1, keepdims=True)
    o_ref[...] = ((x * lax.rsqrt(ms + EPS)) * g_ref[...]).astype(o_ref.dtype)


def _rmsnorm(x2d, g, out_dtype, name):
    m, d = x2d.shape
    bm = min(ROW_TILE, m)
    return pl.pallas_call(
        _rmsnorm_kernel,
        grid=(m // bm,),
        in_specs=[pl.BlockSpec((bm, d), lambda i: (i, 0)),
                  pl.BlockSpec((1, d), lambda i: (0, 0))],
        out_specs=pl.BlockSpec((bm, d), lambda i: (i, 0)),
        out_shape=jax.ShapeDtypeStruct((m, d), out_dtype),
        compiler_params=_cparams(("parallel",)),
        name=name,
    )(x2d, g.reshape(1, d).astype(F32))


def _mm_kernel(*refs, n_parts, has_res, epilogue):
    o_ref = refs[-1]
    acc = jnp.dot(refs[0][...], refs[n_parts][...], preferred_element_type=F32)
    for p in range(1, n_parts):
        acc = acc + jnp.dot(refs[p][...], refs[n_parts + p][...], preferred_element_type=F32)
    if has_res:
        acc = refs[2 * n_parts][...] + acc
    if epilogue is not None:
        acc = epilogue(acc)
    o_ref[...] = acc.astype(o_ref.dtype)


def _matmul(xs, ws, out_dtype, name, res=None, n=None, bm=MM_BM, bn=MM_BN, epilogue=None):
    ws = [w if isinstance(w, tuple) else (w, 0, 0) for w in ws]
    m = xs[0].shape[0]
    n = ws[0][0].shape[1] if n is None else n
    bm = min(bm, m)
    bn = min(bn, n)
    assert m % bm == 0 and n % bn == 0
    in_specs = [pl.BlockSpec((bm, x.shape[1]), lambda i, j: (i, 0)) for x in xs]
    for x, (w, rb, cb) in zip(xs, ws):
        assert w.shape[0] % x.shape[1] == 0 and (cb + n // bn) * bn <= w.shape[1]
        in_specs.append(pl.BlockSpec((x.shape[1], bn), lambda i, j, rb=rb, cb=cb: (rb, cb + j)))
    args = [*xs, *[w for w, _, _ in ws]]
    if res is not None:
        in_specs.append(pl.BlockSpec((bm, bn), lambda i, j: (i, j)))
        args.append(res)
    return pl.pallas_call(
        functools.partial(_mm_kernel, n_parts=len(xs), has_res=res is not None, epilogue=epilogue),
        grid=(m // bm, n // bn),
        in_specs=in_specs,
        out_specs=pl.BlockSpec((bm, bn), lambda i, j: (i, j)),
        out_shape=jax.ShapeDtypeStruct((m, n), out_dtype),
        compiler_params=_cparams(("parallel", "parallel")),
        name=name,
    )(*args)


def _silu(x):
    return x * _sigmoid(x)


def _mm_nt_kernel(x_ref, wt_ref, o_ref, *, epilogue):
    acc = lax.dot_general(x_ref[...], wt_ref[...], (((1,), (1,)), ((), ())), preferred_element_type=F32)
    if epilogue is not None:
        acc = epilogue(acc)
    o_ref[...] = acc.astype(o_ref.dtype)


def _matmul_nt(x, wt, row0, n, out_dtype, name, bn=MM_BN, epilogue=None):
    m, k = x.shape
    bm = min(MM_BM, m)
    bn = min(bn, n)
    assert m % bm == 0 and n % bn == 0 and row0 % WT_ROW_ALIGN == 0 and bn % WT_ROW_ALIGN == 0
    assert row0 + n <= wt.shape[0] and wt.shape[1] == k
    return pl.pallas_call(
        functools.partial(_mm_nt_kernel, epilogue=epilogue),
        grid=(m // bm, n // bn),
        in_specs=[pl.BlockSpec((bm, k), lambda i, j: (i, 0)),
                  pl.BlockSpec((pl.Element(bn), pl.Element(k)),
                               lambda i, j: (pl.multiple_of(row0 + j * bn, WT_ROW_ALIGN), 0))],
        out_specs=pl.BlockSpec((bm, bn), lambda i, j: (i, j)),
        out_shape=jax.ShapeDtypeStruct((m, n), out_dtype),
        compiler_params=_cparams(("parallel", "parallel")),
        name=name,
    )(x, wt)


def _t5_bucket_np(dist):
    d = np.maximum(dist, 0)
    max_exact = NUM_BUCKETS // 2
    d_f = np.maximum(d, 1).astype(np.float32)
    ratio = np.log(d_f / np.float32(max_exact)) / np.float32(math.log(MAX_DISTANCE / max_exact))
    large = max_exact + (ratio * np.float32(NUM_BUCKETS - max_exact)).astype(np.int32)
    large = np.minimum(large, NUM_BUCKETS - 1)
    return np.where(d < max_exact, d, large)


def _bias_blocks(tab):
    n = LANES
    assert np.all(_t5_bucket_np(np.arange(n, 64 * n)) == NUM_BUCKETS - 1)
    h = tab.shape[1]
    rel = (tab - tab[NUM_BUCKETS - 1]).astype(F32) * LOG2E
    out = []
    for off in (0, n):
        buckets = _t5_bucket_np(off + n - 1 - np.arange(2 * n - 1))
        onehot = np.zeros((2 * n, NUM_BUCKETS), np.float32)
        onehot[np.arange(2 * n - 1), buckets] = 1.0
        w = jnp.dot(jnp.asarray(onehot), rel, precision=lax.Precision.HIGHEST).T
        t = jnp.tile(w, (1, n))[:, :n * (2 * n - 1)].reshape(h, n, 2 * n - 1)
        out.append(t[:, :, n - 1:])
    return jnp.stack(out)


def _dsa_prep_kernel(p_ref, kvg_ref, ig_ref, ib_ref, ckv_ref, kbd_ref, w_ref):
    p = p_ref[0]
    ts = p.shape[0]
    c = p[:, :KV_LORA]
    cn = (c * lax.rsqrt(jnp.mean(c * c, axis=-1, keepdims=True) + EPS)) * kvg_ref[...]
    ckv_ref[0] = cn.astype(BF16)
    k = p[:, KV_LORA:KV_LORA + IDX_DIM]
    mu = jnp.mean(k, axis=-1, keepdims=True)
    var = jnp.mean(jnp.square(k - mu), axis=-1, keepdims=True)
    kn = ((k - mu) * lax.rsqrt(var + EPS)) * ig_ref[...] + ib_ref[...]
    zeros = jnp.zeros((ts, LANES - IDX_DIM), F32)
    top = jnp.concatenate([kn, zeros], axis=1).T
    bot = jnp.concatenate([zeros, kn], axis=1).T
    kbd_ref[0, 0] = jnp.concatenate([top, bot], axis=1).astype(BF16)
    w_ref[0] = p[:, KV_LORA + LANES:KV_LORA + LANES + N_IDX_HEADS] * IDX_SCALE


def _dsa_prep(ps, kv_g, idx_g, idx_b):
    b, s, wid = ps.shape
    ts = ATT_T
    nkt = s // ts
    return pl.pallas_call(
        _dsa_prep_kernel,
        grid=(b, nkt),
        in_specs=[pl.BlockSpec((1, ts, wid), lambda bi, i: (bi, i, 0)),
                  pl.BlockSpec((1, KV_LORA), lambda bi, i: (0, 0)),
                  pl.BlockSpec((1, IDX_DIM), lambda bi, i: (0, 0)),
                  pl.BlockSpec((1, IDX_DIM), lambda bi, i: (0, 0))],
        out_specs=[pl.BlockSpec((1, ts, KV_LORA), lambda bi, i: (bi, i, 0)),
                   pl.BlockSpec((1, 1, LANES, 2 * ts), lambda bi, i: (bi, i, 0, 0)),
                   pl.BlockSpec((1, ts, N_IDX_HEADS), lambda bi, i: (bi, i, 0))],
        out_shape=[jax.ShapeDtypeStruct((b, s, KV_LORA), BF16),
                   jax.ShapeDtypeStruct((b, nkt, LANES, 2 * ts), BF16),
                   jax.ShapeDtypeStruct((b, s, N_IDX_HEADS), F32)],
        compiler_params=_cparams(("parallel", "parallel")),
        name="dsa_prep",
    )(ps, kv_g.reshape(1, -1).astype(F32), idx_g.reshape(1, -1).astype(F32),
      idx_b.reshape(1, -1).astype(F32))


def _dsa_index_kernel(qi_ref, w_ref, kbd_ref, mask_ref, key_ref, keyt_ref, wb_ref, t_ref, j_ref,
                      *, topk, nkt_total, seq_bits):
    tq = tk = ATT_T
    rep = tk // LANES
    sl = SUBLANES
    i = pl.program_id(1)
    t0 = i * tq
    nkt = i + 1

    row = t0 + lax.broadcasted_iota(I32, (tq, tk), 0)
    col = lax.broadcasted_iota(I32, (tq, tk), 1)
    row_t = lax.broadcasted_iota(I32, (tk, tq), 0)
    col_t = t0 + lax.broadcasted_iota(I32, (tk, tq), 1)
    sub8 = lax.broadcasted_iota(I32, (sl, tq), 0)

    def to_key(score):
        score = jnp.where(score == 0.0, 0.0, score)
        bits = lax.bitcast_convert_type(score, I32)
        return bits ^ ((bits >> 31) & np.int32(0x7FFFFFFF))

    for h in range(N_IDX_HEADS):
        wb_ref[h] = jnp.broadcast_to(w_ref[0, :, h:h + 1], (tq, LANES))

    def idx_body(kt, carry):
        kb = kbd_ref[0, kt]
        acc = jnp.zeros((tq, tk), F32)
        for hp in range(N_IDX_HEADS // 2):
            rel = jnp.dot(qi_ref[0, :, hp * LANES:(hp + 1) * LANES], kb,
                          preferred_element_type=F32)
            acc = acc + jnp.maximum(rel[:, :tk], 0.0) * _tile_lanes(wb_ref[2 * hp], rep)
            acc = acc + jnp.maximum(rel[:, tk:], 0.0) * _tile_lanes(wb_ref[2 * hp + 1], rep)
        key_ref[kt] = jnp.where(kt * tk + col <= row, to_key(acc), INT_MIN)
        keyt_ref[kt] = jnp.where(kt * tk + row_t <= col_t, to_key(acc.T), INT_MIN)
        return carry

    lax.fori_loop(0, nkt, idx_body, 0)

    def count(pred_fn):
        def body(kt, c):
            xs = [pred_fn(keyt_ref[kt, r * sl:(r + 1) * sl, :], kt * tk + r * sl) for r in range(tk // sl)]
            while len(xs) > 1:
                xs = [xs[a] + xs[a + 1] for a in range(0, len(xs), 2)]
            return c + xs[0]
        c = lax.fori_loop(0, nkt, body, jnp.zeros((sl, tq), F32))
        return jnp.broadcast_to(jnp.sum(c, axis=0, keepdims=True), (sl, tq))

    t_ref[...] = jnp.full((sl, tq), INT_MIN, I32)

    def bit_body(it, carry):
        cand = t_ref[...] ^ jnp.left_shift(jnp.int32(1), 31 - it)
        n_ge = count(lambda k, s0: jnp.where(k >= cand, 1.0, 0.0))
        t_ref[...] = jnp.where(n_ge >= topk, cand, t_ref[...])
        return carry

    lax.fori_loop(0, 32, bit_body, 0)

    thr = t_ref[...]
    n_gt = count(lambda k, s0: jnp.where(k > thr, 1.0, 0.0))
    n_ge = count(lambda k, s0: jnp.where(k >= thr, 1.0, 0.0))
    need = topk - n_gt
    j_ref[...] = jnp.full((sl, tq), 2 ** 30, I32)

    @pl.when(jnp.max(n_ge) > topk)
    def _():
        j_ref[...] = jnp.zeros((sl, tq), I32)

        def jbit_body(it, carry):
            cand = j_ref[...] | jnp.left_shift(jnp.int32(1), seq_bits - 1 - it)
            n_lt = count(lambda k, s0: jnp.where(k == thr, jnp.where(s0 + sub8 < cand, 1.0, 0.0), 0.0))
            j_ref[...] = jnp.where(n_lt < need, cand, j_ref[...])
            return carry

        lax.fori_loop(0, seq_bits, jbit_body, 0)

    def per_row(v8):
        wide = jnp.broadcast_to(lax.bitcast_convert_type(v8[0:1], F32), (LANES, tq))
        return _tile_lanes(lax.bitcast_convert_type(wide.T, I32), rep)

    thr_t = per_row(thr)
    jmax_t = per_row(j_ref[...])

    def mask_body(kt, carry):
        k = key_ref[kt]
        s_idx = kt * tk + col
        keep_tie = jnp.where(k == thr_t, jnp.where(s_idx <= jmax_t, 0.0, MASK_NEG), MASK_NEG)
        keep = jnp.where(k > thr_t, 0.0, keep_tie)
        mask_ref[0, 0, kt] = jnp.where(s_idx <= row, keep, MASK_NEG).astype(BF16)
        return carry

    lax.fori_loop(0, nkt, mask_body, 0)

    def fill_body(kt, carry):
        mask_ref[0, 0, kt] = jnp.full((tq, tk), MASK_NEG, BF16)
        return carry

    lax.fori_loop(nkt, nkt_total, fill_body, 0)


def _dsa_index(pqi, col_qi, kbd, widx):
    b, s, _ = pqi.shape
    tq = tk = ATT_T
    nkt = s // tk
    wq = N_IDX_HEADS * IDX_DIM
    topk = min(TOPK_MAX, s // 4)
    kern = functools.partial(_dsa_index_kernel, topk=topk, nkt_total=nkt,
                             seq_bits=max(1, (s - 1).bit_length()))
    return pl.pallas_call(
        kern,
        grid=(b, s // tq),
        in_specs=[pl.BlockSpec((1, tq, wq), lambda bi, i: (bi, i, col_qi)),
                  pl.BlockSpec((1, tq, N_IDX_HEADS), lambda bi, i: (bi, i, 0)),
                  pl.BlockSpec((1, nkt, LANES, 2 * tk), lambda bi, i: (bi, 0, 0, 0))],
        out_specs=pl.BlockSpec((1, 1, nkt, tq, tk), lambda bi, i: (bi, i, 0, 0, 0)),
        out_shape=jax.ShapeDtypeStruct((b, s // tq, nkt, tq, tk), BF16),
        scratch_shapes=[pltpu.VMEM((nkt, tq, tk), I32),
                        pltpu.VMEM((nkt, tk, tq), I32),
                        pltpu.VMEM((N_IDX_HEADS, tq, LANES), F32),
                        pltpu.VMEM((SUBLANES, tq), I32),
                        pltpu.VMEM((SUBLANES, tq), I32)],
        compiler_params=_cparams(("parallel", "parallel")),
        name="dsa_index",
    )(pqi, widx, kbd)


def _flash_kernel(*refs, mode, nb, topb):
    if mode == "mask":
        q_ref, k_ref, v_ref, z_ref, d_ref, mask_ref, y_ref = refs[:7]
    else:
        q_ref, k_ref, v_ref, z_ref, d_ref, y_ref, km_ref = refs[:7]
    qs_ref, acc_ref, m_ref, p_ref, alpha_ref, x_ref = refs[7:]
    tq = ATT_T
    tk = ATT_TK
    hb = q_ref.shape[2] // HEAD_DIM
    rep = tk // LANES
    i = pl.program_id(2)
    jd = i // 2
    par = i % 2
    pm = par.astype(F32)
    zero_blk = jnp.zeros((LANES, LANES), F32)
    ones_blk = jnp.ones((tk, LANES), BF16)

    if mode == "sel":
        @pl.when(i == 0)
        def _():
            km_ref[...] = jnp.zeros(km_ref.shape, F32)
            for h in range(hb):
                for n in range(nb):
                    kblk = k_ref[0, n * tq:(n + 1) * tq, h * HEAD_DIM:(h + 1) * HEAD_DIM].astype(F32)
                    km_ref[h, n:n + 1, :] = jnp.mean(kblk, axis=0, keepdims=True)
        nbp = -(-nb // SUBLANES) * SUBLANES
        blk = lax.broadcasted_iota(I32, (nbp, tq), 0).astype(F32)
        lane = lax.broadcasted_iota(I32, (tk, LANES), 1)
        second_half = lax.broadcasted_iota(I32, (tk, LANES), 0) >= tq

    for h in range(hb):
        q = q_ref[0, :, h * HEAD_DIM:(h + 1) * HEAD_DIM]
        qsc = (q.astype(F32) * (ATTN_SCALE * LOG2E)).astype(BF16)
        if mode == "sel":
            gate = lax.dot_general(km_ref[h, 0:nbp, :].astype(BF16), q, (((1,), (1,)), ((), ())),
                                   preferred_element_type=F32)
            gate = jnp.where(blk < i.astype(F32), gate, -jnp.inf)
            sel = jnp.full((nbp, tq), MASK_NEG, F32)
            for _ in range(topb):
                gmax = jnp.max(gate, axis=0, keepdims=True)
                first = jnp.min(jnp.where(gate == gmax, blk, float(nbp)), axis=0, keepdims=True)
                hit = jnp.where(blk == first, jnp.where(gmax > -jnp.inf, 1.0, 0.0), 0.0)
                sel = jnp.where(hit > 0.0, 0.0, sel)
                gate = jnp.where(blk == first, -jnp.inf, gate)
            selb = jnp.concatenate([sel, jnp.full((LANES - nbp, tq), MASK_NEG, F32)], axis=0).T
            qs_ref[h] = jnp.concatenate([qsc, selb.astype(BF16)], axis=1)
        else:
            qs_ref[h] = qsc
        m_ref[h] = jnp.full((tq, LANES), MASK_NEG, F32)
        acc_ref[h] = jnp.zeros((tq, 2 * HEAD_DIM), F32)
        p_ref[h] = jnp.zeros((tq, tk), BF16)
        alpha_ref[h] = jnp.ones((tq, LANES), F32)

    def qk_stage(j, is_diag):
        r0 = pl.multiple_of(j * tk, tk)
        if mode == "sel":
            first = jnp.where(is_diag, jnp.where(par == 1, 2 * jd, -1), 2 * j)
            second = jnp.where(is_diag, -1, 2 * j + 1)
            onehot = jnp.where(lane == jnp.where(second_half, second, first), 1.0, 0.0).astype(BF16)
        for h in range(hb):
            kblk = k_ref[0, pl.ds(r0, tk), h * HEAD_DIM:(h + 1) * HEAD_DIM]
            if mode == "sel":
                kblk = jnp.concatenate([kblk, onehot], axis=1)
            x_ref[h] = lax.dot_general(qs_ref[h], kblk, (((1,), (1,)), ((), ())),
                                       preferred_element_type=F32)

    def bias_tile(h, kind, j=None):
        d0, d1 = d_ref[0, h], d_ref[1, h]
        if kind == "past":
            near = jnp.where(j == jd - 1, 1.0 - pm, 0.0)
            top = [zero_blk, zero_blk, zero_blk, d1 * near]
            bot = [zero_blk] * 4
        else:
            top = [d0 * (1.0 - pm), d1 * pm, d0 * pm, zero_blk]
            bot = [d1 * (1.0 - pm), d0 * (1.0 - pm), d1 * pm, d0 * pm]
        return jnp.concatenate([jnp.concatenate(top, axis=1), jnp.concatenate(bot, axis=1)], axis=0)

    def softmax_stage(j, kind):
        if mode == "mask":
            mask_tile = jnp.concatenate([mask_ref[0, 0, 2 * j], mask_ref[0, 0, 2 * j + 1]],
                                        axis=1).astype(F32)
        elif kind == "diag":
            row = lax.broadcasted_iota(I32, (tq, tk), 0)
            col = lax.broadcasted_iota(I32, (tq, tk), 1)
            causal = col <= row + tq * par
        for h in range(hb):
            x = x_ref[h]
            if mode == "mask":
                x = x + mask_tile
            x = x + bias_tile(h, kind, j)
            if kind == "diag" and mode == "sel":
                x = jnp.where(causal, x, MASK_NEG)
            m_prev = m_ref[h]
            m_new = jnp.maximum(m_prev, jnp.max(x, axis=1, keepdims=True))
            alpha_ref[h] = jnp.exp2(m_prev - m_new)
            p_ref[h] = jnp.exp2(x - _tile_lanes(m_new, rep)).astype(BF16)
            m_ref[h] = m_new

    def pv_stage(j):
        r0 = pl.multiple_of(j * tk, tk)
        for h in range(hb):
            vblk = v_ref[0, pl.ds(r0, tk), h * HEAD_DIM:(h + 1) * HEAD_DIM]
            pv = jnp.dot(p_ref[h], jnp.concatenate([vblk, ones_blk], axis=1),
                         preferred_element_type=F32)
            acc_ref[h] = acc_ref[h] * _tile_lanes(alpha_ref[h], 2) + pv

    def step(j, kind):
        pv_stage(jnp.maximum(j - 1, 0))
        softmax_stage(j, kind)
        if kind != "diag":
            qk_stage(j + 1, j + 1 == jd)

    qk_stage(0, jd == 0)

    def past_body(j, carry):
        step(j, "past")
        return carry

    lax.fori_loop(0, jd, past_body, 0)
    step(jd, "diag")
    pv_stage(jd)

    for h in range(hb):
        acc = acc_ref[h]
        o = acc[:, :HEAD_DIM] / acc[:, HEAD_DIM:]
        zz = z_ref[0, :, h * HEAD_DIM:(h + 1) * HEAD_DIM].astype(F32)
        y_ref[0, :, h * HEAD_DIM:(h + 1) * HEAD_DIM] = (o * (zz * _sigmoid(zz))).astype(BF16)


def _flash_attention(q, k, v, z, dtab, n_heads, name, mask=None):
    (qa, cq), (ka, ck), (va, cv), (za, cz) = q, k, v, z
    b, s, _ = qa.shape
    tq = ATT_T
    tk = ATT_TK
    assert s % tk == 0
    nb = s // tq
    hb = ATT_HB
    wg = hb * HEAD_DIM
    mode = "sel" if mask is None else "mask"
    kern = functools.partial(_flash_kernel, mode=mode, nb=nb, topb=min(MOBA_TOPB, nb - 1))
    in_specs = [pl.BlockSpec((1, tq, wg), lambda bi, g, i: (bi, i, cq + g)),
                pl.BlockSpec((1, s, wg), lambda bi, g, i: (bi, 0, ck + g)),
                pl.BlockSpec((1, s, wg), lambda bi, g, i: (bi, 0, cv + g)),
                pl.BlockSpec((1, tq, wg), lambda bi, g, i: (bi, i, cz + g)),
                pl.BlockSpec((2, hb, LANES, LANES), lambda bi, g, i: (0, g, 0, 0))]
    args = [qa, ka, va, za, dtab]
    scratch = []
    if mode == "mask":
        in_specs.append(pl.BlockSpec((1, 1, nb, tq, tq), lambda bi, g, i: (bi, i, 0, 0, 0)))
        args.append(mask)
        kq = HEAD_DIM
    else:
        scratch.append(pltpu.VMEM((hb, LANES, HEAD_DIM), F32))
        kq = 2 * HEAD_DIM
    scratch += [pltpu.VMEM((hb, tq, kq), BF16),
                pltpu.VMEM((hb, tq, 2 * HEAD_DIM), F32),
                pltpu.VMEM((hb, tq, LANES), F32),
                pltpu.VMEM((hb, tq, tk), BF16),
                pltpu.VMEM((hb, tq, LANES), F32),
                pltpu.VMEM((hb, tq, tk), F32)]
    return pl.pallas_call(
        kern,
        grid=(b, n_heads // hb, nb),
        in_specs=in_specs,
        out_specs=pl.BlockSpec((1, tq, wg), lambda bi, g, i: (bi, i, g)),
        out_shape=jax.ShapeDtypeStruct((b, s, n_heads * HEAD_DIM), BF16),
        scratch_shapes=scratch,
        compiler_params=_cparams(("parallel", "parallel", "arbitrary")),
        name=name,
    )(*args)


def _softplus(x):
    return jnp.maximum(x, 0.0) + jnp.log1p(jnp.exp(-jnp.abs(x)))


def _rglru_kernel(xr_ref, gz_ref, cw_ref, cb_ref, wa_ref, ba_ref, wx_ref, bx_ref, lam_ref, y_ref,
                  xbuf_ref, a_ref, b_ref, h_ref):
    tc = xr_ref.shape[1]
    c = pl.program_id(2)
    halo = SUBLANES

    @pl.when(c == 0)
    def _():
        xbuf_ref[0:halo, :] = jnp.zeros((halo, xbuf_ref.shape[1]), F32)
        h_ref[...] = jnp.zeros(h_ref.shape, F32)

    @pl.when(c > 0)
    def _():
        xbuf_ref[0:halo, :] = xbuf_ref[tc:tc + halo, :]

    xbuf_ref[halo:halo + tc, :] = xr_ref[0]
    xfull = xbuf_ref[...]
    u = xfull * cw_ref[0:1, :]
    for j in range(1, CONV_W):
        u = xfull * cw_ref[j:j + 1, :] + pltpu.roll(u, 1, 0)
    xc = (u + cb_ref[...])[halo:, :]

    xcb = xc.astype(BF16)
    tr = jnp.tanh(jnp.dot(xcb, wa_ref[0], preferred_element_type=F32) + ba_ref[...])
    ti = jnp.tanh(jnp.dot(xcb, wx_ref[0], preferred_element_type=F32) + bx_ref[...])
    half_c = (-0.5 * RG_C * LOG2E) * _softplus(-lam_ref[...])
    a = jnp.exp2(tr * half_c + half_c)
    one_m_a2 = 1.0 - a * a
    mult = jnp.where(one_m_a2 > 0.0, one_m_a2 * lax.rsqrt(one_m_a2), 0.0)
    bt = (mult * xc) * (0.5 * ti + 0.5)

    nseg = SUBLANES
    seg = tc // nseg
    pitch = seg + RG_SEG_PAD
    nslab = a_ref.shape[0]
    for sl in range(nslab):
        for sg in range(nseg):
            a_ref[sl, sg * pitch:sg * pitch + seg, :] = a[sg * seg:(sg + 1) * seg, sl * LANES:(sl + 1) * LANES]
            b_ref[sl, sg * pitch:sg * pitch + seg, :] = bt[sg * seg:(sg + 1) * seg, sl * LANES:(sl + 1) * LANES]

    def seg_step(j, carry):
        out = []
        for sl in range(nslab):
            hl, pr = carry[2 * sl], carry[2 * sl + 1]
            rows = pl.ds(j, nseg, stride=pitch)
            a8 = a_ref[sl, rows, :]
            hl = a8 * hl + b_ref[sl, rows, :]
            pr = a8 * pr
            b_ref[sl, rows, :] = hl
            a_ref[sl, rows, :] = pr
            out += [hl, pr]
        return tuple(out)

    init = tuple(v for _ in range(nslab)
                 for v in (jnp.zeros((nseg, LANES), F32), jnp.ones((nseg, LANES), F32)))
    ends = lax.fori_loop(0, seg, seg_step, init, unroll=RG_SCAN_UNROLL)

    gate = gz_ref[0].astype(F32)
    for sl in range(nslab):
        h_end, p_end = ends[2 * sl], ends[2 * sl + 1]
        h_in = h_ref[:, sl * LANES:(sl + 1) * LANES]
        for sg in range(nseg):
            rows = slice(sg * pitch, sg * pitch + seg)
            hs = b_ref[sl, rows, :] + a_ref[sl, rows, :] * h_in
            y_ref[0, sg * seg:(sg + 1) * seg, sl * LANES:(sl + 1) * LANES] = (
                hs * gate[sg * seg:(sg + 1) * seg, sl * LANES:(sl + 1) * LANES]).astype(BF16)
            h_in = p_end[sg:sg + 1, :] * h_in + h_end[sg:sg + 1, :]
        h_ref[:, sl * LANES:(sl + 1) * LANES] = h_in


def _rglru(xr, gz, conv_w, conv_b, wa_g, b_a, wx_g, b_x, lam):
    b, s, d = xr.shape
    g = wa_g.shape[0]
    cg = d // g
    tc = min(RG_TC, s)
    assert ((tc // SUBLANES + RG_SEG_PAD) // SUBLANES) % 2 == 1 and cg % LANES == 0
    row = lambda v: v.reshape(1, d).astype(F32)
    vec_spec = pl.BlockSpec((1, cg), lambda bi, gi, c: (0, gi))
    return pl.pallas_call(
        _rglru_kernel,
        grid=(b, g, s // tc),
        in_specs=[pl.BlockSpec((1, tc, cg), lambda bi, gi, c: (bi, c, gi)),
                  pl.BlockSpec((1, tc, cg), lambda bi, gi, c: (bi, c, gi)),
                  pl.BlockSpec((CONV_W, cg), lambda bi, gi, c: (0, gi)),
                  vec_spec,
                  pl.BlockSpec((1, cg, cg), lambda bi, gi, c: (gi, 0, 0)),
                  vec_spec,
                  pl.BlockSpec((1, cg, cg), lambda bi, gi, c: (gi, 0, 0)),
                  vec_spec,
                  vec_spec],
        out_specs=pl.BlockSpec((1, tc, cg), lambda bi, gi, c: (bi, c, gi)),
        out_shape=jax.ShapeDtypeStruct((b, s, d), BF16),
        scratch_shapes=[pltpu.VMEM((tc + SUBLANES, cg), F32),
                        pltpu.VMEM((cg // LANES, tc + SUBLANES * RG_SEG_PAD, LANES), F32),
                        pltpu.VMEM((cg // LANES, tc + SUBLANES * RG_SEG_PAD, LANES), F32),
                        pltpu.VMEM((1, cg), F32)],
        compiler_params=_cparams(("parallel", "parallel", "arbitrary")),
        name="rglru",
    )(xr, gz, conv_w.astype(F32), row(conv_b), wa_g * 0.5, row(b_a) * 0.5, wx_g * 0.5, row(b_x) * 0.5,
      row(lam))


def _group_block_diag(w, pair):
    nb, k, _ = w.shape
    out = None
    for p in range(pair):
        lo, hi = p * k, (pair - 1 - p) * k
        blk = jnp.pad(w[p::pair], ((0, 0), (lo, hi), (lo, hi)))
        out = blk if out is None else out + blk
    return out


def _attention_layer(x2d, b, s, norm_g, w_in, kv_g, w_uk, w_uv, idx_g, idx_b, w_out, rel_bias):
    d_model = x2d.shape[1]
    n_heads = w_uk.shape[0]
    wq = n_heads * HEAD_DIM
    assert N_IDX_HEADS * IDX_DIM == wq
    sizes = (wq, KV_LORA, wq, IDX_DIM, N_IDX_HEADS, wq, wq, wq, 2 * wq)
    offs = [int(o) for o in np.concatenate([[0], np.cumsum(sizes)])]
    wt = w_in.T.astype(BF16)
    zrows = lambda n: jnp.zeros((n, d_model), BF16)
    wt_small = jnp.concatenate([wt[offs[1]:offs[2]], wt[offs[3]:offs[4]], zrows(LANES - IDX_DIM),
                                wt[offs[4]:offs[5]], zrows(LANES - N_IDX_HEADS)], axis=0)

    h = _rmsnorm(x2d, norm_g, BF16, "rmsnorm0")
    pqa = _matmul_nt(h, wt, offs[0], wq, BF16, "proj_attn_qa").reshape(b, s, wq)
    pqi = _matmul_nt(h, wt, offs[2], wq, BF16, "proj_attn_qi").reshape(b, s, wq)
    ptail = _matmul_nt(h, wt, offs[5], 5 * wq, BF16, "proj_attn_tail").reshape(b, s, 5 * wq)
    psmall = _matmul_nt(h, wt_small, 0, wt_small.shape[0], F32, "proj_attn_small").reshape(b, s, -1)

    ckv, kbd, widx = _dsa_prep(psmall, kv_g, idx_g, idx_b)
    mask = _dsa_index(pqi, 0, kbd, widx)
    w_kv = jnp.concatenate([jnp.transpose(w_uk, (1, 0, 2)).reshape(KV_LORA, wq),
                            jnp.transpose(w_uv, (1, 0, 2)).reshape(KV_LORA, wq)], axis=1).astype(BF16)
    kv = _matmul([ckv.reshape(b * s, KV_LORA)], [w_kv], BF16, "dsa_kv_up", bm=2 * MM_BM).reshape(b, s, 2 * wq)

    dtab_a = _bias_blocks(rel_bias[:, :n_heads])
    dtab_b = _bias_blocks(rel_bias[:, n_heads:])
    gpw = n_heads // ATT_HB
    ya = _flash_attention((pqa, 0), (kv, 0), (kv, gpw), (ptail, 3 * gpw), dtab_a, n_heads,
                          "dsa_attention", mask=mask)
    yb = _flash_attention((ptail, 0), (ptail, gpw), (ptail, 2 * gpw), (ptail, 4 * gpw), dtab_b,
                          n_heads, "moba_attention")

    w_out_b = w_out.astype(BF16)
    return _matmul([ya.reshape(b * s, wq), yb.reshape(b * s, wq)], [(w_out_b, 0, 0), (w_out_b, 1, 0)],
                   F32, "proj_attn_out", res=x2d, bm=MM_BM // 2)


def _recurrent_layer(x2d, b, s, norm_g, w_in, conv_w, conv_b, w_a, b_a, w_x, b_x, lam, w_out):
    d_rnn = conv_b.shape[0]
    h = _rmsnorm(x2d, norm_g, BF16, "rmsnorm1")
    w_in_b = w_in.astype(BF16)
    assert d_rnn % MM_BN == 0
    xr = _matmul([h], [(w_in_b, 0, 0)], F32, "proj_rec_x", n=d_rnn)
    gz = _matmul([h], [(w_in_b, 0, d_rnn // MM_BN)], BF16, "proj_rec_z", n=d_rnn, epilogue=_silu)
    y = _rglru(xr.reshape(b, s, d_rnn), gz.reshape(b, s, d_rnn), conv_w, conv_b,
               _group_block_diag(w_a.astype(BF16), RG_PAIR), b_a,
               _group_block_diag(w_x.astype(BF16), RG_PAIR), b_x, lam)
    return _matmul([y.reshape(b * s, d_rnn)], [w_out.astype(BF16)], F32, "proj_rec_out",
                   res=x2d, bm=MM_BM // 2)


def kernel(x, norm_g, final_g, rel_bias, attn_w_in, attn_kv_g, attn_w_uk, attn_w_uv, idx_k_g, idx_k_b,
           attn_w_out, rec_w_in, rec_conv_w, rec_conv_b, rec_w_a, rec_b_a, rec_w_x, rec_b_x,
           rec_lambda, rec_w_out):
    b, s, d = x.shape
    depth = norm_g.shape[0]
    x2d = x.reshape(b * s, d)
    for layer in range(depth):
        li = layer // 2
        if layer % 2 == 0:
            x2d = _attention_layer(x2d, b, s, norm_g[layer], attn_w_in[li], attn_kv_g[li],
                                   attn_w_uk[li], attn_w_uv[li], idx_k_g[li], idx_k_b[li],
                                   attn_w_out[li], rel_bias)
        else:
            x2d = _recurrent_layer(x2d, b, s, norm_g[layer], rec_w_in[li], rec_conv_w[li],
                                   rec_conv_b[li], rec_w_a[li], rec_b_a[li], rec_w_x[li],
                                   rec_b_x[li], rec_lambda[li], rec_w_out[li])
    return _rmsnorm(x2d, final_g, x.dtype, "rmsnorm_final").reshape(b, s, d)
```

```python
import functools
import math

import numpy as np
import jax
import jax.numpy as jnp
from jax import lax
from jax.experimental import pallas as pl
from jax.experimental.pallas import tpu as pltpu

F32 = jnp.float32
BF16 = jnp.bfloat16
I32 = jnp.int32

HEAD_DIM = 128
KV_LORA = 512
N_IDX_HEADS = 32
IDX_DIM = 64
TOPK_MAX = 256
MOBA_BLOCK = 256
MOBA_TOPB = 3
RG_BLOCKS = 16
CONV_W = 4
RG_C = 8.0
NUM_BUCKETS = 32
MAX_DISTANCE = 128
EPS = 1e-6
ATTN_SCALE = HEAD_DIM ** -0.5
IDX_SCALE = (N_IDX_HEADS ** -0.5) * (IDX_DIM ** -0.5)
LOG2E = math.log2(math.e)

LANES = 128
SUBLANES = 8
WT_ROW_ALIGN = 16
VMEM_LIMIT_BYTES = 56 * 2 ** 20
MASK_NEG = -1e30
INT_MIN = np.int32(-2 ** 31)
KEY_BYTES = 4

ROW_TILE = 256
MM_BM = 1024
MM_BN = 1024
ATT_T = 256
ATT_TK = 2 * ATT_T
ATT_HB = 4
RG_PAIR = 2
RG_TC = 1024
RG_SEG_PAD = 8
RG_SCAN_UNROLL = 4
assert ATT_T == MOBA_BLOCK and ATT_T == 2 * LANES


def _cparams(sem):
    return pltpu.CompilerParams(dimension_semantics=sem, vmem_limit_bytes=VMEM_LIMIT_BYTES)


def _sigmoid(x):
    return 0.5 * jnp.tanh(0.5 * x) + 0.5


def _tile_rows(x, n):
    return x if n == 1 else jnp.concatenate([x] * n, axis=0)


def _tile_lanes(x, n):
    return x if n == 1 else jnp.concatenate([x] * n, axis=1)


def _rmsnorm_kernel(x_ref, g_ref, o_ref):
    x = x_ref[...]
    ms = jnp.mean(x * x, axis=-1, keepdims=True)
    o_ref[...] = ((x * lax.rsqrt(ms + EPS)) * g_ref[...]).astype(o_ref.dtype)


def _rmsnorm(x2d, g, out_dtype, name):
    m, d = x2d.shape
    bm = min(ROW_TILE, m)
    return pl.pallas_call(
        _rmsnorm_kernel,
        grid=(m // bm,),
        in_specs=[pl.BlockSpec((bm, d), lambda i: (i, 0)),
                  pl.BlockSpec((1, d), lambda i: (0, 0))],
        out_specs=pl.BlockSpec((bm, d), lambda i: (i, 0)),
        out_shape=jax.ShapeDtypeStruct((m, d), out_dtype),
        compiler_params=_cparams(("parallel",)),
        name=name,
    )(x2d, g.reshape(1, d).astype(F32))


def _mm_kernel(*refs, n_parts, has_res, epilogue):
    o_ref = refs[-1]
    acc = jnp.dot(refs[0][...], refs[n_parts][...], preferred_element_type=F32)
    for p in range(1, n_parts):
        acc = acc + jnp.dot(refs[p][...], refs[n_parts + p][...], preferred_element_type=F32)
    if has_res:
        acc = refs[2 * n_parts][...] + acc
    if epilogue is not None:
        acc = epilogue(acc)
    o_ref[...] = acc.astype(o_ref.dtype)


def _matmul(xs, ws, out_dtype, name, res=None, n=None, bm=MM_BM, bn=MM_BN, epilogue=None):
    ws = [w if isinstance(w, tuple) else (w, 0, 0) for w in ws]
    m = xs[0].shape[0]
    n = ws[0][0].shape[1] if n is None else n
    bm = min(bm, m)
    bn = min(bn, n)
    assert m % bm == 0 and n % bn == 0
    in_specs = [pl.BlockSpec((bm, x.shape[1]), lambda i, j: (i, 0)) for x in xs]
    for x, (w, rb, cb) in zip(xs, ws):
        assert w.shape[0] % x.shape[1] == 0 and (cb + n // bn) * bn <= w.shape[1]
        in_specs.append(pl.BlockSpec((x.shape[1], bn), lambda i, j, rb=rb, cb=cb: (rb, cb + j)))
    args = [*xs, *[w for w, _, _ in ws]]
    if res is not None:
        in_specs.append(pl.BlockSpec((bm, bn), lambda i, j: (i, j)))
        args.append(res)
    return pl.pallas_call(
        functools.partial(_mm_kernel, n_parts=len(xs), has_res=res is not None, epilogue=epilogue),
        grid=(m // bm, n // bn),
        in_specs=in_specs,
        out_specs=pl.BlockSpec((bm, bn), lambda i, j: (i, j)),
        out_shape=jax.ShapeDtypeStruct((m, n), out_dtype),
        compiler_params=_cparams(("parallel", "parallel")),
        name=name,
    )(*args)


def _silu(x):
    return x * _sigmoid(x)


def _mm_nt_kernel(x_ref, wt_ref, o_ref, *, epilogue):
    acc = lax.dot_general(x_ref[...], wt_ref[...], (((1,), (1,)), ((), ())), preferred_element_type=F32)
    if epilogue is not None:
        acc = epilogue(acc)
    o_ref[...] = acc.astype(o_ref.dtype)


def _matmul_nt(x, wt, row0, n, out_dtype, name, bn=MM_BN, epilogue=None):
    m, k = x.shape
    bm = min(MM_BM, m)
    bn = min(bn, n)
    assert m % bm == 0 and n % bn == 0 and row0 % WT_ROW_ALIGN == 0 and bn % WT_ROW_ALIGN == 0
    assert row0 + n <= wt.shape[0] and wt.shape[1] == k
    return pl.pallas_call(
        functools.partial(_mm_nt_kernel, epilogue=epilogue),
        grid=(m // bm, n // bn),
        in_specs=[pl.BlockSpec((bm, k), lambda i, j: (i, 0)),
                  pl.BlockSpec((pl.Element(bn), pl.Element(k)),
                               lambda i, j: (pl.multiple_of(row0 + j * bn, WT_ROW_ALIGN), 0))],
        out_specs=pl.BlockSpec((bm, bn), lambda i, j: (i, j)),
        out_shape=jax.ShapeDtypeStruct((m, n), out_dtype),
        compiler_params=_cparams(("parallel", "parallel")),
        name=name,
    )(x, wt)


def _t5_bucket_np(dist):
    d = np.maximum(dist, 0)
    max_exact = NUM_BUCKETS // 2
    d_f = np.maximum(d, 1).astype(np.float32)
    ratio = np.log(d_f / np.float32(max_exact)) / np.float32(math.log(MAX_DISTANCE / max_exact))
    large = max_exact + (ratio * np.float32(NUM_BUCKETS - max_exact)).astype(np.int32)
    large = np.minimum(large, NUM_BUCKETS - 1)
    return np.where(d < max_exact, d, large)


def _bias_blocks(tab):
    n = LANES
    assert np.all(_t5_bucket_np(np.arange(n, 64 * n)) == NUM_BUCKETS - 1)
    h = tab.shape[1]
    rel = (tab - tab[NUM_BUCKETS - 1]).astype(F32) * LOG2E
    out = []
    for off in (0, n):
        buckets = _t5_bucket_np(off + n - 1 - np.arange(2 * n - 1))
        onehot = np.zeros((2 * n, NUM_BUCKETS), np.float32)
        onehot[np.arange(2 * n - 1), buckets] = 1.0
        w = jnp.dot(jnp.asarray(onehot), rel, precision=lax.Precision.HIGHEST).T
        t = jnp.tile(w, (1, n))[:, :n * (2 * n - 1)].reshape(h, n, 2 * n - 1)
        out.append(t[:, :, n - 1:])
    return jnp.stack(out)


def _dsa_prep_kernel(p_ref, kvg_ref, ig_ref, ib_ref, ckv_ref, kbd_ref, w_ref):
    p = p_ref[0]
    ts = p.shape[0]
    c = p[:, :KV_LORA]
    cn = (c * lax.rsqrt(jnp.mean(c * c, axis=-1, keepdims=True) + EPS)) * kvg_ref[...]
    ckv_ref[0] = cn.astype(BF16)
    k = p[:, KV_LORA:KV_LORA + IDX_DIM]
    mu = jnp.mean(k, axis=-1, keepdims=True)
    var = jnp.mean(jnp.square(k - mu), axis=-1, keepdims=True)
    kn = ((k - mu) * lax.rsqrt(var + EPS)) * ig_ref[...] + ib_ref[...]
    zeros = jnp.zeros((ts, LANES - IDX_DIM), F32)
    top = jnp.concatenate([kn, zeros], axis=1).T
    bot = jnp.concatenate([zeros, kn], axis=1).T
    kbd_ref[0, 0] = jnp.concatenate([top, bot], axis=1).astype(BF16)
    w_ref[0] = p[:, KV_LORA + LANES:KV_LORA + LANES + N_IDX_HEADS] * IDX_SCALE


def _dsa_prep(ps, kv_g, idx_g, idx_b):
    b, s, wid = ps.shape
    ts = ATT_T
    nkt = s // ts
    return pl.pallas_call(
        _dsa_prep_kernel,
        grid=(b, nkt),
        in_specs=[pl.BlockSpec((1, ts, wid), lambda bi, i: (bi, i, 0)),
                  pl.BlockSpec((1, KV_LORA), lambda bi, i: (0, 0)),
                  pl.BlockSpec((1, IDX_DIM), lambda bi, i: (0, 0)),
                  pl.BlockSpec((1, IDX_DIM), lambda bi, i: (0, 0))],
        out_specs=[pl.BlockSpec((1, ts, KV_LORA), lambda bi, i: (bi, i, 0)),
                   pl.BlockSpec((1, 1, LANES, 2 * ts), lambda bi, i: (bi, i, 0, 0)),
                   pl.BlockSpec((1, ts, N_IDX_HEADS), lambda bi, i: (bi, i, 0))],
        out_shape=[jax.ShapeDtypeStruct((b, s, KV_LORA), BF16),
                   jax.ShapeDtypeStruct((b, nkt, LANES, 2 * ts), BF16),
                   jax.ShapeDtypeStruct((b, s, N_IDX_HEADS), F32)],
        compiler_params=_cparams(("parallel", "parallel")),
        name="dsa_prep",
    )(ps, kv_g.reshape(1, -1).astype(F32), idx_g.reshape(1, -1).astype(F32),
      idx_b.reshape(1, -1).astype(F32))


def _dsa_index_kernel(qi_ref, w_ref, kbd_ref, mask_ref, key_ref, plane_ref, wb_ref, t_ref, j_ref,
                      *, topk, nkt_total, seq_bits):
    tq = tk = ATT_T
    rep = tk // LANES
    sl = SUBLANES
    i = pl.program_id(1)
    t0 = i * tq
    nkt = i + 1

    row = t0 + lax.broadcasted_iota(I32, (tq, tk), 0)
    col = lax.broadcasted_iota(I32, (tq, tk), 1)
    row_t = lax.broadcasted_iota(I32, (tk, tq), 0)
    col_t = t0 + lax.broadcasted_iota(I32, (tk, tq), 1)

    def to_key(score):
        score = jnp.where(score == 0.0, 0.0, score)
        bits = lax.bitcast_convert_type(score, I32)
        return bits ^ ((bits >> 31) & np.int32(0x7FFFFFFF))

    for h in range(N_IDX_HEADS):
        wb_ref[h] = jnp.broadcast_to(w_ref[0, :, h:h + 1], (tq, LANES))

    def idx_body(kt, carry):
        kb = kbd_ref[0, kt]
        acc = jnp.zeros((tq, tk), F32)
        for hp in range(N_IDX_HEADS // 2):
            rel = jnp.dot(qi_ref[0, :, hp * LANES:(hp + 1) * LANES], kb,
                          preferred_element_type=F32)
            acc = acc + jnp.maximum(rel[:, :tk], 0.0) * _tile_lanes(wb_ref[2 * hp], rep)
            acc = acc + jnp.maximum(rel[:, tk:], 0.0) * _tile_lanes(wb_ref[2 * hp + 1], rep)
        key_ref[kt] = jnp.where(kt * tk + col <= row, to_key(acc), INT_MIN)
        ukey = jnp.where(kt * tk + row_t <= col_t, to_key(acc.T), INT_MIN) ^ INT_MIN
        for bi in range(KEY_BYTES):
            byte = (ukey >> (8 * (KEY_BYTES - 1 - bi))) & np.int32(0xFF)
            plane_ref[bi, kt] = byte.astype(F32).astype(BF16)
        return carry

    lax.fori_loop(0, nkt, idx_body, 0)

    one, zero = jnp.ones((), BF16), jnp.zeros((), BF16)

    def count(plane, pred_fn):
        rows = 2 * sl
        def body(kt, c):
            xs = [pred_fn(plane_ref[plane, kt, r * rows:(r + 1) * rows, :], kt * tk + r * rows)
                  for r in range(tk // rows)]
            while len(xs) > 1:
                xs = [xs[a] + xs[a + 1] for a in range(0, len(xs), 2)]
            return c + xs[0]
        c = lax.fori_loop(0, nkt, body, jnp.zeros((rows, tq), BF16))
        return jnp.broadcast_to(jnp.sum(c.astype(F32), axis=0, keepdims=True), (sl, tq))

    def rep16(v8):
        return jnp.concatenate([v8, v8], axis=0).astype(BF16)

    target = jnp.full((sl, tq), float(topk), F32)
    thr_u = jnp.zeros((sl, tq), I32)
    for bi in range(KEY_BYTES):
        t_ref[...] = jnp.zeros((sl, tq), F32)

        def bit_body(it, carry, bi=bi, target=target):
            cand = t_ref[...] + jnp.left_shift(jnp.int32(1), 7 - it).astype(F32)
            c16 = rep16(cand)
            n_ge = count(bi, lambda d, s0: jnp.where(d >= c16, one, zero))
            t_ref[...] = jnp.where(n_ge >= target, cand, t_ref[...])
            return carry

        lax.fori_loop(0, 8, bit_body, 0)
        tb = t_ref[...]
        tb16 = rep16(tb)
        thr_u = (thr_u << 8) | tb.astype(I32)
        if bi + 1 < KEY_BYTES:
            target = target - count(bi, lambda d, s0: jnp.where(d > tb16, one, zero))

            def narrow_body(kt, carry, bi=bi):
                plane_ref[bi + 1, kt] = jnp.where(plane_ref[bi, kt] == _tile_rows(tb16, tk // (2 * sl)),
                                                  plane_ref[bi + 1, kt], -one)
                return carry

            lax.fori_loop(0, nkt, narrow_body, 0)

    last = KEY_BYTES - 1
    n_gt = count(last, lambda d, s0: jnp.where(d > tb16, one, zero))
    n_ge = count(last, lambda d, s0: jnp.where(d >= tb16, one, zero))
    need = target - n_gt
    j_ref[...] = jnp.full((sl, tq), 2 ** 30, I32)
    sub16 = lax.broadcasted_iota(I32, (2 * sl, tq), 0)

    @pl.when(jnp.max(n_ge - target) > 0.0)
    def _():
        j_ref[...] = jnp.zeros((sl, tq), I32)

        def jbit_body(it, carry):
            cand = j_ref[...] | jnp.left_shift(jnp.int32(1), seq_bits - 1 - it)
            c2 = jnp.concatenate([cand, cand], axis=0)
            n_lt = count(last, lambda d, s0: jnp.where(
                d == tb16, jnp.where(s0 + sub16 < c2, 1.0, 0.0).astype(BF16), zero))
            j_ref[...] = jnp.where(n_lt < need, cand, j_ref[...])
            return carry

        lax.fori_loop(0, seq_bits, jbit_body, 0)

    thr = thr_u ^ INT_MIN

    def per_row(v8):
        wide = jnp.broadcast_to(lax.bitcast_convert_type(v8[0:1], F32), (LANES, tq))
        return _tile_lanes(lax.bitcast_convert_type(wide.T, I32), rep)

    thr_t = per_row(thr)
    jmax_t = per_row(j_ref[...])

    def mask_body(kt, carry):
        k = key_ref[kt]
        s_idx = kt * tk + col
        keep_tie = jnp.where(k == thr_t, jnp.where(s_idx <= jmax_t, 0.0, MASK_NEG), MASK_NEG)
        keep = jnp.where(k > thr_t, 0.0, keep_tie)
        mask_ref[0, 0, kt] = jnp.where(s_idx <= row, keep, MASK_NEG).astype(BF16)
        return carry

    lax.fori_loop(0, nkt, mask_body, 0)

    def fill_body(kt, carry):
        mask_ref[0, 0, kt] = jnp.full((tq, tk), MASK_NEG, BF16)
        return carry

    lax.fori_loop(nkt, nkt_total, fill_body, 0)


def _dsa_index(pqi, col_qi, kbd, widx):
    b, s, _ = pqi.shape
    tq = tk = ATT_T
    nkt = s // tk
    wq = N_IDX_HEADS * IDX_DIM
    topk = min(TOPK_MAX, s // 4)
    kern = functools.partial(_dsa_index_kernel, topk=topk, nkt_total=nkt,
                             seq_bits=max(1, (s - 1).bit_length()))
    return pl.pallas_call(
        kern,
        grid=(b, s // tq),
        in_specs=[pl.BlockSpec((1, tq, wq), lambda bi, i: (bi, i, col_qi)),
                  pl.BlockSpec((1, tq, N_IDX_HEADS), lambda bi, i: (bi, i, 0)),
                  pl.BlockSpec((1, nkt, LANES, 2 * tk), lambda bi, i: (bi, 0, 0, 0))],
        out_specs=pl.BlockSpec((1, 1, nkt, tq, tk), lambda bi, i: (bi, i, 0, 0, 0)),
        out_shape=jax.ShapeDtypeStruct((b, s // tq, nkt, tq, tk), BF16),
        scratch_shapes=[pltpu.VMEM((nkt, tq, tk), I32),
                        pltpu.VMEM((KEY_BYTES, nkt, tk, tq), BF16),
                        pltpu.VMEM((N_IDX_HEADS, tq, LANES), F32),
                        pltpu.VMEM((SUBLANES, tq), F32),
                        pltpu.VMEM((SUBLANES, tq), I32)],
        compiler_params=_cparams(("parallel", "parallel")),
        name="dsa_index",
    )(pqi, widx, kbd)


def _flash_kernel(*refs, mode, nb, topb):
    if mode == "mask":
        q_ref, k_ref, v_ref, z_ref, d_ref, mask_ref, y_ref = refs[:7]
    else:
        q_ref, k_ref, v_ref, z_ref, d_ref, y_ref, km_ref = refs[:7]
    qs_ref, acc_ref, m_ref, p_ref, alpha_ref, x_ref = refs[7:]
    tq = ATT_T
    tk = ATT_TK
    hb = q_ref.shape[2] // HEAD_DIM
    rep = tk // LANES
    i = pl.program_id(2)
    jd = i // 2
    par = i % 2
    pm = par.astype(F32)
    zero_blk = jnp.zeros((LANES, LANES), F32)
    ones_blk = jnp.ones((tk, LANES), BF16)

    if mode == "sel":
        @pl.when(i == 0)
        def _():
            km_ref[...] = jnp.zeros(km_ref.shape, F32)
            for h in range(hb):
                for n in range(nb):
                    kblk = k_ref[0, n * tq:(n + 1) * tq, h * HEAD_DIM:(h + 1) * HEAD_DIM].astype(F32)
                    km_ref[h, n:n + 1, :] = jnp.mean(kblk, axis=0, keepdims=True)
        nbp = -(-nb // SUBLANES) * SUBLANES
        blk = lax.broadcasted_iota(I32, (nbp, tq), 0).astype(F32)
        lane = lax.broadcasted_iota(I32, (tk, LANES), 1)
        second_half = lax.broadcasted_iota(I32, (tk, LANES), 0) >= tq

    for h in range(hb):
        q = q_ref[0, :, h * HEAD_DIM:(h + 1) * HEAD_DIM]
        qsc = (q.astype(F32) * (ATTN_SCALE * LOG2E)).astype(BF16)
        if mode == "sel":
            gate = lax.dot_general(km_ref[h, 0:nbp, :].astype(BF16), q, (((1,), (1,)), ((), ())),
                                   preferred_element_type=F32)
            gate = jnp.where(blk < i.astype(F32), gate, -jnp.inf)
            sel = jnp.full((nbp, tq), MASK_NEG, F32)
            for _ in range(topb):
                gmax = jnp.max(gate, axis=0, keepdims=True)
                first = jnp.min(jnp.where(gate == gmax, blk, float(nbp)), axis=0, keepdims=True)
                hit = jnp.where(blk == first, jnp.where(gmax > -jnp.inf, 1.0, 0.0), 0.0)
                sel = jnp.where(hit > 0.0, 0.0, sel)
                gate = jnp.where(blk == first, -jnp.inf, gate)
            selb = jnp.concatenate([sel, jnp.full((LANES - nbp, tq), MASK_NEG, F32)], axis=0).T
            qs_ref[h] = jnp.concatenate([qsc, selb.astype(BF16)], axis=1)
        else:
            qs_ref[h] = qsc
        m_ref[h] = jnp.full((tq, LANES), MASK_NEG, F32)
        acc_ref[h] = jnp.zeros((tq, 2 * HEAD_DIM), F32)
        p_ref[h] = jnp.zeros((tq, tk), BF16)
        alpha_ref[h] = jnp.ones((tq, LANES), F32)

    def qk_stage(j, is_diag):
        r0 = pl.multiple_of(j * tk, tk)
        if mode == "sel":
            first = jnp.where(is_diag, jnp.where(par == 1, 2 * jd, -1), 2 * j)
            second = jnp.where(is_diag, -1, 2 * j + 1)
            onehot = jnp.where(lane == jnp.where(second_half, second, first), 1.0, 0.0).astype(BF16)
        for h in range(hb):
            kblk = k_ref[0, pl.ds(r0, tk), h * HEAD_DIM:(h + 1) * HEAD_DIM]
            if mode == "sel":
                kblk = jnp.concatenate([kblk, onehot], axis=1)
            x_ref[h] = lax.dot_general(qs_ref[h], kblk, (((1,), (1,)), ((), ())),
                                       preferred_element_type=F32)

    def bias_tile(h, kind, j=None):
        d0, d1 = d_ref[0, h], d_ref[1, h]
        if kind == "past":
            near = jnp.where(j == jd - 1, 1.0 - pm, 0.0)
            top = [zero_blk, zero_blk, zero_blk, d1 * near]
            bot = [zero_blk] * 4
        else:
            top = [d0 * (1.0 - pm), d1 * pm, d0 * pm, zero_blk]
            bot = [d1 * (1.0 - pm), d0 * (1.0 - pm), d1 * pm, d0 * pm]
        return jnp.concatenate([jnp.concatenate(top, axis=1), jnp.concatenate(bot, axis=1)], axis=0)

    def softmax_stage(j, kind):
        if mode == "mask":
            mask_tile = jnp.concatenate([mask_ref[0, 0, 2 * j], mask_ref[0, 0, 2 * j + 1]],
                                        axis=1).astype(F32)
        elif kind == "diag":
            row = lax.broadcasted_iota(I32, (tq, tk), 0)
            col = lax.broadcasted_iota(I32, (tq, tk), 1)
            causal = col <= row + tq * par
        for h in range(hb):
            x = x_ref[h]
            if mode == "mask":
                x = x + mask_tile
            x = x + bias_tile(h, kind, j)
            if kind == "diag" and mode == "sel":
                x = jnp.where(causal, x, MASK_NEG)
            m_prev = m_ref[h]
            m_new = jnp.maximum(m_prev, jnp.max(x, axis=1, keepdims=True))
            alpha_ref[h] = jnp.exp2(m_prev - m_new)
            p_ref[h] = jnp.exp2(x - _tile_lanes(m_new, rep)).astype(BF16)
            m_ref[h] = m_new

    def pv_stage(j):
        r0 = pl.multiple_of(j * tk, tk)
        for h in range(hb):
            vblk = v_ref[0, pl.ds(r0, tk), h * HEAD_DIM:(h + 1) * HEAD_DIM]
            pv = jnp.dot(p_ref[h], jnp.concatenate([vblk, ones_blk], axis=1),
                         preferred_element_type=F32)
            acc_ref[h] = acc_ref[h] * _tile_lanes(alpha_ref[h], 2) + pv

    def step(j, kind):
        pv_stage(jnp.maximum(j - 1, 0))
        softmax_stage(j, kind)
        if kind != "diag":
            qk_stage(j + 1, j + 1 == jd)

    qk_stage(0, jd == 0)

    def past_body(j, carry):
        step(j, "past")
        return carry

    lax.fori_loop(0, jd, past_body, 0)
    step(jd, "diag")
    pv_stage(jd)

    for h in range(hb):
        acc = acc_ref[h]
        o = acc[:, :HEAD_DIM] / acc[:, HEAD_DIM:]
        zz = z_ref[0, :, h * HEAD_DIM:(h + 1) * HEAD_DIM].astype(F32)
        y_ref[0, :, h * HEAD_DIM:(h + 1) * HEAD_DIM] = (o * (zz * _sigmoid(zz))).astype(BF16)


def _flash_attention(q, k, v, z, dtab, n_heads, name, mask=None):
    (qa, cq), (ka, ck), (va, cv), (za, cz) = q, k, v, z
    b, s, _ = qa.shape
    tq = ATT_T
    tk = ATT_TK
    assert s % tk == 0
    nb = s // tq
    hb = ATT_HB
    wg = hb * HEAD_DIM
    mode = "sel" if mask is None else "mask"
    kern = functools.partial(_flash_kernel, mode=mode, nb=nb, topb=min(MOBA_TOPB, nb - 1))
    in_specs = [pl.BlockSpec((1, tq, wg), lambda bi, g, i: (bi, i, cq + g)),
                pl.BlockSpec((1, s, wg), lambda bi, g, i: (bi, 0, ck + g)),
                pl.BlockSpec((1, s, wg), lambda bi, g, i: (bi, 0, cv + g)),
                pl.BlockSpec((1, tq, wg), lambda bi, g, i: (bi, i, cz + g)),
                pl.BlockSpec((2, hb, LANES, LANES), lambda bi, g, i: (0, g, 0, 0))]
    args = [qa, ka, va, za, dtab]
    scratch = []
    if mode == "mask":
        in_specs.append(pl.BlockSpec((1, 1, nb, tq, tq), lambda bi, g, i: (bi, i, 0, 0, 0)))
        args.append(mask)
        kq = HEAD_DIM
    else:
        scratch.append(pltpu.VMEM((hb, LANES, HEAD_DIM), F32))
        kq = 2 * HEAD_DIM
    scratch += [pltpu.VMEM((hb, tq, kq), BF16),
                pltpu.VMEM((hb, tq, 2 * HEAD_DIM), F32),
                pltpu.VMEM((hb, tq, LANES), F32),
                pltpu.VMEM((hb, tq, tk), BF16),
                pltpu.VMEM((hb, tq, LANES), F32),
                pltpu.VMEM((hb, tq, tk), F32)]
    return pl.pallas_call(
        kern,
        grid=(b, n_heads // hb, nb),
        in_specs=in_specs,
        out_specs=pl.BlockSpec((1, tq, wg), lambda bi, g, i: (bi, i, g)),
        out_shape=jax.ShapeDtypeStruct((b, s, n_heads * HEAD_DIM), BF16),
        scratch_shapes=scratch,
        compiler_params=_cparams(("parallel", "parallel", "arbitrary")),
        name=name,
    )(*args)


def _softplus(x):
    return jnp.maximum(x, 0.0) + jnp.log1p(jnp.exp(-jnp.abs(x)))


def _rglru_kernel(xr_ref, gz_ref, cw_ref, cb_ref, wa_ref, ba_ref, wx_ref, bx_ref, lam_ref, y_ref,
                  xbuf_ref, a_ref, b_ref, h_ref):
    tc = xr_ref.shape[1]
    c = pl.program_id(2)
    halo = SUBLANES

    @pl.when(c == 0)
    def _():
        xbuf_ref[0:halo, :] = jnp.zeros((halo, xbuf_ref.shape[1]), F32)
        h_ref[...] = jnp.zeros(h_ref.shape, F32)

    @pl.when(c > 0)
    def _():
        xbuf_ref[0:halo, :] = xbuf_ref[tc:tc + halo, :]

    xbuf_ref[halo:halo + tc, :] = xr_ref[0]
    xfull = xbuf_ref[...]
    u = xfull * cw_ref[0:1, :]
    for j in range(1, CONV_W):
        u = xfull * cw_ref[j:j + 1, :] + pltpu.roll(u, 1, 0)
    xc = (u + cb_ref[...])[halo:, :]

    xcb = xc.astype(BF16)
    tr = jnp.tanh(jnp.dot(xcb, wa_ref[0], preferred_element_type=F32) + ba_ref[...])
    ti = jnp.tanh(jnp.dot(xcb, wx_ref[0], preferred_element_type=F32) + bx_ref[...])
    half_c = (-0.5 * RG_C * LOG2E) * _softplus(-lam_ref[...])
    a = jnp.exp2(tr * half_c + half_c)
    one_m_a2 = 1.0 - a * a
    mult = jnp.where(one_m_a2 > 0.0, one_m_a2 * lax.rsqrt(one_m_a2), 0.0)
    bt = (mult * xc) * (0.5 * ti + 0.5)

    nseg = SUBLANES
    seg = tc // nseg
    pitch = seg + RG_SEG_PAD
    nslab = a_ref.shape[0]
    for sl in range(nslab):
        for sg in range(nseg):
            a_ref[sl, sg * pitch:sg * pitch + seg, :] = a[sg * seg:(sg + 1) * seg, sl * LANES:(sl + 1) * LANES]
            b_ref[sl, sg * pitch:sg * pitch + seg, :] = bt[sg * seg:(sg + 1) * seg, sl * LANES:(sl + 1) * LANES]

    def seg_step(j, carry):
        out = []
        for sl in range(nslab):
            hl, pr = carry[2 * sl], carry[2 * sl + 1]
            rows = pl.ds(j, nseg, stride=pitch)
            a8 = a_ref[sl, rows, :]
            hl = a8 * hl + b_ref[sl, rows, :]
            pr = a8 * pr
            b_ref[sl, rows, :] = hl
            a_ref[sl, rows, :] = pr
            out += [hl, pr]
        return tuple(out)

    init = tuple(v for _ in range(nslab)
                 for v in (jnp.zeros((nseg, LANES), F32), jnp.ones((nseg, LANES), F32)))
    ends = lax.fori_loop(0, seg, seg_step, init, unroll=RG_SCAN_UNROLL)

    gate = gz_ref[0].astype(F32)
    for sl in range(nslab):
        h_end, p_end = ends[2 * sl], ends[2 * sl + 1]
        h_in = h_ref[:, sl * LANES:(sl + 1) * LANES]
        for sg in range(nseg):
            rows = slice(sg * pitch, sg * pitch + seg)
            hs = b_ref[sl, rows, :] + a_ref[sl, rows, :] * h_in
            y_ref[0, sg * seg:(sg + 1) * seg, sl * LANES:(sl + 1) * LANES] = (
                hs * gate[sg * seg:(sg + 1) * seg, sl * LANES:(sl + 1) * LANES]).astype(BF16)
            h_in = p_end[sg:sg + 1, :] * h_in + h_end[sg:sg + 1, :]
        h_ref[:, sl * LANES:(sl + 1) * LANES] = h_in


def _rglru(xr, gz, conv_w, conv_b, wa_g, b_a, wx_g, b_x, lam):
    b, s, d = xr.shape
    g = wa_g.shape[0]
    cg = d // g
    tc = min(RG_TC, s)
    assert ((tc // SUBLANES + RG_SEG_PAD) // SUBLANES) % 2 == 1 and cg % LANES == 0
    row = lambda v: v.reshape(1, d).astype(F32)
    vec_spec = pl.BlockSpec((1, cg), lambda bi, gi, c: (0, gi))
    return pl.pallas_call(
        _rglru_kernel,
        grid=(b, g, s // tc),
        in_specs=[pl.BlockSpec((1, tc, cg), lambda bi, gi, c: (bi, c, gi)),
                  pl.BlockSpec((1, tc, cg), lambda bi, gi, c: (bi, c, gi)),
                  pl.BlockSpec((CONV_W, cg), lambda bi, gi, c: (0, gi)),
                  vec_spec,
                  pl.BlockSpec((1, cg, cg), lambda bi, gi, c: (gi, 0, 0)),
                  vec_spec,
                  pl.BlockSpec((1, cg, cg), lambda bi, gi, c: (gi, 0, 0)),
                  vec_spec,
                  vec_spec],
        out_specs=pl.BlockSpec((1, tc, cg), lambda bi, gi, c: (bi, c, gi)),
        out_shape=jax.ShapeDtypeStruct((b, s, d), BF16),
        scratch_shapes=[pltpu.VMEM((tc + SUBLANES, cg), F32),
                        pltpu.VMEM((cg // LANES, tc + SUBLANES * RG_SEG_PAD, LANES), F32),
                        pltpu.VMEM((cg // LANES, tc + SUBLANES * RG_SEG_PAD, LANES), F32),
                        pltpu.VMEM((1, cg), F32)],
        compiler_params=_cparams(("parallel", "parallel", "arbitrary")),
        name="rglru",
    )(xr, gz, conv_w.astype(F32), row(conv_b), wa_g * 0.5, row(b_a) * 0.5, wx_g * 0.5, row(b_x) * 0.5,
      row(lam))


def _group_block_diag(w, pair):
    nb, k, _ = w.shape
    out = None
    for p in range(pair):
        lo, hi = p * k, (pair - 1 - p) * k
        blk = jnp.pad(w[p::pair], ((0, 0), (lo, hi), (lo, hi)))
        out = blk if out is None else out + blk
    return out


def _attention_layer(x2d, b, s, norm_g, w_in, kv_g, w_uk, w_uv, idx_g, idx_b, w_out, rel_bias):
    d_model = x2d.shape[1]
    n_heads = w_uk.shape[0]
    wq = n_heads * HEAD_DIM
    assert N_IDX_HEADS * IDX_DIM == wq
    sizes = (wq, KV_LORA, wq, IDX_DIM, N_IDX_HEADS, wq, wq, wq, 2 * wq)
    offs = [int(o) for o in np.concatenate([[0], np.cumsum(sizes)])]
    wt = w_in.T.astype(BF16)
    zrows = lambda n: jnp.zeros((n, d_model), BF16)
    wt_small = jnp.concatenate([wt[offs[1]:offs[2]], wt[offs[3]:offs[4]], zrows(LANES - IDX_DIM),
                                wt[offs[4]:offs[5]], zrows(LANES - N_IDX_HEADS)], axis=0)

    h = _rmsnorm(x2d, norm_g, BF16, "rmsnorm0")
    pqa = _matmul_nt(h, wt, offs[0], wq, BF16, "proj_attn_qa").reshape(b, s, wq)
    pqi = _matmul_nt(h, wt, offs[2], wq, BF16, "proj_attn_qi").reshape(b, s, wq)
    ptail = _matmul_nt(h, wt, offs[5], 5 * wq, BF16, "proj_attn_tail").reshape(b, s, 5 * wq)
    psmall = _matmul_nt(h, wt_small, 0, wt_small.shape[0], F32, "proj_attn_small").reshape(b, s, -1)

    ckv, kbd, widx = _dsa_prep(psmall, kv_g, idx_g, idx_b)
    mask = _dsa_index(pqi, 0, kbd, widx)
    w_kv = jnp.concatenate([jnp.transpose(w_uk, (1, 0, 2)).reshape(KV_LORA, wq),
                            jnp.transpose(w_uv, (1, 0, 2)).reshape(KV_LORA, wq)], axis=1).astype(BF16)
    kv = _matmul([ckv.reshape(b * s, KV_LORA)], [w_kv], BF16, "dsa_kv_up", bm=2 * MM_BM).reshape(b, s, 2 * wq)

    dtab_a = _bias_blocks(rel_bias[:, :n_heads])
    dtab_b = _bias_blocks(rel_bias[:, n_heads:])
    gpw = n_heads // ATT_HB
    ya = _flash_attention((pqa, 0), (kv, 0), (kv, gpw), (ptail, 3 * gpw), dtab_a, n_heads,
                          "dsa_attention", mask=mask)
    yb = _flash_attention((ptail, 0), (ptail, gpw), (ptail, 2 * gpw), (ptail, 4 * gpw), dtab_b,
                          n_heads, "moba_attention")

    w_out_b = w_out.astype(BF16)
    return _matmul([ya.reshape(b * s, wq), yb.reshape(b * s, wq)], [(w_out_b, 0, 0), (w_out_b, 1, 0)],
                   F32, "proj_attn_out", res=x2d, bn=MM_BN // 2)


def _recurrent_layer(x2d, b, s, norm_g, w_in, conv_w, conv_b, w_a, b_a, w_x, b_x, lam, w_out):
    d_rnn = conv_b.shape[0]
    h = _rmsnorm(x2d, norm_g, BF16, "rmsnorm1")
    w_in_b = w_in.astype(BF16)
    assert d_rnn % MM_BN == 0
    xr = _matmul([h], [(w_in_b, 0, 0)], F32, "proj_rec_x", n=d_rnn)
    gz = _matmul([h], [(w_in_b, 0, d_rnn // MM_BN)], BF16, "proj_rec_z", n=d_rnn, epilogue=_silu)
    y = _rglru(xr.reshape(b, s, d_rnn), gz.reshape(b, s, d_rnn), conv_w, conv_b,
               _group_block_diag(w_a.astype(BF16), RG_PAIR), b_a,
               _group_block_diag(w_x.astype(BF16), RG_PAIR), b_x, lam)
    return _matmul([y.reshape(b * s, d_rnn)], [w_out.astype(BF16)], F32, "proj_rec_out",
                   res=x2d, bn=MM_BN // 2)


def kernel(x, norm_g, final_g, rel_bias, attn_w_in, attn_kv_g, attn_w_uk, attn_w_uv, idx_k_g, idx_k_b,
           attn_w_out, rec_w_in, rec_conv_w, rec_conv_b, rec_w_a, rec_b_a, rec_w_x, rec_b_x,
           rec_lambda, rec_w_out):
    b, s, d = x.shape
    depth = norm_g.shape[0]
    x2d = x.reshape(b * s, d)
    for layer in range(depth):
        li = layer // 2
        if layer % 2 == 0:
            x2d = _attention_layer(x2d, b, s, norm_g[layer], attn_w_in[li], attn_kv_g[li],
                                   attn_w_uk[li], attn_w_uv[li], idx_k_g[li], idx_k_b[li],
                                   attn_w_out[li], rel_bias)
        else:
            x2d = _recurrent_layer(x2d, b, s, norm_g[layer], rec_w_in[li], rec_conv_w[li],
                                   rec_conv_b[li], rec_w_a[li], rec_b_a[li], rec_w_x[li],
                                   rec_b_x[li], rec_lambda[li], rec_w_out[li])
    return _rmsnorm(x2d, final_g, x.dtype, "rmsnorm_final").reshape(b, s, d)
```

```python
import functools
import math

import numpy as np
import jax
import jax.numpy as jnp
from jax import lax
from jax.experimental import pallas as pl
from jax.experimental.pallas import tpu as pltpu

F32 = jnp.float32
BF16 = jnp.bfloat16
I32 = jnp.int32

HEAD_DIM = 128
KV_LORA = 512
N_IDX_HEADS = 32
IDX_DIM = 64
TOPK_MAX = 256
MOBA_BLOCK = 256
MOBA_TOPB = 3
RG_BLOCKS = 16
CONV_W = 4
RG_C = 8.0
NUM_BUCKETS = 32
MAX_DISTANCE = 128
EPS = 1e-6
ATTN_SCALE = HEAD_DIM ** -0.5
IDX_SCALE = (N_IDX_HEADS ** -0.5) * (IDX_DIM ** -0.5)
LOG2E = math.log2(math.e)

LANES = 128
SUBLANES = 8
WT_ROW_ALIGN = 16
VMEM_LIMIT_BYTES = 56 * 2 ** 20
MASK_NEG = -1e30
INT_MIN = np.int32(-2 ** 31)
KEY_BYTES = 4

ROW_TILE = 512
MM_BM = 1024
MM_BN = 1024
ATT_T = 256
ATT_TK = 2 * ATT_T
ATT_HB = 4
RG_PAIR = 2
RG_TC = 1024
RG_SEG_PAD = 8
RG_SCAN_UNROLL = 4
assert ATT_T == MOBA_BLOCK and ATT_T == 2 * LANES


def _cparams(sem):
    return pltpu.CompilerParams(dimension_semantics=sem, vmem_limit_bytes=VMEM_LIMIT_BYTES)


def _sigmoid(x):
    return 0.5 * jnp.tanh(0.5 * x) + 0.5


def _tile_rows(x, n):
    return x if n == 1 else jnp.concatenate([x] * n, axis=0)


def _tile_lanes(x, n):
    return x if n == 1 else jnp.concatenate([x] * n, axis=1)


def _rmsnorm_kernel(x_ref, g_ref, o_ref):
    x = x_ref[...]
    ms = jnp.mean(x * x, axis=-1, keepdims=True)
    o_ref[...] = ((x * lax.rsqrt(ms + EPS)) * g_ref[...]).astype(o_ref.dtype)


def _rmsnorm(x2d, g, out_dtype, name):
    m, d = x2d.shape
    bm = min(ROW_TILE, m)
    return pl.pallas_call(
        _rmsnorm_kernel,
        grid=(m // bm,),
        in_specs=[pl.BlockSpec((bm, d), lambda i: (i, 0)),
                  pl.BlockSpec((1, d), lambda i: (0, 0))],
        out_specs=pl.BlockSpec((bm, d), lambda i: (i, 0)),
        out_shape=jax.ShapeDtypeStruct((m, d), out_dtype),
        compiler_params=_cparams(("parallel",)),
        name=name,
    )(x2d, g.reshape(1, d).astype(F32))


def _mm_kernel(*refs, n_parts, has_res, epilogue):
    o_ref = refs[-1]
    acc = jnp.dot(refs[0][...], refs[n_parts][...], preferred_element_type=F32)
    for p in range(1, n_parts):
        acc = acc + jnp.dot(refs[p][...], refs[n_parts + p][...], preferred_element_type=F32)
    if has_res:
        acc = refs[2 * n_parts][...] + acc
    if epilogue is not None:
        acc = epilogue(acc)
    o_ref[...] = acc.astype(o_ref.dtype)


def _matmul(xs, ws, out_dtype, name, res=None, n=None, bm=MM_BM, bn=MM_BN, epilogue=None):
    ws = [w if isinstance(w, tuple) else (w, 0, 0) for w in ws]
    m = xs[0].shape[0]
    n = ws[0][0].shape[1] if n is None else n
    bm = min(bm, m)
    bn = min(bn, n)
    assert m % bm == 0 and n % bn == 0
    in_specs = [pl.BlockSpec((bm, x.shape[1]), lambda i, j: (i, 0)) for x in xs]
    for x, (w, rb, cb) in zip(xs, ws):
        assert w.shape[0] % x.shape[1] == 0 and (cb + n // bn) * bn <= w.shape[1]
        in_specs.append(pl.BlockSpec((x.shape[1], bn), lambda i, j, rb=rb, cb=cb: (rb, cb + j)))
    args = [*xs, *[w for w, _, _ in ws]]
    if res is not None:
        in_specs.append(pl.BlockSpec((bm, bn), lambda i, j: (i, j)))
        args.append(res)
    return pl.pallas_call(
        functools.partial(_mm_kernel, n_parts=len(xs), has_res=res is not None, epilogue=epilogue),
        grid=(m // bm, n // bn),
        in_specs=in_specs,
        out_specs=pl.BlockSpec((bm, bn), lambda i, j: (i, j)),
        out_shape=jax.ShapeDtypeStruct((m, n), out_dtype),
        compiler_params=_cparams(("parallel", "parallel")),
        name=name,
    )(*args)


def _silu(x):
    return x * _sigmoid(x)


def _mm_nt_kernel(x_ref, wt_ref, o_ref, *, epilogue):
    acc = lax.dot_general(x_ref[...], wt_ref[...], (((1,), (1,)), ((), ())), preferred_element_type=F32)
    if epilogue is not None:
        acc = epilogue(acc)
    o_ref[...] = acc.astype(o_ref.dtype)


def _matmul_nt(x, wt, row0, n, out_dtype, name, bn=MM_BN, epilogue=None):
    m, k = x.shape
    bm = min(MM_BM, m)
    bn = min(bn, n)
    assert m % bm == 0 and n % bn == 0 and row0 % WT_ROW_ALIGN == 0 and bn % WT_ROW_ALIGN == 0
    assert row0 + n <= wt.shape[0] and wt.shape[1] == k
    return pl.pallas_call(
        functools.partial(_mm_nt_kernel, epilogue=epilogue),
        grid=(m // bm, n // bn),
        in_specs=[pl.BlockSpec((bm, k), lambda i, j: (i, 0)),
                  pl.BlockSpec((pl.Element(bn), pl.Element(k)),
                               lambda i, j: (pl.multiple_of(row0 + j * bn, WT_ROW_ALIGN), 0))],
        out_specs=pl.BlockSpec((bm, bn), lambda i, j: (i, j)),
        out_shape=jax.ShapeDtypeStruct((m, n), out_dtype),
        compiler_params=_cparams(("parallel", "parallel")),
        name=name,
    )(x, wt)


def _t5_bucket_np(dist):
    d = np.maximum(dist, 0)
    max_exact = NUM_BUCKETS // 2
    d_f = np.maximum(d, 1).astype(np.float32)
    ratio = np.log(d_f / np.float32(max_exact)) / np.float32(math.log(MAX_DISTANCE / max_exact))
    large = max_exact + (ratio * np.float32(NUM_BUCKETS - max_exact)).astype(np.int32)
    large = np.minimum(large, NUM_BUCKETS - 1)
    return np.where(d < max_exact, d, large)


def _bias_blocks(tab):
    n = LANES
    assert np.all(_t5_bucket_np(np.arange(n, 64 * n)) == NUM_BUCKETS - 1)
    h = tab.shape[1]
    rel = (tab - tab[NUM_BUCKETS - 1]).astype(F32) * LOG2E
    out = []
    for off in (0, n):
        buckets = _t5_bucket_np(off + n - 1 - np.arange(2 * n - 1))
        onehot = np.zeros((2 * n, NUM_BUCKETS), np.float32)
        onehot[np.arange(2 * n - 1), buckets] = 1.0
        w = jnp.dot(jnp.asarray(onehot), rel, precision=lax.Precision.HIGHEST).T
        t = jnp.tile(w, (1, n))[:, :n * (2 * n - 1)].reshape(h, n, 2 * n - 1)
        out.append(t[:, :, n - 1:])
    return jnp.stack(out)


def _dsa_prep_kernel(p_ref, kvg_ref, ig_ref, ib_ref, ckv_ref, kbd_ref, w_ref):
    p = p_ref[0]
    ts = p.shape[0]
    c = p[:, :KV_LORA]
    cn = (c * lax.rsqrt(jnp.mean(c * c, axis=-1, keepdims=True) + EPS)) * kvg_ref[...]
    ckv_ref[0] = cn.astype(BF16)
    k = p[:, KV_LORA:KV_LORA + IDX_DIM]
    mu = jnp.mean(k, axis=-1, keepdims=True)
    var = jnp.mean(jnp.square(k - mu), axis=-1, keepdims=True)
    kn = ((k - mu) * lax.rsqrt(var + EPS)) * ig_ref[...] + ib_ref[...]
    zeros = jnp.zeros((ts, LANES - IDX_DIM), F32)
    top = jnp.concatenate([kn, zeros], axis=1).T
    bot = jnp.concatenate([zeros, kn], axis=1).T
    kbd_ref[0, 0] = jnp.concatenate([top, bot], axis=1).astype(BF16)
    w_ref[0] = p[:, KV_LORA + LANES:KV_LORA + LANES + N_IDX_HEADS] * IDX_SCALE


def _dsa_prep(ps, kv_g, idx_g, idx_b):
    b, s, wid = ps.shape
    ts = ATT_T
    nkt = s // ts
    return pl.pallas_call(
        _dsa_prep_kernel,
        grid=(b, nkt),
        in_specs=[pl.BlockSpec((1, ts, wid), lambda bi, i: (bi, i, 0)),
                  pl.BlockSpec((1, KV_LORA), lambda bi, i: (0, 0)),
                  pl.BlockSpec((1, IDX_DIM), lambda bi, i: (0, 0)),
                  pl.BlockSpec((1, IDX_DIM), lambda bi, i: (0, 0))],
        out_specs=[pl.BlockSpec((1, ts, KV_LORA), lambda bi, i: (bi, i, 0)),
                   pl.BlockSpec((1, 1, LANES, 2 * ts), lambda bi, i: (bi, i, 0, 0)),
                   pl.BlockSpec((1, ts, N_IDX_HEADS), lambda bi, i: (bi, i, 0))],
        out_shape=[jax.ShapeDtypeStruct((b, s, KV_LORA), BF16),
                   jax.ShapeDtypeStruct((b, nkt, LANES, 2 * ts), BF16),
                   jax.ShapeDtypeStruct((b, s, N_IDX_HEADS), F32)],
        compiler_params=_cparams(("parallel", "parallel")),
        name="dsa_prep",
    )(ps, kv_g.reshape(1, -1).astype(F32), idx_g.reshape(1, -1).astype(F32),
      idx_b.reshape(1, -1).astype(F32))


def _dsa_index_kernel(qi_ref, w_ref, kbd_ref, mask_ref, key_ref, plane_ref, wb_ref, t_ref, j_ref,
                      *, topk, nkt_total, seq_bits):
    tq = tk = ATT_T
    rep = tk // LANES
    sl = SUBLANES
    i = pl.program_id(1)
    t0 = i * tq
    nkt = i + 1

    row = t0 + lax.broadcasted_iota(I32, (tq, tk), 0)
    col = lax.broadcasted_iota(I32, (tq, tk), 1)
    row_t = lax.broadcasted_iota(I32, (tk, tq), 0)
    col_t = t0 + lax.broadcasted_iota(I32, (tk, tq), 1)

    def to_key(score):
        score = jnp.where(score == 0.0, 0.0, score)
        bits = lax.bitcast_convert_type(score, I32)
        return bits ^ ((bits >> 31) & np.int32(0x7FFFFFFF))

    for h in range(N_IDX_HEADS):
        wb_ref[h] = jnp.broadcast_to(w_ref[0, :, h:h + 1], (tq, LANES))

    def idx_body(kt, carry):
        kb = kbd_ref[0, kt]
        acc = jnp.zeros((tq, tk), F32)
        for hp in range(N_IDX_HEADS // 2):
            rel = jnp.dot(qi_ref[0, :, hp * LANES:(hp + 1) * LANES], kb,
                          preferred_element_type=F32)
            acc = acc + jnp.maximum(rel[:, :tk], 0.0) * _tile_lanes(wb_ref[2 * hp], rep)
            acc = acc + jnp.maximum(rel[:, tk:], 0.0) * _tile_lanes(wb_ref[2 * hp + 1], rep)
        key_ref[kt] = jnp.where(kt * tk + col <= row, to_key(acc), INT_MIN)
        ukey = jnp.where(kt * tk + row_t <= col_t, to_key(acc.T), INT_MIN) ^ INT_MIN
        for bi in range(KEY_BYTES):
            byte = (ukey >> (8 * (KEY_BYTES - 1 - bi))) & np.int32(0xFF)
            plane_ref[bi, kt] = byte.astype(F32).astype(BF16)
        return carry

    lax.fori_loop(0, nkt, idx_body, 0)

    one, zero = jnp.ones((), BF16), jnp.zeros((), BF16)

    def count(plane, pred_fn):
        rows = 2 * sl
        def body(kt, c):
            xs = [pred_fn(plane_ref[plane, kt, r * rows:(r + 1) * rows, :], kt * tk + r * rows)
                  for r in range(tk // rows)]
            while len(xs) > 1:
                xs = [xs[a] + xs[a + 1] for a in range(0, len(xs), 2)]
            return c + xs[0]
        c = lax.fori_loop(0, nkt, body, jnp.zeros((rows, tq), BF16))
        return jnp.broadcast_to(jnp.sum(c.astype(F32), axis=0, keepdims=True), (sl, tq))

    def rep16(v8):
        return jnp.concatenate([v8, v8], axis=0).astype(BF16)

    target = jnp.full((sl, tq), float(topk), F32)
    thr_u = jnp.zeros((sl, tq), I32)
    for bi in range(KEY_BYTES):
        t_ref[...] = jnp.zeros((sl, tq), F32)

        def bit_body(it, carry, bi=bi, target=target):
            cand = t_ref[...] + jnp.left_shift(jnp.int32(1), 7 - it).astype(F32)
            c16 = rep16(cand)
            n_ge = count(bi, lambda d, s0: jnp.where(d >= c16, one, zero))
            t_ref[...] = jnp.where(n_ge >= target, cand, t_ref[...])
            return carry

        lax.fori_loop(0, 8, bit_body, 0)
        tb = t_ref[...]
        tb16 = rep16(tb)
        thr_u = (thr_u << 8) | tb.astype(I32)
        if bi + 1 < KEY_BYTES:
            target = target - count(bi, lambda d, s0: jnp.where(d > tb16, one, zero))

            def narrow_body(kt, carry, bi=bi):
                plane_ref[bi + 1, kt] = jnp.where(plane_ref[bi, kt] == _tile_rows(tb16, tk // (2 * sl)),
                                                  plane_ref[bi + 1, kt], -one)
                return carry

            lax.fori_loop(0, nkt, narrow_body, 0)

    last = KEY_BYTES - 1
    n_gt = count(last, lambda d, s0: jnp.where(d > tb16, one, zero))
    n_ge = count(last, lambda d, s0: jnp.where(d >= tb16, one, zero))
    need = target - n_gt
    j_ref[...] = jnp.full((sl, tq), 2 ** 30, I32)
    sub16 = lax.broadcasted_iota(I32, (2 * sl, tq), 0)

    @pl.when(jnp.max(n_ge - target) > 0.0)
    def _():
        j_ref[...] = jnp.zeros((sl, tq), I32)

        def jbit_body(it, carry):
            cand = j_ref[...] | jnp.left_shift(jnp.int32(1), seq_bits - 1 - it)
            c2 = jnp.concatenate([cand, cand], axis=0)
            n_lt = count(last, lambda d, s0: jnp.where(
                d == tb16, jnp.where(s0 + sub16 < c2, 1.0, 0.0).astype(BF16), zero))
            j_ref[...] = jnp.where(n_lt < need, cand, j_ref[...])
            return carry

        lax.fori_loop(0, seq_bits, jbit_body, 0)

    thr = thr_u ^ INT_MIN

    def per_row(v8):
        wide = jnp.broadcast_to(lax.bitcast_convert_type(v8[0:1], F32), (LANES, tq))
        return _tile_lanes(lax.bitcast_convert_type(wide.T, I32), rep)

    thr_t = per_row(thr)
    jmax_t = per_row(j_ref[...])

    def mask_body(kt, carry):
        k = key_ref[kt]
        s_idx = kt * tk + col
        keep_tie = jnp.where(k == thr_t, jnp.where(s_idx <= jmax_t, 0.0, MASK_NEG), MASK_NEG)
        keep = jnp.where(k > thr_t, 0.0, keep_tie)
        mask_ref[0, 0, kt] = jnp.where(s_idx <= row, keep, MASK_NEG).astype(BF16)
        return carry

    lax.fori_loop(0, nkt, mask_body, 0)

    def fill_body(kt, carry):
        mask_ref[0, 0, kt] = jnp.full((tq, tk), MASK_NEG, BF16)
        return carry

    lax.fori_loop(nkt, nkt_total, fill_body, 0)


def _dsa_index(pqi, col_qi, kbd, widx):
    b, s, _ = pqi.shape
    tq = tk = ATT_T
    nkt = s // tk
    wq = N_IDX_HEADS * IDX_DIM
    topk = min(TOPK_MAX, s // 4)
    assert s // (2 * SUBLANES) <= 256
    kern = functools.partial(_dsa_index_kernel, topk=topk, nkt_total=nkt,
                             seq_bits=max(1, (s - 1).bit_length()))
    return pl.pallas_call(
        kern,
        grid=(b, s // tq),
        in_specs=[pl.BlockSpec((1, tq, wq), lambda bi, i: (bi, i, col_qi)),
                  pl.BlockSpec((1, tq, N_IDX_HEADS), lambda bi, i: (bi, i, 0)),
                  pl.BlockSpec((1, nkt, LANES, 2 * tk), lambda bi, i: (bi, 0, 0, 0))],
        out_specs=pl.BlockSpec((1, 1, nkt, tq, tk), lambda bi, i: (bi, i, 0, 0, 0)),
        out_shape=jax.ShapeDtypeStruct((b, s // tq, nkt, tq, tk), BF16),
        scratch_shapes=[pltpu.VMEM((nkt, tq, tk), I32),
                        pltpu.VMEM((KEY_BYTES, nkt, tk, tq), BF16),
                        pltpu.VMEM((N_IDX_HEADS, tq, LANES), F32),
                        pltpu.VMEM((SUBLANES, tq), F32),
                        pltpu.VMEM((SUBLANES, tq), I32)],
        compiler_params=_cparams(("parallel", "parallel")),
        name="dsa_index",
    )(pqi, widx, kbd)


def _flash_kernel(*refs, mode, nb, topb):
    if mode == "mask":
        q_ref, k_ref, v_ref, z_ref, d_ref, mask_ref, y_ref = refs[:7]
    else:
        q_ref, k_ref, v_ref, z_ref, d_ref, y_ref, km_ref = refs[:7]
    qs_ref, acc_ref, m_ref, p_ref, alpha_ref, x_ref = refs[7:]
    tq = ATT_T
    tk = ATT_TK
    hb = q_ref.shape[2] // HEAD_DIM
    rep = tk // LANES
    i = pl.program_id(2)
    jd = i // 2
    par = i % 2
    pm = par.astype(F32)
    zero_blk = jnp.zeros((LANES, LANES), F32)
    ones_blk = jnp.ones((tk, LANES), BF16)

    if mode == "sel":
        @pl.when(i == 0)
        def _():
            km_ref[...] = jnp.zeros(km_ref.shape, F32)
            for h in range(hb):
                for n in range(nb):
                    kblk = k_ref[0, n * tq:(n + 1) * tq, h * HEAD_DIM:(h + 1) * HEAD_DIM].astype(F32)
                    km_ref[h, n:n + 1, :] = jnp.mean(kblk, axis=0, keepdims=True)
        nbp = -(-nb // SUBLANES) * SUBLANES
        blk = lax.broadcasted_iota(I32, (nbp, tq), 0).astype(F32)
        lane = lax.broadcasted_iota(I32, (tk, LANES), 1)
        second_half = lax.broadcasted_iota(I32, (tk, LANES), 0) >= tq

    for h in range(hb):
        q = q_ref[0, :, h * HEAD_DIM:(h + 1) * HEAD_DIM]
        qsc = (q.astype(F32) * (ATTN_SCALE * LOG2E)).astype(BF16)
        if mode == "sel":
            gate = lax.dot_general(km_ref[h, 0:nbp, :].astype(BF16), q, (((1,), (1,)), ((), ())),
                                   preferred_element_type=F32)
            gate = jnp.where(blk < i.astype(F32), gate, -jnp.inf)
            sel = jnp.full((nbp, tq), MASK_NEG, F32)
            for _ in range(topb):
                gmax = jnp.max(gate, axis=0, keepdims=True)
                first = jnp.min(jnp.where(gate == gmax, blk, float(nbp)), axis=0, keepdims=True)
                hit = jnp.where(blk == first, jnp.where(gmax > -jnp.inf, 1.0, 0.0), 0.0)
                sel = jnp.where(hit > 0.0, 0.0, sel)
                gate = jnp.where(blk == first, -jnp.inf, gate)
            selb = jnp.concatenate([sel, jnp.full((LANES - nbp, tq), MASK_NEG, F32)], axis=0).T
            qs_ref[h] = jnp.concatenate([qsc, selb.astype(BF16)], axis=1)
        else:
            qs_ref[h] = qsc
        m_ref[h] = jnp.full((tq, LANES), MASK_NEG, F32)
        acc_ref[h] = jnp.zeros((tq, 2 * HEAD_DIM), F32)
        p_ref[h] = jnp.zeros((tq, tk), BF16)
        alpha_ref[h] = jnp.ones((tq, LANES), F32)

    def qk_stage(j, is_diag):
        r0 = pl.multiple_of(j * tk, tk)
        if mode == "sel":
            first = jnp.where(is_diag, jnp.where(par == 1, 2 * jd, -1), 2 * j)
            second = jnp.where(is_diag, -1, 2 * j + 1)
            onehot = jnp.where(lane == jnp.where(second_half, second, first), 1.0, 0.0).astype(BF16)
        for h in range(hb):
            kblk = k_ref[0, pl.ds(r0, tk), h * HEAD_DIM:(h + 1) * HEAD_DIM]
            if mode == "sel":
                kblk = jnp.concatenate([kblk, onehot], axis=1)
            x_ref[h] = lax.dot_general(qs_ref[h], kblk, (((1,), (1,)), ((), ())),
                                       preferred_element_type=F32)

    def bias_tile(h, kind, j=None):
        d0, d1 = d_ref[0, h], d_ref[1, h]
        if kind == "past":
            near = jnp.where(j == jd - 1, 1.0 - pm, 0.0)
            top = [zero_blk, zero_blk, zero_blk, d1 * near]
            bot = [zero_blk] * 4
        else:
            top = [d0 * (1.0 - pm), d1 * pm, d0 * pm, zero_blk]
            bot = [d1 * (1.0 - pm), d0 * (1.0 - pm), d1 * pm, d0 * pm]
        return jnp.concatenate([jnp.concatenate(top, axis=1), jnp.concatenate(bot, axis=1)], axis=0)

    def softmax_stage(j, kind):
        if mode == "mask":
            mask_tile = jnp.concatenate([mask_ref[0, 0, 2 * j], mask_ref[0, 0, 2 * j + 1]],
                                        axis=1).astype(F32)
        elif kind == "diag":
            row = lax.broadcasted_iota(I32, (tq, tk), 0)
            col = lax.broadcasted_iota(I32, (tq, tk), 1)
            causal = col <= row + tq * par
        for h in range(hb):
            x = x_ref[h]
            if mode == "mask":
                x = x + mask_tile
            x = x + bias_tile(h, kind, j)
            if kind == "diag" and mode == "sel":
                x = jnp.where(causal, x, MASK_NEG)
            m_prev = m_ref[h]
            m_new = jnp.maximum(m_prev, jnp.max(x, axis=1, keepdims=True))
            alpha_ref[h] = jnp.exp2(m_prev - m_new)
            p_ref[h] = jnp.exp2(x - _tile_lanes(m_new, rep)).astype(BF16)
            m_ref[h] = m_new

    def pv_stage(j):
        r0 = pl.multiple_of(j * tk, tk)
        for h in range(hb):
            vblk = v_ref[0, pl.ds(r0, tk), h * HEAD_DIM:(h + 1) * HEAD_DIM]
            pv = jnp.dot(p_ref[h], jnp.concatenate([vblk, ones_blk], axis=1),
                         preferred_element_type=F32)
            acc_ref[h] = acc_ref[h] * _tile_lanes(alpha_ref[h], 2) + pv

    def step(j, kind):
        pv_stage(jnp.maximum(j - 1, 0))
        softmax_stage(j, kind)
        if kind != "diag":
            qk_stage(j + 1, j + 1 == jd)

    qk_stage(0, jd == 0)

    def past_body(j, carry):
        step(j, "past")
        return carry

    lax.fori_loop(0, jd, past_body, 0)
    step(jd, "diag")
    pv_stage(jd)

    for h in range(hb):
        acc = acc_ref[h]
        o = acc[:, :HEAD_DIM] / acc[:, HEAD_DIM:]
        zz = z_ref[0, :, h * HEAD_DIM:(h + 1) * HEAD_DIM].astype(F32)
        y_ref[0, :, h * HEAD_DIM:(h + 1) * HEAD_DIM] = (o * (zz * _sigmoid(zz))).astype(BF16)


def _flash_attention(q, k, v, z, dtab, n_heads, name, mask=None):
    (qa, cq), (ka, ck), (va, cv), (za, cz) = q, k, v, z
    b, s, _ = qa.shape
    tq = ATT_T
    tk = ATT_TK
    assert s % tk == 0
    nb = s // tq
    hb = ATT_HB
    wg = hb * HEAD_DIM
    mode = "sel" if mask is None else "mask"
    kern = functools.partial(_flash_kernel, mode=mode, nb=nb, topb=min(MOBA_TOPB, nb - 1))
    in_specs = [pl.BlockSpec((1, tq, wg), lambda bi, g, i: (bi, i, cq + g)),
                pl.BlockSpec((1, s, wg), lambda bi, g, i: (bi, 0, ck + g)),
                pl.BlockSpec((1, s, wg), lambda bi, g, i: (bi, 0, cv + g)),
                pl.BlockSpec((1, tq, wg), lambda bi, g, i: (bi, i, cz + g)),
                pl.BlockSpec((2, hb, LANES, LANES), lambda bi, g, i: (0, g, 0, 0))]
    args = [qa, ka, va, za, dtab]
    scratch = []
    if mode == "mask":
        in_specs.append(pl.BlockSpec((1, 1, nb, tq, tq), lambda bi, g, i: (bi, i, 0, 0, 0)))
        args.append(mask)
        kq = HEAD_DIM
    else:
        scratch.append(pltpu.VMEM((hb, LANES, HEAD_DIM), F32))
        kq = 2 * HEAD_DIM
    scratch += [pltpu.VMEM((hb, tq, kq), BF16),
                pltpu.VMEM((hb, tq, 2 * HEAD_DIM), F32),
                pltpu.VMEM((hb, tq, LANES), F32),
                pltpu.VMEM((hb, tq, tk), BF16),
                pltpu.VMEM((hb, tq, LANES), F32),
                pltpu.VMEM((hb, tq, tk), F32)]
    return pl.pallas_call(
        kern,
        grid=(b, n_heads // hb, nb),
        in_specs=in_specs,
        out_specs=pl.BlockSpec((1, tq, wg), lambda bi, g, i: (bi, i, g)),
        out_shape=jax.ShapeDtypeStruct((b, s, n_heads * HEAD_DIM), BF16),
        scratch_shapes=scratch,
        compiler_params=_cparams(("parallel", "parallel", "arbitrary")),
        name=name,
    )(*args)


def _softplus(x):
    return jnp.maximum(x, 0.0) + jnp.log1p(jnp.exp(-jnp.abs(x)))


def _rglru_kernel(xr_ref, gz_ref, cw_ref, cb_ref, wa_ref, ba_ref, wx_ref, bx_ref, lam_ref, y_ref,
                  xbuf_ref, a_ref, b_ref, h_ref):
    tc = xr_ref.shape[1]
    c = pl.program_id(2)
    halo = SUBLANES

    @pl.when(c == 0)
    def _():
        xbuf_ref[0:halo, :] = jnp.zeros((halo, xbuf_ref.shape[1]), F32)
        h_ref[...] = jnp.zeros(h_ref.shape, F32)

    @pl.when(c > 0)
    def _():
        xbuf_ref[0:halo, :] = xbuf_ref[tc:tc + halo, :]

    xbuf_ref[halo:halo + tc, :] = xr_ref[0]
    xfull = xbuf_ref[...]
    u = xfull * cw_ref[0:1, :]
    for j in range(1, CONV_W):
        u = xfull * cw_ref[j:j + 1, :] + pltpu.roll(u, 1, 0)
    xc = (u + cb_ref[...])[halo:, :]

    xcb = xc.astype(BF16)
    tr = jnp.tanh(jnp.dot(xcb, wa_ref[0], preferred_element_type=F32) + ba_ref[...])
    ti = jnp.tanh(jnp.dot(xcb, wx_ref[0], preferred_element_type=F32) + bx_ref[...])
    half_c = (-0.5 * RG_C * LOG2E) * _softplus(-lam_ref[...])
    a = jnp.exp2(tr * half_c + half_c)
    one_m_a2 = 1.0 - a * a
    mult = jnp.where(one_m_a2 > 0.0, one_m_a2 * lax.rsqrt(one_m_a2), 0.0)
    bt = (mult * xc) * (0.5 * ti + 0.5)

    nseg = SUBLANES
    seg = tc // nseg
    pitch = seg + RG_SEG_PAD
    nslab = a_ref.shape[0]
    for sl in range(nslab):
        for sg in range(nseg):
            a_ref[sl, sg * pitch:sg * pitch + seg, :] = a[sg * seg:(sg + 1) * seg, sl * LANES:(sl + 1) * LANES]
            b_ref[sl, sg * pitch:sg * pitch + seg, :] = bt[sg * seg:(sg + 1) * seg, sl * LANES:(sl + 1) * LANES]

    def seg_step(j, carry):
        out = []
        for sl in range(nslab):
            hl, pr = carry[2 * sl], carry[2 * sl + 1]
            rows = pl.ds(j, nseg, stride=pitch)
            a8 = a_ref[sl, rows, :]
            hl = a8 * hl + b_ref[sl, rows, :]
            pr = a8 * pr
            b_ref[sl, rows, :] = hl
            a_ref[sl, rows, :] = pr
            out += [hl, pr]
        return tuple(out)

    init = tuple(v for _ in range(nslab)
                 for v in (jnp.zeros((nseg, LANES), F32), jnp.ones((nseg, LANES), F32)))
    ends = lax.fori_loop(0, seg, seg_step, init, unroll=RG_SCAN_UNROLL)

    gate = gz_ref[0].astype(F32)
    for sl in range(nslab):
        h_end, p_end = ends[2 * sl], ends[2 * sl + 1]
        h_in = h_ref[:, sl * LANES:(sl + 1) * LANES]
        for sg in range(nseg):
            rows = slice(sg * pitch, sg * pitch + seg)
            hs = b_ref[sl, rows, :] + a_ref[sl, rows, :] * h_in
            y_ref[0, sg * seg:(sg + 1) * seg, sl * LANES:(sl + 1) * LANES] = (
                hs * gate[sg * seg:(sg + 1) * seg, sl * LANES:(sl + 1) * LANES]).astype(BF16)
            h_in = p_end[sg:sg + 1, :] * h_in + h_end[sg:sg + 1, :]
        h_ref[:, sl * LANES:(sl + 1) * LANES] = h_in


def _rglru(xr, gz, conv_w, conv_b, wa_g, b_a, wx_g, b_x, lam):
    b, s, d = xr.shape
    g = wa_g.shape[0]
    cg = d // g
    tc = min(RG_TC, s)
    assert ((tc // SUBLANES + RG_SEG_PAD) // SUBLANES) % 2 == 1 and cg % LANES == 0
    row = lambda v: v.reshape(1, d).astype(F32)
    vec_spec = pl.BlockSpec((1, cg), lambda bi, gi, c: (0, gi))
    return pl.pallas_call(
        _rglru_kernel,
        grid=(b, g, s // tc),
        in_specs=[pl.BlockSpec((1, tc, cg), lambda bi, gi, c: (bi, c, gi)),
                  pl.BlockSpec((1, tc, cg), lambda bi, gi, c: (bi, c, gi)),
                  pl.BlockSpec((CONV_W, cg), lambda bi, gi, c: (0, gi)),
                  vec_spec,
                  pl.BlockSpec((1, cg, cg), lambda bi, gi, c: (gi, 0, 0)),
                  vec_spec,
                  pl.BlockSpec((1, cg, cg), lambda bi, gi, c: (gi, 0, 0)),
                  vec_spec,
                  vec_spec],
        out_specs=pl.BlockSpec((1, tc, cg), lambda bi, gi, c: (bi, c, gi)),
        out_shape=jax.ShapeDtypeStruct((b, s, d), BF16),
        scratch_shapes=[pltpu.VMEM((tc + SUBLANES, cg), F32),
                        pltpu.VMEM((cg // LANES, tc + SUBLANES * RG_SEG_PAD, LANES), F32),
                        pltpu.VMEM((cg // LANES, tc + SUBLANES * RG_SEG_PAD, LANES), F32),
                        pltpu.VMEM((1, cg), F32)],
        compiler_params=_cparams(("parallel", "parallel", "arbitrary")),
        name="rglru",
    )(xr, gz, conv_w.astype(F32), row(conv_b), wa_g * 0.5, row(b_a) * 0.5, wx_g * 0.5, row(b_x) * 0.5,
      row(lam))


def _group_block_diag(w, pair):
    nb, k, _ = w.shape
    out = None
    for p in range(pair):
        lo, hi = p * k, (pair - 1 - p) * k
        blk = jnp.pad(w[p::pair], ((0, 0), (lo, hi), (lo, hi)))
        out = blk if out is None else out + blk
    return out


def _attention_layer(x2d, b, s, norm_g, w_in, kv_g, w_uk, w_uv, idx_g, idx_b, w_out, rel_bias):
    d_model = x2d.shape[1]
    n_heads = w_uk.shape[0]
    wq = n_heads * HEAD_DIM
    assert N_IDX_HEADS * IDX_DIM == wq
    sizes = (wq, KV_LORA, wq, IDX_DIM, N_IDX_HEADS, wq, wq, wq, 2 * wq)
    offs = [int(o) for o in np.concatenate([[0], np.cumsum(sizes)])]
    wt = w_in.T.astype(BF16)
    zrows = lambda n: jnp.zeros((n, d_model), BF16)
    wt_small = jnp.concatenate([wt[offs[1]:offs[2]], wt[offs[3]:offs[4]], zrows(LANES - IDX_DIM),
                                wt[offs[4]:offs[5]], zrows(LANES - N_IDX_HEADS)], axis=0)

    h = _rmsnorm(x2d, norm_g, BF16, "rmsnorm0")
    pqa = _matmul_nt(h, wt, offs[0], wq, BF16, "proj_attn_qa").reshape(b, s, wq)
    pqi = _matmul_nt(h, wt, offs[2], wq, BF16, "proj_attn_qi").reshape(b, s, wq)
    ptail = _matmul_nt(h, wt, offs[5], 5 * wq, BF16, "proj_attn_tail").reshape(b, s, 5 * wq)
    psmall = _matmul_nt(h, wt_small, 0, wt_small.shape[0], F32, "proj_attn_small").reshape(b, s, -1)

    ckv, kbd, widx = _dsa_prep(psmall, kv_g, idx_g, idx_b)
    mask = _dsa_index(pqi, 0, kbd, widx)
    w_kv = jnp.concatenate([jnp.transpose(w_uk, (1, 0, 2)).reshape(KV_LORA, wq),
                            jnp.transpose(w_uv, (1, 0, 2)).reshape(KV_LORA, wq)], axis=1).astype(BF16)
    kv = _matmul([ckv.reshape(b * s, KV_LORA)], [w_kv], BF16, "dsa_kv_up", bm=2 * MM_BM).reshape(b, s, 2 * wq)

    dtab_a = _bias_blocks(rel_bias[:, :n_heads])
    dtab_b = _bias_blocks(rel_bias[:, n_heads:])
    gpw = n_heads // ATT_HB
    ya = _flash_attention((pqa, 0), (kv, 0), (kv, gpw), (ptail, 3 * gpw), dtab_a, n_heads,
                          "dsa_attention", mask=mask)
    yb = _flash_attention((ptail, 0), (ptail, gpw), (ptail, 2 * gpw), (ptail, 4 * gpw), dtab_b,
                          n_heads, "moba_attention")

    w_out_b = w_out.astype(BF16)
    return _matmul([ya.reshape(b * s, wq), yb.reshape(b * s, wq)], [(w_out_b, 0, 0), (w_out_b, 1, 0)],
                   F32, "proj_attn_out", res=x2d, bn=MM_BN // 2)


def _recurrent_layer(x2d, b, s, norm_g, w_in, conv_w, conv_b, w_a, b_a, w_x, b_x, lam, w_out):
    d_rnn = conv_b.shape[0]
    h = _rmsnorm(x2d, norm_g, BF16, "rmsnorm1")
    w_in_b = w_in.astype(BF16)
    assert d_rnn % MM_BN == 0
    xr = _matmul([h], [(w_in_b, 0, 0)], F32, "proj_rec_x", n=d_rnn)
    gz = _matmul([h], [(w_in_b, 0, d_rnn // MM_BN)], BF16, "proj_rec_z", n=d_rnn, epilogue=_silu)
    y = _rglru(xr.reshape(b, s, d_rnn), gz.reshape(b, s, d_rnn), conv_w, conv_b,
               _group_block_diag(w_a.astype(BF16), RG_PAIR), b_a,
               _group_block_diag(w_x.astype(BF16), RG_PAIR), b_x, lam)
    return _matmul([y.reshape(b * s, d_rnn)], [w_out.astype(BF16)], F32, "proj_rec_out",
                   res=x2d, bn=MM_BN // 2)


def kernel(x, norm_g, final_g, rel_bias, attn_w_in, attn_kv_g, attn_w_uk, attn_w_uv, idx_k_g, idx_k_b,
           attn_w_out, rec_w_in, rec_conv_w, rec_conv_b, rec_w_a, rec_b_a, rec_w_x, rec_b_x,
           rec_lambda, rec_w_out):
    b, s, d = x.shape
    depth = norm_g.shape[0]
    x2d = x.reshape(b * s, d)
    for layer in range(depth):
        li = layer // 2
        if layer % 2 == 0:
            x2d = _attention_layer(x2d, b, s, norm_g[layer], attn_w_in[li], attn_kv_g[li],
                                   attn_w_uk[li], attn_w_uv[li], idx_k_g[li], idx_k_b[li],
                                   attn_w_out[li], rel_bias)
        else:
            x2d = _recurrent_layer(x2d, b, s, norm_g[layer], rec_w_in[li], rec_conv_w[li],
                                   rec_conv_b[li], rec_w_a[li], rec_b_a[li], rec_w_x[li],
                                   rec_b_x[li], rec_lambda[li], rec_w_out[li])
    return _rmsnorm(x2d, final_g, x.dtype, "rmsnorm_final").reshape(b, s, d)
```

```python
import functools
import math

import numpy as np
import jax
import jax.numpy as jnp
from jax import lax
from jax.experimental import pallas as pl
from jax.experimental.pallas import tpu as pltpu

F32 = jnp.float32
BF16 = jnp.bfloat16
I32 = jnp.int32

HEAD_DIM = 128
KV_LORA = 512
N_IDX_HEADS = 32
IDX_DIM = 64
TOPK_MAX = 256
MOBA_BLOCK = 256
MOBA_TOPB = 3
RG_BLOCKS = 16
CONV_W = 4
RG_C = 8.0
NUM_BUCKETS = 32
MAX_DISTANCE = 128
EPS = 1e-6
ATTN_SCALE = HEAD_DIM ** -0.5
IDX_SCALE = (N_IDX_HEADS ** -0.5) * (IDX_DIM ** -0.5)
LOG2E = math.log2(math.e)

LANES = 128
SUBLANES = 8
WT_ROW_ALIGN = 16
VMEM_LIMIT_BYTES = 56 * 2 ** 20
MASK_NEG = -1e30
INT_MIN = np.int32(-2 ** 31)
KEY_BYTES = 4

ROW_TILE = 512
MM_BM = 1024
MM_BN = 1024
ATT_T = 256
ATT_TK = 2 * ATT_T
ATT_HB = 4
RG_PAIR = 2
RG_TC = 1024
RG_SEG_PAD = 8
RG_SCAN_UNROLL = 4
assert ATT_T == MOBA_BLOCK and ATT_T == 2 * LANES


def _cparams(sem):
    return pltpu.CompilerParams(dimension_semantics=sem, vmem_limit_bytes=VMEM_LIMIT_BYTES)


def _sigmoid(x):
    return 0.5 * jnp.tanh(0.5 * x) + 0.5


def _tile_rows(x, n):
    return x if n == 1 else jnp.concatenate([x] * n, axis=0)


def _tile_lanes(x, n):
    return x if n == 1 else jnp.concatenate([x] * n, axis=1)


def _rmsnorm_kernel(x_ref, g_ref, o_ref):
    x = x_ref[...]
    ms = jnp.mean(x * x, axis=-1, keepdims=True)
    o_ref[...] = ((x * lax.rsqrt(ms + EPS)) * g_ref[...]).astype(o_ref.dtype)


def _rmsnorm(x2d, g, out_dtype, name):
    m, d = x2d.shape
    bm = min(ROW_TILE, m)
    return pl.pallas_call(
        _rmsnorm_kernel,
        grid=(m // bm,),
        in_specs=[pl.BlockSpec((bm, d), lambda i: (i, 0)),
                  pl.BlockSpec((1, d), lambda i: (0, 0))],
        out_specs=pl.BlockSpec((bm, d), lambda i: (i, 0)),
        out_shape=jax.ShapeDtypeStruct((m, d), out_dtype),
        compiler_params=_cparams(("parallel",)),
        name=name,
    )(x2d, g.reshape(1, d).astype(F32))


def _mm_kernel(*refs, n_parts, has_res, epilogue):
    o_ref = refs[-1]
    acc = jnp.dot(refs[0][...], refs[n_parts][...], preferred_element_type=F32)
    for p in range(1, n_parts):
        acc = acc + jnp.dot(refs[p][...], refs[n_parts + p][...], preferred_element_type=F32)
    if has_res:
        acc = refs[2 * n_parts][...] + acc
    if epilogue is not None:
        acc = epilogue(acc)
    o_ref[...] = acc.astype(o_ref.dtype)


def _matmul(xs, ws, out_dtype, name, res=None, n=None, bm=MM_BM, bn=MM_BN, epilogue=None):
    ws = [w if isinstance(w, tuple) else (w, 0, 0) for w in ws]
    m = xs[0].shape[0]
    n = ws[0][0].shape[1] if n is None else n
    bm = min(bm, m)
    bn = min(bn, n)
    assert m % bm == 0 and n % bn == 0
    in_specs = [pl.BlockSpec((bm, x.shape[1]), lambda i, j: (i, 0)) for x in xs]
    for x, (w, rb, cb) in zip(xs, ws):
        assert w.shape[0] % x.shape[1] == 0 and (cb + n // bn) * bn <= w.shape[1]
        in_specs.append(pl.BlockSpec((x.shape[1], bn), lambda i, j, rb=rb, cb=cb: (rb, cb + j)))
    args = [*xs, *[w for w, _, _ in ws]]
    if res is not None:
        in_specs.append(pl.BlockSpec((bm, bn), lambda i, j: (i, j)))
        args.append(res)
    return pl.pallas_call(
        functools.partial(_mm_kernel, n_parts=len(xs), has_res=res is not None, epilogue=epilogue),
        grid=(m // bm, n // bn),
        in_specs=in_specs,
        out_specs=pl.BlockSpec((bm, bn), lambda i, j: (i, j)),
        out_shape=jax.ShapeDtypeStruct((m, n), out_dtype),
        compiler_params=_cparams(("parallel", "parallel")),
        name=name,
    )(*args)


def _silu(x):
    return x * _sigmoid(x)


def _mm_nt_kernel(x_ref, wt_ref, o_ref, *, epilogue):
    acc = lax.dot_general(x_ref[...], wt_ref[...], (((1,), (1,)), ((), ())), preferred_element_type=F32)
    if epilogue is not None:
        acc = epilogue(acc)
    o_ref[...] = acc.astype(o_ref.dtype)


def _matmul_nt(x, wt, row0, n, out_dtype, name, bn=MM_BN, epilogue=None):
    m, k = x.shape
    bm = min(MM_BM, m)
    bn = min(bn, n)
    assert m % bm == 0 and n % bn == 0 and row0 % WT_ROW_ALIGN == 0 and bn % WT_ROW_ALIGN == 0
    assert row0 + n <= wt.shape[0] and wt.shape[1] == k
    return pl.pallas_call(
        functools.partial(_mm_nt_kernel, epilogue=epilogue),
        grid=(m // bm, n // bn),
        in_specs=[pl.BlockSpec((bm, k), lambda i, j: (i, 0)),
                  pl.BlockSpec((pl.Element(bn), pl.Element(k)),
                               lambda i, j: (pl.multiple_of(row0 + j * bn, WT_ROW_ALIGN), 0))],
        out_specs=pl.BlockSpec((bm, bn), lambda i, j: (i, j)),
        out_shape=jax.ShapeDtypeStruct((m, n), out_dtype),
        compiler_params=_cparams(("parallel", "parallel")),
        name=name,
    )(x, wt)


def _t5_bucket_np(dist):
    d = np.maximum(dist, 0)
    max_exact = NUM_BUCKETS // 2
    d_f = np.maximum(d, 1).astype(np.float32)
    ratio = np.log(d_f / np.float32(max_exact)) / np.float32(math.log(MAX_DISTANCE / max_exact))
    large = max_exact + (ratio * np.float32(NUM_BUCKETS - max_exact)).astype(np.int32)
    large = np.minimum(large, NUM_BUCKETS - 1)
    return np.where(d < max_exact, d, large)


def _bias_blocks(tab):
    n = LANES
    assert np.all(_t5_bucket_np(np.arange(n, 64 * n)) == NUM_BUCKETS - 1)
    h = tab.shape[1]
    rel = (tab - tab[NUM_BUCKETS - 1]).astype(F32) * LOG2E
    out = []
    for off in (0, n):
        buckets = _t5_bucket_np(off + n - 1 - np.arange(2 * n - 1))
        onehot = np.zeros((2 * n, NUM_BUCKETS), np.float32)
        onehot[np.arange(2 * n - 1), buckets] = 1.0
        w = jnp.dot(jnp.asarray(onehot), rel, precision=lax.Precision.HIGHEST).T
        t = jnp.tile(w, (1, n))[:, :n * (2 * n - 1)].reshape(h, n, 2 * n - 1)
        out.append(t[:, :, n - 1:])
    return jnp.stack(out)


def _dsa_prep_kernel(p_ref, kvg_ref, ig_ref, ib_ref, ckv_ref, kbd_ref, w_ref):
    p = p_ref[0]
    ts = p.shape[0]
    c = p[:, :KV_LORA]
    cn = (c * lax.rsqrt(jnp.mean(c * c, axis=-1, keepdims=True) + EPS)) * kvg_ref[...]
    ckv_ref[0] = cn.astype(BF16)
    k = p[:, KV_LORA:KV_LORA + IDX_DIM]
    mu = jnp.mean(k, axis=-1, keepdims=True)
    var = jnp.mean(jnp.square(k - mu), axis=-1, keepdims=True)
    kn = ((k - mu) * lax.rsqrt(var + EPS)) * ig_ref[...] + ib_ref[...]
    zeros = jnp.zeros((ts, LANES - IDX_DIM), F32)
    top = jnp.concatenate([kn, zeros], axis=1).T
    bot = jnp.concatenate([zeros, kn], axis=1).T
    kbd_ref[0, 0] = jnp.concatenate([top, bot], axis=1).astype(BF16)
    w_ref[0] = p[:, KV_LORA + LANES:KV_LORA + LANES + N_IDX_HEADS] * IDX_SCALE


def _dsa_prep(ps, kv_g, idx_g, idx_b):
    b, s, wid = ps.shape
    ts = ATT_T
    nkt = s // ts
    return pl.pallas_call(
        _dsa_prep_kernel,
        grid=(b, nkt),
        in_specs=[pl.BlockSpec((1, ts, wid), lambda bi, i: (bi, i, 0)),
                  pl.BlockSpec((1, KV_LORA), lambda bi, i: (0, 0)),
                  pl.BlockSpec((1, IDX_DIM), lambda bi, i: (0, 0)),
                  pl.BlockSpec((1, IDX_DIM), lambda bi, i: (0, 0))],
        out_specs=[pl.BlockSpec((1, ts, KV_LORA), lambda bi, i: (bi, i, 0)),
                   pl.BlockSpec((1, 1, LANES, 2 * ts), lambda bi, i: (bi, i, 0, 0)),
                   pl.BlockSpec((1, ts, N_IDX_HEADS), lambda bi, i: (bi, i, 0))],
        out_shape=[jax.ShapeDtypeStruct((b, s, KV_LORA), BF16),
                   jax.ShapeDtypeStruct((b, nkt, LANES, 2 * ts), BF16),
                   jax.ShapeDtypeStruct((b, s, N_IDX_HEADS), F32)],
        compiler_params=_cparams(("parallel", "parallel")),
        name="dsa_prep",
    )(ps, kv_g.reshape(1, -1).astype(F32), idx_g.reshape(1, -1).astype(F32),
      idx_b.reshape(1, -1).astype(F32))


def _dsa_index_kernel(qi_ref, w_ref, kbd_ref, mask_ref, key_ref, plane_ref, wb_ref, t_ref, j_ref,
                      *, topk, nkt_total, seq_bits):
    tq = tk = ATT_T
    rep = tk // LANES
    sl = SUBLANES
    i = pl.program_id(1)
    t0 = i * tq
    nkt = i + 1

    row = t0 + lax.broadcasted_iota(I32, (tq, tk), 0)
    col = lax.broadcasted_iota(I32, (tq, tk), 1)
    row_t = lax.broadcasted_iota(I32, (tk, tq), 0)
    col_t = t0 + lax.broadcasted_iota(I32, (tk, tq), 1)

    def to_key(score):
        score = jnp.where(score == 0.0, 0.0, score)
        bits = lax.bitcast_convert_type(score, I32)
        return bits ^ ((bits >> 31) & np.int32(0x7FFFFFFF))

    for h in range(N_IDX_HEADS):
        wb_ref[h] = jnp.broadcast_to(w_ref[0, :, h:h + 1], (tq, LANES))

    def idx_body(kt, carry):
        kb = kbd_ref[0, kt]
        acc = jnp.zeros((tq, tk), F32)
        for hp in range(N_IDX_HEADS // 2):
            rel = jnp.dot(qi_ref[0, :, hp * LANES:(hp + 1) * LANES], kb,
                          preferred_element_type=F32)
            acc = acc + jnp.maximum(rel[:, :tk], 0.0) * _tile_lanes(wb_ref[2 * hp], rep)
            acc = acc + jnp.maximum(rel[:, tk:], 0.0) * _tile_lanes(wb_ref[2 * hp + 1], rep)
        key_ref[kt] = jnp.where(kt * tk + col <= row, to_key(acc), INT_MIN)
        ukey = jnp.where(kt * tk + row_t <= col_t, to_key(acc.T), INT_MIN) ^ INT_MIN
        for bi in range(KEY_BYTES):
            byte = (ukey >> (8 * (KEY_BYTES - 1 - bi))) & np.int32(0xFF)
            plane_ref[bi, kt] = byte.astype(F32).astype(BF16)
        return carry

    lax.fori_loop(0, nkt, idx_body, 0)

    one, zero = jnp.ones((), BF16), jnp.zeros((), BF16)

    def count(plane, pred_fn):
        rows = 2 * sl
        def body(kt, c):
            xs = [pred_fn(plane_ref[plane, kt, r * rows:(r + 1) * rows, :], kt * tk + r * rows)
                  for r in range(tk // rows)]
            while len(xs) > 1:
                xs = [xs[a] + xs[a + 1] for a in range(0, len(xs), 2)]
            return c + xs[0]
        c = lax.fori_loop(0, nkt, body, jnp.zeros((rows, tq), BF16))
        return jnp.broadcast_to(jnp.sum(c.astype(F32), axis=0, keepdims=True), (sl, tq))

    def rep16(v8):
        return jnp.concatenate([v8, v8], axis=0).astype(BF16)

    target = jnp.full((sl, tq), float(topk), F32)
    thr_u = jnp.zeros((sl, tq), I32)
    for bi in range(KEY_BYTES):
        t_ref[...] = jnp.zeros((sl, tq), F32)

        def bit_body(it, carry, bi=bi, target=target):
            cand = t_ref[...] + jnp.left_shift(jnp.int32(1), 7 - it).astype(F32)
            c16 = rep16(cand)
            n_ge = count(bi, lambda d, s0: jnp.where(d >= c16, one, zero))
            t_ref[...] = jnp.where(n_ge >= target, cand, t_ref[...])
            return carry

        lax.fori_loop(0, 8, bit_body, 0)
        tb = t_ref[...]
        tb16 = rep16(tb)
        thr_u = (thr_u << 8) | tb.astype(I32)
        if bi + 1 < KEY_BYTES:
            target = target - count(bi, lambda d, s0: jnp.where(d > tb16, one, zero))

            def narrow_body(kt, carry, bi=bi):
                plane_ref[bi + 1, kt] = jnp.where(plane_ref[bi, kt] == _tile_rows(tb16, tk // (2 * sl)),
                                                  plane_ref[bi + 1, kt], -one)
                return carry

            lax.fori_loop(0, nkt, narrow_body, 0)

    last = KEY_BYTES - 1
    n_gt = count(last, lambda d, s0: jnp.where(d > tb16, one, zero))
    n_ge = count(last, lambda d, s0: jnp.where(d >= tb16, one, zero))
    need = target - n_gt
    j_ref[...] = jnp.full((sl, tq), 2 ** 30, I32)
    sub16 = lax.broadcasted_iota(I32, (2 * sl, tq), 0)

    @pl.when(jnp.max(n_ge - target) > 0.0)
    def _():
        j_ref[...] = jnp.zeros((sl, tq), I32)

        def jbit_body(it, carry):
            cand = j_ref[...] | jnp.left_shift(jnp.int32(1), seq_bits - 1 - it)
            c2 = jnp.concatenate([cand, cand], axis=0)
            n_lt = count(last, lambda d, s0: jnp.where(
                d == tb16, jnp.where(s0 + sub16 < c2, 1.0, 0.0).astype(BF16), zero))
            j_ref[...] = jnp.where(n_lt < need, cand, j_ref[...])
            return carry

        lax.fori_loop(0, seq_bits, jbit_body, 0)

    thr = thr_u ^ INT_MIN

    def per_row(v8):
        wide = jnp.broadcast_to(lax.bitcast_convert_type(v8[0:1], F32), (LANES, tq))
        return _tile_lanes(lax.bitcast_convert_type(wide.T, I32), rep)

    thr_t = per_row(thr)
    jmax_t = per_row(j_ref[...])

    def mask_body(kt, carry):
        k = key_ref[kt]
        s_idx = kt * tk + col
        keep_tie = jnp.where(k == thr_t, jnp.where(s_idx <= jmax_t, 0.0, MASK_NEG), MASK_NEG)
        keep = jnp.where(k > thr_t, 0.0, keep_tie)
        mask_ref[0, 0, kt] = jnp.where(s_idx <= row, keep, MASK_NEG).astype(BF16)
        return carry

    lax.fori_loop(0, nkt, mask_body, 0)

    def fill_body(kt, carry):
        mask_ref[0, 0, kt] = jnp.full((tq, tk), MASK_NEG, BF16)
        return carry

    lax.fori_loop(nkt, nkt_total, fill_body, 0)


def _dsa_index(pqi, col_qi, kbd, widx):
    b, s, _ = pqi.shape
    tq = tk = ATT_T
    nkt = s // tk
    wq = N_IDX_HEADS * IDX_DIM
    topk = min(TOPK_MAX, s // 4)
    assert s // (2 * SUBLANES) <= 256
    kern = functools.partial(_dsa_index_kernel, topk=topk, nkt_total=nkt,
                             seq_bits=max(1, (s - 1).bit_length()))
    return pl.pallas_call(
        kern,
        grid=(b, s // tq),
        in_specs=[pl.BlockSpec((1, tq, wq), lambda bi, i: (bi, i, col_qi)),
                  pl.BlockSpec((1, tq, N_IDX_HEADS), lambda bi, i: (bi, i, 0)),
                  pl.BlockSpec((1, nkt, LANES, 2 * tk), lambda bi, i: (bi, 0, 0, 0))],
        out_specs=pl.BlockSpec((1, 1, nkt, tq, tk), lambda bi, i: (bi, i, 0, 0, 0)),
        out_shape=jax.ShapeDtypeStruct((b, s // tq, nkt, tq, tk), BF16),
        scratch_shapes=[pltpu.VMEM((nkt, tq, tk), I32),
                        pltpu.VMEM((KEY_BYTES, nkt, tk, tq), BF16),
                        pltpu.VMEM((N_IDX_HEADS, tq, LANES), F32),
                        pltpu.VMEM((SUBLANES, tq), F32),
                        pltpu.VMEM((SUBLANES, tq), I32)],
        compiler_params=_cparams(("parallel", "parallel")),
        name="dsa_index",
    )(pqi, widx, kbd)


def _flash_kernel(*refs, mode, nb, topb):
    if mode == "mask":
        q_ref, k_ref, v_ref, z_ref, d_ref, mask_ref, y_ref = refs[:7]
    else:
        q_ref, k_ref, v_ref, z_ref, d_ref, y_ref, km_ref = refs[:7]
    qs_ref, acc_ref, m_ref, p_ref, alpha_ref, x_ref = refs[7:]
    blk_rows = ATT_T
    t = ATT_TK
    hb = q_ref.shape[2] // HEAD_DIM
    rep = t // LANES
    i = pl.program_id(2)
    ones_blk = jnp.ones((t, LANES), BF16)

    if mode == "sel":
        @pl.when(i == 0)
        def _():
            km_ref[...] = jnp.zeros(km_ref.shape, F32)
            for h in range(hb):
                for n in range(nb):
                    kblk = k_ref[0, n * blk_rows:(n + 1) * blk_rows,
                                 h * HEAD_DIM:(h + 1) * HEAD_DIM].astype(F32)
                    km_ref[h, n:n + 1, :] = jnp.mean(kblk, axis=0, keepdims=True)
        nbp = -(-nb // SUBLANES) * SUBLANES
        blk = lax.broadcasted_iota(I32, (nbp, t), 0).astype(F32)
        own = (2 * i + lax.broadcasted_iota(I32, (nbp, t), 1) // blk_rows).astype(F32)
        lane = lax.broadcasted_iota(I32, (t, LANES), 1)
        second_half = lax.broadcasted_iota(I32, (t, LANES), 0) >= blk_rows

    for h in range(hb):
        q = q_ref[0, :, h * HEAD_DIM:(h + 1) * HEAD_DIM]
        qsc = (q.astype(F32) * (ATTN_SCALE * LOG2E)).astype(BF16)
        if mode == "sel":
            gate = lax.dot_general(km_ref[h, 0:nbp, :].astype(BF16), q, (((1,), (1,)), ((), ())),
                                   preferred_element_type=F32)
            gate = jnp.where(blk < own, gate, -jnp.inf)
            sel = jnp.full((nbp, t), MASK_NEG, F32)
            for _ in range(topb):
                gmax = jnp.max(gate, axis=0, keepdims=True)
                first = jnp.min(jnp.where(gate == gmax, blk, float(nbp)), axis=0, keepdims=True)
                hit = jnp.where(blk == first, jnp.where(gmax > -jnp.inf, 1.0, 0.0), 0.0)
                sel = jnp.where(hit > 0.0, 0.0, sel)
                gate = jnp.where(blk == first, -jnp.inf, gate)
            sel = jnp.where(blk == own, 0.0, sel)
            selb = jnp.concatenate([sel, jnp.full((LANES - nbp, t), MASK_NEG, F32)], axis=0).T
            qs_ref[h] = jnp.concatenate([qsc, selb.astype(BF16)], axis=1)
        else:
            qs_ref[h] = qsc
        m_ref[h] = jnp.full((t, LANES), MASK_NEG, F32)
        acc_ref[h] = jnp.zeros((t, 2 * HEAD_DIM), F32)
        p_ref[h] = jnp.zeros((t, t), BF16)
        alpha_ref[h] = jnp.ones((t, LANES), F32)

    def qk_stage(j):
        r0 = pl.multiple_of(j * t, t)
        if mode == "sel":
            onehot = jnp.where(lane == jnp.where(second_half, 2 * j + 1, 2 * j), 1.0, 0.0).astype(BF16)
        for h in range(hb):
            kblk = k_ref[0, pl.ds(r0, t), h * HEAD_DIM:(h + 1) * HEAD_DIM]
            if mode == "sel":
                kblk = jnp.concatenate([kblk, onehot], axis=1)
            x_ref[h] = lax.dot_general(qs_ref[h], kblk, (((1,), (1,)), ((), ())),
                                       preferred_element_type=F32)

    def add_bias(x, h, kind, j):
        d0, d1 = d_ref[0, h], d_ref[1, h]
        blk_of = lambda r, c: x[r * LANES:(r + 1) * LANES, c * LANES:(c + 1) * LANES]
        rows = []
        for r in range(rep):
            cols = []
            for c in range(rep):
                b = blk_of(r, c)
                if kind == "past":
                    if r == 0 and c == rep - 1:
                        b = b + d1 * jnp.where(j == i - 1, 1.0, 0.0)
                elif r == c:
                    b = b + d0
                elif r == c + 1:
                    b = b + d1
                cols.append(b)
            rows.append(jnp.concatenate(cols, axis=1))
        return jnp.concatenate(rows, axis=0)

    def softmax_stage(j, kind):
        if mode == "mask":
            mask_tile = jnp.concatenate(
                [jnp.concatenate([mask_ref[0, r, 2 * j], mask_ref[0, r, 2 * j + 1]], axis=1)
                 for r in range(2)], axis=0).astype(F32)
        elif kind == "diag":
            row = lax.broadcasted_iota(I32, (t, t), 0)
            col = lax.broadcasted_iota(I32, (t, t), 1)
        for h in range(hb):
            x = x_ref[h]
            if mode == "mask":
                x = x + mask_tile
            x = add_bias(x, h, kind, j)
            if kind == "diag" and mode == "sel":
                x = jnp.where(col <= row, x, MASK_NEG)
            m_prev = m_ref[h]
            m_new = jnp.maximum(m_prev, jnp.max(x, axis=1, keepdims=True))
            alpha_ref[h] = jnp.exp2(m_prev - m_new)
            p_ref[h] = jnp.exp2(x - _tile_lanes(m_new, rep)).astype(BF16)
            m_ref[h] = m_new

    def pv_stage(j):
        r0 = pl.multiple_of(j * t, t)
        for h in range(hb):
            vblk = v_ref[0, pl.ds(r0, t), h * HEAD_DIM:(h + 1) * HEAD_DIM]
            pv = jnp.dot(p_ref[h], jnp.concatenate([vblk, ones_blk], axis=1),
                         preferred_element_type=F32)
            acc_ref[h] = acc_ref[h] * _tile_lanes(alpha_ref[h], 2) + pv

    def step(j, kind):
        pv_stage(jnp.maximum(j - 1, 0))
        softmax_stage(j, kind)
        if kind != "diag":
            qk_stage(j + 1)

    qk_stage(0)

    def past_body(j, carry):
        step(j, "past")
        return carry

    lax.fori_loop(0, i, past_body, 0)
    step(i, "diag")
    pv_stage(i)

    for h in range(hb):
        acc = acc_ref[h]
        o = acc[:, :HEAD_DIM] / acc[:, HEAD_DIM:]
        zz = z_ref[0, :, h * HEAD_DIM:(h + 1) * HEAD_DIM].astype(F32)
        y_ref[0, :, h * HEAD_DIM:(h + 1) * HEAD_DIM] = (o * (zz * _sigmoid(zz))).astype(BF16)


def _flash_attention(q, k, v, z, dtab, n_heads, name, mask=None):
    (qa, cq), (ka, ck), (va, cv), (za, cz) = q, k, v, z
    b, s, _ = qa.shape
    t = ATT_TK
    assert s % t == 0
    nb = s // ATT_T
    hb = ATT_HB
    wg = hb * HEAD_DIM
    mode = "sel" if mask is None else "mask"
    kern = functools.partial(_flash_kernel, mode=mode, nb=nb, topb=min(MOBA_TOPB, nb - 1))
    in_specs = [pl.BlockSpec((1, t, wg), lambda bi, g, i: (bi, i, cq + g)),
                pl.BlockSpec((1, s, wg), lambda bi, g, i: (bi, 0, ck + g)),
                pl.BlockSpec((1, s, wg), lambda bi, g, i: (bi, 0, cv + g)),
                pl.BlockSpec((1, t, wg), lambda bi, g, i: (bi, i, cz + g)),
                pl.BlockSpec((2, hb, LANES, LANES), lambda bi, g, i: (0, g, 0, 0))]
    args = [qa, ka, va, za, dtab]
    scratch = []
    if mode == "mask":
        in_specs.append(pl.BlockSpec((1, t // ATT_T, nb, ATT_T, ATT_T), lambda bi, g, i: (bi, i, 0, 0, 0)))
        args.append(mask)
        kq = HEAD_DIM
    else:
        scratch.append(pltpu.VMEM((hb, LANES, HEAD_DIM), F32))
        kq = 2 * HEAD_DIM
    scratch += [pltpu.VMEM((hb, t, kq), BF16),
                pltpu.VMEM((hb, t, 2 * HEAD_DIM), F32),
                pltpu.VMEM((hb, t, LANES), F32),
                pltpu.VMEM((hb, t, t), BF16),
                pltpu.VMEM((hb, t, LANES), F32),
                pltpu.VMEM((hb, t, t), F32)]
    return pl.pallas_call(
        kern,
        grid=(b, n_heads // hb, s // t),
        in_specs=in_specs,
        out_specs=pl.BlockSpec((1, t, wg), lambda bi, g, i: (bi, i, g)),
        out_shape=jax.ShapeDtypeStruct((b, s, n_heads * HEAD_DIM), BF16),
        scratch_shapes=scratch,
        compiler_params=_cparams(("parallel", "parallel", "arbitrary")),
        name=name,
    )(*args)


def _softplus(x):
    return jnp.maximum(x, 0.0) + jnp.log1p(jnp.exp(-jnp.abs(x)))


def _rglru_kernel(xr_ref, gz_ref, cw_ref, cb_ref, wa_ref, ba_ref, wx_ref, bx_ref, lam_ref, y_ref,
                  xbuf_ref, a_ref, b_ref, h_ref):
    tc = xr_ref.shape[1]
    c = pl.program_id(2)
    halo = SUBLANES

    @pl.when(c == 0)
    def _():
        xbuf_ref[0:halo, :] = jnp.zeros((halo, xbuf_ref.shape[1]), F32)
        h_ref[...] = jnp.zeros(h_ref.shape, F32)

    @pl.when(c > 0)
    def _():
        xbuf_ref[0:halo, :] = xbuf_ref[tc:tc + halo, :]

    xbuf_ref[halo:halo + tc, :] = xr_ref[0]
    xfull = xbuf_ref[...]
    u = xfull * cw_ref[0:1, :]
    for j in range(1, CONV_W):
        u = xfull * cw_ref[j:j + 1, :] + pltpu.roll(u, 1, 0)
    xc = (u + cb_ref[...])[halo:, :]

    xcb = xc.astype(BF16)
    tr = jnp.tanh(jnp.dot(xcb, wa_ref[0], preferred_element_type=F32) + ba_ref[...])
    ti = jnp.tanh(jnp.dot(xcb, wx_ref[0], preferred_element_type=F32) + bx_ref[...])
    half_c = (-0.5 * RG_C * LOG2E) * _softplus(-lam_ref[...])
    a = jnp.exp2(tr * half_c + half_c)
    one_m_a2 = 1.0 - a * a
    mult = jnp.where(one_m_a2 > 0.0, one_m_a2 * lax.rsqrt(one_m_a2), 0.0)
    bt = (mult * xc) * (0.5 * ti + 0.5)

    nseg = SUBLANES
    seg = tc // nseg
    pitch = seg + RG_SEG_PAD
    nslab = a_ref.shape[0]
    for sl in range(nslab):
        for sg in range(nseg):
            a_ref[sl, sg * pitch:sg * pitch + seg, :] = a[sg * seg:(sg + 1) * seg, sl * LANES:(sl + 1) * LANES]
            b_ref[sl, sg * pitch:sg * pitch + seg, :] = bt[sg * seg:(sg + 1) * seg, sl * LANES:(sl + 1) * LANES]

    def seg_step(j, carry):
        out = []
        for sl in range(nslab):
            hl, pr = carry[2 * sl], carry[2 * sl + 1]
            rows = pl.ds(j, nseg, stride=pitch)
            a8 = a_ref[sl, rows, :]
            hl = a8 * hl + b_ref[sl, rows, :]
            pr = a8 * pr
            b_ref[sl, rows, :] = hl
            a_ref[sl, rows, :] = pr
            out += [hl, pr]
        return tuple(out)

    init = tuple(v for _ in range(nslab)
                 for v in (jnp.zeros((nseg, LANES), F32), jnp.ones((nseg, LANES), F32)))
    ends = lax.fori_loop(0, seg, seg_step, init, unroll=RG_SCAN_UNROLL)

    gate = gz_ref[0].astype(F32)
    for sl in range(nslab):
        h_end, p_end = ends[2 * sl], ends[2 * sl + 1]
        h_in = h_ref[:, sl * LANES:(sl + 1) * LANES]
        for sg in range(nseg):
            rows = slice(sg * pitch, sg * pitch + seg)
            hs = b_ref[sl, rows, :] + a_ref[sl, rows, :] * h_in
            y_ref[0, sg * seg:(sg + 1) * seg, sl * LANES:(sl + 1) * LANES] = (
                hs * gate[sg * seg:(sg + 1) * seg, sl * LANES:(sl + 1) * LANES]).astype(BF16)
            h_in = p_end[sg:sg + 1, :] * h_in + h_end[sg:sg + 1, :]
        h_ref[:, sl * LANES:(sl + 1) * LANES] = h_in


def _rglru(xr, gz, conv_w, conv_b, wa_g, b_a, wx_g, b_x, lam):
    b, s, d = xr.shape
    g = wa_g.shape[0]
    cg = d // g
    tc = min(RG_TC, s)
    assert ((tc // SUBLANES + RG_SEG_PAD) // SUBLANES) % 2 == 1 and cg % LANES == 0
    row = lambda v: v.reshape(1, d).astype(F32)
    vec_spec = pl.BlockSpec((1, cg), lambda bi, gi, c: (0, gi))
    return pl.pallas_call(
        _rglru_kernel,
        grid=(b, g, s // tc),
        in_specs=[pl.BlockSpec((1, tc, cg), lambda bi, gi, c: (bi, c, gi)),
                  pl.BlockSpec((1, tc, cg), lambda bi, gi, c: (bi, c, gi)),
                  pl.BlockSpec((CONV_W, cg), lambda bi, gi, c: (0, gi)),
                  vec_spec,
                  pl.BlockSpec((1, cg, cg), lambda bi, gi, c: (gi, 0, 0)),
                  vec_spec,
                  pl.BlockSpec((1, cg, cg), lambda bi, gi, c: (gi, 0, 0)),
                  vec_spec,
                  vec_spec],
        out_specs=pl.BlockSpec((1, tc, cg), lambda bi, gi, c: (bi, c, gi)),
        out_shape=jax.ShapeDtypeStruct((b, s, d), BF16),
        scratch_shapes=[pltpu.VMEM((tc + SUBLANES, cg), F32),
                        pltpu.VMEM((cg // LANES, tc + SUBLANES * RG_SEG_PAD, LANES), F32),
                        pltpu.VMEM((cg // LANES, tc + SUBLANES * RG_SEG_PAD, LANES), F32),
                        pltpu.VMEM((1, cg), F32)],
        compiler_params=_cparams(("parallel", "parallel", "arbitrary")),
        name="rglru",
    )(xr, gz, conv_w.astype(F32), row(conv_b), wa_g * 0.5, row(b_a) * 0.5, wx_g * 0.5, row(b_x) * 0.5,
      row(lam))


def _group_block_diag(w, pair):
    nb, k, _ = w.shape
    out = None
    for p in range(pair):
        lo, hi = p * k, (pair - 1 - p) * k
        blk = jnp.pad(w[p::pair], ((0, 0), (lo, hi), (lo, hi)))
        out = blk if out is None else out + blk
    return out


def _attention_layer(x2d, b, s, norm_g, w_in, kv_g, w_uk, w_uv, idx_g, idx_b, w_out, rel_bias):
    d_model = x2d.shape[1]
    n_heads = w_uk.shape[0]
    wq = n_heads * HEAD_DIM
    assert N_IDX_HEADS * IDX_DIM == wq
    sizes = (wq, KV_LORA, wq, IDX_DIM, N_IDX_HEADS, wq, wq, wq, 2 * wq)
    offs = [int(o) for o in np.concatenate([[0], np.cumsum(sizes)])]
    wt = w_in.T.astype(BF16)
    zrows = lambda n: jnp.zeros((n, d_model), BF16)
    wt_small = jnp.concatenate([wt[offs[1]:offs[2]], wt[offs[3]:offs[4]], zrows(LANES - IDX_DIM),
                                wt[offs[4]:offs[5]], zrows(LANES - N_IDX_HEADS)], axis=0)

    h = _rmsnorm(x2d, norm_g, BF16, "rmsnorm0")
    pqa = _matmul_nt(h, wt, offs[0], wq, BF16, "proj_attn_qa").reshape(b, s, wq)
    pqi = _matmul_nt(h, wt, offs[2], wq, BF16, "proj_attn_qi").reshape(b, s, wq)
    ptail = _matmul_nt(h, wt, offs[5], 5 * wq, BF16, "proj_attn_tail").reshape(b, s, 5 * wq)
    psmall = _matmul_nt(h, wt_small, 0, wt_small.shape[0], F32, "proj_attn_small").reshape(b, s, -1)

    ckv, kbd, widx = _dsa_prep(psmall, kv_g, idx_g, idx_b)
    mask = _dsa_index(pqi, 0, kbd, widx)
    w_kv = jnp.concatenate([jnp.transpose(w_uk, (1, 0, 2)).reshape(KV_LORA, wq),
                            jnp.transpose(w_uv, (1, 0, 2)).reshape(KV_LORA, wq)], axis=1).astype(BF16)
    kv = _matmul([ckv.reshape(b * s, KV_LORA)], [w_kv], BF16, "dsa_kv_up", bm=2 * MM_BM).reshape(b, s, 2 * wq)

    dtab_a = _bias_blocks(rel_bias[:, :n_heads])
    dtab_b = _bias_blocks(rel_bias[:, n_heads:])
    gpw = n_heads // ATT_HB
    ya = _flash_attention((pqa, 0), (kv, 0), (kv, gpw), (ptail, 3 * gpw), dtab_a, n_heads,
                          "dsa_attention", mask=mask)
    yb = _flash_attention((ptail, 0), (ptail, gpw), (ptail, 2 * gpw), (ptail, 4 * gpw), dtab_b,
                          n_heads, "moba_attention")

    w_out_b = w_out.astype(BF16)
    return _matmul([ya.reshape(b * s, wq), yb.reshape(b * s, wq)], [(w_out_b, 0, 0), (w_out_b, 1, 0)],
                   F32, "proj_attn_out", res=x2d, bn=MM_BN // 2)


def _recurrent_layer(x2d, b, s, norm_g, w_in, conv_w, conv_b, w_a, b_a, w_x, b_x, lam, w_out):
    d_rnn = conv_b.shape[0]
    h = _rmsnorm(x2d, norm_g, BF16, "rmsnorm1")
    w_in_b = w_in.astype(BF16)
    assert d_rnn % MM_BN == 0
    xr = _matmul([h], [(w_in_b, 0, 0)], F32, "proj_rec_x", n=d_rnn)
    gz = _matmul([h], [(w_in_b, 0, d_rnn // MM_BN)], BF16, "proj_rec_z", n=d_rnn, epilogue=_silu)
    y = _rglru(xr.reshape(b, s, d_rnn), gz.reshape(b, s, d_rnn), conv_w, conv_b,
               _group_block_diag(w_a.astype(BF16), RG_PAIR), b_a,
               _group_block_diag(w_x.astype(BF16), RG_PAIR), b_x, lam)
    return _matmul([y.reshape(b * s, d_rnn)], [w_out.astype(BF16)], F32, "proj_rec_out",
                   res=x2d, bn=MM_BN // 2)


def kernel(x, norm_g, final_g, rel_bias, attn_w_in, attn_kv_g, attn_w_uk, attn_w_uv, idx_k_g, idx_k_b,
           attn_w_out, rec_w_in, rec_conv_w, rec_conv_b, rec_w_a, rec_b_a, rec_w_x, rec_b_x,
           rec_lambda, rec_w_out):
    b, s, d = x.shape
    depth = norm_g.shape[0]
    x2d = x.reshape(b * s, d)
    for layer in range(depth):
        li = layer // 2
        if layer % 2 == 0:
            x2d = _attention_layer(x2d, b, s, norm_g[layer], attn_w_in[li], attn_kv_g[li],
                                   attn_w_uk[li], attn_w_uv[li], idx_k_g[li], idx_k_b[li],
                                   attn_w_out[li], rel_bias)
        else:
            x2d = _recurrent_layer(x2d, b, s, norm_g[layer], rec_w_in[li], rec_conv_w[li],
                                   rec_conv_b[li], rec_w_a[li], rec_b_a[li], rec_w_x[li],
                                   rec_b_x[li], rec_lambda[li], rec_w_out[li])
    return _rmsnorm(x2d, final_g, x.dtype, "rmsnorm_final").reshape(b, s, d)
```

```python
import functools
import math

import numpy as np
import jax
import jax.numpy as jnp
from jax import lax
from jax.experimental import pallas as pl
from jax.experimental.pallas import tpu as pltpu

F32 = jnp.float32
BF16 = jnp.bfloat16
I32 = jnp.int32

HEAD_DIM = 128
KV_LORA = 512
N_IDX_HEADS = 32
IDX_DIM = 64
TOPK_MAX = 256
MOBA_BLOCK = 256
MOBA_TOPB = 3
RG_BLOCKS = 16
CONV_W = 4
RG_C = 8.0
NUM_BUCKETS = 32
MAX_DISTANCE = 128
EPS = 1e-6
ATTN_SCALE = HEAD_DIM ** -0.5
IDX_SCALE = (N_IDX_HEADS ** -0.5) * (IDX_DIM ** -0.5)
LOG2E = math.log2(math.e)

LANES = 128
SUBLANES = 8
WT_ROW_ALIGN = 16
VMEM_LIMIT_BYTES = 56 * 2 ** 20
MASK_NEG = -1e30
INT_MIN = np.int32(-2 ** 31)
KEY_BYTES = 4

ROW_TILE = 512
MM_BM = 1024
MM_BN = 1024
ATT_T = 256
ATT_TK = 2 * ATT_T
ATT_HB = 4
IDX_TQ = 2 * ATT_T
RG_PAIR = 2
RG_TC = 1024
RG_SEG_PAD = 8
RG_SCAN_UNROLL = 4
assert ATT_T == MOBA_BLOCK and ATT_T == 2 * LANES


def _cparams(sem):
    return pltpu.CompilerParams(dimension_semantics=sem, vmem_limit_bytes=VMEM_LIMIT_BYTES)


def _sigmoid(x):
    return 0.5 * jnp.tanh(0.5 * x) + 0.5


def _tile_rows(x, n):
    return x if n == 1 else jnp.concatenate([x] * n, axis=0)


def _tile_lanes(x, n):
    return x if n == 1 else jnp.concatenate([x] * n, axis=1)


def _rmsnorm_kernel(x_ref, g_ref, o_ref):
    x = x_ref[...]
    ms = jnp.mean(x * x, axis=-1, keepdims=True)
    o_ref[...] = ((x * lax.rsqrt(ms + EPS)) * g_ref[...]).astype(o_ref.dtype)


def _rmsnorm(x2d, g, out_dtype, name):
    m, d = x2d.shape
    bm = min(ROW_TILE, m)
    return pl.pallas_call(
        _rmsnorm_kernel,
        grid=(m // bm,),
        in_specs=[pl.BlockSpec((bm, d), lambda i: (i, 0)),
                  pl.BlockSpec((1, d), lambda i: (0, 0))],
        out_specs=pl.BlockSpec((bm, d), lambda i: (i, 0)),
        out_shape=jax.ShapeDtypeStruct((m, d), out_dtype),
        compiler_params=_cparams(("parallel",)),
        name=name,
    )(x2d, g.reshape(1, d).astype(F32))


def _mm_kernel(*refs, n_parts, has_res, epilogue):
    o_ref = refs[-1]
    acc = jnp.dot(refs[0][...], refs[n_parts][...], preferred_element_type=F32)
    for p in range(1, n_parts):
        acc = acc + jnp.dot(refs[p][...], refs[n_parts + p][...], preferred_element_type=F32)
    if has_res:
        acc = refs[2 * n_parts][...] + acc
    if epilogue is not None:
        acc = epilogue(acc)
    o_ref[...] = acc.astype(o_ref.dtype)


def _matmul(xs, ws, out_dtype, name, res=None, n=None, bm=MM_BM, bn=MM_BN, epilogue=None):
    ws = [w if isinstance(w, tuple) else (w, 0, 0) for w in ws]
    m = xs[0].shape[0]
    n = ws[0][0].shape[1] if n is None else n
    bm = min(bm, m)
    bn = min(bn, n)
    assert m % bm == 0 and n % bn == 0
    in_specs = [pl.BlockSpec((bm, x.shape[1]), lambda i, j: (i, 0)) for x in xs]
    for x, (w, rb, cb) in zip(xs, ws):
        assert w.shape[0] % x.shape[1] == 0 and (cb + n // bn) * bn <= w.shape[1]
        in_specs.append(pl.BlockSpec((x.shape[1], bn), lambda i, j, rb=rb, cb=cb: (rb, cb + j)))
    args = [*xs, *[w for w, _, _ in ws]]
    if res is not None:
        in_specs.append(pl.BlockSpec((bm, bn), lambda i, j: (i, j)))
        args.append(res)
    return pl.pallas_call(
        functools.partial(_mm_kernel, n_parts=len(xs), has_res=res is not None, epilogue=epilogue),
        grid=(m // bm, n // bn),
        in_specs=in_specs,
        out_specs=pl.BlockSpec((bm, bn), lambda i, j: (i, j)),
        out_shape=jax.ShapeDtypeStruct((m, n), out_dtype),
        compiler_params=_cparams(("parallel", "parallel")),
        name=name,
    )(*args)


def _silu(x):
    return x * _sigmoid(x)


def _mm_nt_kernel(x_ref, wt_ref, o_ref, *, epilogue):
    acc = lax.dot_general(x_ref[...], wt_ref[...], (((1,), (1,)), ((), ())), preferred_element_type=F32)
    if epilogue is not None:
        acc = epilogue(acc)
    o_ref[...] = acc.astype(o_ref.dtype)


def _matmul_nt(x, wt, row0, n, out_dtype, name, bn=MM_BN, epilogue=None):
    m, k = x.shape
    bm = min(MM_BM, m)
    bn = min(bn, n)
    assert m % bm == 0 and n % bn == 0 and row0 % WT_ROW_ALIGN == 0 and bn % WT_ROW_ALIGN == 0
    assert row0 + n <= wt.shape[0] and wt.shape[1] == k
    return pl.pallas_call(
        functools.partial(_mm_nt_kernel, epilogue=epilogue),
        grid=(m // bm, n // bn),
        in_specs=[pl.BlockSpec((bm, k), lambda i, j: (i, 0)),
                  pl.BlockSpec((pl.Element(bn), pl.Element(k)),
                               lambda i, j: (pl.multiple_of(row0 + j * bn, WT_ROW_ALIGN), 0))],
        out_specs=pl.BlockSpec((bm, bn), lambda i, j: (i, j)),
        out_shape=jax.ShapeDtypeStruct((m, n), out_dtype),
        compiler_params=_cparams(("parallel", "parallel")),
        name=name,
    )(x, wt)


def _t5_bucket_np(dist):
    d = np.maximum(dist, 0)
    max_exact = NUM_BUCKETS // 2
    d_f = np.maximum(d, 1).astype(np.float32)
    ratio = np.log(d_f / np.float32(max_exact)) / np.float32(math.log(MAX_DISTANCE / max_exact))
    large = max_exact + (ratio * np.float32(NUM_BUCKETS - max_exact)).astype(np.int32)
    large = np.minimum(large, NUM_BUCKETS - 1)
    return np.where(d < max_exact, d, large)


def _bias_blocks(tab):
    n = LANES
    assert np.all(_t5_bucket_np(np.arange(n, 64 * n)) == NUM_BUCKETS - 1)
    h = tab.shape[1]
    rel = (tab - tab[NUM_BUCKETS - 1]).astype(F32) * LOG2E
    out = []
    for off in (0, n):
        buckets = _t5_bucket_np(off + n - 1 - np.arange(2 * n - 1))
        onehot = np.zeros((2 * n, NUM_BUCKETS), np.float32)
        onehot[np.arange(2 * n - 1), buckets] = 1.0
        w = jnp.dot(jnp.asarray(onehot), rel, precision=lax.Precision.HIGHEST).T
        t = jnp.tile(w, (1, n))[:, :n * (2 * n - 1)].reshape(h, n, 2 * n - 1)
        out.append(t[:, :, n - 1:])
    return jnp.stack(out)


def _dsa_prep_kernel(p_ref, kvg_ref, ig_ref, ib_ref, ckv_ref, kbd_ref, w_ref):
    p = p_ref[0]
    ts = p.shape[0]
    c = p[:, :KV_LORA]
    cn = (c * lax.rsqrt(jnp.mean(c * c, axis=-1, keepdims=True) + EPS)) * kvg_ref[...]
    ckv_ref[0] = cn.astype(BF16)
    k = p[:, KV_LORA:KV_LORA + IDX_DIM]
    mu = jnp.mean(k, axis=-1, keepdims=True)
    var = jnp.mean(jnp.square(k - mu), axis=-1, keepdims=True)
    kn = ((k - mu) * lax.rsqrt(var + EPS)) * ig_ref[...] + ib_ref[...]
    zeros = jnp.zeros((ts, LANES - IDX_DIM), F32)
    top = jnp.concatenate([kn, zeros], axis=1).T
    bot = jnp.concatenate([zeros, kn], axis=1).T
    kbd_ref[0, 0] = jnp.concatenate([top, bot], axis=1).astype(BF16)
    w_ref[0] = p[:, KV_LORA + LANES:KV_LORA + LANES + N_IDX_HEADS] * IDX_SCALE


def _dsa_prep(ps, kv_g, idx_g, idx_b):
    b, s, wid = ps.shape
    ts = ATT_T
    nkt = s // ts
    return pl.pallas_call(
        _dsa_prep_kernel,
        grid=(b, nkt),
        in_specs=[pl.BlockSpec((1, ts, wid), lambda bi, i: (bi, i, 0)),
                  pl.BlockSpec((1, KV_LORA), lambda bi, i: (0, 0)),
                  pl.BlockSpec((1, IDX_DIM), lambda bi, i: (0, 0)),
                  pl.BlockSpec((1, IDX_DIM), lambda bi, i: (0, 0))],
        out_specs=[pl.BlockSpec((1, ts, KV_LORA), lambda bi, i: (bi, i, 0)),
                   pl.BlockSpec((1, 1, LANES, 2 * ts), lambda bi, i: (bi, i, 0, 0)),
                   pl.BlockSpec((1, ts, N_IDX_HEADS), lambda bi, i: (bi, i, 0))],
        out_shape=[jax.ShapeDtypeStruct((b, s, KV_LORA), BF16),
                   jax.ShapeDtypeStruct((b, nkt, LANES, 2 * ts), BF16),
                   jax.ShapeDtypeStruct((b, s, N_IDX_HEADS), F32)],
        compiler_params=_cparams(("parallel", "parallel")),
        name="dsa_prep",
    )(ps, kv_g.reshape(1, -1).astype(F32), idx_g.reshape(1, -1).astype(F32),
      idx_b.reshape(1, -1).astype(F32))


def _dsa_index_kernel(qi_ref, w_ref, kbd_ref, mask_ref, key_ref, plane_ref, wb_ref, t_ref, j_ref,
                      *, topk, nkt_total, seq_bits):
    tq, tk = IDX_TQ, ATT_T
    rep = tk // LANES
    sl = SUBLANES
    i = pl.program_id(1)
    t0 = i * tq
    nkt = (i + 1) * (tq // tk)

    row = t0 + lax.broadcasted_iota(I32, (tq, tk), 0)
    col = lax.broadcasted_iota(I32, (tq, tk), 1)
    row_t = lax.broadcasted_iota(I32, (tk, tq), 0)
    col_t = t0 + lax.broadcasted_iota(I32, (tk, tq), 1)

    def to_key(score):
        score = jnp.where(score == 0.0, 0.0, score)
        bits = lax.bitcast_convert_type(score, I32)
        return bits ^ ((bits >> 31) & np.int32(0x7FFFFFFF))

    for h in range(N_IDX_HEADS):
        wb_ref[h] = jnp.broadcast_to(w_ref[0, :, h:h + 1], (tq, LANES))

    def idx_body(kt, carry):
        kb = kbd_ref[0, kt]
        acc = jnp.zeros((tq, tk), F32)
        for hp in range(N_IDX_HEADS // 2):
            rel = jnp.dot(qi_ref[0, :, hp * LANES:(hp + 1) * LANES], kb,
                          preferred_element_type=F32)
            acc = acc + jnp.maximum(rel[:, :tk], 0.0) * _tile_lanes(wb_ref[2 * hp], rep)
            acc = acc + jnp.maximum(rel[:, tk:], 0.0) * _tile_lanes(wb_ref[2 * hp + 1], rep)
        key_ref[kt] = jnp.where(kt * tk + col <= row, to_key(acc), INT_MIN)
        ukey = jnp.where(kt * tk + row_t <= col_t, to_key(acc.T), INT_MIN) ^ INT_MIN
        for bi in range(KEY_BYTES):
            byte = (ukey >> (8 * (KEY_BYTES - 1 - bi))) & np.int32(0xFF)
            plane_ref[bi, kt] = byte.astype(F32).astype(BF16)
        return carry

    lax.fori_loop(0, nkt, idx_body, 0)

    one, zero = jnp.ones((), BF16), jnp.zeros((), BF16)

    def count(plane, pred_fn):
        rows = 2 * sl
        def body(kt, c):
            xs = [pred_fn(plane_ref[plane, kt, r * rows:(r + 1) * rows, :], kt * tk + r * rows)
                  for r in range(tk // rows)]
            while len(xs) > 1:
                xs = [xs[a] + xs[a + 1] for a in range(0, len(xs), 2)]
            return c + xs[0]
        c = lax.fori_loop(0, nkt, body, jnp.zeros((rows, tq), BF16))
        return jnp.broadcast_to(jnp.sum(c.astype(F32), axis=0, keepdims=True), (sl, tq))

    def rep16(v8):
        return jnp.concatenate([v8, v8], axis=0).astype(BF16)

    target = jnp.full((sl, tq), float(topk), F32)
    thr_u = jnp.zeros((sl, tq), I32)
    for bi in range(KEY_BYTES):
        t_ref[...] = jnp.zeros((sl, tq), F32)

        def bit_body(it, carry, bi=bi, target=target):
            cand = t_ref[...] + jnp.left_shift(jnp.int32(1), 7 - it).astype(F32)
            c16 = rep16(cand)
            n_ge = count(bi, lambda d, s0: jnp.where(d >= c16, one, zero))
            t_ref[...] = jnp.where(n_ge >= target, cand, t_ref[...])
            return carry

        lax.fori_loop(0, 8, bit_body, 0)
        tb = t_ref[...]
        tb16 = rep16(tb)
        thr_u = (thr_u << 8) | tb.astype(I32)
        if bi + 1 < KEY_BYTES:
            target = target - count(bi, lambda d, s0: jnp.where(d > tb16, one, zero))

            def narrow_body(kt, carry, bi=bi):
                plane_ref[bi + 1, kt] = jnp.where(plane_ref[bi, kt] == _tile_rows(tb16, tk // (2 * sl)),
                                                  plane_ref[bi + 1, kt], -one)
                return carry

            lax.fori_loop(0, nkt, narrow_body, 0)

    last = KEY_BYTES - 1
    n_gt = count(last, lambda d, s0: jnp.where(d > tb16, one, zero))
    n_ge = count(last, lambda d, s0: jnp.where(d >= tb16, one, zero))
    need = target - n_gt
    j_ref[...] = jnp.full((sl, tq), 2 ** 30, I32)
    sub16 = lax.broadcasted_iota(I32, (2 * sl, tq), 0)

    @pl.when(jnp.max(n_ge - target) > 0.0)
    def _():
        j_ref[...] = jnp.zeros((sl, tq), I32)

        def jbit_body(it, carry):
            cand = j_ref[...] | jnp.left_shift(jnp.int32(1), seq_bits - 1 - it)
            c2 = jnp.concatenate([cand, cand], axis=0)
            n_lt = count(last, lambda d, s0: jnp.where(
                d == tb16, jnp.where(s0 + sub16 < c2, 1.0, 0.0).astype(BF16), zero))
            j_ref[...] = jnp.where(n_lt < need, cand, j_ref[...])
            return carry

        lax.fori_loop(0, seq_bits, jbit_body, 0)

    thr = thr_u ^ INT_MIN

    def per_row(v8):
        wide = jnp.broadcast_to(lax.bitcast_convert_type(v8[0:1], F32), (LANES, tq))
        return _tile_lanes(lax.bitcast_convert_type(wide.T, I32), rep)

    thr_t = per_row(thr)
    jmax_t = per_row(j_ref[...])

    def mask_body(kt, carry):
        k = key_ref[kt]
        s_idx = kt * tk + col
        keep_tie = jnp.where(k == thr_t, jnp.where(s_idx <= jmax_t, 0.0, MASK_NEG), MASK_NEG)
        keep = jnp.where(k > thr_t, 0.0, keep_tie)
        full = jnp.where(s_idx <= row, keep, MASK_NEG).astype(BF16)
        for r in range(tq // tk):
            mask_ref[0, r, kt] = full[r * tk:(r + 1) * tk]
        return carry

    lax.fori_loop(0, nkt, mask_body, 0)

    def fill_body(kt, carry):
        for r in range(tq // tk):
            mask_ref[0, r, kt] = jnp.full((tk, tk), MASK_NEG, BF16)
        return carry

    lax.fori_loop(nkt, nkt_total, fill_body, 0)


def _dsa_index(pqi, col_qi, kbd, widx):
    b, s, _ = pqi.shape
    tq, tk = IDX_TQ, ATT_T
    nkt = s // tk
    wq = N_IDX_HEADS * IDX_DIM
    topk = min(TOPK_MAX, s // 4)
    assert s // (2 * SUBLANES) <= 256
    kern = functools.partial(_dsa_index_kernel, topk=topk, nkt_total=nkt,
                             seq_bits=max(1, (s - 1).bit_length()))
    return pl.pallas_call(
        kern,
        grid=(b, s // tq),
        in_specs=[pl.BlockSpec((1, tq, wq), lambda bi, i: (bi, i, col_qi)),
                  pl.BlockSpec((1, tq, N_IDX_HEADS), lambda bi, i: (bi, i, 0)),
                  pl.BlockSpec((1, nkt, LANES, 2 * tk), lambda bi, i: (bi, 0, 0, 0))],
        out_specs=pl.BlockSpec((1, tq // tk, nkt, tk, tk), lambda bi, i: (bi, i, 0, 0, 0)),
        out_shape=jax.ShapeDtypeStruct((b, s // tk, nkt, tk, tk), BF16),
        scratch_shapes=[pltpu.VMEM((nkt, tq, tk), I32),
                        pltpu.VMEM((KEY_BYTES, nkt, tk, tq), BF16),
                        pltpu.VMEM((N_IDX_HEADS, tq, LANES), F32),
                        pltpu.VMEM((SUBLANES, tq), F32),
                        pltpu.VMEM((SUBLANES, tq), I32)],
        compiler_params=_cparams(("parallel", "parallel")),
        name="dsa_index",
    )(pqi, widx, kbd)


def _flash_kernel(*refs, mode, nb, topb):
    if mode == "mask":
        q_ref, k_ref, v_ref, z_ref, d_ref, mask_ref, y_ref = refs[:7]
    else:
        q_ref, k_ref, v_ref, z_ref, d_ref, y_ref, km_ref = refs[:7]
    qs_ref, acc_ref, m_ref, p_ref, alpha_ref, x_ref = refs[7:]
    blk_rows = ATT_T
    t = ATT_TK
    hb = q_ref.shape[2] // HEAD_DIM
    rep = t // LANES
    i = pl.program_id(2)
    ones_blk = jnp.ones((t, LANES), BF16)

    if mode == "sel":
        @pl.when(i == 0)
        def _():
            km_ref[...] = jnp.zeros(km_ref.shape, F32)
            for h in range(hb):
                for n in range(nb):
                    kblk = k_ref[0, n * blk_rows:(n + 1) * blk_rows,
                                 h * HEAD_DIM:(h + 1) * HEAD_DIM].astype(F32)
                    km_ref[h, n:n + 1, :] = jnp.mean(kblk, axis=0, keepdims=True)
        nbp = -(-nb // SUBLANES) * SUBLANES
        blk = lax.broadcasted_iota(I32, (nbp, t), 0).astype(F32)
        own = (2 * i + lax.broadcasted_iota(I32, (nbp, t), 1) // blk_rows).astype(F32)
        lane = lax.broadcasted_iota(I32, (t, LANES), 1)
        second_half = lax.broadcasted_iota(I32, (t, LANES), 0) >= blk_rows

    for h in range(hb):
        q = q_ref[0, :, h * HEAD_DIM:(h + 1) * HEAD_DIM]
        qsc = (q.astype(F32) * (ATTN_SCALE * LOG2E)).astype(BF16)
        if mode == "sel":
            gate = lax.dot_general(km_ref[h, 0:nbp, :].astype(BF16), q, (((1,), (1,)), ((), ())),
                                   preferred_element_type=F32)
            gate = jnp.where(blk < own, gate, -jnp.inf)
            sel = jnp.full((nbp, t), MASK_NEG, F32)
            for _ in range(topb):
                gmax = jnp.max(gate, axis=0, keepdims=True)
                first = jnp.min(jnp.where(gate == gmax, blk, float(nbp)), axis=0, keepdims=True)
                hit = jnp.where(blk == first, jnp.where(gmax > -jnp.inf, 1.0, 0.0), 0.0)
                sel = jnp.where(hit > 0.0, 0.0, sel)
                gate = jnp.where(blk == first, -jnp.inf, gate)
            sel = jnp.where(blk == own, 0.0, sel)
            selb = jnp.concatenate([sel, jnp.full((LANES - nbp, t), MASK_NEG, F32)], axis=0).T
            qs_ref[h] = jnp.concatenate([qsc, selb.astype(BF16)], axis=1)
        else:
            qs_ref[h] = qsc
        m_ref[h] = jnp.full((t, LANES), MASK_NEG, F32)
        acc_ref[h] = jnp.zeros((t, 2 * HEAD_DIM), F32)
        p_ref[h] = jnp.zeros((t, t), BF16)
        alpha_ref[h] = jnp.ones((t, LANES), F32)

    def qk_stage(j):
        r0 = pl.multiple_of(j * t, t)
        if mode == "sel":
            onehot = jnp.where(lane == jnp.where(second_half, 2 * j + 1, 2 * j), 1.0, 0.0).astype(BF16)
        for h in range(hb):
            kblk = k_ref[0, pl.ds(r0, t), h * HEAD_DIM:(h + 1) * HEAD_DIM]
            if mode == "sel":
                kblk = jnp.concatenate([kblk, onehot], axis=1)
            x_ref[h] = lax.dot_general(qs_ref[h], kblk, (((1,), (1,)), ((), ())),
                                       preferred_element_type=F32)

    def add_bias(x, h, kind, j):
        d0, d1 = d_ref[0, h], d_ref[1, h]
        blk_of = lambda r, c: x[r * LANES:(r + 1) * LANES, c * LANES:(c + 1) * LANES]
        rows = []
        for r in range(rep):
            cols = []
            for c in range(rep):
                b = blk_of(r, c)
                if kind == "past":
                    if r == 0 and c == rep - 1:
                        b = b + d1 * jnp.where(j == i - 1, 1.0, 0.0)
                elif r == c:
                    b = b + d0
                elif r == c + 1:
                    b = b + d1
                cols.append(b)
            rows.append(jnp.concatenate(cols, axis=1))
        return jnp.concatenate(rows, axis=0)

    def softmax_stage(j, kind):
        if mode == "mask":
            mask_tile = jnp.concatenate(
                [jnp.concatenate([mask_ref[0, r, 2 * j], mask_ref[0, r, 2 * j + 1]], axis=1)
                 for r in range(2)], axis=0).astype(F32)
        elif kind == "diag":
            row = lax.broadcasted_iota(I32, (t, t), 0)
            col = lax.broadcasted_iota(I32, (t, t), 1)
        for h in range(hb):
            x = x_ref[h]
            if mode == "mask":
                x = x + mask_tile
            x = add_bias(x, h, kind, j)
            if kind == "diag" and mode == "sel":
                x = jnp.where(col <= row, x, MASK_NEG)
            m_prev = m_ref[h]
            m_new = jnp.maximum(m_prev, jnp.max(x, axis=1, keepdims=True))
            alpha_ref[h] = jnp.exp2(m_prev - m_new)
            p_ref[h] = jnp.exp2(x - _tile_lanes(m_new, rep)).astype(BF16)
            m_ref[h] = m_new

    def pv_stage(j):
        r0 = pl.multiple_of(j * t, t)
        for h in range(hb):
            vblk = v_ref[0, pl.ds(r0, t), h * HEAD_DIM:(h + 1) * HEAD_DIM]
            pv = jnp.dot(p_ref[h], jnp.concatenate([vblk, ones_blk], axis=1),
                         preferred_element_type=F32)
            acc_ref[h] = acc_ref[h] * _tile_lanes(alpha_ref[h], 2) + pv

    def step(j, kind):
        pv_stage(jnp.maximum(j - 1, 0))
        softmax_stage(j, kind)
        if kind != "diag":
            qk_stage(j + 1)

    qk_stage(0)

    def past_body(j, carry):
        step(j, "past")
        return carry

    lax.fori_loop(0, i, past_body, 0)
    step(i, "diag")
    pv_stage(i)

    for h in range(hb):
        acc = acc_ref[h]
        o = acc[:, :HEAD_DIM] / acc[:, HEAD_DIM:]
        zz = z_ref[0, :, h * HEAD_DIM:(h + 1) * HEAD_DIM].astype(F32)
        y_ref[0, :, h * HEAD_DIM:(h + 1) * HEAD_DIM] = (o * (zz * _sigmoid(zz))).astype(BF16)


def _flash_attention(q, k, v, z, dtab, n_heads, name, mask=None):
    (qa, cq), (ka, ck), (va, cv), (za, cz) = q, k, v, z
    b, s, _ = qa.shape
    t = ATT_TK
    assert s % t == 0
    nb = s // ATT_T
    hb = ATT_HB
    wg = hb * HEAD_DIM
    mode = "sel" if mask is None else "mask"
    kern = functools.partial(_flash_kernel, mode=mode, nb=nb, topb=min(MOBA_TOPB, nb - 1))
    in_specs = [pl.BlockSpec((1, t, wg), lambda bi, g, i: (bi, i, cq + g)),
                pl.BlockSpec((1, s, wg), lambda bi, g, i: (bi, 0, ck + g)),
                pl.BlockSpec((1, s, wg), lambda bi, g, i: (bi, 0, cv + g)),
                pl.BlockSpec((1, t, wg), lambda bi, g, i: (bi, i, cz + g)),
                pl.BlockSpec((2, hb, LANES, LANES), lambda bi, g, i: (0, g, 0, 0))]
    args = [qa, ka, va, za, dtab]
    scratch = []
    if mode == "mask":
        in_specs.append(pl.BlockSpec((1, t // ATT_T, nb, ATT_T, ATT_T), lambda bi, g, i: (bi, i, 0, 0, 0)))
        args.append(mask)
        kq = HEAD_DIM
    else:
        scratch.append(pltpu.VMEM((hb, LANES, HEAD_DIM), F32))
        kq = 2 * HEAD_DIM
    scratch += [pltpu.VMEM((hb, t, kq), BF16),
                pltpu.VMEM((hb, t, 2 * HEAD_DIM), F32),
                pltpu.VMEM((hb, t, LANES), F32),
                pltpu.VMEM((hb, t, t), BF16),
                pltpu.VMEM((hb, t, LANES), F32),
                pltpu.VMEM((hb, t, t), F32)]
    return pl.pallas_call(
        kern,
        grid=(b, n_heads // hb, s // t),
        in_specs=in_specs,
        out_specs=pl.BlockSpec((1, t, wg), lambda bi, g, i: (bi, i, g)),
        out_shape=jax.ShapeDtypeStruct((b, s, n_heads * HEAD_DIM), BF16),
        scratch_shapes=scratch,
        compiler_params=_cparams(("parallel", "parallel", "arbitrary")),
        name=name,
    )(*args)


def _softplus(x):
    return jnp.maximum(x, 0.0) + jnp.log1p(jnp.exp(-jnp.abs(x)))


def _rglru_kernel(xr_ref, gz_ref, cw_ref, cb_ref, wa_ref, ba_ref, wx_ref, bx_ref, lam_ref, y_ref,
                  xbuf_ref, a_ref, b_ref, h_ref):
    tc = xr_ref.shape[1]
    c = pl.program_id(2)
    halo = SUBLANES

    @pl.when(c == 0)
    def _():
        xbuf_ref[0:halo, :] = jnp.zeros((halo, xbuf_ref.shape[1]), F32)
        h_ref[...] = jnp.zeros(h_ref.shape, F32)

    @pl.when(c > 0)
    def _():
        xbuf_ref[0:halo, :] = xbuf_ref[tc:tc + halo, :]

    xbuf_ref[halo:halo + tc, :] = xr_ref[0]
    xfull = xbuf_ref[...]
    u = xfull * cw_ref[0:1, :]
    for j in range(1, CONV_W):
        u = xfull * cw_ref[j:j + 1, :] + pltpu.roll(u, 1, 0)
    xc = (u + cb_ref[...])[halo:, :]

    xcb = xc.astype(BF16)
    tr = jnp.tanh(jnp.dot(xcb, wa_ref[0], preferred_element_type=F32) + ba_ref[...])
    ti = jnp.tanh(jnp.dot(xcb, wx_ref[0], preferred_element_type=F32) + bx_ref[...])
    half_c = (-0.5 * RG_C * LOG2E) * _softplus(-lam_ref[...])
    a = jnp.exp2(tr * half_c + half_c)
    one_m_a2 = 1.0 - a * a
    mult = jnp.where(one_m_a2 > 0.0, one_m_a2 * lax.rsqrt(one_m_a2), 0.0)
    bt = (mult * xc) * (0.5 * ti + 0.5)

    nseg = SUBLANES
    seg = tc // nseg
    pitch = seg + RG_SEG_PAD
    nslab = a_ref.shape[0]
    for sl in range(nslab):
        for sg in range(nseg):
            a_ref[sl, sg * pitch:sg * pitch + seg, :] = a[sg * seg:(sg + 1) * seg, sl * LANES:(sl + 1) * LANES]
            b_ref[sl, sg * pitch:sg * pitch + seg, :] = bt[sg * seg:(sg + 1) * seg, sl * LANES:(sl + 1) * LANES]

    def seg_step(j, carry):
        out = []
        for sl in range(nslab):
            hl, pr = carry[2 * sl], carry[2 * sl + 1]
            rows = pl.ds(j, nseg, stride=pitch)
            a8 = a_ref[sl, rows, :]
            hl = a8 * hl + b_ref[sl, rows, :]
            pr = a8 * pr
            b_ref[sl, rows, :] = hl
            a_ref[sl, rows, :] = pr
            out += [hl, pr]
        return tuple(out)

    init = tuple(v for _ in range(nslab)
                 for v in (jnp.zeros((nseg, LANES), F32), jnp.ones((nseg, LANES), F32)))
    ends = lax.fori_loop(0, seg, seg_step, init, unroll=RG_SCAN_UNROLL)

    gate = gz_ref[0].astype(F32)
    for sl in range(nslab):
        h_end, p_end = ends[2 * sl], ends[2 * sl + 1]
        h_in = h_ref[:, sl * LANES:(sl + 1) * LANES]
        for sg in range(nseg):
            rows = slice(sg * pitch, sg * pitch + seg)
            hs = b_ref[sl, rows, :] + a_ref[sl, rows, :] * h_in
            y_ref[0, sg * seg:(sg + 1) * seg, sl * LANES:(sl + 1) * LANES] = (
                hs * gate[sg * seg:(sg + 1) * seg, sl * LANES:(sl + 1) * LANES]).astype(BF16)
            h_in = p_end[sg:sg + 1, :] * h_in + h_end[sg:sg + 1, :]
        h_ref[:, sl * LANES:(sl + 1) * LANES] = h_in


def _rglru(xr, gz, conv_w, conv_b, wa_g, b_a, wx_g, b_x, lam):
    b, s, d = xr.shape
    g = wa_g.shape[0]
    cg = d // g
    tc = min(RG_TC, s)
    assert ((tc // SUBLANES + RG_SEG_PAD) // SUBLANES) % 2 == 1 and cg % LANES == 0
    row = lambda v: v.reshape(1, d).astype(F32)
    vec_spec = pl.BlockSpec((1, cg), lambda bi, gi, c: (0, gi))
    return pl.pallas_call(
        _rglru_kernel,
        grid=(b, g, s // tc),
        in_specs=[pl.BlockSpec((1, tc, cg), lambda bi, gi, c: (bi, c, gi)),
                  pl.BlockSpec((1, tc, cg), lambda bi, gi, c: (bi, c, gi)),
                  pl.BlockSpec((CONV_W, cg), lambda bi, gi, c: (0, gi)),
                  vec_spec,
                  pl.BlockSpec((1, cg, cg), lambda bi, gi, c: (gi, 0, 0)),
                  vec_spec,
                  pl.BlockSpec((1, cg, cg), lambda bi, gi, c: (gi, 0, 0)),
                  vec_spec,
                  vec_spec],
        out_specs=pl.BlockSpec((1, tc, cg), lambda bi, gi, c: (bi, c, gi)),
        out_shape=jax.ShapeDtypeStruct((b, s, d), BF16),
        scratch_shapes=[pltpu.VMEM((tc + SUBLANES, cg), F32),
                        pltpu.VMEM((cg // LANES, tc + SUBLANES * RG_SEG_PAD, LANES), F32),
                        pltpu.VMEM((cg // LANES, tc + SUBLANES * RG_SEG_PAD, LANES), F32),
                        pltpu.VMEM((1, cg), F32)],
        compiler_params=_cparams(("parallel", "parallel", "arbitrary")),
        name="rglru",
    )(xr, gz, conv_w.astype(F32), row(conv_b), wa_g * 0.5, row(b_a) * 0.5, wx_g * 0.5, row(b_x) * 0.5,
      row(lam))


def _group_block_diag(w, pair):
    nb, k, _ = w.shape
    out = None
    for p in range(pair):
        lo, hi = p * k, (pair - 1 - p) * k
        blk = jnp.pad(w[p::pair], ((0, 0), (lo, hi), (lo, hi)))
        out = blk if out is None else out + blk
    return out


def _attention_layer(x2d, b, s, norm_g, w_in, kv_g, w_uk, w_uv, idx_g, idx_b, w_out, rel_bias):
    d_model = x2d.shape[1]
    n_heads = w_uk.shape[0]
    wq = n_heads * HEAD_DIM
    assert N_IDX_HEADS * IDX_DIM == wq
    sizes = (wq, KV_LORA, wq, IDX_DIM, N_IDX_HEADS, wq, wq, wq, 2 * wq)
    offs = [int(o) for o in np.concatenate([[0], np.cumsum(sizes)])]
    wt = w_in.T.astype(BF16)
    zrows = lambda n: jnp.zeros((n, d_model), BF16)
    wt_small = jnp.concatenate([wt[offs[1]:offs[2]], wt[offs[3]:offs[4]], zrows(LANES - IDX_DIM),
                                wt[offs[4]:offs[5]], zrows(LANES - N_IDX_HEADS)], axis=0)

    h = _rmsnorm(x2d, norm_g, BF16, "rmsnorm0")
    pqa = _matmul_nt(h, wt, offs[0], wq, BF16, "proj_attn_qa").reshape(b, s, wq)
    pqi = _matmul_nt(h, wt, offs[2], wq, BF16, "proj_attn_qi").reshape(b, s, wq)
    ptail = _matmul_nt(h, wt, offs[5], 5 * wq, BF16, "proj_attn_tail").reshape(b, s, 5 * wq)
    psmall = _matmul_nt(h, wt_small, 0, wt_small.shape[0], F32, "proj_attn_small").reshape(b, s, -1)

    ckv, kbd, widx = _dsa_prep(psmall, kv_g, idx_g, idx_b)
    mask = _dsa_index(pqi, 0, kbd, widx)
    w_kv = jnp.concatenate([jnp.transpose(w_uk, (1, 0, 2)).reshape(KV_LORA, wq),
                            jnp.transpose(w_uv, (1, 0, 2)).reshape(KV_LORA, wq)], axis=1).astype(BF16)
    kv = _matmul([ckv.reshape(b * s, KV_LORA)], [w_kv], BF16, "dsa_kv_up", bm=2 * MM_BM).reshape(b, s, 2 * wq)

    dtab_a = _bias_blocks(rel_bias[:, :n_heads])
    dtab_b = _bias_blocks(rel_bias[:, n_heads:])
    gpw = n_heads // ATT_HB
    ya = _flash_attention((pqa, 0), (kv, 0), (kv, gpw), (ptail, 3 * gpw), dtab_a, n_heads,
                          "dsa_attention", mask=mask)
    yb = _flash_attention((ptail, 0), (ptail, gpw), (ptail, 2 * gpw), (ptail, 4 * gpw), dtab_b,
                          n_heads, "moba_attention")

    w_out_b = w_out.astype(BF16)
    return _matmul([ya.reshape(b * s, wq), yb.reshape(b * s, wq)], [(w_out_b, 0, 0), (w_out_b, 1, 0)],
                   F32, "proj_attn_out", res=x2d, bn=MM_BN // 2)


def _recurrent_layer(x2d, b, s, norm_g, w_in, conv_w, conv_b, w_a, b_a, w_x, b_x, lam, w_out):
    d_rnn = conv_b.shape[0]
    h = _rmsnorm(x2d, norm_g, BF16, "rmsnorm1")
    w_in_b = w_in.astype(BF16)
    assert d_rnn % MM_BN == 0
    xr = _matmul([h], [(w_in_b, 0, 0)], F32, "proj_rec_x", n=d_rnn)
    gz = _matmul([h], [(w_in_b, 0, d_rnn // MM_BN)], BF16, "proj_rec_z", n=d_rnn, epilogue=_silu)
    y = _rglru(xr.reshape(b, s, d_rnn), gz.reshape(b, s, d_rnn), conv_w, conv_b,
               _group_block_diag(w_a.astype(BF16), RG_PAIR), b_a,
               _group_block_diag(w_x.astype(BF16), RG_PAIR), b_x, lam)
    return _matmul([y.reshape(b * s, d_rnn)], [w_out.astype(BF16)], F32, "proj_rec_out",
                   res=x2d, bn=MM_BN // 2)


def kernel(x, norm_g, final_g, rel_bias, attn_w_in, attn_kv_g, attn_w_uk, attn_w_uv, idx_k_g, idx_k_b,
           attn_w_out, rec_w_in, rec_conv_w, rec_conv_b, rec_w_a, rec_b_a, rec_w_x, rec_b_x,
           rec_lambda, rec_w_out):
    b, s, d = x.shape
    depth = norm_g.shape[0]
    x2d = x.reshape(b * s, d)
    for layer in range(depth):
        li = layer // 2
        if layer % 2 == 0:
            x2d = _attention_layer(x2d, b, s, norm_g[layer], attn_w_in[li], attn_kv_g[li],
                                   attn_w_uk[li], attn_w_uv[li], idx_k_g[li], idx_k_b[li],
                                   attn_w_out[li], rel_bias)
        else:
            x2d = _recurrent_layer(x2d, b, s, norm_g[layer], rec_w_in[li], rec_conv_w[li],
                                   rec_conv_b[li], rec_w_a[li], rec_b_a[li], rec_w_x[li],
                                   rec_b_x[li], rec_lambda[li], rec_w_out[li])
    return _rmsnorm(x2d, final_g, x.dtype, "rmsnorm_final").reshape(b, s, d)
```

```python
import functools
import math

import numpy as np
import jax
import jax.numpy as jnp
from jax import lax
from jax.experimental import pallas as pl
from jax.experimental.pallas import tpu as pltpu

F32 = jnp.float32
BF16 = jnp.bfloat16
I32 = jnp.int32

HEAD_DIM = 128
KV_LORA = 512
N_IDX_HEADS = 32
IDX_DIM = 64
TOPK_MAX = 256
MOBA_BLOCK = 256
MOBA_TOPB = 3
RG_BLOCKS = 16
CONV_W = 4
RG_C = 8.0
NUM_BUCKETS = 32
MAX_DISTANCE = 128
EPS = 1e-6
ATTN_SCALE = HEAD_DIM ** -0.5
IDX_SCALE = (N_IDX_HEADS ** -0.5) * (IDX_DIM ** -0.5)
LOG2E = math.log2(math.e)

LANES = 128
SUBLANES = 8
WT_ROW_ALIGN = 16
VMEM_LIMIT_BYTES = 56 * 2 ** 20
MASK_NEG = -1e30
INT_MIN = np.int32(-2 ** 31)
KEY_BYTES = 4

ROW_TILE = 512
MM_BM = 1024
MM_BN = 1024
ATT_T = 256
ATT_TK = 2 * ATT_T
ATT_HB = 4
IDX_TQ = 2 * ATT_T
RG_PAIR = 2
RG_TC = 1024
RG_SEG_PAD = 8
RG_SCAN_UNROLL = 4
assert ATT_T == MOBA_BLOCK and ATT_T == 2 * LANES


def _cparams(sem):
    return pltpu.CompilerParams(dimension_semantics=sem, vmem_limit_bytes=VMEM_LIMIT_BYTES)


def _sigmoid(x):
    return 0.5 * jnp.tanh(0.5 * x) + 0.5


def _tile_rows(x, n):
    return x if n == 1 else jnp.concatenate([x] * n, axis=0)


def _tile_lanes(x, n):
    return x if n == 1 else jnp.concatenate([x] * n, axis=1)


def _rmsnorm_kernel(x_ref, g_ref, o_ref):
    x = x_ref[...]
    ms = jnp.mean(x * x, axis=-1, keepdims=True)
    o_ref[...] = ((x * lax.rsqrt(ms + EPS)) * g_ref[...]).astype(o_ref.dtype)


def _rmsnorm(x2d, g, out_dtype, name):
    m, d = x2d.shape
    bm = min(ROW_TILE, m)
    return pl.pallas_call(
        _rmsnorm_kernel,
        grid=(m // bm,),
        in_specs=[pl.BlockSpec((bm, d), lambda i: (i, 0)),
                  pl.BlockSpec((1, d), lambda i: (0, 0))],
        out_specs=pl.BlockSpec((bm, d), lambda i: (i, 0)),
        out_shape=jax.ShapeDtypeStruct((m, d), out_dtype),
        compiler_params=_cparams(("parallel",)),
        name=name,
    )(x2d, g.reshape(1, d).astype(F32))


def _mm_kernel(*refs, n_parts, has_res, epilogue):
    o_ref = refs[-1]
    acc = jnp.dot(refs[0][...], refs[n_parts][...], preferred_element_type=F32)
    for p in range(1, n_parts):
        acc = acc + jnp.dot(refs[p][...], refs[n_parts + p][...], preferred_element_type=F32)
    if has_res:
        acc = refs[2 * n_parts][...] + acc
    if epilogue is not None:
        acc = epilogue(acc)
    o_ref[...] = acc.astype(o_ref.dtype)


def _matmul(xs, ws, out_dtype, name, res=None, n=None, bm=MM_BM, bn=MM_BN, epilogue=None):
    ws = [w if isinstance(w, tuple) else (w, 0, 0) for w in ws]
    m = xs[0].shape[0]
    n = ws[0][0].shape[1] if n is None else n
    bm = min(bm, m)
    bn = min(bn, n)
    assert m % bm == 0 and n % bn == 0
    in_specs = [pl.BlockSpec((bm, x.shape[1]), lambda i, j: (i, 0)) for x in xs]
    for x, (w, rb, cb) in zip(xs, ws):
        assert w.shape[0] % x.shape[1] == 0 and (cb + n // bn) * bn <= w.shape[1]
        in_specs.append(pl.BlockSpec((x.shape[1], bn), lambda i, j, rb=rb, cb=cb: (rb, cb + j)))
    args = [*xs, *[w for w, _, _ in ws]]
    if res is not None:
        in_specs.append(pl.BlockSpec((bm, bn), lambda i, j: (i, j)))
        args.append(res)
    return pl.pallas_call(
        functools.partial(_mm_kernel, n_parts=len(xs), has_res=res is not None, epilogue=epilogue),
        grid=(m // bm, n // bn),
        in_specs=in_specs,
        out_specs=pl.BlockSpec((bm, bn), lambda i, j: (i, j)),
        out_shape=jax.ShapeDtypeStruct((m, n), out_dtype),
        compiler_params=_cparams(("parallel", "parallel")),
        name=name,
    )(*args)


def _silu(x):
    return x * _sigmoid(x)


def _mm_nt_kernel(x_ref, wt_ref, o_ref, *, epilogue):
    acc = lax.dot_general(x_ref[...], wt_ref[...], (((1,), (1,)), ((), ())), preferred_element_type=F32)
    if epilogue is not None:
        acc = epilogue(acc)
    o_ref[...] = acc.astype(o_ref.dtype)


def _matmul_nt(x, wt, row0, n, out_dtype, name, bn=MM_BN, epilogue=None):
    m, k = x.shape
    bm = min(MM_BM, m)
    bn = min(bn, n)
    assert m % bm == 0 and n % bn == 0 and row0 % WT_ROW_ALIGN == 0 and bn % WT_ROW_ALIGN == 0
    assert row0 + n <= wt.shape[0] and wt.shape[1] == k
    return pl.pallas_call(
        functools.partial(_mm_nt_kernel, epilogue=epilogue),
        grid=(m // bm, n // bn),
        in_specs=[pl.BlockSpec((bm, k), lambda i, j: (i, 0)),
                  pl.BlockSpec((pl.Element(bn), pl.Element(k)),
                               lambda i, j: (pl.multiple_of(row0 + j * bn, WT_ROW_ALIGN), 0))],
        out_specs=pl.BlockSpec((bm, bn), lambda i, j: (i, j)),
        out_shape=jax.ShapeDtypeStruct((m, n), out_dtype),
        compiler_params=_cparams(("parallel", "parallel")),
        name=name,
    )(x, wt)


def _t5_bucket_np(dist):
    d = np.maximum(dist, 0)
    max_exact = NUM_BUCKETS // 2
    d_f = np.maximum(d, 1).astype(np.float32)
    ratio = np.log(d_f / np.float32(max_exact)) / np.float32(math.log(MAX_DISTANCE / max_exact))
    large = max_exact + (ratio * np.float32(NUM_BUCKETS - max_exact)).astype(np.int32)
    large = np.minimum(large, NUM_BUCKETS - 1)
    return np.where(d < max_exact, d, large)


def _bias_blocks(tab):
    n = LANES
    assert np.all(_t5_bucket_np(np.arange(n, 64 * n)) == NUM_BUCKETS - 1)
    h = tab.shape[1]
    rel = (tab - tab[NUM_BUCKETS - 1]).astype(F32) * LOG2E
    out = []
    for off in (0, n):
        buckets = _t5_bucket_np(off + n - 1 - np.arange(2 * n - 1))
        onehot = np.zeros((2 * n, NUM_BUCKETS), np.float32)
        onehot[np.arange(2 * n - 1), buckets] = 1.0
        w = jnp.dot(jnp.asarray(onehot), rel, precision=lax.Precision.HIGHEST).T
        t = jnp.tile(w, (1, n))[:, :n * (2 * n - 1)].reshape(h, n, 2 * n - 1)
        out.append(t[:, :, n - 1:])
    return jnp.stack(out)


def _dsa_prep_kernel(p_ref, kvg_ref, ig_ref, ib_ref, ckv_ref, kbd_ref, w_ref):
    p = p_ref[0]
    ts = p.shape[0]
    c = p[:, :KV_LORA]
    cn = (c * lax.rsqrt(jnp.mean(c * c, axis=-1, keepdims=True) + EPS)) * kvg_ref[...]
    ckv_ref[0] = cn.astype(BF16)
    k = p[:, KV_LORA:KV_LORA + IDX_DIM]
    mu = jnp.mean(k, axis=-1, keepdims=True)
    var = jnp.mean(jnp.square(k - mu), axis=-1, keepdims=True)
    kn = ((k - mu) * lax.rsqrt(var + EPS)) * ig_ref[...] + ib_ref[...]
    zeros = jnp.zeros((ts, LANES - IDX_DIM), F32)
    top = jnp.concatenate([kn, zeros], axis=1).T
    bot = jnp.concatenate([zeros, kn], axis=1).T
    kbd_ref[0, 0] = jnp.concatenate([top, bot], axis=1).astype(BF16)
    w_ref[0] = p[:, KV_LORA + LANES:KV_LORA + LANES + N_IDX_HEADS] * IDX_SCALE


def _dsa_prep(ps, kv_g, idx_g, idx_b):
    b, s, wid = ps.shape
    ts = ATT_T
    nkt = s // ts
    return pl.pallas_call(
        _dsa_prep_kernel,
        grid=(b, nkt),
        in_specs=[pl.BlockSpec((1, ts, wid), lambda bi, i: (bi, i, 0)),
                  pl.BlockSpec((1, KV_LORA), lambda bi, i: (0, 0)),
                  pl.BlockSpec((1, IDX_DIM), lambda bi, i: (0, 0)),
                  pl.BlockSpec((1, IDX_DIM), lambda bi, i: (0, 0))],
        out_specs=[pl.BlockSpec((1, ts, KV_LORA), lambda bi, i: (bi, i, 0)),
                   pl.BlockSpec((1, 1, LANES, 2 * ts), lambda bi, i: (bi, i, 0, 0)),
                   pl.BlockSpec((1, ts, N_IDX_HEADS), lambda bi, i: (bi, i, 0))],
        out_shape=[jax.ShapeDtypeStruct((b, s, KV_LORA), BF16),
                   jax.ShapeDtypeStruct((b, nkt, LANES, 2 * ts), BF16),
                   jax.ShapeDtypeStruct((b, s, N_IDX_HEADS), F32)],
        compiler_params=_cparams(("parallel", "parallel")),
        name="dsa_prep",
    )(ps, kv_g.reshape(1, -1).astype(F32), idx_g.reshape(1, -1).astype(F32),
      idx_b.reshape(1, -1).astype(F32))


def _dsa_index_kernel(qi_ref, w_ref, kbd_ref, mask_ref, key_ref, plane_ref, wb_ref, t_ref, j_ref,
                      *, topk, nkt_total, seq_bits):
    tq, tk = IDX_TQ, ATT_T
    rep = tk // LANES
    sl = SUBLANES
    i = pl.program_id(1)
    t0 = i * tq
    nkt = (i + 1) * (tq // tk)

    row = t0 + lax.broadcasted_iota(I32, (tq, tk), 0)
    col = lax.broadcasted_iota(I32, (tq, tk), 1)
    row_t = lax.broadcasted_iota(I32, (tk, tq), 0)
    col_t = t0 + lax.broadcasted_iota(I32, (tk, tq), 1)

    def to_key(score):
        score = jnp.where(score == 0.0, 0.0, score)
        bits = lax.bitcast_convert_type(score, I32)
        return bits ^ ((bits >> 31) & np.int32(0x7FFFFFFF))

    for h in range(N_IDX_HEADS):
        wb_ref[h] = jnp.broadcast_to(w_ref[0, :, h:h + 1], (tq, LANES))

    def idx_body(kt, carry):
        kb = kbd_ref[0, kt]
        acc = jnp.zeros((tq, tk), F32)
        for hp in range(N_IDX_HEADS // 2):
            rel = jnp.dot(qi_ref[0, :, hp * LANES:(hp + 1) * LANES], kb,
                          preferred_element_type=F32)
            acc = acc + jnp.maximum(rel[:, :tk], 0.0) * _tile_lanes(wb_ref[2 * hp], rep)
            acc = acc + jnp.maximum(rel[:, tk:], 0.0) * _tile_lanes(wb_ref[2 * hp + 1], rep)
        key_ref[kt] = jnp.where(kt * tk + col <= row, to_key(acc), INT_MIN)
        ukey = jnp.where(kt * tk + row_t <= col_t, to_key(acc.T), INT_MIN) ^ INT_MIN
        for bi in range(KEY_BYTES):
            byte = (ukey >> (8 * (KEY_BYTES - 1 - bi))) & np.int32(0xFF)
            plane_ref[bi, kt] = byte.astype(F32).astype(BF16)
        return carry

    lax.fori_loop(0, nkt, idx_body, 0)

    one, zero = jnp.ones((), BF16), jnp.zeros((), BF16)

    def count(plane, pred_fn):
        rows = 2 * sl
        def body(kt, c):
            xs = [pred_fn(plane_ref[plane, kt, r * rows:(r + 1) * rows, :], kt * tk + r * rows)
                  for r in range(tk // rows)]
            while len(xs) > 1:
                xs = [xs[a] + xs[a + 1] for a in range(0, len(xs), 2)]
            return c + xs[0]
        c = lax.fori_loop(0, nkt, body, jnp.zeros((rows, tq), BF16))
        return jnp.broadcast_to(jnp.sum(c.astype(F32), axis=0, keepdims=True), (sl, tq))

    def rep16(v8):
        return jnp.concatenate([v8, v8], axis=0).astype(BF16)

    target = jnp.full((sl, tq), float(topk), F32)
    thr_u = jnp.zeros((sl, tq), I32)
    for bi in range(KEY_BYTES):
        t_ref[...] = jnp.zeros((sl, tq), F32)

        def bit_body(it, carry, bi=bi, target=target):
            cand = t_ref[...] + jnp.left_shift(jnp.int32(1), 7 - it).astype(F32)
            c16 = rep16(cand)
            n_ge = count(bi, lambda d, s0: jnp.where(d >= c16, one, zero))
            t_ref[...] = jnp.where(n_ge >= target, cand, t_ref[...])
            return carry

        lax.fori_loop(0, 8, bit_body, 0)
        tb = t_ref[...]
        tb16 = rep16(tb)
        thr_u = (thr_u << 8) | tb.astype(I32)
        if bi + 1 < KEY_BYTES:
            target = target - count(bi, lambda d, s0: jnp.where(d > tb16, one, zero))

            def narrow_body(kt, carry, bi=bi):
                plane_ref[bi + 1, kt] = jnp.where(plane_ref[bi, kt] == _tile_rows(tb16, tk // (2 * sl)),
                                                  plane_ref[bi + 1, kt], -one)
                return carry

            lax.fori_loop(0, nkt, narrow_body, 0)

    last = KEY_BYTES - 1
    n_gt = count(last, lambda d, s0: jnp.where(d > tb16, one, zero))
    n_ge = count(last, lambda d, s0: jnp.where(d >= tb16, one, zero))
    need = target - n_gt
    j_ref[...] = jnp.full((sl, tq), 2 ** 30, I32)
    sub16 = lax.broadcasted_iota(I32, (2 * sl, tq), 0)

    @pl.when(jnp.max(n_ge - target) > 0.0)
    def _():
        j_ref[...] = jnp.zeros((sl, tq), I32)

        def jbit_body(it, carry):
            cand = j_ref[...] | jnp.left_shift(jnp.int32(1), seq_bits - 1 - it)
            c2 = jnp.concatenate([cand, cand], axis=0)
            n_lt = count(last, lambda d, s0: jnp.where(
                d == tb16, jnp.where(s0 + sub16 < c2, 1.0, 0.0).astype(BF16), zero))
            j_ref[...] = jnp.where(n_lt < need, cand, j_ref[...])
            return carry

        lax.fori_loop(0, seq_bits, jbit_body, 0)

    thr = thr_u ^ INT_MIN

    def per_row(v8):
        wide = jnp.broadcast_to(lax.bitcast_convert_type(v8[0:1], F32), (LANES, tq))
        return _tile_lanes(lax.bitcast_convert_type(wide.T, I32), rep)

    thr_t = per_row(thr)
    jmax_t = per_row(j_ref[...])

    def mask_body(kt, carry):
        k = key_ref[kt]
        s_idx = kt * tk + col
        keep_tie = jnp.where(k == thr_t, jnp.where(s_idx <= jmax_t, 0.0, MASK_NEG), MASK_NEG)
        keep = jnp.where(k > thr_t, 0.0, keep_tie)
        full = jnp.where(s_idx <= row, keep, MASK_NEG).astype(BF16)
        for r in range(tq // tk):
            mask_ref[0, r, kt] = full[r * tk:(r + 1) * tk]
        return carry

    lax.fori_loop(0, nkt, mask_body, 0)

    def fill_body(kt, carry):
        for r in range(tq // tk):
            mask_ref[0, r, kt] = jnp.full((tk, tk), MASK_NEG, BF16)
        return carry

    lax.fori_loop(nkt, nkt_total, fill_body, 0)


def _dsa_index(pqi, col_qi, kbd, widx):
    b, s, _ = pqi.shape
    tq, tk = IDX_TQ, ATT_T
    nkt = s // tk
    wq = N_IDX_HEADS * IDX_DIM
    topk = min(TOPK_MAX, s // 4)
    assert s // (2 * SUBLANES) <= 256
    kern = functools.partial(_dsa_index_kernel, topk=topk, nkt_total=nkt,
                             seq_bits=max(1, (s - 1).bit_length()))
    return pl.pallas_call(
        kern,
        grid=(b, s // tq),
        in_specs=[pl.BlockSpec((1, tq, wq), lambda bi, i: (bi, i, col_qi)),
                  pl.BlockSpec((1, tq, N_IDX_HEADS), lambda bi, i: (bi, i, 0)),
                  pl.BlockSpec((1, nkt, LANES, 2 * tk), lambda bi, i: (bi, 0, 0, 0))],
        out_specs=pl.BlockSpec((1, tq // tk, nkt, tk, tk), lambda bi, i: (bi, i, 0, 0, 0)),
        out_shape=jax.ShapeDtypeStruct((b, s // tk, nkt, tk, tk), BF16),
        scratch_shapes=[pltpu.VMEM((nkt, tq, tk), I32),
                        pltpu.VMEM((KEY_BYTES, nkt, tk, tq), BF16),
                        pltpu.VMEM((N_IDX_HEADS, tq, LANES), F32),
                        pltpu.VMEM((SUBLANES, tq), F32),
                        pltpu.VMEM((SUBLANES, tq), I32)],
        compiler_params=_cparams(("parallel", "parallel")),
        name="dsa_index",
    )(pqi, widx, kbd)


def _flash_kernel(*refs, mode, nb, topb):
    if mode == "mask":
        q_ref, k_ref, v_ref, z_ref, d_ref, mask_ref, y_ref = refs[:7]
    else:
        q_ref, k_ref, v_ref, z_ref, d_ref, y_ref, km_ref = refs[:7]
    qs_ref, acc_ref, m_ref, p_ref, alpha_ref, x_ref = refs[7:]
    blk_rows = ATT_T
    t = ATT_TK
    hb = q_ref.shape[2] // HEAD_DIM
    rep = t // LANES
    i = pl.program_id(2)
    ones_blk = jnp.ones((t, LANES), BF16)

    if mode == "sel":
        @pl.when(i == 0)
        def _():
            km_ref[...] = jnp.zeros(km_ref.shape, F32)
            for h in range(hb):
                for n in range(nb):
                    kblk = k_ref[0, n * blk_rows:(n + 1) * blk_rows,
                                 h * HEAD_DIM:(h + 1) * HEAD_DIM].astype(F32)
                    km_ref[h, n:n + 1, :] = jnp.mean(kblk, axis=0, keepdims=True)
        nbp = -(-nb // SUBLANES) * SUBLANES
        blk = lax.broadcasted_iota(I32, (nbp, t), 0).astype(F32)
        own = (2 * i + lax.broadcasted_iota(I32, (nbp, t), 1) // blk_rows).astype(F32)
        lane = lax.broadcasted_iota(I32, (t, LANES), 1)
        second_half = lax.broadcasted_iota(I32, (t, LANES), 0) >= blk_rows

    for h in range(hb):
        q = q_ref[0, :, h * HEAD_DIM:(h + 1) * HEAD_DIM]
        qsc = (q.astype(F32) * (ATTN_SCALE * LOG2E)).astype(BF16)
        if mode == "sel":
            gate = lax.dot_general(km_ref[h, 0:nbp, :].astype(BF16), q, (((1,), (1,)), ((), ())),
                                   preferred_element_type=F32)
            gate = jnp.where(blk < own, gate, -jnp.inf)
            sel = jnp.full((nbp, t), MASK_NEG, F32)
            for _ in range(topb):
                gmax = jnp.max(gate, axis=0, keepdims=True)
                first = jnp.min(jnp.where(gate == gmax, blk, float(nbp)), axis=0, keepdims=True)
                hit = jnp.where(blk == first, jnp.where(gmax > -jnp.inf, 1.0, 0.0), 0.0)
                sel = jnp.where(hit > 0.0, 0.0, sel)
                gate = jnp.where(blk == first, -jnp.inf, gate)
            sel = jnp.where(blk == own, 0.0, sel)
            selb = jnp.concatenate([sel, jnp.full((LANES - nbp, t), MASK_NEG, F32)], axis=0).T
            qs_ref[h] = jnp.concatenate([qsc, selb.astype(BF16)], axis=1)
        else:
            qs_ref[h] = qsc
        m_ref[h] = jnp.full((t, LANES), MASK_NEG, F32)
        acc_ref[h] = jnp.zeros((t, 2 * HEAD_DIM), F32)
        p_ref[h] = jnp.zeros((t, t), BF16)
        alpha_ref[h] = jnp.ones((t, LANES), F32)

    def qk_stage(j):
        r0 = pl.multiple_of(j * t, t)
        if mode == "sel":
            onehot = jnp.where(lane == jnp.where(second_half, 2 * j + 1, 2 * j), 1.0, 0.0).astype(BF16)
        for h in range(hb):
            kblk = k_ref[0, pl.ds(r0, t), h * HEAD_DIM:(h + 1) * HEAD_DIM]
            if mode == "sel":
                kblk = jnp.concatenate([kblk, onehot], axis=1)
            x_ref[h] = lax.dot_general(qs_ref[h], kblk, (((1,), (1,)), ((), ())),
                                       preferred_element_type=F32)

    def add_bias(x, h, kind, j):
        d0, d1 = d_ref[0, h], d_ref[1, h]
        blk_of = lambda r, c: x[r * LANES:(r + 1) * LANES, c * LANES:(c + 1) * LANES]
        rows = []
        for r in range(rep):
            cols = []
            for c in range(rep):
                b = blk_of(r, c)
                if kind == "past":
                    if r == 0 and c == rep - 1:
                        b = b + d1 * jnp.where(j == i - 1, 1.0, 0.0)
                elif r == c:
                    b = b + d0
                elif r == c + 1:
                    b = b + d1
                cols.append(b)
            rows.append(jnp.concatenate(cols, axis=1))
        return jnp.concatenate(rows, axis=0)

    def softmax_stage(j, kind):
        if mode == "mask":
            mask_tile = jnp.concatenate(
                [jnp.concatenate([mask_ref[0, r, 2 * j], mask_ref[0, r, 2 * j + 1]], axis=1)
                 for r in range(2)], axis=0).astype(F32)
        elif kind == "diag":
            row = lax.broadcasted_iota(I32, (t, t), 0)
            col = lax.broadcasted_iota(I32, (t, t), 1)
        for h in range(hb):
            x = x_ref[h]
            if mode == "mask":
                x = x + mask_tile
            x = add_bias(x, h, kind, j)
            if kind == "diag" and mode == "sel":
                x = jnp.where(col <= row, x, MASK_NEG)
            m_prev = m_ref[h]
            m_new = jnp.maximum(m_prev, jnp.max(x, axis=1, keepdims=True))
            alpha_ref[h] = jnp.exp2(m_prev - m_new)
            p_ref[h] = jnp.exp2(x - _tile_lanes(m_new, rep)).astype(BF16)
            m_ref[h] = m_new

    def pv_stage(j):
        r0 = pl.multiple_of(j * t, t)
        for h in range(hb):
            vblk = v_ref[0, pl.ds(r0, t), h * HEAD_DIM:(h + 1) * HEAD_DIM]
            pv = jnp.dot(p_ref[h], jnp.concatenate([vblk, ones_blk], axis=1),
                         preferred_element_type=F32)
            acc_ref[h] = acc_ref[h] * _tile_lanes(alpha_ref[h], 2) + pv

    def step(j, kind):
        pv_stage(jnp.maximum(j - 1, 0))
        softmax_stage(j, kind)
        if kind != "diag":
            qk_stage(j + 1)

    qk_stage(0)

    def past_body(j, carry):
        step(j, "past")
        return carry

    lax.fori_loop(0, i, past_body, 0)
    step(i, "diag")
    pv_stage(i)

    for h in range(hb):
        acc = acc_ref[h]
        o = acc[:, :HEAD_DIM] / acc[:, HEAD_DIM:]
        zz = z_ref[0, :, h * HEAD_DIM:(h + 1) * HEAD_DIM].astype(F32)
        y_ref[0, :, h * HEAD_DIM:(h + 1) * HEAD_DIM] = (o * (zz * _sigmoid(zz))).astype(BF16)


def _flash_attention(q, k, v, z, dtab, n_heads, name, mask=None):
    (qa, cq), (ka, ck), (va, cv), (za, cz) = q, k, v, z
    b, s, _ = qa.shape
    t = ATT_TK
    assert s % t == 0
    nb = s // ATT_T
    hb = ATT_HB
    wg = hb * HEAD_DIM
    mode = "sel" if mask is None else "mask"
    kern = functools.partial(_flash_kernel, mode=mode, nb=nb, topb=min(MOBA_TOPB, nb - 1))
    in_specs = [pl.BlockSpec((1, t, wg), lambda bi, g, i: (bi, i, cq + g)),
                pl.BlockSpec((1, s, wg), lambda bi, g, i: (bi, 0, ck + g)),
                pl.BlockSpec((1, s, wg), lambda bi, g, i: (bi, 0, cv + g)),
                pl.BlockSpec((1, t, wg), lambda bi, g, i: (bi, i, cz + g)),
                pl.BlockSpec((2, hb, LANES, LANES), lambda bi, g, i: (0, g, 0, 0))]
    args = [qa, ka, va, za, dtab]
    scratch = []
    if mode == "mask":
        in_specs.append(pl.BlockSpec((1, t // ATT_T, nb, ATT_T, ATT_T), lambda bi, g, i: (bi, i, 0, 0, 0)))
        args.append(mask)
        kq = HEAD_DIM
    else:
        scratch.append(pltpu.VMEM((hb, LANES, HEAD_DIM), F32))
        kq = 2 * HEAD_DIM
    scratch += [pltpu.VMEM((hb, t, kq), BF16),
                pltpu.VMEM((hb, t, 2 * HEAD_DIM), F32),
                pltpu.VMEM((hb, t, LANES), F32),
                pltpu.VMEM((hb, t, t), BF16),
                pltpu.VMEM((hb, t, LANES), F32),
                pltpu.VMEM((hb, t, t), F32)]
    return pl.pallas_call(
        kern,
        grid=(b, n_heads // hb, s // t),
        in_specs=in_specs,
        out_specs=pl.BlockSpec((1, t, wg), lambda bi, g, i: (bi, i, g)),
        out_shape=jax.ShapeDtypeStruct((b, s, n_heads * HEAD_DIM), BF16),
        scratch_shapes=scratch,
        compiler_params=_cparams(("parallel", "parallel", "arbitrary")),
        name=name,
    )(*args)


def _softplus(x):
    return jnp.maximum(x, 0.0) + jnp.log1p(jnp.exp(-jnp.abs(x)))


def _rglru_kernel(xr_ref, gz_ref, cw_ref, cb_ref, wa_ref, ba_ref, wx_ref, bx_ref, lam_ref, y_ref,
                  xbuf_ref, a_ref, b_ref, h_ref):
    tc = xr_ref.shape[1]
    c = pl.program_id(2)
    halo = SUBLANES

    @pl.when(c == 0)
    def _():
        xbuf_ref[0:halo, :] = jnp.zeros((halo, xbuf_ref.shape[1]), F32)
        h_ref[...] = jnp.zeros(h_ref.shape, F32)

    @pl.when(c > 0)
    def _():
        xbuf_ref[0:halo, :] = xbuf_ref[tc:tc + halo, :]

    xbuf_ref[halo:halo + tc, :] = xr_ref[0]
    xfull = xbuf_ref[...]
    u = xfull * cw_ref[0:1, :]
    for j in range(1, CONV_W):
        u = xfull * cw_ref[j:j + 1, :] + pltpu.roll(u, 1, 0)
    xc = (u + cb_ref[...])[halo:, :]

    xcb = xc.astype(BF16)
    tr = jnp.tanh(jnp.dot(xcb, wa_ref[0], preferred_element_type=F32) + ba_ref[...])
    ti = jnp.tanh(jnp.dot(xcb, wx_ref[0], preferred_element_type=F32) + bx_ref[...])
    half_c = (-0.5 * RG_C * LOG2E) * _softplus(-lam_ref[...])
    a = jnp.exp2(tr * half_c + half_c)
    one_m_a2 = 1.0 - a * a
    mult = jnp.where(one_m_a2 > 0.0, one_m_a2 * lax.rsqrt(one_m_a2), 0.0)
    bt = (mult * xc) * (0.5 * ti + 0.5)

    nseg = SUBLANES
    seg = tc // nseg
    pitch = seg + RG_SEG_PAD
    nslab = a_ref.shape[0]
    for sl in range(nslab):
        for sg in range(nseg):
            a_ref[sl, sg * pitch:sg * pitch + seg, :] = a[sg * seg:(sg + 1) * seg, sl * LANES:(sl + 1) * LANES]
            b_ref[sl, sg * pitch:sg * pitch + seg, :] = bt[sg * seg:(sg + 1) * seg, sl * LANES:(sl + 1) * LANES]

    def seg_step(j, carry):
        out = []
        for sl in range(nslab):
            hl, pr = carry[2 * sl], carry[2 * sl + 1]
            rows = pl.ds(j, nseg, stride=pitch)
            a8 = a_ref[sl, rows, :]
            hl = a8 * hl + b_ref[sl, rows, :]
            pr = a8 * pr
            b_ref[sl, rows, :] = hl
            a_ref[sl, rows, :] = pr
            out += [hl, pr]
        return tuple(out)

    init = tuple(v for _ in range(nslab)
                 for v in (jnp.zeros((nseg, LANES), F32), jnp.ones((nseg, LANES), F32)))
    ends = lax.fori_loop(0, seg, seg_step, init, unroll=RG_SCAN_UNROLL)

    gate = gz_ref[0].astype(F32)
    for sl in range(nslab):
        h_end, p_end = ends[2 * sl], ends[2 * sl + 1]
        h_in = h_ref[:, sl * LANES:(sl + 1) * LANES]
        for sg in range(nseg):
            rows = slice(sg * pitch, sg * pitch + seg)
            hs = b_ref[sl, rows, :] + a_ref[sl, rows, :] * h_in
            y_ref[0, sg * seg:(sg + 1) * seg, sl * LANES:(sl + 1) * LANES] = (
                hs * gate[sg * seg:(sg + 1) * seg, sl * LANES:(sl + 1) * LANES]).astype(BF16)
            h_in = p_end[sg:sg + 1, :] * h_in + h_end[sg:sg + 1, :]
        h_ref[:, sl * LANES:(sl + 1) * LANES] = h_in


def _rglru(xr, gz, conv_w, conv_b, wa_g, b_a, wx_g, b_x, lam):
    b, s, d = xr.shape
    g = wa_g.shape[0]
    cg = d // g
    tc = min(RG_TC, s)
    assert ((tc // SUBLANES + RG_SEG_PAD) // SUBLANES) % 2 == 1 and cg % LANES == 0
    row = lambda v: v.reshape(1, d).astype(F32)
    vec_spec = pl.BlockSpec((1, cg), lambda bi, gi, c: (0, gi))
    return pl.pallas_call(
        _rglru_kernel,
        grid=(b, g, s // tc),
        in_specs=[pl.BlockSpec((1, tc, cg), lambda bi, gi, c: (bi, c, gi)),
                  pl.BlockSpec((1, tc, cg), lambda bi, gi, c: (bi, c, gi)),
                  pl.BlockSpec((CONV_W, cg), lambda bi, gi, c: (0, gi)),
                  vec_spec,
                  pl.BlockSpec((1, cg, cg), lambda bi, gi, c: (gi, 0, 0)),
                  vec_spec,
                  pl.BlockSpec((1, cg, cg), lambda bi, gi, c: (gi, 0, 0)),
                  vec_spec,
                  vec_spec],
        out_specs=pl.BlockSpec((1, tc, cg), lambda bi, gi, c: (bi, c, gi)),
        out_shape=jax.ShapeDtypeStruct((b, s, d), BF16),
        scratch_shapes=[pltpu.VMEM((tc + SUBLANES, cg), F32),
                        pltpu.VMEM((cg // LANES, tc + SUBLANES * RG_SEG_PAD, LANES), F32),
                        pltpu.VMEM((cg // LANES, tc + SUBLANES * RG_SEG_PAD, LANES), F32),
                        pltpu.VMEM((1, cg), F32)],
        compiler_params=_cparams(("parallel", "parallel", "arbitrary")),
        name="rglru",
    )(xr, gz, conv_w.astype(F32), row(conv_b), wa_g * 0.5, row(b_a) * 0.5, wx_g * 0.5, row(b_x) * 0.5,
      row(lam))


def _group_block_diag(w, pair):
    nb, k, _ = w.shape
    out = None
    for p in range(pair):
        lo, hi = p * k, (pair - 1 - p) * k
        blk = jnp.pad(w[p::pair], ((0, 0), (lo, hi), (lo, hi)))
        out = blk if out is None else out + blk
    return out


def _attention_layer(x2d, b, s, norm_g, w_in, kv_g, w_uk, w_uv, idx_g, idx_b, w_out, rel_bias):
    d_model = x2d.shape[1]
    n_heads = w_uk.shape[0]
    wq = n_heads * HEAD_DIM
    assert N_IDX_HEADS * IDX_DIM == wq
    sizes = (wq, KV_LORA, wq, IDX_DIM, N_IDX_HEADS, wq, wq, wq, 2 * wq)
    offs = [int(o) for o in np.concatenate([[0], np.cumsum(sizes)])]
    wt = w_in.T.astype(BF16)
    zrows = lambda n: jnp.zeros((n, d_model), BF16)
    wt_small = jnp.concatenate([wt[offs[1]:offs[2]], wt[offs[3]:offs[4]], zrows(LANES - IDX_DIM),
                                wt[offs[4]:offs[5]], zrows(LANES - N_IDX_HEADS)], axis=0)

    h = _rmsnorm(x2d, norm_g, BF16, "rmsnorm0")
    pqa = _matmul_nt(h, wt, offs[0], wq, BF16, "proj_attn_qa").reshape(b, s, wq)
    pqi = _matmul_nt(h, wt, offs[2], wq, BF16, "proj_attn_qi").reshape(b, s, wq)
    ptail = _matmul_nt(h, wt, offs[5], 5 * wq, BF16, "proj_attn_tail").reshape(b, s, 5 * wq)
    psmall = _matmul_nt(h, wt_small, 0, wt_small.shape[0], F32, "proj_attn_small").reshape(b, s, -1)

    ckv, kbd, widx = _dsa_prep(psmall, kv_g, idx_g, idx_b)
    mask = _dsa_index(pqi, 0, kbd, widx)
    w_kv = jnp.concatenate([jnp.transpose(w_uk, (1, 0, 2)).reshape(KV_LORA, wq),
                            jnp.transpose(w_uv, (1, 0, 2)).reshape(KV_LORA, wq)], axis=1).astype(BF16)
    kv = _matmul([ckv.reshape(b * s, KV_LORA)], [w_kv], BF16, "dsa_kv_up", bm=2 * MM_BM).reshape(b, s, 2 * wq)

    dtab_a = _bias_blocks(rel_bias[:, :n_heads])
    dtab_b = _bias_blocks(rel_bias[:, n_heads:])
    gpw = n_heads // ATT_HB
    ya = _flash_attention((pqa, 0), (kv, 0), (kv, gpw), (ptail, 3 * gpw), dtab_a, n_heads,
                          "dsa_attention", mask=mask)
    yb = _flash_attention((ptail, 0), (ptail, gpw), (ptail, 2 * gpw), (ptail, 4 * gpw), dtab_b,
                          n_heads, "moba_attention")

    w_out_b = w_out.astype(BF16)
    return _matmul([ya.reshape(b * s, wq), yb.reshape(b * s, wq)], [(w_out_b, 0, 0), (w_out_b, 1, 0)],
                   F32, "proj_attn_out", res=x2d)


def _recurrent_layer(x2d, b, s, norm_g, w_in, conv_w, conv_b, w_a, b_a, w_x, b_x, lam, w_out):
    d_rnn = conv_b.shape[0]
    h = _rmsnorm(x2d, norm_g, BF16, "rmsnorm1")
    w_in_b = w_in.astype(BF16)
    assert d_rnn % MM_BN == 0
    xr = _matmul([h], [(w_in_b, 0, 0)], F32, "proj_rec_x", n=d_rnn)
    gz = _matmul([h], [(w_in_b, 0, d_rnn // MM_BN)], BF16, "proj_rec_z", n=d_rnn, epilogue=_silu)
    y = _rglru(xr.reshape(b, s, d_rnn), gz.reshape(b, s, d_rnn), conv_w, conv_b,
               _group_block_diag(w_a.astype(BF16), RG_PAIR), b_a,
               _group_block_diag(w_x.astype(BF16), RG_PAIR), b_x, lam)
    return _matmul([y.reshape(b * s, d_rnn)], [w_out.astype(BF16)], F32, "proj_rec_out",
                   res=x2d, bn=MM_BN // 2)


def kernel(x, norm_g, final_g, rel_bias, attn_w_in, attn_kv_g, attn_w_uk, attn_w_uv, idx_k_g, idx_k_b,
           attn_w_out, rec_w_in, rec_conv_w, rec_conv_b, rec_w_a, rec_b_a, rec_w_x, rec_b_x,
           rec_lambda, rec_w_out):
    b, s, d = x.shape
    depth = norm_g.shape[0]
    x2d = x.reshape(b * s, d)
    for layer in range(depth):
        li = layer // 2
        if layer % 2 == 0:
            x2d = _attention_layer(x2d, b, s, norm_g[layer], attn_w_in[li], attn_kv_g[li],
                                   attn_w_uk[li], attn_w_uv[li], idx_k_g[li], idx_k_b[li],
                                   attn_w_out[li], rel_bias)
        else:
            x2d = _recurrent_layer(x2d, b, s, norm_g[layer], rec_w_in[li], rec_conv_w[li],
                                   rec_conv_b[li], rec_w_a[li], rec_b_a[li], rec_w_x[li],
                                   rec_b_x[li], rec_lambda[li], rec_w_out[li])
    return _rmsnorm(x2d, final_g, x.dtype, "rmsnorm_final").reshape(b, s, d)
```

```python
import functools
import math

import numpy as np
import jax
import jax.numpy as jnp
from jax import lax
from jax.experimental import pallas as pl
from jax.experimental.pallas import tpu as pltpu

F32 = jnp.float32
BF16 = jnp.bfloat16
I32 = jnp.int32

HEAD_DIM = 128
KV_LORA = 512
N_IDX_HEADS = 32
IDX_DIM = 64
TOPK_MAX = 256
MOBA_BLOCK = 256
MOBA_TOPB = 3
RG_BLOCKS = 16
CONV_W = 4
RG_C = 8.0
NUM_BUCKETS = 32
MAX_DISTANCE = 128
EPS = 1e-6
ATTN_SCALE = HEAD_DIM ** -0.5
IDX_SCALE = (N_IDX_HEADS ** -0.5) * (IDX_DIM ** -0.5)
LOG2E = math.log2(math.e)

LANES = 128
SUBLANES = 8
WT_ROW_ALIGN = 16
VMEM_LIMIT_BYTES = 56 * 2 ** 20
MASK_NEG = -1e30
INT_MIN = np.int32(-2 ** 31)
KEY_BYTES = 4

ROW_TILE = 512
MM_BM = 1024
MM_BN = 1024
ATT_T = 256
ATT_TK = 2 * ATT_T
ATT_HB = 4
IDX_TQ = 2 * ATT_T
RG_PAIR = 2
RG_TC = 1024
RG_SEG_PAD = 8
RG_SCAN_UNROLL = 4
assert ATT_T == MOBA_BLOCK and ATT_T == 2 * LANES


def _cparams(sem):
    return pltpu.CompilerParams(dimension_semantics=sem, vmem_limit_bytes=VMEM_LIMIT_BYTES)


def _sigmoid(x):
    return 0.5 * jnp.tanh(0.5 * x) + 0.5


def _tile_rows(x, n):
    return x if n == 1 else jnp.concatenate([x] * n, axis=0)


def _tile_lanes(x, n):
    return x if n == 1 else jnp.concatenate([x] * n, axis=1)


def _rmsnorm_kernel(x_ref, g_ref, o_ref):
    x = x_ref[...]
    ms = jnp.mean(x * x, axis=-1, keepdims=True)
    o_ref[...] = ((x * lax.rsqrt(ms + EPS)) * g_ref[...]).astype(o_ref.dtype)


def _rmsnorm(x2d, g, out_dtype, name):
    m, d = x2d.shape
    bm = min(ROW_TILE, m)
    return pl.pallas_call(
        _rmsnorm_kernel,
        grid=(m // bm,),
        in_specs=[pl.BlockSpec((bm, d), lambda i: (i, 0)),
                  pl.BlockSpec((1, d), lambda i: (0, 0))],
        out_specs=pl.BlockSpec((bm, d), lambda i: (i, 0)),
        out_shape=jax.ShapeDtypeStruct((m, d), out_dtype),
        compiler_params=_cparams(("parallel",)),
        name=name,
    )(x2d, g.reshape(1, d).astype(F32))


def _mm_kernel(*refs, n_parts, has_res, epilogue):
    o_ref = refs[-1]
    acc = jnp.dot(refs[0][...], refs[n_parts][...], preferred_element_type=F32)
    for p in range(1, n_parts):
        acc = acc + jnp.dot(refs[p][...], refs[n_parts + p][...], preferred_element_type=F32)
    if has_res:
        acc = refs[2 * n_parts][...] + acc
    if epilogue is not None:
        acc = epilogue(acc)
    o_ref[...] = acc.astype(o_ref.dtype)


def _matmul(xs, ws, out_dtype, name, res=None, n=None, bm=MM_BM, bn=MM_BN, epilogue=None):
    ws = [w if isinstance(w, tuple) else (w, 0, 0) for w in ws]
    m = xs[0].shape[0]
    n = ws[0][0].shape[1] if n is None else n
    bm = min(bm, m)
    bn = min(bn, n)
    assert m % bm == 0 and n % bn == 0
    in_specs = [pl.BlockSpec((bm, x.shape[1]), lambda i, j: (i, 0)) for x in xs]
    for x, (w, rb, cb) in zip(xs, ws):
        assert w.shape[0] % x.shape[1] == 0 and (cb + n // bn) * bn <= w.shape[1]
        in_specs.append(pl.BlockSpec((x.shape[1], bn), lambda i, j, rb=rb, cb=cb: (rb, cb + j)))
    args = [*xs, *[w for w, _, _ in ws]]
    if res is not None:
        in_specs.append(pl.BlockSpec((bm, bn), lambda i, j: (i, j)))
        args.append(res)
    return pl.pallas_call(
        functools.partial(_mm_kernel, n_parts=len(xs), has_res=res is not None, epilogue=epilogue),
        grid=(m // bm, n // bn),
        in_specs=in_specs,
        out_specs=pl.BlockSpec((bm, bn), lambda i, j: (i, j)),
        out_shape=jax.ShapeDtypeStruct((m, n), out_dtype),
        compiler_params=_cparams(("parallel", "parallel")),
        name=name,
    )(*args)


def _silu(x):
    return x * _sigmoid(x)


def _mm_nt_kernel(x_ref, wt_ref, o_ref, *, epilogue):
    acc = lax.dot_general(x_ref[...], wt_ref[...], (((1,), (1,)), ((), ())), preferred_element_type=F32)
    if epilogue is not None:
        acc = epilogue(acc)
    o_ref[...] = acc.astype(o_ref.dtype)


def _matmul_nt(x, wt, row0, n, out_dtype, name, bn=MM_BN, epilogue=None):
    m, k = x.shape
    bm = min(MM_BM, m)
    bn = min(bn, n)
    assert m % bm == 0 and n % bn == 0 and row0 % WT_ROW_ALIGN == 0 and bn % WT_ROW_ALIGN == 0
    assert row0 + n <= wt.shape[0] and wt.shape[1] == k
    return pl.pallas_call(
        functools.partial(_mm_nt_kernel, epilogue=epilogue),
        grid=(m // bm, n // bn),
        in_specs=[pl.BlockSpec((bm, k), lambda i, j: (i, 0)),
                  pl.BlockSpec((pl.Element(bn), pl.Element(k)),
                               lambda i, j: (pl.multiple_of(row0 + j * bn, WT_ROW_ALIGN), 0))],
        out_specs=pl.BlockSpec((bm, bn), lambda i, j: (i, j)),
        out_shape=jax.ShapeDtypeStruct((m, n), out_dtype),
        compiler_params=_cparams(("parallel", "parallel")),
        name=name,
    )(x, wt)


def _t5_bucket_np(dist):
    d = np.maximum(dist, 0)
    max_exact = NUM_BUCKETS // 2
    d_f = np.maximum(d, 1).astype(np.float32)
    ratio = np.log(d_f / np.float32(max_exact)) / np.float32(math.log(MAX_DISTANCE / max_exact))
    large = max_exact + (ratio * np.float32(NUM_BUCKETS - max_exact)).astype(np.int32)
    large = np.minimum(large, NUM_BUCKETS - 1)
    return np.where(d < max_exact, d, large)


def _bias_blocks(tab):
    n = LANES
    assert np.all(_t5_bucket_np(np.arange(n, 64 * n)) == NUM_BUCKETS - 1)
    h = tab.shape[1]
    rel = (tab - tab[NUM_BUCKETS - 1]).astype(F32) * LOG2E
    out = []
    for off in (0, n):
        buckets = _t5_bucket_np(off + n - 1 - np.arange(2 * n - 1))
        onehot = np.zeros((2 * n, NUM_BUCKETS), np.float32)
        onehot[np.arange(2 * n - 1), buckets] = 1.0
        w = jnp.dot(jnp.asarray(onehot), rel, precision=lax.Precision.HIGHEST).T
        t = jnp.tile(w, (1, n))[:, :n * (2 * n - 1)].reshape(h, n, 2 * n - 1)
        out.append(t[:, :, n - 1:])
    return jnp.stack(out)


def _dsa_prep_kernel(p_ref, kvg_ref, ig_ref, ib_ref, ckv_ref, kbd_ref, w_ref):
    p = p_ref[0]
    ts = p.shape[0]
    c = p[:, :KV_LORA]
    cn = (c * lax.rsqrt(jnp.mean(c * c, axis=-1, keepdims=True) + EPS)) * kvg_ref[...]
    ckv_ref[0] = cn.astype(BF16)
    k = p[:, KV_LORA:KV_LORA + IDX_DIM]
    mu = jnp.mean(k, axis=-1, keepdims=True)
    var = jnp.mean(jnp.square(k - mu), axis=-1, keepdims=True)
    kn = ((k - mu) * lax.rsqrt(var + EPS)) * ig_ref[...] + ib_ref[...]
    zeros = jnp.zeros((ts, LANES - IDX_DIM), F32)
    top = jnp.concatenate([kn, zeros], axis=1).T
    bot = jnp.concatenate([zeros, kn], axis=1).T
    kbd_ref[0, 0] = jnp.concatenate([top, bot], axis=1).astype(BF16)
    w_ref[0] = p[:, KV_LORA + LANES:KV_LORA + LANES + N_IDX_HEADS] * IDX_SCALE


def _dsa_prep(ps, kv_g, idx_g, idx_b):
    b, s, wid = ps.shape
    ts = ATT_T
    nkt = s // ts
    return pl.pallas_call(
        _dsa_prep_kernel,
        grid=(b, nkt),
        in_specs=[pl.BlockSpec((1, ts, wid), lambda bi, i: (bi, i, 0)),
                  pl.BlockSpec((1, KV_LORA), lambda bi, i: (0, 0)),
                  pl.BlockSpec((1, IDX_DIM), lambda bi, i: (0, 0)),
                  pl.BlockSpec((1, IDX_DIM), lambda bi, i: (0, 0))],
        out_specs=[pl.BlockSpec((1, ts, KV_LORA), lambda bi, i: (bi, i, 0)),
                   pl.BlockSpec((1, 1, LANES, 2 * ts), lambda bi, i: (bi, i, 0, 0)),
                   pl.BlockSpec((1, ts, N_IDX_HEADS), lambda bi, i: (bi, i, 0))],
        out_shape=[jax.ShapeDtypeStruct((b, s, KV_LORA), BF16),
                   jax.ShapeDtypeStruct((b, nkt, LANES, 2 * ts), BF16),
                   jax.ShapeDtypeStruct((b, s, N_IDX_HEADS), F32)],
        compiler_params=_cparams(("parallel", "parallel")),
        name="dsa_prep",
    )(ps, kv_g.reshape(1, -1).astype(F32), idx_g.reshape(1, -1).astype(F32),
      idx_b.reshape(1, -1).astype(F32))


def _dsa_index_kernel(qi_ref, w_ref, kbd_ref, mask_ref, key_ref, plane_ref, wb_ref, t_ref, j_ref,
                      *, topk, nkt_total, seq_bits):
    tq, tk = IDX_TQ, ATT_T
    rep = tk // LANES
    sl = SUBLANES
    i = pl.program_id(1)
    t0 = i * tq
    nkt = (i + 1) * (tq // tk)

    row = t0 + lax.broadcasted_iota(I32, (tq, tk), 0)
    col = lax.broadcasted_iota(I32, (tq, tk), 1)
    row_t = lax.broadcasted_iota(I32, (tk, tq), 0)
    col_t = t0 + lax.broadcasted_iota(I32, (tk, tq), 1)

    def to_key(score):
        score = jnp.where(score == 0.0, 0.0, score)
        bits = lax.bitcast_convert_type(score, I32)
        return bits ^ ((bits >> 31) & np.int32(0x7FFFFFFF))

    for h in range(N_IDX_HEADS):
        wb_ref[h] = jnp.broadcast_to(w_ref[0, :, h:h + 1], (tq, LANES))

    def idx_body(kt, carry):
        kb = kbd_ref[0, kt]
        acc = jnp.zeros((tq, tk), F32)
        for hp in range(N_IDX_HEADS // 2):
            rel = jnp.dot(qi_ref[0, :, hp * LANES:(hp + 1) * LANES], kb,
                          preferred_element_type=F32)
            acc = acc + jnp.maximum(rel[:, :tk], 0.0) * _tile_lanes(wb_ref[2 * hp], rep)
            acc = acc + jnp.maximum(rel[:, tk:], 0.0) * _tile_lanes(wb_ref[2 * hp + 1], rep)
        key_ref[kt] = jnp.where(kt * tk + col <= row, to_key(acc), INT_MIN)
        ukey = jnp.where(kt * tk + row_t <= col_t, to_key(acc.T), INT_MIN) ^ INT_MIN
        for bi in range(KEY_BYTES):
            byte = (ukey >> (8 * (KEY_BYTES - 1 - bi))) & np.int32(0xFF)
            plane_ref[bi, kt] = byte.astype(F32).astype(BF16)
        return carry

    lax.fori_loop(0, nkt, idx_body, 0)

    one, zero = jnp.ones((), BF16), jnp.zeros((), BF16)

    def count(plane, pred_fn):
        rows = 2 * sl
        def body(kt, c):
            xs = [pred_fn(plane_ref[plane, kt, r * rows:(r + 1) * rows, :], kt * tk + r * rows)
                  for r in range(tk // rows)]
            while len(xs) > 1:
                xs = [xs[a] + xs[a + 1] for a in range(0, len(xs), 2)]
            return c + xs[0]
        c = lax.fori_loop(0, nkt, body, jnp.zeros((rows, tq), BF16))
        return jnp.broadcast_to(jnp.sum(c.astype(F32), axis=0, keepdims=True), (sl, tq))

    def rep16(v8):
        return jnp.concatenate([v8, v8], axis=0).astype(BF16)

    target = jnp.full((sl, tq), float(topk), F32)
    thr_u = jnp.zeros((sl, tq), I32)
    for bi in range(KEY_BYTES):
        t_ref[...] = jnp.zeros((sl, tq), F32)

        def bit_body(it, carry, bi=bi, target=target):
            cand = t_ref[...] + jnp.left_shift(jnp.int32(1), 7 - it).astype(F32)
            c16 = rep16(cand)
            n_ge = count(bi, lambda d, s0: jnp.where(d >= c16, one, zero))
            t_ref[...] = jnp.where(n_ge >= target, cand, t_ref[...])
            return carry

        lax.fori_loop(0, 8, bit_body, 0)
        tb = t_ref[...]
        tb16 = rep16(tb)
        thr_u = (thr_u << 8) | tb.astype(I32)
        if bi + 1 < KEY_BYTES:
            target = target - count(bi, lambda d, s0: jnp.where(d > tb16, one, zero))

            def narrow_body(kt, carry, bi=bi):
                plane_ref[bi + 1, kt] = jnp.where(plane_ref[bi, kt] == _tile_rows(tb16, tk // (2 * sl)),
                                                  plane_ref[bi + 1, kt], -one)
                return carry

            lax.fori_loop(0, nkt, narrow_body, 0)

    last = KEY_BYTES - 1
    n_gt = count(last, lambda d, s0: jnp.where(d > tb16, one, zero))
    n_ge = count(last, lambda d, s0: jnp.where(d >= tb16, one, zero))
    need = target - n_gt
    j_ref[...] = jnp.full((sl, tq), 2 ** 30, I32)
    sub16 = lax.broadcasted_iota(I32, (2 * sl, tq), 0)

    @pl.when(jnp.max(n_ge - target) > 0.0)
    def _():
        j_ref[...] = jnp.zeros((sl, tq), I32)

        def jbit_body(it, carry):
            cand = j_ref[...] | jnp.left_shift(jnp.int32(1), seq_bits - 1 - it)
            c2 = jnp.concatenate([cand, cand], axis=0)
            n_lt = count(last, lambda d, s0: jnp.where(
                d == tb16, jnp.where(s0 + sub16 < c2, 1.0, 0.0).astype(BF16), zero))
            j_ref[...] = jnp.where(n_lt < need, cand, j_ref[...])
            return carry

        lax.fori_loop(0, seq_bits, jbit_body, 0)

    thr = thr_u ^ INT_MIN

    def per_row(v8):
        wide = jnp.broadcast_to(lax.bitcast_convert_type(v8[0:1], F32), (LANES, tq))
        return _tile_lanes(lax.bitcast_convert_type(wide.T, I32), rep)

    thr_t = per_row(thr)
    jmax_t = per_row(j_ref[...])

    def mask_body(kt, carry):
        k = key_ref[kt]
        s_idx = kt * tk + col
        keep_tie = jnp.where(k == thr_t, jnp.where(s_idx <= jmax_t, 0.0, MASK_NEG), MASK_NEG)
        keep = jnp.where(k > thr_t, 0.0, keep_tie)
        full = jnp.where(s_idx <= row, keep, MASK_NEG).astype(BF16)
        for r in range(tq // tk):
            mask_ref[0, r, kt] = full[r * tk:(r + 1) * tk]
        return carry

    lax.fori_loop(0, nkt, mask_body, 0)

    def fill_body(kt, carry):
        for r in range(tq // tk):
            mask_ref[0, r, kt] = jnp.full((tk, tk), MASK_NEG, BF16)
        return carry

    lax.fori_loop(nkt, nkt_total, fill_body, 0)


def _dsa_index(pqi, col_qi, kbd, widx):
    b, s, _ = pqi.shape
    tq, tk = IDX_TQ, ATT_T
    nkt = s // tk
    wq = N_IDX_HEADS * IDX_DIM
    topk = min(TOPK_MAX, s // 4)
    assert s // (2 * SUBLANES) <= 256
    kern = functools.partial(_dsa_index_kernel, topk=topk, nkt_total=nkt,
                             seq_bits=max(1, (s - 1).bit_length()))
    return pl.pallas_call(
        kern,
        grid=(b, s // tq),
        in_specs=[pl.BlockSpec((1, tq, wq), lambda bi, i: (bi, i, col_qi)),
                  pl.BlockSpec((1, tq, N_IDX_HEADS), lambda bi, i: (bi, i, 0)),
                  pl.BlockSpec((1, nkt, LANES, 2 * tk), lambda bi, i: (bi, 0, 0, 0))],
        out_specs=pl.BlockSpec((1, tq // tk, nkt, tk, tk), lambda bi, i: (bi, i, 0, 0, 0)),
        out_shape=jax.ShapeDtypeStruct((b, s // tk, nkt, tk, tk), BF16),
        scratch_shapes=[pltpu.VMEM((nkt, tq, tk), I32),
                        pltpu.VMEM((KEY_BYTES, nkt, tk, tq), BF16),
                        pltpu.VMEM((N_IDX_HEADS, tq, LANES), F32),
                        pltpu.VMEM((SUBLANES, tq), F32),
                        pltpu.VMEM((SUBLANES, tq), I32)],
        compiler_params=_cparams(("parallel", "parallel")),
        name="dsa_index",
    )(pqi, widx, kbd)


def _flash_kernel(*refs, mode, nb, topb):
    if mode == "mask":
        q_ref, k_ref, v_ref, z_ref, d_ref, mask_ref, y_ref = refs[:7]
    else:
        q_ref, k_ref, v_ref, z_ref, d_ref, y_ref, km_ref = refs[:7]
    qs_ref, acc_ref, m_ref = refs[7:]
    blk_rows = ATT_T
    t = ATT_TK
    hb = q_ref.shape[2] // HEAD_DIM
    rep = t // LANES
    i = pl.program_id(2)
    ones_blk = jnp.ones((t, LANES), BF16)

    if mode == "sel":
        @pl.when(i == 0)
        def _():
            km_ref[...] = jnp.zeros(km_ref.shape, F32)
            for h in range(hb):
                for n in range(nb):
                    kblk = k_ref[0, n * blk_rows:(n + 1) * blk_rows,
                                 h * HEAD_DIM:(h + 1) * HEAD_DIM].astype(F32)
                    km_ref[h, n:n + 1, :] = jnp.mean(kblk, axis=0, keepdims=True)
        nbp = -(-nb // SUBLANES) * SUBLANES
        blk = lax.broadcasted_iota(I32, (nbp, t), 0).astype(F32)
        own = (2 * i + lax.broadcasted_iota(I32, (nbp, t), 1) // blk_rows).astype(F32)
        lane = lax.broadcasted_iota(I32, (t, LANES), 1)
        second_half = lax.broadcasted_iota(I32, (t, LANES), 0) >= blk_rows

    for h in range(hb):
        q = q_ref[0, :, h * HEAD_DIM:(h + 1) * HEAD_DIM]
        qsc = (q.astype(F32) * (ATTN_SCALE * LOG2E)).astype(BF16)
        if mode == "sel":
            gate = lax.dot_general(km_ref[h, 0:nbp, :].astype(BF16), q, (((1,), (1,)), ((), ())),
                                   preferred_element_type=F32)
            gate = jnp.where(blk < own, gate, -jnp.inf)
            sel = jnp.full((nbp, t), MASK_NEG, F32)
            for _ in range(topb):
                gmax = jnp.max(gate, axis=0, keepdims=True)
                first = jnp.min(jnp.where(gate == gmax, blk, float(nbp)), axis=0, keepdims=True)
                hit = jnp.where(blk == first, jnp.where(gmax > -jnp.inf, 1.0, 0.0), 0.0)
                sel = jnp.where(hit > 0.0, 0.0, sel)
                gate = jnp.where(blk == first, -jnp.inf, gate)
            sel = jnp.where(blk == own, 0.0, sel)
            selb = jnp.concatenate([sel, jnp.full((LANES - nbp, t), MASK_NEG, F32)], axis=0).T
            qs_ref[h] = jnp.concatenate([qsc, selb.astype(BF16)], axis=1)
        else:
            qs_ref[h] = qsc
        m_ref[h] = jnp.full((t, LANES), MASK_NEG, F32)
        acc_ref[h] = jnp.zeros((t, 2 * HEAD_DIM), F32)

    def add_bias(x, h, kind, j):
        d0, d1 = d_ref[0, h], d_ref[1, h]
        blk_of = lambda r, c: x[r * LANES:(r + 1) * LANES, c * LANES:(c + 1) * LANES]
        rows = []
        for r in range(rep):
            cols = []
            for c in range(rep):
                b = blk_of(r, c)
                if kind == "past":
                    if r == 0 and c == rep - 1:
                        b = b + d1 * jnp.where(j == i - 1, 1.0, 0.0)
                elif r == c:
                    b = b + d0
                elif r == c + 1:
                    b = b + d1
                cols.append(b)
            rows.append(jnp.concatenate(cols, axis=1))
        return jnp.concatenate(rows, axis=0)

    def attend(j, kind):
        r0 = pl.multiple_of(j * t, t)
        if mode == "mask":
            mask_tile = jnp.concatenate(
                [jnp.concatenate([mask_ref[0, r, 2 * j], mask_ref[0, r, 2 * j + 1]], axis=1)
                 for r in range(2)], axis=0).astype(F32)
        else:
            onehot = jnp.where(lane == jnp.where(second_half, 2 * j + 1, 2 * j), 1.0, 0.0).astype(BF16)
            if kind == "diag":
                row = lax.broadcasted_iota(I32, (t, t), 0)
                col = lax.broadcasted_iota(I32, (t, t), 1)
        for h in range(hb):
            kblk = k_ref[0, pl.ds(r0, t), h * HEAD_DIM:(h + 1) * HEAD_DIM]
            vblk = v_ref[0, pl.ds(r0, t), h * HEAD_DIM:(h + 1) * HEAD_DIM]
            if mode == "sel":
                kblk = jnp.concatenate([kblk, onehot], axis=1)
            x = lax.dot_general(qs_ref[h], kblk, (((1,), (1,)), ((), ())),
                                preferred_element_type=F32)
            if mode == "mask":
                x = x + mask_tile
            x = add_bias(x, h, kind, j)
            if kind == "diag" and mode == "sel":
                x = jnp.where(col <= row, x, MASK_NEG)
            m_prev = m_ref[h]
            m_new = jnp.maximum(m_prev, jnp.max(x, axis=1, keepdims=True))
            alpha = jnp.exp2(m_prev - m_new)
            p = jnp.exp2(x - _tile_lanes(m_new, rep)).astype(BF16)
            m_ref[h] = m_new
            pv = jnp.dot(p, jnp.concatenate([vblk, ones_blk], axis=1), preferred_element_type=F32)
            acc_ref[h] = acc_ref[h] * _tile_lanes(alpha, 2) + pv

    def past_body(j, carry):
        attend(j, "past")
        return carry

    lax.fori_loop(0, i, past_body, 0)
    attend(i, "diag")

    for h in range(hb):
        acc = acc_ref[h]
        o = acc[:, :HEAD_DIM] / acc[:, HEAD_DIM:]
        zz = z_ref[0, :, h * HEAD_DIM:(h + 1) * HEAD_DIM].astype(F32)
        y_ref[0, :, h * HEAD_DIM:(h + 1) * HEAD_DIM] = (o * (zz * _sigmoid(zz))).astype(BF16)


def _flash_attention(q, k, v, z, dtab, n_heads, name, mask=None):
    (qa, cq), (ka, ck), (va, cv), (za, cz) = q, k, v, z
    b, s, _ = qa.shape
    t = ATT_TK
    assert s % t == 0
    nb = s // ATT_T
    hb = ATT_HB
    wg = hb * HEAD_DIM
    mode = "sel" if mask is None else "mask"
    kern = functools.partial(_flash_kernel, mode=mode, nb=nb, topb=min(MOBA_TOPB, nb - 1))
    in_specs = [pl.BlockSpec((1, t, wg), lambda bi, g, i: (bi, i, cq + g)),
                pl.BlockSpec((1, s, wg), lambda bi, g, i: (bi, 0, ck + g)),
                pl.BlockSpec((1, s, wg), lambda bi, g, i: (bi, 0, cv + g)),
                pl.BlockSpec((1, t, wg), lambda bi, g, i: (bi, i, cz + g)),
                pl.BlockSpec((2, hb, LANES, LANES), lambda bi, g, i: (0, g, 0, 0))]
    args = [qa, ka, va, za, dtab]
    scratch = []
    if mode == "mask":
        in_specs.append(pl.BlockSpec((1, t // ATT_T, nb, ATT_T, ATT_T), lambda bi, g, i: (bi, i, 0, 0, 0)))
        args.append(mask)
        kq = HEAD_DIM
    else:
        scratch.append(pltpu.VMEM((hb, LANES, HEAD_DIM), F32))
        kq = 2 * HEAD_DIM
    scratch += [pltpu.VMEM((hb, t, kq), BF16),
                pltpu.VMEM((hb, t, 2 * HEAD_DIM), F32),
                pltpu.VMEM((hb, t, LANES), F32)]
    return pl.pallas_call(
        kern,
        grid=(b, n_heads // hb, s // t),
        in_specs=in_specs,
        out_specs=pl.BlockSpec((1, t, wg), lambda bi, g, i: (bi, i, g)),
        out_shape=jax.ShapeDtypeStruct((b, s, n_heads * HEAD_DIM), BF16),
        scratch_shapes=scratch,
        compiler_params=_cparams(("parallel", "parallel", "arbitrary")),
        name=name,
    )(*args)


def _softplus(x):
    return jnp.maximum(x, 0.0) + jnp.log1p(jnp.exp(-jnp.abs(x)))


def _rglru_kernel(xr_ref, gz_ref, cw_ref, cb_ref, wa_ref, ba_ref, wx_ref, bx_ref, lam_ref, y_ref,
                  xbuf_ref, a_ref, b_ref, h_ref):
    tc = xr_ref.shape[1]
    c = pl.program_id(2)
    halo = SUBLANES

    @pl.when(c == 0)
    def _():
        xbuf_ref[0:halo, :] = jnp.zeros((halo, xbuf_ref.shape[1]), F32)
        h_ref[...] = jnp.zeros(h_ref.shape, F32)

    @pl.when(c > 0)
    def _():
        xbuf_ref[0:halo, :] = xbuf_ref[tc:tc + halo, :]

    xbuf_ref[halo:halo + tc, :] = xr_ref[0]
    xfull = xbuf_ref[...]
    u = xfull * cw_ref[0:1, :]
    for j in range(1, CONV_W):
        u = xfull * cw_ref[j:j + 1, :] + pltpu.roll(u, 1, 0)
    xc = (u + cb_ref[...])[halo:, :]

    xcb = xc.astype(BF16)
    tr = jnp.tanh(jnp.dot(xcb, wa_ref[0], preferred_element_type=F32) + ba_ref[...])
    ti = jnp.tanh(jnp.dot(xcb, wx_ref[0], preferred_element_type=F32) + bx_ref[...])
    half_c = (-0.5 * RG_C * LOG2E) * _softplus(-lam_ref[...])
    a = jnp.exp2(tr * half_c + half_c)
    one_m_a2 = 1.0 - a * a
    mult = jnp.where(one_m_a2 > 0.0, one_m_a2 * lax.rsqrt(one_m_a2), 0.0)
    bt = (mult * xc) * (0.5 * ti + 0.5)

    nseg = SUBLANES
    seg = tc // nseg
    pitch = seg + RG_SEG_PAD
    nslab = a_ref.shape[0]
    for sl in range(nslab):
        for sg in range(nseg):
            a_ref[sl, sg * pitch:sg * pitch + seg, :] = a[sg * seg:(sg + 1) * seg, sl * LANES:(sl + 1) * LANES]
            b_ref[sl, sg * pitch:sg * pitch + seg, :] = bt[sg * seg:(sg + 1) * seg, sl * LANES:(sl + 1) * LANES]

    def seg_step(j, carry):
        out = []
        for sl in range(nslab):
            hl, pr = carry[2 * sl], carry[2 * sl + 1]
            rows = pl.ds(j, nseg, stride=pitch)
            a8 = a_ref[sl, rows, :]
            hl = a8 * hl + b_ref[sl, rows, :]
            pr = a8 * pr
            b_ref[sl, rows, :] = hl
            a_ref[sl, rows, :] = pr
            out += [hl, pr]
        return tuple(out)

    init = tuple(v for _ in range(nslab)
                 for v in (jnp.zeros((nseg, LANES), F32), jnp.ones((nseg, LANES), F32)))
    ends = lax.fori_loop(0, seg, seg_step, init, unroll=RG_SCAN_UNROLL)

    gate = gz_ref[0].astype(F32)
    for sl in range(nslab):
        h_end, p_end = ends[2 * sl], ends[2 * sl + 1]
        h_in = h_ref[:, sl * LANES:(sl + 1) * LANES]
        for sg in range(nseg):
            rows = slice(sg * pitch, sg * pitch + seg)
            hs = b_ref[sl, rows, :] + a_ref[sl, rows, :] * h_in
            y_ref[0, sg * seg:(sg + 1) * seg, sl * LANES:(sl + 1) * LANES] = (
                hs * gate[sg * seg:(sg + 1) * seg, sl * LANES:(sl + 1) * LANES]).astype(BF16)
            h_in = p_end[sg:sg + 1, :] * h_in + h_end[sg:sg + 1, :]
        h_ref[:, sl * LANES:(sl + 1) * LANES] = h_in


def _rglru(xr, gz, conv_w, conv_b, wa_g, b_a, wx_g, b_x, lam):
    b, s, d = xr.shape
    g = wa_g.shape[0]
    cg = d // g
    tc = min(RG_TC, s)
    assert ((tc // SUBLANES + RG_SEG_PAD) // SUBLANES) % 2 == 1 and cg % LANES == 0
    row = lambda v: v.reshape(1, d).astype(F32)
    vec_spec = pl.BlockSpec((1, cg), lambda bi, gi, c: (0, gi))
    return pl.pallas_call(
        _rglru_kernel,
        grid=(b, g, s // tc),
        in_specs=[pl.BlockSpec((1, tc, cg), lambda bi, gi, c: (bi, c, gi)),
                  pl.BlockSpec((1, tc, cg), lambda bi, gi, c: (bi, c, gi)),
                  pl.BlockSpec((CONV_W, cg), lambda bi, gi, c: (0, gi)),
                  vec_spec,
                  pl.BlockSpec((1, cg, cg), lambda bi, gi, c: (gi, 0, 0)),
                  vec_spec,
                  pl.BlockSpec((1, cg, cg), lambda bi, gi, c: (gi, 0, 0)),
                  vec_spec,
                  vec_spec],
        out_specs=pl.BlockSpec((1, tc, cg), lambda bi, gi, c: (bi, c, gi)),
        out_shape=jax.ShapeDtypeStruct((b, s, d), BF16),
        scratch_shapes=[pltpu.VMEM((tc + SUBLANES, cg), F32),
                        pltpu.VMEM((cg // LANES, tc + SUBLANES * RG_SEG_PAD, LANES), F32),
                        pltpu.VMEM((cg // LANES, tc + SUBLANES * RG_SEG_PAD, LANES), F32),
                        pltpu.VMEM((1, cg), F32)],
        compiler_params=_cparams(("parallel", "parallel", "arbitrary")),
        name="rglru",
    )(xr, gz, conv_w.astype(F32), row(conv_b), wa_g * 0.5, row(b_a) * 0.5, wx_g * 0.5, row(b_x) * 0.5,
      row(lam))


def _group_block_diag(w, pair):
    nb, k, _ = w.shape
    out = None
    for p in range(pair):
        lo, hi = p * k, (pair - 1 - p) * k
        blk = jnp.pad(w[p::pair], ((0, 0), (lo, hi), (lo, hi)))
        out = blk if out is None else out + blk
    return out


def _attention_layer(x2d, b, s, norm_g, w_in, kv_g, w_uk, w_uv, idx_g, idx_b, w_out, rel_bias):
    d_model = x2d.shape[1]
    n_heads = w_uk.shape[0]
    wq = n_heads * HEAD_DIM
    assert N_IDX_HEADS * IDX_DIM == wq
    sizes = (wq, KV_LORA, wq, IDX_DIM, N_IDX_HEADS, wq, wq, wq, 2 * wq)
    offs = [int(o) for o in np.concatenate([[0], np.cumsum(sizes)])]
    wt = w_in.T.astype(BF16)
    zrows = lambda n: jnp.zeros((n, d_model), BF16)
    wt_small = jnp.concatenate([wt[offs[1]:offs[2]], wt[offs[3]:offs[4]], zrows(LANES - IDX_DIM),
                                wt[offs[4]:offs[5]], zrows(LANES - N_IDX_HEADS)], axis=0)

    h = _rmsnorm(x2d, norm_g, BF16, "rmsnorm0")
    pqa = _matmul_nt(h, wt, offs[0], wq, BF16, "proj_attn_qa").reshape(b, s, wq)
    pqi = _matmul_nt(h, wt, offs[2], wq, BF16, "proj_attn_qi").reshape(b, s, wq)
    ptail = _matmul_nt(h, wt, offs[5], 5 * wq, BF16, "proj_attn_tail").reshape(b, s, 5 * wq)
    psmall = _matmul_nt(h, wt_small, 0, wt_small.shape[0], F32, "proj_attn_small").reshape(b, s, -1)

    ckv, kbd, widx = _dsa_prep(psmall, kv_g, idx_g, idx_b)
    mask = _dsa_index(pqi, 0, kbd, widx)
    w_kv = jnp.concatenate([jnp.transpose(w_uk, (1, 0, 2)).reshape(KV_LORA, wq),
                            jnp.transpose(w_uv, (1, 0, 2)).reshape(KV_LORA, wq)], axis=1).astype(BF16)
    kv = _matmul([ckv.reshape(b * s, KV_LORA)], [w_kv], BF16, "dsa_kv_up", bm=2 * MM_BM).reshape(b, s, 2 * wq)

    dtab_a = _bias_blocks(rel_bias[:, :n_heads])
    dtab_b = _bias_blocks(rel_bias[:, n_heads:])
    gpw = n_heads // ATT_HB
    ya = _flash_attention((pqa, 0), (kv, 0), (kv, gpw), (ptail, 3 * gpw), dtab_a, n_heads,
                          "dsa_attention", mask=mask)
    yb = _flash_attention((ptail, 0), (ptail, gpw), (ptail, 2 * gpw), (ptail, 4 * gpw), dtab_b,
                          n_heads, "moba_attention")

    w_out_b = w_out.astype(BF16)
    return _matmul([ya.reshape(b * s, wq), yb.reshape(b * s, wq)], [(w_out_b, 0, 0), (w_out_b, 1, 0)],
                   F32, "proj_attn_out", res=x2d)


def _recurrent_layer(x2d, b, s, norm_g, w_in, conv_w, conv_b, w_a, b_a, w_x, b_x, lam, w_out):
    d_rnn = conv_b.shape[0]
    h = _rmsnorm(x2d, norm_g, BF16, "rmsnorm1")
    w_in_b = w_in.astype(BF16)
    assert d_rnn % MM_BN == 0
    xr = _matmul([h], [(w_in_b, 0, 0)], F32, "proj_rec_x", n=d_rnn)
    gz = _matmul([h], [(w_in_b, 0, d_rnn // MM_BN)], BF16, "proj_rec_z", n=d_rnn, epilogue=_silu)
    y = _rglru(xr.reshape(b, s, d_rnn), gz.reshape(b, s, d_rnn), conv_w, conv_b,
               _group_block_diag(w_a.astype(BF16), RG_PAIR), b_a,
               _group_block_diag(w_x.astype(BF16), RG_PAIR), b_x, lam)
    return _matmul([y.reshape(b * s, d_rnn)], [w_out.astype(BF16)], F32, "proj_rec_out",
                   res=x2d, bn=MM_BN // 2)


def kernel(x, norm_g, final_g, rel_bias, attn_w_in, attn_kv_g, attn_w_uk, attn_w_uv, idx_k_g, idx_k_b,
           attn_w_out, rec_w_in, rec_conv_w, rec_conv_b, rec_w_a, rec_b_a, rec_w_x, rec_b_x,
           rec_lambda, rec_w_out):
    b, s, d = x.shape
    depth = norm_g.shape[0]
    x2d = x.reshape(b * s, d)
    for layer in range(depth):
        li = layer // 2
        if layer % 2 == 0:
            x2d = _attention_layer(x2d, b, s, norm_g[layer], attn_w_in[li], attn_kv_g[li],
                                   attn_w_uk[li], attn_w_uv[li], idx_k_g[li], idx_k_b[li],
                                   attn_w_out[li], rel_bias)
        else:
            x2d = _recurrent_layer(x2d, b, s, norm_g[layer], rec_w_in[li], rec_conv_w[li],
                                   rec_conv_b[li], rec_w_a[li], rec_b_a[li], rec_w_x[li],
                                   rec_b_x[li], rec_lambda[li], rec_w_out[li])
    return _rmsnorm(x2d, final_g, x.dtype, "rmsnorm_final").reshape(b, s, d)
```

```python
import functools
import math

import numpy as np
import jax
import jax.numpy as jnp
from jax import lax
from jax.experimental import pallas as pl
from jax.experimental.pallas import tpu as pltpu

F32 = jnp.float32
BF16 = jnp.bfloat16
I32 = jnp.int32

HEAD_DIM = 128
KV_LORA = 512
N_IDX_HEADS = 32
IDX_DIM = 64
TOPK_MAX = 256
MOBA_BLOCK = 256
MOBA_TOPB = 3
RG_BLOCKS = 16
CONV_W = 4
RG_C = 8.0
NUM_BUCKETS = 32
MAX_DISTANCE = 128
EPS = 1e-6
ATTN_SCALE = HEAD_DIM ** -0.5
IDX_SCALE = (N_IDX_HEADS ** -0.5) * (IDX_DIM ** -0.5)
LOG2E = math.log2(math.e)

LANES = 128
SUBLANES = 8
WT_ROW_ALIGN = 16
VMEM_LIMIT_BYTES = 56 * 2 ** 20
MASK_NEG = -1e30
INT_MIN = np.int32(-2 ** 31)
KEY_BYTES = 4

ROW_TILE = 512
MM_BM = 1024
MM_BN = 1024
ATT_T = 256
ATT_TK = 2 * ATT_T
ATT_HB = 4
IDX_TQ = 2 * ATT_T
RG_PAIR = 2
RG_TC = 1024
RG_SEG_PAD = 8
RG_SCAN_UNROLL = 4
assert ATT_T == MOBA_BLOCK and ATT_T == 2 * LANES


def _cparams(sem):
    return pltpu.CompilerParams(dimension_semantics=sem, vmem_limit_bytes=VMEM_LIMIT_BYTES)


def _sigmoid(x):
    return 0.5 * jnp.tanh(0.5 * x) + 0.5


def _tile_rows(x, n):
    return x if n == 1 else jnp.concatenate([x] * n, axis=0)


def _tile_lanes(x, n):
    return x if n == 1 else jnp.concatenate([x] * n, axis=1)


def _rmsnorm_kernel(x_ref, g_ref, o_ref):
    x = x_ref[...]
    ms = jnp.mean(x * x, axis=-1, keepdims=True)
    o_ref[...] = ((x * lax.rsqrt(ms + EPS)) * g_ref[...]).astype(o_ref.dtype)


def _rmsnorm(x2d, g, out_dtype, name):
    m, d = x2d.shape
    bm = min(ROW_TILE, m)
    return pl.pallas_call(
        _rmsnorm_kernel,
        grid=(m // bm,),
        in_specs=[pl.BlockSpec((bm, d), lambda i: (i, 0)),
                  pl.BlockSpec((1, d), lambda i: (0, 0))],
        out_specs=pl.BlockSpec((bm, d), lambda i: (i, 0)),
        out_shape=jax.ShapeDtypeStruct((m, d), out_dtype),
        compiler_params=_cparams(("parallel",)),
        name=name,
    )(x2d, g.reshape(1, d).astype(F32))


def _mm_kernel(*refs, n_parts, has_res, epilogue):
    o_ref = refs[-1]
    acc = jnp.dot(refs[0][...], refs[n_parts][...], preferred_element_type=F32)
    for p in range(1, n_parts):
        acc = acc + jnp.dot(refs[p][...], refs[n_parts + p][...], preferred_element_type=F32)
    if has_res:
        acc = refs[2 * n_parts][...] + acc
    if epilogue is not None:
        acc = epilogue(acc)
    o_ref[...] = acc.astype(o_ref.dtype)


def _matmul(xs, ws, out_dtype, name, res=None, n=None, bm=MM_BM, bn=MM_BN, epilogue=None):
    ws = [w if isinstance(w, tuple) else (w, 0, 0) for w in ws]
    m = xs[0].shape[0]
    n = ws[0][0].shape[1] if n is None else n
    bm = min(bm, m)
    bn = min(bn, n)
    assert m % bm == 0 and n % bn == 0
    in_specs = [pl.BlockSpec((bm, x.shape[1]), lambda i, j: (i, 0)) for x in xs]
    for x, (w, rb, cb) in zip(xs, ws):
        assert w.shape[0] % x.shape[1] == 0 and (cb + n // bn) * bn <= w.shape[1]
        in_specs.append(pl.BlockSpec((x.shape[1], bn), lambda i, j, rb=rb, cb=cb: (rb, cb + j)))
    args = [*xs, *[w for w, _, _ in ws]]
    if res is not None:
        in_specs.append(pl.BlockSpec((bm, bn), lambda i, j: (i, j)))
        args.append(res)
    return pl.pallas_call(
        functools.partial(_mm_kernel, n_parts=len(xs), has_res=res is not None, epilogue=epilogue),
        grid=(m // bm, n // bn),
        in_specs=in_specs,
        out_specs=pl.BlockSpec((bm, bn), lambda i, j: (i, j)),
        out_shape=jax.ShapeDtypeStruct((m, n), out_dtype),
        compiler_params=_cparams(("parallel", "parallel")),
        name=name,
    )(*args)


def _silu(x):
    return x * _sigmoid(x)


def _mm_nt_kernel(x_ref, wt_ref, o_ref, *, epilogue):
    acc = lax.dot_general(x_ref[...], wt_ref[...], (((1,), (1,)), ((), ())), preferred_element_type=F32)
    if epilogue is not None:
        acc = epilogue(acc)
    o_ref[...] = acc.astype(o_ref.dtype)


def _matmul_nt(x, wt, row0, n, out_dtype, name, bn=MM_BN, epilogue=None):
    m, k = x.shape
    bm = min(MM_BM, m)
    bn = min(bn, n)
    assert m % bm == 0 and n % bn == 0 and row0 % WT_ROW_ALIGN == 0 and bn % WT_ROW_ALIGN == 0
    assert row0 + n <= wt.shape[0] and wt.shape[1] == k
    return pl.pallas_call(
        functools.partial(_mm_nt_kernel, epilogue=epilogue),
        grid=(m // bm, n // bn),
        in_specs=[pl.BlockSpec((bm, k), lambda i, j: (i, 0)),
                  pl.BlockSpec((pl.Element(bn), pl.Element(k)),
                               lambda i, j: (pl.multiple_of(row0 + j * bn, WT_ROW_ALIGN), 0))],
        out_specs=pl.BlockSpec((bm, bn), lambda i, j: (i, j)),
        out_shape=jax.ShapeDtypeStruct((m, n), out_dtype),
        compiler_params=_cparams(("parallel", "parallel")),
        name=name,
    )(x, wt)


def _t5_bucket_np(dist):
    d = np.maximum(dist, 0)
    max_exact = NUM_BUCKETS // 2
    d_f = np.maximum(d, 1).astype(np.float32)
    ratio = np.log(d_f / np.float32(max_exact)) / np.float32(math.log(MAX_DISTANCE / max_exact))
    large = max_exact + (ratio * np.float32(NUM_BUCKETS - max_exact)).astype(np.int32)
    large = np.minimum(large, NUM_BUCKETS - 1)
    return np.where(d < max_exact, d, large)


def _bias_blocks(tab):
    n = LANES
    assert np.all(_t5_bucket_np(np.arange(n, 64 * n)) == NUM_BUCKETS - 1)
    h = tab.shape[1]
    rel = (tab - tab[NUM_BUCKETS - 1]).astype(F32) * LOG2E
    out = []
    for off in (0, n):
        buckets = _t5_bucket_np(off + n - 1 - np.arange(2 * n - 1))
        onehot = np.zeros((2 * n, NUM_BUCKETS), np.float32)
        onehot[np.arange(2 * n - 1), buckets] = 1.0
        w = jnp.dot(jnp.asarray(onehot), rel, precision=lax.Precision.HIGHEST).T
        t = jnp.tile(w, (1, n))[:, :n * (2 * n - 1)].reshape(h, n, 2 * n - 1)
        out.append(t[:, :, n - 1:])
    return jnp.stack(out)


def _dsa_prep_kernel(p_ref, kvg_ref, ig_ref, ib_ref, ckv_ref, kbd_ref, w_ref):
    p = p_ref[0]
    ts = p.shape[0]
    c = p[:, :KV_LORA]
    cn = (c * lax.rsqrt(jnp.mean(c * c, axis=-1, keepdims=True) + EPS)) * kvg_ref[...]
    ckv_ref[0] = cn.astype(BF16)
    k = p[:, KV_LORA:KV_LORA + IDX_DIM]
    mu = jnp.mean(k, axis=-1, keepdims=True)
    var = jnp.mean(jnp.square(k - mu), axis=-1, keepdims=True)
    kn = ((k - mu) * lax.rsqrt(var + EPS)) * ig_ref[...] + ib_ref[...]
    zeros = jnp.zeros((ts, LANES - IDX_DIM), F32)
    top = jnp.concatenate([kn, zeros], axis=1).T
    bot = jnp.concatenate([zeros, kn], axis=1).T
    kbd_ref[0, 0] = jnp.concatenate([top, bot], axis=1).astype(BF16)
    w_ref[0] = p[:, KV_LORA + LANES:KV_LORA + LANES + N_IDX_HEADS] * IDX_SCALE


def _dsa_prep(ps, kv_g, idx_g, idx_b):
    b, s, wid = ps.shape
    ts = ATT_T
    nkt = s // ts
    return pl.pallas_call(
        _dsa_prep_kernel,
        grid=(b, nkt),
        in_specs=[pl.BlockSpec((1, ts, wid), lambda bi, i: (bi, i, 0)),
                  pl.BlockSpec((1, KV_LORA), lambda bi, i: (0, 0)),
                  pl.BlockSpec((1, IDX_DIM), lambda bi, i: (0, 0)),
                  pl.BlockSpec((1, IDX_DIM), lambda bi, i: (0, 0))],
        out_specs=[pl.BlockSpec((1, ts, KV_LORA), lambda bi, i: (bi, i, 0)),
                   pl.BlockSpec((1, 1, LANES, 2 * ts), lambda bi, i: (bi, i, 0, 0)),
                   pl.BlockSpec((1, ts, N_IDX_HEADS), lambda bi, i: (bi, i, 0))],
        out_shape=[jax.ShapeDtypeStruct((b, s, KV_LORA), BF16),
                   jax.ShapeDtypeStruct((b, nkt, LANES, 2 * ts), BF16),
                   jax.ShapeDtypeStruct((b, s, N_IDX_HEADS), F32)],
        compiler_params=_cparams(("parallel", "parallel")),
        name="dsa_prep",
    )(ps, kv_g.reshape(1, -1).astype(F32), idx_g.reshape(1, -1).astype(F32),
      idx_b.reshape(1, -1).astype(F32))


def _dsa_index_kernel(qi_ref, w_ref, kbd_ref, mask_ref, key_ref, plane_ref, wb_ref, t_ref, j_ref,
                      *, topk, nkt_total, seq_bits):
    tq, tk = IDX_TQ, ATT_T
    rep = tk // LANES
    sl = SUBLANES
    i = pl.program_id(1)
    t0 = i * tq
    nkt = (i + 1) * (tq // tk)

    row = t0 + lax.broadcasted_iota(I32, (tq, tk), 0)
    col = lax.broadcasted_iota(I32, (tq, tk), 1)
    row_t = lax.broadcasted_iota(I32, (tk, tq), 0)
    col_t = t0 + lax.broadcasted_iota(I32, (tk, tq), 1)

    def to_key(score):
        score = jnp.where(score == 0.0, 0.0, score)
        bits = lax.bitcast_convert_type(score, I32)
        return bits ^ ((bits >> 31) & np.int32(0x7FFFFFFF))

    for h in range(N_IDX_HEADS):
        wb_ref[h] = jnp.broadcast_to(w_ref[0, :, h:h + 1], (tq, LANES))

    def idx_body(kt, carry):
        kb = kbd_ref[0, kt]
        acc = jnp.zeros((tq, tk), F32)
        for hp in range(N_IDX_HEADS // 2):
            rel = jnp.dot(qi_ref[0, :, hp * LANES:(hp + 1) * LANES], kb,
                          preferred_element_type=F32)
            acc = acc + jnp.maximum(rel[:, :tk], 0.0) * _tile_lanes(wb_ref[2 * hp], rep)
            acc = acc + jnp.maximum(rel[:, tk:], 0.0) * _tile_lanes(wb_ref[2 * hp + 1], rep)
        key_ref[kt] = jnp.where(kt * tk + col <= row, to_key(acc), INT_MIN)
        ukey = jnp.where(kt * tk + row_t <= col_t, to_key(acc.T), INT_MIN) ^ INT_MIN
        for bi in range(KEY_BYTES):
            byte = (ukey >> (8 * (KEY_BYTES - 1 - bi))) & np.int32(0xFF)
            plane_ref[bi, kt] = byte.astype(F32).astype(BF16)
        return carry

    lax.fori_loop(0, nkt, idx_body, 0)

    one, zero = jnp.ones((), BF16), jnp.zeros((), BF16)

    def count(plane, pred_fn):
        rows = 2 * sl
        def body(kt, c):
            xs = [pred_fn(plane_ref[plane, kt, r * rows:(r + 1) * rows, :], kt * tk + r * rows)
                  for r in range(tk // rows)]
            while len(xs) > 1:
                xs = [xs[a] + xs[a + 1] for a in range(0, len(xs), 2)]
            return c + xs[0]
        c = lax.fori_loop(0, nkt, body, jnp.zeros((rows, tq), BF16))
        return jnp.broadcast_to(jnp.sum(c.astype(F32), axis=0, keepdims=True), (sl, tq))

    def rep16(v8):
        return jnp.concatenate([v8, v8], axis=0).astype(BF16)

    target = jnp.full((sl, tq), float(topk), F32)
    thr_u = jnp.zeros((sl, tq), I32)
    for bi in range(KEY_BYTES):
        t_ref[...] = jnp.zeros((sl, tq), F32)

        def bit_body(it, carry, bi=bi, target=target):
            cand = t_ref[...] + jnp.left_shift(jnp.int32(1), 7 - it).astype(F32)
            c16 = rep16(cand)
            n_ge = count(bi, lambda d, s0: jnp.where(d >= c16, one, zero))
            t_ref[...] = jnp.where(n_ge >= target, cand, t_ref[...])
            return carry

        lax.fori_loop(0, 8, bit_body, 0)
        tb = t_ref[...]
        tb16 = rep16(tb)
        thr_u = (thr_u << 8) | tb.astype(I32)
        if bi + 1 < KEY_BYTES:
            target = target - count(bi, lambda d, s0: jnp.where(d > tb16, one, zero))

            def narrow_body(kt, carry, bi=bi):
                plane_ref[bi + 1, kt] = jnp.where(plane_ref[bi, kt] == _tile_rows(tb16, tk // (2 * sl)),
                                                  plane_ref[bi + 1, kt], -one)
                return carry

            lax.fori_loop(0, nkt, narrow_body, 0)

    last = KEY_BYTES - 1
    n_gt = count(last, lambda d, s0: jnp.where(d > tb16, one, zero))
    n_ge = count(last, lambda d, s0: jnp.where(d >= tb16, one, zero))
    need = target - n_gt
    j_ref[...] = jnp.full((sl, tq), 2 ** 30, I32)
    sub16 = lax.broadcasted_iota(I32, (2 * sl, tq), 0)

    @pl.when(jnp.max(n_ge - target) > 0.0)
    def _():
        j_ref[...] = jnp.zeros((sl, tq), I32)

        def jbit_body(it, carry):
            cand = j_ref[...] | jnp.left_shift(jnp.int32(1), seq_bits - 1 - it)
            c2 = jnp.concatenate([cand, cand], axis=0)
            n_lt = count(last, lambda d, s0: jnp.where(
                d == tb16, jnp.where(s0 + sub16 < c2, 1.0, 0.0).astype(BF16), zero))
            j_ref[...] = jnp.where(n_lt < need, cand, j_ref[...])
            return carry

        lax.fori_loop(0, seq_bits, jbit_body, 0)

    thr = thr_u ^ INT_MIN

    def per_row(v8):
        wide = jnp.broadcast_to(lax.bitcast_convert_type(v8[0:1], F32), (LANES, tq))
        return _tile_lanes(lax.bitcast_convert_type(wide.T, I32), rep)

    thr_t = per_row(thr)
    jmax_t = per_row(j_ref[...])

    def mask_body(kt, carry):
        k = key_ref[kt]
        s_idx = kt * tk + col
        keep_tie = jnp.where(k == thr_t, jnp.where(s_idx <= jmax_t, 0.0, MASK_NEG), MASK_NEG)
        keep = jnp.where(k > thr_t, 0.0, keep_tie)
        full = jnp.where(s_idx <= row, keep, MASK_NEG).astype(BF16)
        for r in range(tq // tk):
            mask_ref[0, r, kt] = full[r * tk:(r + 1) * tk]
        return carry

    lax.fori_loop(0, nkt, mask_body, 0)

    def fill_body(kt, carry):
        for r in range(tq // tk):
            mask_ref[0, r, kt] = jnp.full((tk, tk), MASK_NEG, BF16)
        return carry

    lax.fori_loop(nkt, nkt_total, fill_body, 0)


def _dsa_index(pqi, col_qi, kbd, widx):
    b, s, _ = pqi.shape
    tq, tk = IDX_TQ, ATT_T
    nkt = s // tk
    wq = N_IDX_HEADS * IDX_DIM
    topk = min(TOPK_MAX, s // 4)
    assert s // (2 * SUBLANES) <= 256
    kern = functools.partial(_dsa_index_kernel, topk=topk, nkt_total=nkt,
                             seq_bits=max(1, (s - 1).bit_length()))
    return pl.pallas_call(
        kern,
        grid=(b, s // tq),
        in_specs=[pl.BlockSpec((1, tq, wq), lambda bi, i: (bi, i, col_qi)),
                  pl.BlockSpec((1, tq, N_IDX_HEADS), lambda bi, i: (bi, i, 0)),
                  pl.BlockSpec((1, nkt, LANES, 2 * tk), lambda bi, i: (bi, 0, 0, 0))],
        out_specs=pl.BlockSpec((1, tq // tk, nkt, tk, tk), lambda bi, i: (bi, i, 0, 0, 0)),
        out_shape=jax.ShapeDtypeStruct((b, s // tk, nkt, tk, tk), BF16),
        scratch_shapes=[pltpu.VMEM((nkt, tq, tk), I32),
                        pltpu.VMEM((KEY_BYTES, nkt, tk, tq), BF16),
                        pltpu.VMEM((N_IDX_HEADS, tq, LANES), F32),
                        pltpu.VMEM((SUBLANES, tq), F32),
                        pltpu.VMEM((SUBLANES, tq), I32)],
        compiler_params=_cparams(("parallel", "parallel")),
        name="dsa_index",
    )(pqi, widx, kbd)


def _flash_kernel(*refs, mode, nb, topb):
    if mode == "mask":
        q_ref, k_ref, v_ref, z_ref, d_ref, mask_ref, y_ref = refs[:7]
    else:
        q_ref, k_ref, v_ref, z_ref, d_ref, y_ref, km_ref = refs[:7]
    qs_ref, acc_ref, m_ref = refs[7:]
    blk_rows = ATT_T
    t = ATT_TK
    hb = q_ref.shape[2] // HEAD_DIM
    rep = t // LANES
    i = pl.program_id(2)

    if mode == "sel":
        @pl.when(i == 0)
        def _():
            km_ref[...] = jnp.zeros(km_ref.shape, F32)
            for h in range(hb):
                for n in range(nb):
                    kblk = k_ref[0, n * blk_rows:(n + 1) * blk_rows,
                                 h * HEAD_DIM:(h + 1) * HEAD_DIM].astype(F32)
                    km_ref[h, n:n + 1, :] = jnp.mean(kblk, axis=0, keepdims=True)
        nbp = -(-nb // SUBLANES) * SUBLANES
        blk = lax.broadcasted_iota(I32, (nbp, t), 0).astype(F32)
        own = (2 * i + lax.broadcasted_iota(I32, (nbp, t), 1) // blk_rows).astype(F32)

    for h in range(hb):
        q = q_ref[0, :, h * HEAD_DIM:(h + 1) * HEAD_DIM]
        qsc = (q.astype(F32) * (ATTN_SCALE * LOG2E)).astype(BF16)
        if mode == "sel":
            gate = lax.dot_general(km_ref[h, 0:nbp, :].astype(BF16), q, (((1,), (1,)), ((), ())),
                                   preferred_element_type=F32)
            gate = jnp.where(blk < own, gate, -jnp.inf)
            sel = jnp.full((nbp, t), MASK_NEG, F32)
            for _ in range(topb):
                gmax = jnp.max(gate, axis=0, keepdims=True)
                first = jnp.min(jnp.where(gate == gmax, blk, float(nbp)), axis=0, keepdims=True)
                hit = jnp.where(blk == first, jnp.where(gmax > -jnp.inf, 1.0, 0.0), 0.0)
                sel = jnp.where(hit > 0.0, 0.0, sel)
                gate = jnp.where(blk == first, -jnp.inf, gate)
            sel = jnp.where(blk == own, 0.0, sel)
            selb = jnp.concatenate([sel, jnp.full((LANES - nbp, t), MASK_NEG, F32)], axis=0).T
            qs_ref[h] = jnp.concatenate([qsc, selb.astype(BF16)], axis=1)
        else:
            qs_ref[h] = qsc
        m_ref[h] = jnp.full((t, LANES), MASK_NEG, F32)
        acc_ref[h] = jnp.zeros((t, 2 * HEAD_DIM), F32)

    def add_bias(x, h, kind, j, row_blk0):
        d0, d1 = d_ref[0, h], d_ref[1, h]
        blk_of = lambda r, c: x[r * LANES:(r + 1) * LANES, c * LANES:(c + 1) * LANES]
        rows = []
        for r in range(x.shape[0] // LANES):
            cols = []
            for c in range(x.shape[1] // LANES):
                b = blk_of(r, c)
                if kind == "past":
                    if row_blk0 + r == 0 and c == rep - 1:
                        b = b + d1 * jnp.where(j == i - 1, 1.0, 0.0)
                elif row_blk0 + r == c:
                    b = b + d0
                elif row_blk0 + r == c + 1:
                    b = b + d1
                cols.append(b)
            rows.append(jnp.concatenate(cols, axis=1))
        return jnp.concatenate(rows, axis=0)

    def attend(j, kind, row0=0, nrows=t, ncols=t):
        r0 = pl.multiple_of(j * t, t)
        rsl = slice(row0, row0 + nrows)
        ones_blk = jnp.ones((ncols, LANES), BF16)
        if mode == "mask":
            mask_tile = jnp.concatenate(
                [jnp.concatenate([mask_ref[0, r, 2 * j + c] for c in range(ncols // blk_rows)], axis=1)
                 for r in range(row0 // blk_rows, (row0 + nrows) // blk_rows)], axis=0).astype(F32)
        else:
            lane = lax.broadcasted_iota(I32, (ncols, LANES), 1)
            second_half = lax.broadcasted_iota(I32, (ncols, LANES), 0) >= blk_rows
            onehot = jnp.where(lane == jnp.where(second_half, 2 * j + 1, 2 * j), 1.0, 0.0).astype(BF16)
            if kind == "diag":
                row = row0 + lax.broadcasted_iota(I32, (nrows, ncols), 0)
                col = lax.broadcasted_iota(I32, (nrows, ncols), 1)
        for h in range(hb):
            kblk = k_ref[0, pl.ds(r0, ncols), h * HEAD_DIM:(h + 1) * HEAD_DIM]
            vblk = v_ref[0, pl.ds(r0, ncols), h * HEAD_DIM:(h + 1) * HEAD_DIM]
            if mode == "sel":
                kblk = jnp.concatenate([kblk, onehot], axis=1)
            x = lax.dot_general(qs_ref[h, rsl], kblk, (((1,), (1,)), ((), ())),
                                preferred_element_type=F32)
            if mode == "mask":
                x = x + mask_tile
            x = add_bias(x, h, kind, j, row0 // LANES)
            if kind == "diag" and mode == "sel":
                x = jnp.where(col <= row, x, MASK_NEG)
            m_prev = m_ref[h, rsl]
            m_new = jnp.maximum(m_prev, jnp.max(x, axis=1, keepdims=True))
            alpha = jnp.exp2(m_prev - m_new)
            p = jnp.exp2(x - _tile_lanes(m_new, ncols // LANES)).astype(BF16)
            m_ref[h, rsl] = m_new
            pv = jnp.dot(p, jnp.concatenate([vblk, ones_blk], axis=1), preferred_element_type=F32)
            acc_ref[h, rsl] = acc_ref[h, rsl] * _tile_lanes(alpha, 2) + pv

    def past_body(j, carry):
        attend(j, "past")
        return carry

    lax.fori_loop(0, i, past_body, 0)
    attend(i, "diag", 0, blk_rows, blk_rows)
    attend(i, "diag", blk_rows, t - blk_rows, t)

    for h in range(hb):
        acc = acc_ref[h]
        o = acc[:, :HEAD_DIM] / acc[:, HEAD_DIM:]
        zz = z_ref[0, :, h * HEAD_DIM:(h + 1) * HEAD_DIM].astype(F32)
        y_ref[0, :, h * HEAD_DIM:(h + 1) * HEAD_DIM] = (o * (zz * _sigmoid(zz))).astype(BF16)


def _flash_attention(q, k, v, z, dtab, n_heads, name, mask=None):
    (qa, cq), (ka, ck), (va, cv), (za, cz) = q, k, v, z
    b, s, _ = qa.shape
    t = ATT_TK
    assert s % t == 0
    nb = s // ATT_T
    hb = ATT_HB
    wg = hb * HEAD_DIM
    mode = "sel" if mask is None else "mask"
    kern = functools.partial(_flash_kernel, mode=mode, nb=nb, topb=min(MOBA_TOPB, nb - 1))
    in_specs = [pl.BlockSpec((1, t, wg), lambda bi, g, i: (bi, i, cq + g)),
                pl.BlockSpec((1, s, wg), lambda bi, g, i: (bi, 0, ck + g)),
                pl.BlockSpec((1, s, wg), lambda bi, g, i: (bi, 0, cv + g)),
                pl.BlockSpec((1, t, wg), lambda bi, g, i: (bi, i, cz + g)),
                pl.BlockSpec((2, hb, LANES, LANES), lambda bi, g, i: (0, g, 0, 0))]
    args = [qa, ka, va, za, dtab]
    scratch = []
    if mode == "mask":
        in_specs.append(pl.BlockSpec((1, t // ATT_T, nb, ATT_T, ATT_T), lambda bi, g, i: (bi, i, 0, 0, 0)))
        args.append(mask)
        kq = HEAD_DIM
    else:
        scratch.append(pltpu.VMEM((hb, LANES, HEAD_DIM), F32))
        kq = 2 * HEAD_DIM
    scratch += [pltpu.VMEM((hb, t, kq), BF16),
                pltpu.VMEM((hb, t, 2 * HEAD_DIM), F32),
                pltpu.VMEM((hb, t, LANES), F32)]
    return pl.pallas_call(
        kern,
        grid=(b, n_heads // hb, s // t),
        in_specs=in_specs,
        out_specs=pl.BlockSpec((1, t, wg), lambda bi, g, i: (bi, i, g)),
        out_shape=jax.ShapeDtypeStruct((b, s, n_heads * HEAD_DIM), BF16),
        scratch_shapes=scratch,
        compiler_params=_cparams(("parallel", "parallel", "arbitrary")),
        name=name,
    )(*args)


def _softplus(x):
    return jnp.maximum(x, 0.0) + jnp.log1p(jnp.exp(-jnp.abs(x)))


def _rglru_kernel(xr_ref, gz_ref, cw_ref, cb_ref, wa_ref, ba_ref, wx_ref, bx_ref, lam_ref, y_ref,
                  xbuf_ref, a_ref, b_ref, h_ref):
    tc = xr_ref.shape[1]
    c = pl.program_id(2)
    halo = SUBLANES

    @pl.when(c == 0)
    def _():
        xbuf_ref[0:halo, :] = jnp.zeros((halo, xbuf_ref.shape[1]), F32)
        h_ref[...] = jnp.zeros(h_ref.shape, F32)

    @pl.when(c > 0)
    def _():
        xbuf_ref[0:halo, :] = xbuf_ref[tc:tc + halo, :]

    xbuf_ref[halo:halo + tc, :] = xr_ref[0]
    xfull = xbuf_ref[...]
    u = xfull * cw_ref[0:1, :]
    for j in range(1, CONV_W):
        u = xfull * cw_ref[j:j + 1, :] + pltpu.roll(u, 1, 0)
    xc = (u + cb_ref[...])[halo:, :]

    xcb = xc.astype(BF16)
    tr = jnp.tanh(jnp.dot(xcb, wa_ref[0], preferred_element_type=F32) + ba_ref[...])
    ti = jnp.tanh(jnp.dot(xcb, wx_ref[0], preferred_element_type=F32) + bx_ref[...])
    half_c = (-0.5 * RG_C * LOG2E) * _softplus(-lam_ref[...])
    a = jnp.exp2(tr * half_c + half_c)
    one_m_a2 = 1.0 - a * a
    mult = jnp.where(one_m_a2 > 0.0, one_m_a2 * lax.rsqrt(one_m_a2), 0.0)
    bt = (mult * xc) * (0.5 * ti + 0.5)

    nseg = SUBLANES
    seg = tc // nseg
    pitch = seg + RG_SEG_PAD
    nslab = a_ref.shape[0]
    for sl in range(nslab):
        for sg in range(nseg):
            a_ref[sl, sg * pitch:sg * pitch + seg, :] = a[sg * seg:(sg + 1) * seg, sl * LANES:(sl + 1) * LANES]
            b_ref[sl, sg * pitch:sg * pitch + seg, :] = bt[sg * seg:(sg + 1) * seg, sl * LANES:(sl + 1) * LANES]

    def seg_step(j, carry):
        out = []
        for sl in range(nslab):
            hl, pr = carry[2 * sl], carry[2 * sl + 1]
            rows = pl.ds(j, nseg, stride=pitch)
            a8 = a_ref[sl, rows, :]
            hl = a8 * hl + b_ref[sl, rows, :]
            pr = a8 * pr
            b_ref[sl, rows, :] = hl
            a_ref[sl, rows, :] = pr
            out += [hl, pr]
        return tuple(out)

    init = tuple(v for _ in range(nslab)
                 for v in (jnp.zeros((nseg, LANES), F32), jnp.ones((nseg, LANES), F32)))
    ends = lax.fori_loop(0, seg, seg_step, init, unroll=RG_SCAN_UNROLL)

    gate = gz_ref[0].astype(F32)
    for sl in range(nslab):
        h_end, p_end = ends[2 * sl], ends[2 * sl + 1]
        h_in = h_ref[:, sl * LANES:(sl + 1) * LANES]
        for sg in range(nseg):
            rows = slice(sg * pitch, sg * pitch + seg)
            hs = b_ref[sl, rows, :] + a_ref[sl, rows, :] * h_in
            y_ref[0, sg * seg:(sg + 1) * seg, sl * LANES:(sl + 1) * LANES] = (
                hs * gate[sg * seg:(sg + 1) * seg, sl * LANES:(sl + 1) * LANES]).astype(BF16)
            h_in = p_end[sg:sg + 1, :] * h_in + h_end[sg:sg + 1, :]
        h_ref[:, sl * LANES:(sl + 1) * LANES] = h_in


def _rglru(xr, gz, conv_w, conv_b, wa_g, b_a, wx_g, b_x, lam):
    b, s, d = xr.shape
    g = wa_g.shape[0]
    cg = d // g
    tc = min(RG_TC, s)
    assert ((tc // SUBLANES + RG_SEG_PAD) // SUBLANES) % 2 == 1 and cg % LANES == 0
    row = lambda v: v.reshape(1, d).astype(F32)
    vec_spec = pl.BlockSpec((1, cg), lambda bi, gi, c: (0, gi))
    return pl.pallas_call(
        _rglru_kernel,
        grid=(b, g, s // tc),
        in_specs=[pl.BlockSpec((1, tc, cg), lambda bi, gi, c: (bi, c, gi)),
                  pl.BlockSpec((1, tc, cg), lambda bi, gi, c: (bi, c, gi)),
                  pl.BlockSpec((CONV_W, cg), lambda bi, gi, c: (0, gi)),
                  vec_spec,
                  pl.BlockSpec((1, cg, cg), lambda bi, gi, c: (gi, 0, 0)),
                  vec_spec,
                  pl.BlockSpec((1, cg, cg), lambda bi, gi, c: (gi, 0, 0)),
                  vec_spec,
                  vec_spec],
        out_specs=pl.BlockSpec((1, tc, cg), lambda bi, gi, c: (bi, c, gi)),
        out_shape=jax.ShapeDtypeStruct((b, s, d), BF16),
        scratch_shapes=[pltpu.VMEM((tc + SUBLANES, cg), F32),
                        pltpu.VMEM((cg // LANES, tc + SUBLANES * RG_SEG_PAD, LANES), F32),
                        pltpu.VMEM((cg // LANES, tc + SUBLANES * RG_SEG_PAD, LANES), F32),
                        pltpu.VMEM((1, cg), F32)],
        compiler_params=_cparams(("parallel", "parallel", "arbitrary")),
        name="rglru",
    )(xr, gz, conv_w.astype(F32), row(conv_b), wa_g * 0.5, row(b_a) * 0.5, wx_g * 0.5, row(b_x) * 0.5,
      row(lam))


def _group_block_diag(w, pair):
    nb, k, _ = w.shape
    out = None
    for p in range(pair):
        lo, hi = p * k, (pair - 1 - p) * k
        blk = jnp.pad(w[p::pair], ((0, 0), (lo, hi), (lo, hi)))
        out = blk if out is None else out + blk
    return out


def _attention_layer(x2d, b, s, norm_g, w_in, kv_g, w_uk, w_uv, idx_g, idx_b, w_out, rel_bias):
    d_model = x2d.shape[1]
    n_heads = w_uk.shape[0]
    wq = n_heads * HEAD_DIM
    assert N_IDX_HEADS * IDX_DIM == wq
    sizes = (wq, KV_LORA, wq, IDX_DIM, N_IDX_HEADS, wq, wq, wq, 2 * wq)
    offs = [int(o) for o in np.concatenate([[0], np.cumsum(sizes)])]
    wt = w_in.T.astype(BF16)
    zrows = lambda n: jnp.zeros((n, d_model), BF16)
    wt_small = jnp.concatenate([wt[offs[1]:offs[2]], wt[offs[3]:offs[4]], zrows(LANES - IDX_DIM),
                                wt[offs[4]:offs[5]], zrows(LANES - N_IDX_HEADS)], axis=0)

    h = _rmsnorm(x2d, norm_g, BF16, "rmsnorm0")
    pqa = _matmul_nt(h, wt, offs[0], wq, BF16, "proj_attn_qa").reshape(b, s, wq)
    pqi = _matmul_nt(h, wt, offs[2], wq, BF16, "proj_attn_qi").reshape(b, s, wq)
    ptail = _matmul_nt(h, wt, offs[5], 5 * wq, BF16, "proj_attn_tail").reshape(b, s, 5 * wq)
    psmall = _matmul_nt(h, wt_small, 0, wt_small.shape[0], F32, "proj_attn_small").reshape(b, s, -1)

    ckv, kbd, widx = _dsa_prep(psmall, kv_g, idx_g, idx_b)
    mask = _dsa_index(pqi, 0, kbd, widx)
    w_kv = jnp.concatenate([jnp.transpose(w_uk, (1, 0, 2)).reshape(KV_LORA, wq),
                            jnp.transpose(w_uv, (1, 0, 2)).reshape(KV_LORA, wq)], axis=1).astype(BF16)
    kv = _matmul([ckv.reshape(b * s, KV_LORA)], [w_kv], BF16, "dsa_kv_up", bm=2 * MM_BM).reshape(b, s, 2 * wq)

    dtab_a = _bias_blocks(rel_bias[:, :n_heads])
    dtab_b = _bias_blocks(rel_bias[:, n_heads:])
    gpw = n_heads // ATT_HB
    ya = _flash_attention((pqa, 0), (kv, 0), (kv, gpw), (ptail, 3 * gpw), dtab_a, n_heads,
                          "dsa_attention", mask=mask)
    yb = _flash_attention((ptail, 0), (ptail, gpw), (ptail, 2 * gpw), (ptail, 4 * gpw), dtab_b,
                          n_heads, "moba_attention")

    w_out_b = w_out.astype(BF16)
    return _matmul([ya.reshape(b * s, wq), yb.reshape(b * s, wq)], [(w_out_b, 0, 0), (w_out_b, 1, 0)],
                   F32, "proj_attn_out", res=x2d)


def _recurrent_layer(x2d, b, s, norm_g, w_in, conv_w, conv_b, w_a, b_a, w_x, b_x, lam, w_out):
    d_rnn = conv_b.shape[0]
    h = _rmsnorm(x2d, norm_g, BF16, "rmsnorm1")
    w_in_b = w_in.astype(BF16)
    assert d_rnn % MM_BN == 0
    xr = _matmul([h], [(w_in_b, 0, 0)], F32, "proj_rec_x", n=d_rnn)
    gz = _matmul([h], [(w_in_b, 0, d_rnn // MM_BN)], BF16, "proj_rec_z", n=d_rnn, epilogue=_silu)
    y = _rglru(xr.reshape(b, s, d_rnn), gz.reshape(b, s, d_rnn), conv_w, conv_b,
               _group_block_diag(w_a.astype(BF16), RG_PAIR), b_a,
               _group_block_diag(w_x.astype(BF16), RG_PAIR), b_x, lam)
    return _matmul([y.reshape(b * s, d_rnn)], [w_out.astype(BF16)], F32, "proj_rec_out",
                   res=x2d, bn=MM_BN // 2)


def kernel(x, norm_g, final_g, rel_bias, attn_w_in, attn_kv_g, attn_w_uk, attn_w_uv, idx_k_g, idx_k_b,
           attn_w_out, rec_w_in, rec_conv_w, rec_conv_b, rec_w_a, rec_b_a, rec_w_x, rec_b_x,
           rec_lambda, rec_w_out):
    b, s, d = x.shape
    depth = norm_g.shape[0]
    x2d = x.reshape(b * s, d)
    for layer in range(depth):
        li = layer // 2
        if layer % 2 == 0:
            x2d = _attention_layer(x2d, b, s, norm_g[layer], attn_w_in[li], attn_kv_g[li],
                                   attn_w_uk[li], attn_w_uv[li], idx_k_g[li], idx_k_b[li],
                                   attn_w_out[li], rel_bias)
        else:
            x2d = _recurrent_layer(x2d, b, s, norm_g[layer], rec_w_in[li], rec_conv_w[li],
                                   rec_conv_b[li], rec_w_a[li], rec_b_a[li], rec_w_x[li],
                                   rec_b_x[li], rec_lambda[li], rec_w_out[li])
    return _rmsnorm(x2d, final_g, x.dtype, "rmsnorm_final").reshape(b, s, d)
```

```python
import functools
import math

import numpy as np
import jax
import jax.numpy as jnp
from jax import lax
from jax.experimental import pallas as pl
from jax.experimental.pallas import tpu as pltpu

F32 = jnp.float32
BF16 = jnp.bfloat16
I32 = jnp.int32

HEAD_DIM = 128
KV_LORA = 512
N_IDX_HEADS = 32
IDX_DIM = 64
TOPK_MAX = 256
MOBA_BLOCK = 256
MOBA_TOPB = 3
RG_BLOCKS = 16
CONV_W = 4
RG_C = 8.0
NUM_BUCKETS = 32
MAX_DISTANCE = 128
EPS = 1e-6
ATTN_SCALE = HEAD_DIM ** -0.5
IDX_SCALE = (N_IDX_HEADS ** -0.5) * (IDX_DIM ** -0.5)
LOG2E = math.log2(math.e)

LANES = 128
SUBLANES = 8
WT_ROW_ALIGN = 16
VMEM_LIMIT_BYTES = 56 * 2 ** 20
MASK_NEG = -1e30
INT_MIN = np.int32(-2 ** 31)
KEY_BYTES = 4

ROW_TILE = 512
MM_BM = 1024
MM_BN = 1024
ATT_T = 256
ATT_TK = 2 * ATT_T
ATT_HB = 4
IDX_TQ = 2 * ATT_T
RG_PAIR = 2
RG_TC = 1024
RG_SEG_PAD = 8
RG_SCAN_UNROLL = 4
assert ATT_T == MOBA_BLOCK and ATT_T == 2 * LANES


def _cparams(sem):
    return pltpu.CompilerParams(dimension_semantics=sem, vmem_limit_bytes=VMEM_LIMIT_BYTES)


def _sigmoid(x):
    return 0.5 * jnp.tanh(0.5 * x) + 0.5


def _tile_rows(x, n):
    return x if n == 1 else jnp.concatenate([x] * n, axis=0)


def _tile_lanes(x, n):
    return x if n == 1 else jnp.concatenate([x] * n, axis=1)


def _rmsnorm_kernel(x_ref, g_ref, o_ref):
    x = x_ref[...]
    ms = jnp.mean(x * x, axis=-1, keepdims=True)
    o_ref[...] = ((x * lax.rsqrt(ms + EPS)) * g_ref[...]).astype(o_ref.dtype)


def _rmsnorm(x2d, g, out_dtype, name):
    m, d = x2d.shape
    bm = min(ROW_TILE, m)
    return pl.pallas_call(
        _rmsnorm_kernel,
        grid=(m // bm,),
        in_specs=[pl.BlockSpec((bm, d), lambda i: (i, 0)),
                  pl.BlockSpec((1, d), lambda i: (0, 0))],
        out_specs=pl.BlockSpec((bm, d), lambda i: (i, 0)),
        out_shape=jax.ShapeDtypeStruct((m, d), out_dtype),
        compiler_params=_cparams(("parallel",)),
        name=name,
    )(x2d, g.reshape(1, d).astype(F32))


def _mm_kernel(*refs, n_parts, has_res, epilogue):
    o_ref = refs[-1]
    acc = jnp.dot(refs[0][...], refs[n_parts][...], preferred_element_type=F32)
    for p in range(1, n_parts):
        acc = acc + jnp.dot(refs[p][...], refs[n_parts + p][...], preferred_element_type=F32)
    if has_res:
        acc = refs[2 * n_parts][...] + acc
    if epilogue is not None:
        acc = epilogue(acc)
    o_ref[...] = acc.astype(o_ref.dtype)


def _matmul(xs, ws, out_dtype, name, res=None, n=None, bm=MM_BM, bn=MM_BN, epilogue=None):
    ws = [w if isinstance(w, tuple) else (w, 0, 0) for w in ws]
    m = xs[0].shape[0]
    n = ws[0][0].shape[1] if n is None else n
    bm = min(bm, m)
    bn = min(bn, n)
    assert m % bm == 0 and n % bn == 0
    in_specs = [pl.BlockSpec((bm, x.shape[1]), lambda i, j: (i, 0)) for x in xs]
    for x, (w, rb, cb) in zip(xs, ws):
        assert w.shape[0] % x.shape[1] == 0 and (cb + n // bn) * bn <= w.shape[1]
        in_specs.append(pl.BlockSpec((x.shape[1], bn), lambda i, j, rb=rb, cb=cb: (rb, cb + j)))
    args = [*xs, *[w for w, _, _ in ws]]
    if res is not None:
        in_specs.append(pl.BlockSpec((bm, bn), lambda i, j: (i, j)))
        args.append(res)
    return pl.pallas_call(
        functools.partial(_mm_kernel, n_parts=len(xs), has_res=res is not None, epilogue=epilogue),
        grid=(m // bm, n // bn),
        in_specs=in_specs,
        out_specs=pl.BlockSpec((bm, bn), lambda i, j: (i, j)),
        out_shape=jax.ShapeDtypeStruct((m, n), out_dtype),
        compiler_params=_cparams(("parallel", "parallel")),
        name=name,
    )(*args)


def _silu(x):
    return x * _sigmoid(x)


def _mm_nt_kernel(x_ref, wt_ref, o_ref, *, epilogue):
    acc = lax.dot_general(x_ref[...], wt_ref[...], (((1,), (1,)), ((), ())), preferred_element_type=F32)
    if epilogue is not None:
        acc = epilogue(acc)
    o_ref[...] = acc.astype(o_ref.dtype)


def _matmul_nt(x, wt, row0, n, out_dtype, name, bn=MM_BN, epilogue=None):
    m, k = x.shape
    bm = min(MM_BM, m)
    bn = min(bn, n)
    assert m % bm == 0 and n % bn == 0 and row0 % WT_ROW_ALIGN == 0 and bn % WT_ROW_ALIGN == 0
    assert row0 + n <= wt.shape[0] and wt.shape[1] == k
    return pl.pallas_call(
        functools.partial(_mm_nt_kernel, epilogue=epilogue),
        grid=(m // bm, n // bn),
        in_specs=[pl.BlockSpec((bm, k), lambda i, j: (i, 0)),
                  pl.BlockSpec((pl.Element(bn), pl.Element(k)),
                               lambda i, j: (pl.multiple_of(row0 + j * bn, WT_ROW_ALIGN), 0))],
        out_specs=pl.BlockSpec((bm, bn), lambda i, j: (i, j)),
        out_shape=jax.ShapeDtypeStruct((m, n), out_dtype),
        compiler_params=_cparams(("parallel", "parallel")),
        name=name,
    )(x, wt)


def _t5_bucket_np(dist):
    d = np.maximum(dist, 0)
    max_exact = NUM_BUCKETS // 2
    d_f = np.maximum(d, 1).astype(np.float32)
    ratio = np.log(d_f / np.float32(max_exact)) / np.float32(math.log(MAX_DISTANCE / max_exact))
    large = max_exact + (ratio * np.float32(NUM_BUCKETS - max_exact)).astype(np.int32)
    large = np.minimum(large, NUM_BUCKETS - 1)
    return np.where(d < max_exact, d, large)


def _bias_blocks(tab):
    n = LANES
    assert np.all(_t5_bucket_np(np.arange(n, 64 * n)) == NUM_BUCKETS - 1)
    h = tab.shape[1]
    rel = (tab - tab[NUM_BUCKETS - 1]).astype(F32) * LOG2E
    out = []
    for off in (0, n):
        buckets = _t5_bucket_np(off + n - 1 - np.arange(2 * n - 1))
        onehot = np.zeros((2 * n, NUM_BUCKETS), np.float32)
        onehot[np.arange(2 * n - 1), buckets] = 1.0
        w = jnp.dot(jnp.asarray(onehot), rel, precision=lax.Precision.HIGHEST).T
        t = jnp.tile(w, (1, n))[:, :n * (2 * n - 1)].reshape(h, n, 2 * n - 1)
        out.append(t[:, :, n - 1:])
    return jnp.stack(out)


def _dsa_prep_kernel(p_ref, kvg_ref, ig_ref, ib_ref, ckv_ref, kbd_ref, w_ref):
    p = p_ref[0]
    ts = p.shape[0]
    c = p[:, :KV_LORA]
    cn = (c * lax.rsqrt(jnp.mean(c * c, axis=-1, keepdims=True) + EPS)) * kvg_ref[...]
    ckv_ref[0] = cn.astype(BF16)
    k = p[:, KV_LORA:KV_LORA + IDX_DIM]
    mu = jnp.mean(k, axis=-1, keepdims=True)
    var = jnp.mean(jnp.square(k - mu), axis=-1, keepdims=True)
    kn = ((k - mu) * lax.rsqrt(var + EPS)) * ig_ref[...] + ib_ref[...]
    zeros = jnp.zeros((ts, LANES - IDX_DIM), F32)
    top = jnp.concatenate([kn, zeros], axis=1).T
    bot = jnp.concatenate([zeros, kn], axis=1).T
    kbd_ref[0, 0] = jnp.concatenate([top, bot], axis=1).astype(BF16)
    w_ref[0] = p[:, KV_LORA + LANES:KV_LORA + LANES + N_IDX_HEADS] * IDX_SCALE


def _dsa_prep(ps, kv_g, idx_g, idx_b):
    b, s, wid = ps.shape
    ts = ATT_T
    nkt = s // ts
    return pl.pallas_call(
        _dsa_prep_kernel,
        grid=(b, nkt),
        in_specs=[pl.BlockSpec((1, ts, wid), lambda bi, i: (bi, i, 0)),
                  pl.BlockSpec((1, KV_LORA), lambda bi, i: (0, 0)),
                  pl.BlockSpec((1, IDX_DIM), lambda bi, i: (0, 0)),
                  pl.BlockSpec((1, IDX_DIM), lambda bi, i: (0, 0))],
        out_specs=[pl.BlockSpec((1, ts, KV_LORA), lambda bi, i: (bi, i, 0)),
                   pl.BlockSpec((1, 1, LANES, 2 * ts), lambda bi, i: (bi, i, 0, 0)),
                   pl.BlockSpec((1, ts, N_IDX_HEADS), lambda bi, i: (bi, i, 0))],
        out_shape=[jax.ShapeDtypeStruct((b, s, KV_LORA), BF16),
                   jax.ShapeDtypeStruct((b, nkt, LANES, 2 * ts), BF16),
                   jax.ShapeDtypeStruct((b, s, N_IDX_HEADS), F32)],
        compiler_params=_cparams(("parallel", "parallel")),
        name="dsa_prep",
    )(ps, kv_g.reshape(1, -1).astype(F32), idx_g.reshape(1, -1).astype(F32),
      idx_b.reshape(1, -1).astype(F32))


def _dsa_index_kernel(qi_ref, w_ref, kbd_ref, mask_ref, key_ref, plane_ref, wb_ref, t_ref, j_ref,
                      *, topk, nkt_total, seq_bits):
    tq, tk = IDX_TQ, ATT_T
    rep = tk // LANES
    sl = SUBLANES
    i = pl.program_id(1)
    t0 = i * tq
    nkt = (i + 1) * (tq // tk)

    row = t0 + lax.broadcasted_iota(I32, (tq, tk), 0)
    col = lax.broadcasted_iota(I32, (tq, tk), 1)

    def to_key(score):
        score = jnp.where(score == 0.0, 0.0, score)
        bits = lax.bitcast_convert_type(score, I32)
        return bits ^ ((bits >> 31) & np.int32(0x7FFFFFFF))

    for h in range(N_IDX_HEADS):
        wb_ref[h] = jnp.broadcast_to(w_ref[0, :, h:h + 1], (tq, LANES))

    def idx_tile(kt, q0, nq):
        kb = kbd_ref[0, kt]
        acc = jnp.zeros((nq, tk), F32)
        for hp in range(N_IDX_HEADS // 2):
            rel = jnp.dot(qi_ref[0, q0:q0 + nq, hp * LANES:(hp + 1) * LANES], kb,
                          preferred_element_type=F32)
            acc = acc + jnp.maximum(rel[:, :tk], 0.0) * _tile_lanes(wb_ref[2 * hp, q0:q0 + nq], rep)
            acc = acc + jnp.maximum(rel[:, tk:], 0.0) * _tile_lanes(wb_ref[2 * hp + 1, q0:q0 + nq], rep)
        q_sub = t0 + q0 + lax.broadcasted_iota(I32, (nq, tk), 0)
        k_lane = kt * tk + lax.broadcasted_iota(I32, (nq, tk), 1)
        k_sub = kt * tk + lax.broadcasted_iota(I32, (tk, nq), 0)
        q_lane = t0 + q0 + lax.broadcasted_iota(I32, (tk, nq), 1)
        if q0:
            key_ref[kt, 0:q0] = jnp.full((q0, tk), INT_MIN, I32)
        key_ref[kt, q0:q0 + nq] = jnp.where(k_lane <= q_sub, to_key(acc), INT_MIN)
        ukey = jnp.where(k_sub <= q_lane, to_key(acc.T), INT_MIN) ^ INT_MIN
        for bi in range(KEY_BYTES):
            byte = (ukey >> (8 * (KEY_BYTES - 1 - bi))) & np.int32(0xFF)
            if q0:
                plane_ref[bi, kt, :, 0:q0] = jnp.zeros((tk, q0), BF16)
            plane_ref[bi, kt, :, q0:q0 + nq] = byte.astype(F32).astype(BF16)

    def idx_body(kt, carry):
        idx_tile(kt, 0, tq)
        return carry

    lax.fori_loop(0, nkt - 1, idx_body, 0)
    idx_tile(nkt - 1, tq - tk, tk)

    one, zero = jnp.ones((), BF16), jnp.zeros((), BF16)

    def count(plane, pred_fn):
        rows = 2 * sl
        def body(kt, c):
            xs = [pred_fn(plane_ref[plane, kt, r * rows:(r + 1) * rows, :], kt * tk + r * rows)
                  for r in range(tk // rows)]
            while len(xs) > 1:
                xs = [xs[a] + xs[a + 1] for a in range(0, len(xs), 2)]
            return c + xs[0]
        c = lax.fori_loop(0, nkt, body, jnp.zeros((rows, tq), BF16))
        return jnp.broadcast_to(jnp.sum(c.astype(F32), axis=0, keepdims=True), (sl, tq))

    def rep16(v8):
        return jnp.concatenate([v8, v8], axis=0).astype(BF16)

    target = jnp.full((sl, tq), float(topk), F32)
    thr_u = jnp.zeros((sl, tq), I32)
    for bi in range(KEY_BYTES):
        t_ref[...] = jnp.zeros((sl, tq), F32)

        def bit_body(it, carry, bi=bi, target=target):
            cand = t_ref[...] + jnp.left_shift(jnp.int32(1), 7 - it).astype(F32)
            c16 = rep16(cand)
            n_ge = count(bi, lambda d, s0: jnp.where(d >= c16, one, zero))
            t_ref[...] = jnp.where(n_ge >= target, cand, t_ref[...])
            return carry

        lax.fori_loop(0, 8, bit_body, 0)
        tb = t_ref[...]
        tb16 = rep16(tb)
        thr_u = (thr_u << 8) | tb.astype(I32)
        if bi + 1 < KEY_BYTES:
            target = target - count(bi, lambda d, s0: jnp.where(d > tb16, one, zero))

            def narrow_body(kt, carry, bi=bi):
                plane_ref[bi + 1, kt] = jnp.where(plane_ref[bi, kt] == _tile_rows(tb16, tk // (2 * sl)),
                                                  plane_ref[bi + 1, kt], -one)
                return carry

            lax.fori_loop(0, nkt, narrow_body, 0)

    last = KEY_BYTES - 1
    n_gt = count(last, lambda d, s0: jnp.where(d > tb16, one, zero))
    n_ge = count(last, lambda d, s0: jnp.where(d >= tb16, one, zero))
    need = target - n_gt
    j_ref[...] = jnp.full((sl, tq), 2 ** 30, I32)
    sub16 = lax.broadcasted_iota(I32, (2 * sl, tq), 0)

    @pl.when(jnp.max(n_ge - target) > 0.0)
    def _():
        j_ref[...] = jnp.zeros((sl, tq), I32)

        def jbit_body(it, carry):
            cand = j_ref[...] | jnp.left_shift(jnp.int32(1), seq_bits - 1 - it)
            c2 = jnp.concatenate([cand, cand], axis=0)
            n_lt = count(last, lambda d, s0: jnp.where(
                d == tb16, jnp.where(s0 + sub16 < c2, 1.0, 0.0).astype(BF16), zero))
            j_ref[...] = jnp.where(n_lt < need, cand, j_ref[...])
            return carry

        lax.fori_loop(0, seq_bits, jbit_body, 0)

    thr = thr_u ^ INT_MIN

    def per_row(v8):
        wide = jnp.broadcast_to(lax.bitcast_convert_type(v8[0:1], F32), (LANES, tq))
        return _tile_lanes(lax.bitcast_convert_type(wide.T, I32), rep)

    thr_t = per_row(thr)
    jmax_t = per_row(j_ref[...])

    def mask_body(kt, carry):
        k = key_ref[kt]
        s_idx = kt * tk + col
        keep_tie = jnp.where(k == thr_t, jnp.where(s_idx <= jmax_t, 0.0, MASK_NEG), MASK_NEG)
        keep = jnp.where(k > thr_t, 0.0, keep_tie)
        full = jnp.where(s_idx <= row, keep, MASK_NEG).astype(BF16)
        for r in range(tq // tk):
            mask_ref[0, r, kt] = full[r * tk:(r + 1) * tk]
        return carry

    lax.fori_loop(0, nkt, mask_body, 0)

    def fill_body(kt, carry):
        for r in range(tq // tk):
            mask_ref[0, r, kt] = jnp.full((tk, tk), MASK_NEG, BF16)
        return carry

    lax.fori_loop(nkt, nkt_total, fill_body, 0)


def _dsa_index(pqi, col_qi, kbd, widx):
    b, s, _ = pqi.shape
    tq, tk = IDX_TQ, ATT_T
    nkt = s // tk
    wq = N_IDX_HEADS * IDX_DIM
    topk = min(TOPK_MAX, s // 4)
    assert s // (2 * SUBLANES) <= 256
    kern = functools.partial(_dsa_index_kernel, topk=topk, nkt_total=nkt,
                             seq_bits=max(1, (s - 1).bit_length()))
    return pl.pallas_call(
        kern,
        grid=(b, s // tq),
        in_specs=[pl.BlockSpec((1, tq, wq), lambda bi, i: (bi, i, col_qi)),
                  pl.BlockSpec((1, tq, N_IDX_HEADS), lambda bi, i: (bi, i, 0)),
                  pl.BlockSpec((1, nkt, LANES, 2 * tk), lambda bi, i: (bi, 0, 0, 0))],
        out_specs=pl.BlockSpec((1, tq // tk, nkt, tk, tk), lambda bi, i: (bi, i, 0, 0, 0)),
        out_shape=jax.ShapeDtypeStruct((b, s // tk, nkt, tk, tk), BF16),
        scratch_shapes=[pltpu.VMEM((nkt, tq, tk), I32),
                        pltpu.VMEM((KEY_BYTES, nkt, tk, tq), BF16),
                        pltpu.VMEM((N_IDX_HEADS, tq, LANES), F32),
                        pltpu.VMEM((SUBLANES, tq), F32),
                        pltpu.VMEM((SUBLANES, tq), I32)],
        compiler_params=_cparams(("parallel", "parallel")),
        name="dsa_index",
    )(pqi, widx, kbd)


def _flash_kernel(*refs, mode, nb, topb):
    if mode == "mask":
        q_ref, k_ref, v_ref, z_ref, d_ref, mask_ref, y_ref = refs[:7]
    else:
        q_ref, k_ref, v_ref, z_ref, d_ref, y_ref, km_ref = refs[:7]
    qs_ref, acc_ref, m_ref = refs[7:]
    blk_rows = ATT_T
    t = ATT_TK
    hb = q_ref.shape[2] // HEAD_DIM
    rep = t // LANES
    i = pl.program_id(2)

    if mode == "sel":
        @pl.when(i == 0)
        def _():
            km_ref[...] = jnp.zeros(km_ref.shape, F32)
            for h in range(hb):
                for n in range(nb):
                    kblk = k_ref[0, n * blk_rows:(n + 1) * blk_rows,
                                 h * HEAD_DIM:(h + 1) * HEAD_DIM].astype(F32)
                    km_ref[h, n:n + 1, :] = jnp.mean(kblk, axis=0, keepdims=True)
        nbp = -(-nb // SUBLANES) * SUBLANES
        blk = lax.broadcasted_iota(I32, (nbp, t), 0).astype(F32)
        own = (2 * i + lax.broadcasted_iota(I32, (nbp, t), 1) // blk_rows).astype(F32)

    for h in range(hb):
        q = q_ref[0, :, h * HEAD_DIM:(h + 1) * HEAD_DIM]
        qsc = (q.astype(F32) * (ATTN_SCALE * LOG2E)).astype(BF16)
        if mode == "sel":
            gate = lax.dot_general(km_ref[h, 0:nbp, :].astype(BF16), q, (((1,), (1,)), ((), ())),
                                   preferred_element_type=F32)
            gate = jnp.where(blk < own, gate, -jnp.inf)
            sel = jnp.full((nbp, t), MASK_NEG, F32)
            for _ in range(topb):
                gmax = jnp.max(gate, axis=0, keepdims=True)
                first = jnp.min(jnp.where(gate == gmax, blk, float(nbp)), axis=0, keepdims=True)
                hit = jnp.where(blk == first, jnp.where(gmax > -jnp.inf, 1.0, 0.0), 0.0)
                sel = jnp.where(hit > 0.0, 0.0, sel)
                gate = jnp.where(blk == first, -jnp.inf, gate)
            sel = jnp.where(blk == own, 0.0, sel)
            selb = jnp.concatenate([sel, jnp.full((LANES - nbp, t), MASK_NEG, F32)], axis=0).T
            qs_ref[h] = jnp.concatenate([qsc, selb.astype(BF16)], axis=1)
        else:
            qs_ref[h] = qsc
        m_ref[h] = jnp.full((t, LANES), MASK_NEG, F32)
        acc_ref[h] = jnp.zeros((t, 2 * HEAD_DIM), F32)

    def add_bias(x, h, kind, j, row_blk0):
        d0, d1 = d_ref[0, h], d_ref[1, h]
        blk_of = lambda r, c: x[r * LANES:(r + 1) * LANES, c * LANES:(c + 1) * LANES]
        rows = []
        for r in range(x.shape[0] // LANES):
            cols = []
            for c in range(x.shape[1] // LANES):
                b = blk_of(r, c)
                if kind == "past":
                    if row_blk0 + r == 0 and c == rep - 1:
                        b = b + d1 * jnp.where(j == i - 1, 1.0, 0.0)
                elif row_blk0 + r == c:
                    b = b + d0
                elif row_blk0 + r == c + 1:
                    b = b + d1
                cols.append(b)
            rows.append(jnp.concatenate(cols, axis=1))
        return jnp.concatenate(rows, axis=0)

    def attend(j, kind, row0=0, nrows=t, ncols=t):
        r0 = pl.multiple_of(j * t, t)
        rsl = slice(row0, row0 + nrows)
        ones_blk = jnp.ones((ncols, LANES), BF16)
        if mode == "mask":
            mask_tile = jnp.concatenate(
                [jnp.concatenate([mask_ref[0, r, 2 * j + c] for c in range(ncols // blk_rows)], axis=1)
                 for r in range(row0 // blk_rows, (row0 + nrows) // blk_rows)], axis=0).astype(F32)
        else:
            lane = lax.broadcasted_iota(I32, (ncols, LANES), 1)
            second_half = lax.broadcasted_iota(I32, (ncols, LANES), 0) >= blk_rows
            onehot = jnp.where(lane == jnp.where(second_half, 2 * j + 1, 2 * j), 1.0, 0.0).astype(BF16)
            if kind == "diag":
                row = row0 + lax.broadcasted_iota(I32, (nrows, ncols), 0)
                col = lax.broadcasted_iota(I32, (nrows, ncols), 1)
        for h in range(hb):
            kblk = k_ref[0, pl.ds(r0, ncols), h * HEAD_DIM:(h + 1) * HEAD_DIM]
            vblk = v_ref[0, pl.ds(r0, ncols), h * HEAD_DIM:(h + 1) * HEAD_DIM]
            if mode == "sel":
                kblk = jnp.concatenate([kblk, onehot], axis=1)
            x = lax.dot_general(qs_ref[h, rsl], kblk, (((1,), (1,)), ((), ())),
                                preferred_element_type=F32)
            if mode == "mask":
                x = x + mask_tile
            x = add_bias(x, h, kind, j, row0 // LANES)
            if kind == "diag" and mode == "sel":
                x = jnp.where(col <= row, x, MASK_NEG)
            m_prev = m_ref[h, rsl]
            m_new = jnp.maximum(m_prev, jnp.max(x, axis=1, keepdims=True))
            alpha = jnp.exp2(m_prev - m_new)
            p = jnp.exp2(x - _tile_lanes(m_new, ncols // LANES)).astype(BF16)
            m_ref[h, rsl] = m_new
            pv = jnp.dot(p, jnp.concatenate([vblk, ones_blk], axis=1), preferred_element_type=F32)
            acc_ref[h, rsl] = acc_ref[h, rsl] * _tile_lanes(alpha, 2) + pv

    def past_body(j, carry):
        attend(j, "past")
        return carry

    lax.fori_loop(0, i, past_body, 0)
    attend(i, "diag", 0, blk_rows, blk_rows)
    attend(i, "diag", blk_rows, t - blk_rows, t)

    for h in range(hb):
        acc = acc_ref[h]
        o = acc[:, :HEAD_DIM] / acc[:, HEAD_DIM:]
        zz = z_ref[0, :, h * HEAD_DIM:(h + 1) * HEAD_DIM].astype(F32)
        y_ref[0, :, h * HEAD_DIM:(h + 1) * HEAD_DIM] = (o * (zz * _sigmoid(zz))).astype(BF16)


def _flash_attention(q, k, v, z, dtab, n_heads, name, mask=None):
    (qa, cq), (ka, ck), (va, cv), (za, cz) = q, k, v, z
    b, s, _ = qa.shape
    t = ATT_TK
    assert s % t == 0
    nb = s // ATT_T
    hb = ATT_HB
    wg = hb * HEAD_DIM
    mode = "sel" if mask is None else "mask"
    kern = functools.partial(_flash_kernel, mode=mode, nb=nb, topb=min(MOBA_TOPB, nb - 1))
    in_specs = [pl.BlockSpec((1, t, wg), lambda bi, g, i: (bi, i, cq + g)),
                pl.BlockSpec((1, s, wg), lambda bi, g, i: (bi, 0, ck + g)),
                pl.BlockSpec((1, s, wg), lambda bi, g, i: (bi, 0, cv + g)),
                pl.BlockSpec((1, t, wg), lambda bi, g, i: (bi, i, cz + g)),
                pl.BlockSpec((2, hb, LANES, LANES), lambda bi, g, i: (0, g, 0, 0))]
    args = [qa, ka, va, za, dtab]
    scratch = []
    if mode == "mask":
        in_specs.append(pl.BlockSpec((1, t // ATT_T, nb, ATT_T, ATT_T), lambda bi, g, i: (bi, i, 0, 0, 0)))
        args.append(mask)
        kq = HEAD_DIM
    else:
        scratch.append(pltpu.VMEM((hb, LANES, HEAD_DIM), F32))
        kq = 2 * HEAD_DIM
    scratch += [pltpu.VMEM((hb, t, kq), BF16),
                pltpu.VMEM((hb, t, 2 * HEAD_DIM), F32),
                pltpu.VMEM((hb, t, LANES), F32)]
    return pl.pallas_call(
        kern,
        grid=(b, n_heads // hb, s // t),
        in_specs=in_specs,
        out_specs=pl.BlockSpec((1, t, wg), lambda bi, g, i: (bi, i, g)),
        out_shape=jax.ShapeDtypeStruct((b, s, n_heads * HEAD_DIM), BF16),
        scratch_shapes=scratch,
        compiler_params=_cparams(("parallel", "parallel", "arbitrary")),
        name=name,
    )(*args)


def _softplus(x):
    return jnp.maximum(x, 0.0) + jnp.log1p(jnp.exp(-jnp.abs(x)))


def _rglru_kernel(xr_ref, gz_ref, cw_ref, cb_ref, wa_ref, ba_ref, wx_ref, bx_ref, lam_ref, y_ref,
                  xbuf_ref, a_ref, b_ref, h_ref):
    tc = xr_ref.shape[1]
    c = pl.program_id(2)
    halo = SUBLANES

    @pl.when(c == 0)
    def _():
        xbuf_ref[0:halo, :] = jnp.zeros((halo, xbuf_ref.shape[1]), F32)
        h_ref[...] = jnp.zeros(h_ref.shape, F32)

    @pl.when(c > 0)
    def _():
        xbuf_ref[0:halo, :] = xbuf_ref[tc:tc + halo, :]

    xbuf_ref[halo:halo + tc, :] = xr_ref[0]
    xfull = xbuf_ref[...]
    u = xfull * cw_ref[0:1, :]
    for j in range(1, CONV_W):
        u = xfull * cw_ref[j:j + 1, :] + pltpu.roll(u, 1, 0)
    xc = (u + cb_ref[...])[halo:, :]

    xcb = xc.astype(BF16)
    tr = jnp.tanh(jnp.dot(xcb, wa_ref[0], preferred_element_type=F32) + ba_ref[...])
    ti = jnp.tanh(jnp.dot(xcb, wx_ref[0], preferred_element_type=F32) + bx_ref[...])
    half_c = (-0.5 * RG_C * LOG2E) * _softplus(-lam_ref[...])
    a = jnp.exp2(tr * half_c + half_c)
    one_m_a2 = 1.0 - a * a
    mult = jnp.where(one_m_a2 > 0.0, one_m_a2 * lax.rsqrt(one_m_a2), 0.0)
    bt = (mult * xc) * (0.5 * ti + 0.5)

    nseg = SUBLANES
    seg = tc // nseg
    pitch = seg + RG_SEG_PAD
    nslab = a_ref.shape[0]
    for sl in range(nslab):
        for sg in range(nseg):
            a_ref[sl, sg * pitch:sg * pitch + seg, :] = a[sg * seg:(sg + 1) * seg, sl * LANES:(sl + 1) * LANES]
            b_ref[sl, sg * pitch:sg * pitch + seg, :] = bt[sg * seg:(sg + 1) * seg, sl * LANES:(sl + 1) * LANES]

    def seg_step(j, carry):
        out = []
        for sl in range(nslab):
            hl, pr = carry[2 * sl], carry[2 * sl + 1]
            rows = pl.ds(j, nseg, stride=pitch)
            a8 = a_ref[sl, rows, :]
            hl = a8 * hl + b_ref[sl, rows, :]
            pr = a8 * pr
            b_ref[sl, rows, :] = hl
            a_ref[sl, rows, :] = pr
            out += [hl, pr]
        return tuple(out)

    init = tuple(v for _ in range(nslab)
                 for v in (jnp.zeros((nseg, LANES), F32), jnp.ones((nseg, LANES), F32)))
    ends = lax.fori_loop(0, seg, seg_step, init, unroll=RG_SCAN_UNROLL)

    gate = gz_ref[0].astype(F32)
    for sl in range(nslab):
        h_end, p_end = ends[2 * sl], ends[2 * sl + 1]
        h_in = h_ref[:, sl * LANES:(sl + 1) * LANES]
        for sg in range(nseg):
            rows = slice(sg * pitch, sg * pitch + seg)
            hs = b_ref[sl, rows, :] + a_ref[sl, rows, :] * h_in
            y_ref[0, sg * seg:(sg + 1) * seg, sl * LANES:(sl + 1) * LANES] = (
                hs * gate[sg * seg:(sg + 1) * seg, sl * LANES:(sl + 1) * LANES]).astype(BF16)
            h_in = p_end[sg:sg + 1, :] * h_in + h_end[sg:sg + 1, :]
        h_ref[:, sl * LANES:(sl + 1) * LANES] = h_in


def _rglru(xr, gz, conv_w, conv_b, wa_g, b_a, wx_g, b_x, lam):
    b, s, d = xr.shape
    g = wa_g.shape[0]
    cg = d // g
    tc = min(RG_TC, s)
    assert ((tc // SUBLANES + RG_SEG_PAD) // SUBLANES) % 2 == 1 and cg % LANES == 0
    row = lambda v: v.reshape(1, d).astype(F32)
    vec_spec = pl.BlockSpec((1, cg), lambda bi, gi, c: (0, gi))
    return pl.pallas_call(
        _rglru_kernel,
        grid=(b, g, s // tc),
        in_specs=[pl.BlockSpec((1, tc, cg), lambda bi, gi, c: (bi, c, gi)),
                  pl.BlockSpec((1, tc, cg), lambda bi, gi, c: (bi, c, gi)),
                  pl.BlockSpec((CONV_W, cg), lambda bi, gi, c: (0, gi)),
                  vec_spec,
                  pl.BlockSpec((1, cg, cg), lambda bi, gi, c: (gi, 0, 0)),
                  vec_spec,
                  pl.BlockSpec((1, cg, cg), lambda bi, gi, c: (gi, 0, 0)),
                  vec_spec,
                  vec_spec],
        out_specs=pl.BlockSpec((1, tc, cg), lambda bi, gi, c: (bi, c, gi)),
        out_shape=jax.ShapeDtypeStruct((b, s, d), BF16),
        scratch_shapes=[pltpu.VMEM((tc + SUBLANES, cg), F32),
                        pltpu.VMEM((cg // LANES, tc + SUBLANES * RG_SEG_PAD, LANES), F32),
                        pltpu.VMEM((cg // LANES, tc + SUBLANES * RG_SEG_PAD, LANES), F32),
                        pltpu.VMEM((1, cg), F32)],
        compiler_params=_cparams(("parallel", "parallel", "arbitrary")),
        name="rglru",
    )(xr, gz, conv_w.astype(F32), row(conv_b), wa_g * 0.5, row(b_a) * 0.5, wx_g * 0.5, row(b_x) * 0.5,
      row(lam))


def _group_block_diag(w, pair):
    nb, k, _ = w.shape
    out = None
    for p in range(pair):
        lo, hi = p * k, (pair - 1 - p) * k
        blk = jnp.pad(w[p::pair], ((0, 0), (lo, hi), (lo, hi)))
        out = blk if out is None else out + blk
    return out


def _attention_layer(x2d, b, s, norm_g, w_in, kv_g, w_uk, w_uv, idx_g, idx_b, w_out, rel_bias):
    d_model = x2d.shape[1]
    n_heads = w_uk.shape[0]
    wq = n_heads * HEAD_DIM
    assert N_IDX_HEADS * IDX_DIM == wq
    sizes = (wq, KV_LORA, wq, IDX_DIM, N_IDX_HEADS, wq, wq, wq, 2 * wq)
    offs = [int(o) for o in np.concatenate([[0], np.cumsum(sizes)])]
    wt = w_in.T.astype(BF16)
    zrows = lambda n: jnp.zeros((n, d_model), BF16)
    wt_small = jnp.concatenate([wt[offs[1]:offs[2]], wt[offs[3]:offs[4]], zrows(LANES - IDX_DIM),
                                wt[offs[4]:offs[5]], zrows(LANES - N_IDX_HEADS)], axis=0)

    h = _rmsnorm(x2d, norm_g, BF16, "rmsnorm0")
    pqa = _matmul_nt(h, wt, offs[0], wq, BF16, "proj_attn_qa").reshape(b, s, wq)
    pqi = _matmul_nt(h, wt, offs[2], wq, BF16, "proj_attn_qi").reshape(b, s, wq)
    ptail = _matmul_nt(h, wt, offs[5], 5 * wq, BF16, "proj_attn_tail").reshape(b, s, 5 * wq)
    psmall = _matmul_nt(h, wt_small, 0, wt_small.shape[0], F32, "proj_attn_small").reshape(b, s, -1)

    ckv, kbd, widx = _dsa_prep(psmall, kv_g, idx_g, idx_b)
    mask = _dsa_index(pqi, 0, kbd, widx)
    w_kv = jnp.concatenate([jnp.transpose(w_uk, (1, 0, 2)).reshape(KV_LORA, wq),
                            jnp.transpose(w_uv, (1, 0, 2)).reshape(KV_LORA, wq)], axis=1).astype(BF16)
    kv = _matmul([ckv.reshape(b * s, KV_LORA)], [w_kv], BF16, "dsa_kv_up", bm=2 * MM_BM).reshape(b, s, 2 * wq)

    dtab_a = _bias_blocks(rel_bias[:, :n_heads])
    dtab_b = _bias_blocks(rel_bias[:, n_heads:])
    gpw = n_heads // ATT_HB
    ya = _flash_attention((pqa, 0), (kv, 0), (kv, gpw), (ptail, 3 * gpw), dtab_a, n_heads,
                          "dsa_attention", mask=mask)
    yb = _flash_attention((ptail, 0), (ptail, gpw), (ptail, 2 * gpw), (ptail, 4 * gpw), dtab_b,
                          n_heads, "moba_attention")

    w_out_b = w_out.astype(BF16)
    return _matmul([ya.reshape(b * s, wq), yb.reshape(b * s, wq)], [(w_out_b, 0, 0), (w_out_b, 1, 0)],
                   F32, "proj_attn_out", res=x2d)


def _recurrent_layer(x2d, b, s, norm_g, w_in, conv_w, conv_b, w_a, b_a, w_x, b_x, lam, w_out):
    d_rnn = conv_b.shape[0]
    h = _rmsnorm(x2d, norm_g, BF16, "rmsnorm1")
    w_in_b = w_in.astype(BF16)
    assert d_rnn % MM_BN == 0
    xr = _matmul([h], [(w_in_b, 0, 0)], F32, "proj_rec_x", n=d_rnn)
    gz = _matmul([h], [(w_in_b, 0, d_rnn // MM_BN)], BF16, "proj_rec_z", n=d_rnn, epilogue=_silu)
    y = _rglru(xr.reshape(b, s, d_rnn), gz.reshape(b, s, d_rnn), conv_w, conv_b,
               _group_block_diag(w_a.astype(BF16), RG_PAIR), b_a,
               _group_block_diag(w_x.astype(BF16), RG_PAIR), b_x, lam)
    return _matmul([y.reshape(b * s, d_rnn)], [w_out.astype(BF16)], F32, "proj_rec_out",
                   res=x2d, bn=MM_BN // 2)


def kernel(x, norm_g, final_g, rel_bias, attn_w_in, attn_kv_g, attn_w_uk, attn_w_uv, idx_k_g, idx_k_b,
           attn_w_out, rec_w_in, rec_conv_w, rec_conv_b, rec_w_a, rec_b_a, rec_w_x, rec_b_x,
           rec_lambda, rec_w_out):
    b, s, d = x.shape
    depth = norm_g.shape[0]
    x2d = x.reshape(b * s, d)
    for layer in range(depth):
        li = layer // 2
        if layer % 2 == 0:
            x2d = _attention_layer(x2d, b, s, norm_g[layer], attn_w_in[li], attn_kv_g[li],
                                   attn_w_uk[li], attn_w_uv[li], idx_k_g[li], idx_k_b[li],
                                   attn_w_out[li], rel_bias)
        else:
            x2d = _recurrent_layer(x2d, b, s, norm_g[layer], rec_w_in[li], rec_conv_w[li],
                                   rec_conv_b[li], rec_w_a[li], rec_b_a[li], rec_w_x[li],
                                   rec_b_x[li], rec_lambda[li], rec_w_out[li])
    return _rmsnorm(x2d, final_g, x.dtype, "rmsnorm_final").reshape(b, s, d)
```

```python
import functools
import math

import numpy as np
import jax
import jax.numpy as jnp
from jax import lax
from jax.experimental import pallas as pl
from jax.experimental.pallas import tpu as pltpu

F32 = jnp.float32
BF16 = jnp.bfloat16
I32 = jnp.int32

HEAD_DIM = 128
KV_LORA = 512
N_IDX_HEADS = 32
IDX_DIM = 64
TOPK_MAX = 256
MOBA_BLOCK = 256
MOBA_TOPB = 3
RG_BLOCKS = 16
CONV_W = 4
RG_C = 8.0
NUM_BUCKETS = 32
MAX_DISTANCE = 128
EPS = 1e-6
ATTN_SCALE = HEAD_DIM ** -0.5
IDX_SCALE = (N_IDX_HEADS ** -0.5) * (IDX_DIM ** -0.5)
LOG2E = math.log2(math.e)

LANES = 128
SUBLANES = 8
WT_ROW_ALIGN = 16
VMEM_LIMIT_BYTES = 56 * 2 ** 20
MASK_NEG = -1e30
INT_MIN = np.int32(-2 ** 31)
KEY_BYTES = 4

ROW_TILE = 512
MM_BM = 1024
MM_BN = 1024
ATT_T = 256
ATT_TK = 2 * ATT_T
ATT_HB = 4
IDX_TQ = 2 * ATT_T
RG_PAIR = 2
RG_TC = 1024
RG_SEG_PAD = 8
RG_SCAN_UNROLL = 4
assert ATT_T == MOBA_BLOCK and ATT_T == 2 * LANES


def _cparams(sem):
    return pltpu.CompilerParams(dimension_semantics=sem, vmem_limit_bytes=VMEM_LIMIT_BYTES)


def _sigmoid(x):
    return 0.5 * jnp.tanh(0.5 * x) + 0.5


def _tile_rows(x, n):
    return x if n == 1 else jnp.concatenate([x] * n, axis=0)


def _tile_lanes(x, n):
    return x if n == 1 else jnp.concatenate([x] * n, axis=1)


def _rmsnorm_kernel(x_ref, g_ref, o_ref):
    x = x_ref[...]
    ms = jnp.mean(x * x, axis=-1, keepdims=True)
    o_ref[...] = ((x * lax.rsqrt(ms + EPS)) * g_ref[...]).astype(o_ref.dtype)


def _rmsnorm(x2d, g, out_dtype, name):
    m, d = x2d.shape
    bm = min(ROW_TILE, m)
    return pl.pallas_call(
        _rmsnorm_kernel,
        grid=(m // bm,),
        in_specs=[pl.BlockSpec((bm, d), lambda i: (i, 0)),
                  pl.BlockSpec((1, d), lambda i: (0, 0))],
        out_specs=pl.BlockSpec((bm, d), lambda i: (i, 0)),
        out_shape=jax.ShapeDtypeStruct((m, d), out_dtype),
        compiler_params=_cparams(("parallel",)),
        name=name,
    )(x2d, g.reshape(1, d).astype(F32))


def _mm_kernel(*refs, n_parts, has_res, epilogue):
    o_ref = refs[-1]
    acc = jnp.dot(refs[0][...], refs[n_parts][...], preferred_element_type=F32)
    for p in range(1, n_parts):
        acc = acc + jnp.dot(refs[p][...], refs[n_parts + p][...], preferred_element_type=F32)
    if has_res:
        acc = refs[2 * n_parts][...] + acc
    if epilogue is not None:
        acc = epilogue(acc)
    o_ref[...] = acc.astype(o_ref.dtype)


def _matmul(xs, ws, out_dtype, name, res=None, n=None, bm=MM_BM, bn=MM_BN, epilogue=None):
    ws = [w if isinstance(w, tuple) else (w, 0, 0) for w in ws]
    m = xs[0].shape[0]
    n = ws[0][0].shape[1] if n is None else n
    bm = min(bm, m)
    bn = min(bn, n)
    assert m % bm == 0 and n % bn == 0
    in_specs = [pl.BlockSpec((bm, x.shape[1]), lambda i, j: (i, 0)) for x in xs]
    for x, (w, rb, cb) in zip(xs, ws):
        assert w.shape[0] % x.shape[1] == 0 and (cb + n // bn) * bn <= w.shape[1]
        in_specs.append(pl.BlockSpec((x.shape[1], bn), lambda i, j, rb=rb, cb=cb: (rb, cb + j)))
    args = [*xs, *[w for w, _, _ in ws]]
    if res is not None:
        in_specs.append(pl.BlockSpec((bm, bn), lambda i, j: (i, j)))
        args.append(res)
    return pl.pallas_call(
        functools.partial(_mm_kernel, n_parts=len(xs), has_res=res is not None, epilogue=epilogue),
        grid=(m // bm, n // bn),
        in_specs=in_specs,
        out_specs=pl.BlockSpec((bm, bn), lambda i, j: (i, j)),
        out_shape=jax.ShapeDtypeStruct((m, n), out_dtype),
        compiler_params=_cparams(("parallel", "parallel")),
        name=name,
    )(*args)


def _silu(x):
    return x * _sigmoid(x)


def _mm_nt_kernel(x_ref, wt_ref, o_ref, *, epilogue):
    acc = lax.dot_general(x_ref[...], wt_ref[...], (((1,), (1,)), ((), ())), preferred_element_type=F32)
    if epilogue is not None:
        acc = epilogue(acc)
    o_ref[...] = acc.astype(o_ref.dtype)


def _matmul_nt(x, wt, row0, n, out_dtype, name, bn=MM_BN, epilogue=None):
    m, k = x.shape
    bm = min(MM_BM, m)
    bn = min(bn, n)
    assert m % bm == 0 and n % bn == 0 and row0 % WT_ROW_ALIGN == 0 and bn % WT_ROW_ALIGN == 0
    assert row0 + n <= wt.shape[0] and wt.shape[1] == k
    return pl.pallas_call(
        functools.partial(_mm_nt_kernel, epilogue=epilogue),
        grid=(m // bm, n // bn),
        in_specs=[pl.BlockSpec((bm, k), lambda i, j: (i, 0)),
                  pl.BlockSpec((pl.Element(bn), pl.Element(k)),
                               lambda i, j: (pl.multiple_of(row0 + j * bn, WT_ROW_ALIGN), 0))],
        out_specs=pl.BlockSpec((bm, bn), lambda i, j: (i, j)),
        out_shape=jax.ShapeDtypeStruct((m, n), out_dtype),
        compiler_params=_cparams(("parallel", "parallel")),
        name=name,
    )(x, wt)


def _t5_bucket_np(dist):
    d = np.maximum(dist, 0)
    max_exact = NUM_BUCKETS // 2
    d_f = np.maximum(d, 1).astype(np.float32)
    ratio = np.log(d_f / np.float32(max_exact)) / np.float32(math.log(MAX_DISTANCE / max_exact))
    large = max_exact + (ratio * np.float32(NUM_BUCKETS - max_exact)).astype(np.int32)
    large = np.minimum(large, NUM_BUCKETS - 1)
    return np.where(d < max_exact, d, large)


def _bias_blocks(tab):
    n = LANES
    assert np.all(_t5_bucket_np(np.arange(n, 64 * n)) == NUM_BUCKETS - 1)
    h = tab.shape[1]
    rel = (tab - tab[NUM_BUCKETS - 1]).astype(F32) * LOG2E
    out = []
    for off in (0, n):
        buckets = _t5_bucket_np(off + n - 1 - np.arange(2 * n - 1))
        onehot = np.zeros((2 * n, NUM_BUCKETS), np.float32)
        onehot[np.arange(2 * n - 1), buckets] = 1.0
        w = jnp.dot(jnp.asarray(onehot), rel, precision=lax.Precision.HIGHEST).T
        t = jnp.tile(w, (1, n))[:, :n * (2 * n - 1)].reshape(h, n, 2 * n - 1)
        out.append(t[:, :, n - 1:])
    return jnp.stack(out)


def _dsa_prep_kernel(p_ref, kvg_ref, ig_ref, ib_ref, ckv_ref, kbd_ref, w_ref):
    p = p_ref[0]
    ts = p.shape[0]
    c = p[:, :KV_LORA]
    cn = (c * lax.rsqrt(jnp.mean(c * c, axis=-1, keepdims=True) + EPS)) * kvg_ref[...]
    ckv_ref[0] = cn.astype(BF16)
    k = p[:, KV_LORA:KV_LORA + IDX_DIM]
    mu = jnp.mean(k, axis=-1, keepdims=True)
    var = jnp.mean(jnp.square(k - mu), axis=-1, keepdims=True)
    kn = ((k - mu) * lax.rsqrt(var + EPS)) * ig_ref[...] + ib_ref[...]
    zeros = jnp.zeros((ts, LANES - IDX_DIM), F32)
    top = jnp.concatenate([kn, zeros], axis=1).T
    bot = jnp.concatenate([zeros, kn], axis=1).T
    kbd_ref[0, 0] = jnp.concatenate([top, bot], axis=1).astype(BF16)
    w_ref[0] = p[:, KV_LORA + LANES:KV_LORA + LANES + N_IDX_HEADS] * IDX_SCALE


def _dsa_prep(ps, kv_g, idx_g, idx_b):
    b, s, wid = ps.shape
    ts = ATT_T
    nkt = s // ts
    return pl.pallas_call(
        _dsa_prep_kernel,
        grid=(b, nkt),
        in_specs=[pl.BlockSpec((1, ts, wid), lambda bi, i: (bi, i, 0)),
                  pl.BlockSpec((1, KV_LORA), lambda bi, i: (0, 0)),
                  pl.BlockSpec((1, IDX_DIM), lambda bi, i: (0, 0)),
                  pl.BlockSpec((1, IDX_DIM), lambda bi, i: (0, 0))],
        out_specs=[pl.BlockSpec((1, ts, KV_LORA), lambda bi, i: (bi, i, 0)),
                   pl.BlockSpec((1, 1, LANES, 2 * ts), lambda bi, i: (bi, i, 0, 0)),
                   pl.BlockSpec((1, ts, N_IDX_HEADS), lambda bi, i: (bi, i, 0))],
        out_shape=[jax.ShapeDtypeStruct((b, s, KV_LORA), BF16),
                   jax.ShapeDtypeStruct((b, nkt, LANES, 2 * ts), BF16),
                   jax.ShapeDtypeStruct((b, s, N_IDX_HEADS), F32)],
        compiler_params=_cparams(("parallel", "parallel")),
        name="dsa_prep",
    )(ps, kv_g.reshape(1, -1).astype(F32), idx_g.reshape(1, -1).astype(F32),
      idx_b.reshape(1, -1).astype(F32))


def _dsa_index_kernel(qi_ref, w_ref, kbd_ref, mask_ref, key_ref, plane_ref, wb_ref, t_ref, j_ref,
                      *, topk, nkt_total, seq_bits):
    tq, tk = IDX_TQ, ATT_T
    rep = tk // LANES
    sl = SUBLANES
    i = pl.program_id(1)
    t0 = i * tq
    nkt = (i + 1) * (tq // tk)

    row = t0 + lax.broadcasted_iota(I32, (tq, tk), 0)
    col = lax.broadcasted_iota(I32, (tq, tk), 1)

    def to_key(score):
        score = jnp.where(score == 0.0, 0.0, score)
        bits = lax.bitcast_convert_type(score, I32)
        return bits ^ ((bits >> 31) & np.int32(0x7FFFFFFF))

    for h in range(N_IDX_HEADS):
        wb_ref[h] = jnp.broadcast_to(w_ref[0, :, h:h + 1], (tq, LANES))

    def idx_tile(kt, q0, nq):
        kb = kbd_ref[0, kt]
        acc = jnp.zeros((nq, tk), F32)
        for hp in range(N_IDX_HEADS // 2):
            rel = jnp.dot(qi_ref[0, q0:q0 + nq, hp * LANES:(hp + 1) * LANES], kb,
                          preferred_element_type=F32)
            acc = acc + jnp.maximum(rel[:, :tk], 0.0) * _tile_lanes(wb_ref[2 * hp, q0:q0 + nq], rep)
            acc = acc + jnp.maximum(rel[:, tk:], 0.0) * _tile_lanes(wb_ref[2 * hp + 1, q0:q0 + nq], rep)
        q_sub = t0 + q0 + lax.broadcasted_iota(I32, (nq, tk), 0)
        k_lane = kt * tk + lax.broadcasted_iota(I32, (nq, tk), 1)
        k_sub = kt * tk + lax.broadcasted_iota(I32, (tk, nq), 0)
        q_lane = t0 + q0 + lax.broadcasted_iota(I32, (tk, nq), 1)
        if q0:
            key_ref[kt, 0:q0] = jnp.full((q0, tk), INT_MIN, I32)
        key_ref[kt, q0:q0 + nq] = jnp.where(k_lane <= q_sub, to_key(acc), INT_MIN)
        ukey = jnp.where(k_sub <= q_lane, to_key(acc.T), INT_MIN) ^ INT_MIN
        for bi in range(KEY_BYTES):
            byte = (ukey >> (8 * (KEY_BYTES - 1 - bi))) & np.int32(0xFF)
            if q0:
                plane_ref[bi, kt, :, 0:q0] = jnp.zeros((tk, q0), BF16)
            plane_ref[bi, kt, :, q0:q0 + nq] = byte.astype(F32).astype(BF16)

    def idx_body(kt, carry):
        idx_tile(kt, 0, tq)
        return carry

    lax.fori_loop(0, nkt - 1, idx_body, 0)
    idx_tile(nkt - 1, tq - tk, tk)

    one, zero = jnp.ones((), BF16), jnp.zeros((), BF16)

    def count(plane, pred_fn):
        rows = 2 * sl
        def body(kt, c):
            xs = [pred_fn(plane_ref[plane, kt, r * rows:(r + 1) * rows, :], kt * tk + r * rows)
                  for r in range(tk // rows)]
            while len(xs) > 1:
                xs = [xs[a] + xs[a + 1] for a in range(0, len(xs), 2)]
            return c + xs[0]
        c = lax.fori_loop(0, nkt, body, jnp.zeros((rows, tq), BF16))
        return jnp.broadcast_to(jnp.sum(c.astype(F32), axis=0, keepdims=True), (sl, tq))

    def rep16(v8):
        return jnp.concatenate([v8, v8], axis=0).astype(BF16)

    target = jnp.full((sl, tq), float(topk), F32)
    thr_u = jnp.zeros((sl, tq), I32)
    for bi in range(KEY_BYTES):
        t_ref[...] = jnp.zeros((sl, tq), F32)

        def bit_body(it, carry, bi=bi, target=target):
            cand = t_ref[...] + jnp.left_shift(jnp.int32(1), 7 - it).astype(F32)
            c16 = rep16(cand)
            n_ge = count(bi, lambda d, s0: jnp.where(d >= c16, one, zero))
            t_ref[...] = jnp.where(n_ge >= target, cand, t_ref[...])
            return carry

        lax.fori_loop(0, 8, bit_body, 0)
        tb = t_ref[...]
        tb16 = rep16(tb)
        thr_u = (thr_u << 8) | tb.astype(I32)
        if bi + 1 < KEY_BYTES:
            target = target - count(bi, lambda d, s0: jnp.where(d > tb16, one, zero))

            def narrow_body(kt, carry, bi=bi):
                plane_ref[bi + 1, kt] = jnp.where(plane_ref[bi, kt] == _tile_rows(tb16, tk // (2 * sl)),
                                                  plane_ref[bi + 1, kt], -one)
                return carry

            lax.fori_loop(0, nkt, narrow_body, 0)

    last = KEY_BYTES - 1
    n_gt = count(last, lambda d, s0: jnp.where(d > tb16, one, zero))
    n_ge = count(last, lambda d, s0: jnp.where(d >= tb16, one, zero))
    need = target - n_gt
    j_ref[...] = jnp.full((sl, tq), 2 ** 30, I32)
    sub16 = lax.broadcasted_iota(I32, (2 * sl, tq), 0)

    @pl.when(jnp.max(n_ge - target) > 0.0)
    def _():
        j_ref[...] = jnp.zeros((sl, tq), I32)

        def jbit_body(it, carry):
            cand = j_ref[...] | jnp.left_shift(jnp.int32(1), seq_bits - 1 - it)
            c2 = jnp.concatenate([cand, cand], axis=0)
            n_lt = count(last, lambda d, s0: jnp.where(
                d == tb16, jnp.where(s0 + sub16 < c2, 1.0, 0.0).astype(BF16), zero))
            j_ref[...] = jnp.where(n_lt < need, cand, j_ref[...])
            return carry

        lax.fori_loop(0, seq_bits, jbit_body, 0)

    thr = thr_u ^ INT_MIN

    def per_row(v8):
        wide = jnp.broadcast_to(lax.bitcast_convert_type(v8[0:1], F32), (LANES, tq))
        return _tile_lanes(lax.bitcast_convert_type(wide.T, I32), rep)

    thr_t = per_row(thr)
    jmax_t = per_row(j_ref[...])

    def mask_body(kt, carry):
        k = key_ref[kt]
        s_idx = kt * tk + col
        keep_tie = jnp.where(k == thr_t, jnp.where(s_idx <= jmax_t, 0.0, MASK_NEG), MASK_NEG)
        keep = jnp.where(k > thr_t, 0.0, keep_tie)
        full = jnp.where(s_idx <= row, keep, MASK_NEG).astype(BF16)
        for r in range(tq // tk):
            mask_ref[0, r, kt] = full[r * tk:(r + 1) * tk]
        return carry

    lax.fori_loop(0, nkt, mask_body, 0)

    def fill_body(kt, carry):
        for r in range(tq // tk):
            mask_ref[0, r, kt] = jnp.full((tk, tk), MASK_NEG, BF16)
        return carry

    lax.fori_loop(nkt, nkt_total, fill_body, 0)


def _dsa_index(pqi, col_qi, kbd, widx):
    b, s, _ = pqi.shape
    tq, tk = IDX_TQ, ATT_T
    nkt = s // tk
    wq = N_IDX_HEADS * IDX_DIM
    topk = min(TOPK_MAX, s // 4)
    assert s // (2 * SUBLANES) <= 256
    kern = functools.partial(_dsa_index_kernel, topk=topk, nkt_total=nkt,
                             seq_bits=max(1, (s - 1).bit_length()))
    return pl.pallas_call(
        kern,
        grid=(b, s // tq),
        in_specs=[pl.BlockSpec((1, tq, wq), lambda bi, i: (bi, i, col_qi)),
                  pl.BlockSpec((1, tq, N_IDX_HEADS), lambda bi, i: (bi, i, 0)),
                  pl.BlockSpec((1, nkt, LANES, 2 * tk), lambda bi, i: (bi, 0, 0, 0))],
        out_specs=pl.BlockSpec((1, tq // tk, nkt, tk, tk), lambda bi, i: (bi, i, 0, 0, 0)),
        out_shape=jax.ShapeDtypeStruct((b, s // tk, nkt, tk, tk), BF16),
        scratch_shapes=[pltpu.VMEM((nkt, tq, tk), I32),
                        pltpu.VMEM((KEY_BYTES, nkt, tk, tq), BF16),
                        pltpu.VMEM((N_IDX_HEADS, tq, LANES), F32),
                        pltpu.VMEM((SUBLANES, tq), F32),
                        pltpu.VMEM((SUBLANES, tq), I32)],
        compiler_params=_cparams(("parallel", "parallel")),
        name="dsa_index",
    )(pqi, widx, kbd)


def _flash_kernel(*refs, mode, nb, topb):
    if mode == "mask":
        q_ref, k_ref, v_ref, z_ref, d_ref, mask_ref, y_ref = refs[:7]
    else:
        q_ref, k_ref, v_ref, z_ref, d_ref, y_ref, km_ref = refs[:7]
    qs_ref, acc_ref, m_ref = refs[7:]
    blk_rows = ATT_T
    t = ATT_TK
    hb = q_ref.shape[2] // HEAD_DIM
    rep = t // LANES
    i = pl.program_id(2)

    if mode == "sel":
        @pl.when(i == 0)
        def _():
            km_ref[...] = jnp.zeros(km_ref.shape, F32)
            for h in range(hb):
                for n in range(nb):
                    kblk = k_ref[0, n * blk_rows:(n + 1) * blk_rows,
                                 h * HEAD_DIM:(h + 1) * HEAD_DIM].astype(F32)
                    km_ref[h, n:n + 1, :] = jnp.mean(kblk, axis=0, keepdims=True)
        nbp = -(-nb // SUBLANES) * SUBLANES
        blk = lax.broadcasted_iota(I32, (nbp, t), 0).astype(F32)
        own = (2 * i + lax.broadcasted_iota(I32, (nbp, t), 1) // blk_rows).astype(F32)

    for h in range(hb):
        q = q_ref[0, :, h * HEAD_DIM:(h + 1) * HEAD_DIM]
        qsc = (q.astype(F32) * (ATTN_SCALE * LOG2E)).astype(BF16)
        if mode == "sel":
            gate = lax.dot_general(km_ref[h, 0:nbp, :].astype(BF16), q, (((1,), (1,)), ((), ())),
                                   preferred_element_type=F32)
            gate = jnp.where(blk < own, gate, -jnp.inf)
            sel = jnp.full((nbp, t), MASK_NEG, F32)
            for _ in range(topb):
                gmax = jnp.max(gate, axis=0, keepdims=True)
                first = jnp.min(jnp.where(gate == gmax, blk, float(nbp)), axis=0, keepdims=True)
                hit = jnp.where(blk == first, jnp.where(gmax > -jnp.inf, 1.0, 0.0), 0.0)
                sel = jnp.where(hit > 0.0, 0.0, sel)
                gate = jnp.where(blk == first, -jnp.inf, gate)
            sel = jnp.where(blk == own, 0.0, sel)
            selb = jnp.concatenate([sel, jnp.full((LANES - nbp, t), MASK_NEG, F32)], axis=0).T
            qs_ref[h] = jnp.concatenate([qsc, selb.astype(BF16)], axis=1)
        else:
            qs_ref[h] = qsc

    def add_bias(x, h, kind, j, row_blk0):
        d0, d1 = d_ref[0, h], d_ref[1, h]
        blk_of = lambda r, c: x[r * LANES:(r + 1) * LANES, c * LANES:(c + 1) * LANES]
        rows = []
        for r in range(x.shape[0] // LANES):
            cols = []
            for c in range(x.shape[1] // LANES):
                b = blk_of(r, c)
                if kind == "past":
                    if row_blk0 + r == 0 and c == rep - 1:
                        b = b + d1 * jnp.where(j == i - 1, 1.0, 0.0)
                elif row_blk0 + r == c:
                    b = b + d0
                elif row_blk0 + r == c + 1:
                    b = b + d1
                cols.append(b)
            rows.append(jnp.concatenate(cols, axis=1))
        return jnp.concatenate(rows, axis=0)

    def attend(j, kind, row0=0, nrows=t, ncols=t, first=False):
        r0 = pl.multiple_of(j * t, t)
        rsl = slice(row0, row0 + nrows)
        ones_blk = jnp.ones((ncols, LANES), BF16)
        if mode == "mask":
            mask_tile = jnp.concatenate(
                [jnp.concatenate([mask_ref[0, r, 2 * j + c] for c in range(ncols // blk_rows)], axis=1)
                 for r in range(row0 // blk_rows, (row0 + nrows) // blk_rows)], axis=0).astype(F32)
        else:
            lane = lax.broadcasted_iota(I32, (ncols, LANES), 1)
            second_half = lax.broadcasted_iota(I32, (ncols, LANES), 0) >= blk_rows
            onehot = jnp.where(lane == jnp.where(second_half, 2 * j + 1, 2 * j), 1.0, 0.0).astype(BF16)
            if kind == "diag":
                row = row0 + lax.broadcasted_iota(I32, (nrows, ncols), 0)
                col = lax.broadcasted_iota(I32, (nrows, ncols), 1)
        for h in range(hb):
            kblk = k_ref[0, pl.ds(r0, ncols), h * HEAD_DIM:(h + 1) * HEAD_DIM]
            vblk = v_ref[0, pl.ds(r0, ncols), h * HEAD_DIM:(h + 1) * HEAD_DIM]
            if mode == "sel":
                kblk = jnp.concatenate([kblk, onehot], axis=1)
            x = lax.dot_general(qs_ref[h, rsl], kblk, (((1,), (1,)), ((), ())),
                                preferred_element_type=F32)
            if mode == "mask":
                x = x + mask_tile
            x = add_bias(x, h, kind, j, row0 // LANES)
            if kind == "diag" and mode == "sel":
                x = jnp.where(col <= row, x, MASK_NEG)
            m_new = jnp.broadcast_to(jnp.max(x, axis=1, keepdims=True), (nrows, LANES))
            if not first:
                m_prev = m_ref[h, rsl]
                m_new = jnp.maximum(m_prev, m_new)
            p = jnp.exp2(x - _tile_lanes(m_new, ncols // LANES)).astype(BF16)
            m_ref[h, rsl] = m_new
            pv = jnp.dot(p, jnp.concatenate([vblk, ones_blk], axis=1), preferred_element_type=F32)
            if first:
                acc_ref[h, rsl] = pv
            else:
                acc_ref[h, rsl] = acc_ref[h, rsl] * _tile_lanes(jnp.exp2(m_prev - m_new), 2) + pv

    def past_body(j, carry):
        attend(j, "past")
        return carry

    attend(i, "diag", 0, blk_rows, blk_rows, first=True)
    attend(i, "diag", blk_rows, t - blk_rows, t, first=True)
    lax.fori_loop(0, i, past_body, 0)

    for h in range(hb):
        acc = acc_ref[h]
        o = acc[:, :HEAD_DIM] / acc[:, HEAD_DIM:]
        zz = z_ref[0, :, h * HEAD_DIM:(h + 1) * HEAD_DIM].astype(F32)
        y_ref[0, :, h * HEAD_DIM:(h + 1) * HEAD_DIM] = (o * (zz * _sigmoid(zz))).astype(BF16)


def _flash_attention(q, k, v, z, dtab, n_heads, name, mask=None):
    (qa, cq), (ka, ck), (va, cv), (za, cz) = q, k, v, z
    b, s, _ = qa.shape
    t = ATT_TK
    assert s % t == 0
    nb = s // ATT_T
    hb = ATT_HB
    wg = hb * HEAD_DIM
    mode = "sel" if mask is None else "mask"
    kern = functools.partial(_flash_kernel, mode=mode, nb=nb, topb=min(MOBA_TOPB, nb - 1))
    in_specs = [pl.BlockSpec((1, t, wg), lambda bi, g, i: (bi, i, cq + g)),
                pl.BlockSpec((1, s, wg), lambda bi, g, i: (bi, 0, ck + g)),
                pl.BlockSpec((1, s, wg), lambda bi, g, i: (bi, 0, cv + g)),
                pl.BlockSpec((1, t, wg), lambda bi, g, i: (bi, i, cz + g)),
                pl.BlockSpec((2, hb, LANES, LANES), lambda bi, g, i: (0, g, 0, 0))]
    args = [qa, ka, va, za, dtab]
    scratch = []
    if mode == "mask":
        in_specs.append(pl.BlockSpec((1, t // ATT_T, nb, ATT_T, ATT_T), lambda bi, g, i: (bi, i, 0, 0, 0)))
        args.append(mask)
        kq = HEAD_DIM
    else:
        scratch.append(pltpu.VMEM((hb, LANES, HEAD_DIM), F32))
        kq = 2 * HEAD_DIM
    scratch += [pltpu.VMEM((hb, t, kq), BF16),
                pltpu.VMEM((hb, t, 2 * HEAD_DIM), F32),
                pltpu.VMEM((hb, t, LANES), F32)]
    return pl.pallas_call(
        kern,
        grid=(b, n_heads // hb, s // t),
        in_specs=in_specs,
        out_specs=pl.BlockSpec((1, t, wg), lambda bi, g, i: (bi, i, g)),
        out_shape=jax.ShapeDtypeStruct((b, s, n_heads * HEAD_DIM), BF16),
        scratch_shapes=scratch,
        compiler_params=_cparams(("parallel", "parallel", "arbitrary")),
        name=name,
    )(*args)


def _softplus(x):
    return jnp.maximum(x, 0.0) + jnp.log1p(jnp.exp(-jnp.abs(x)))


def _rglru_kernel(xr_ref, gz_ref, cw_ref, cb_ref, wa_ref, ba_ref, wx_ref, bx_ref, lam_ref, y_ref,
                  xbuf_ref, a_ref, b_ref, h_ref):
    tc = xr_ref.shape[1]
    c = pl.program_id(2)
    halo = SUBLANES

    @pl.when(c == 0)
    def _():
        xbuf_ref[0:halo, :] = jnp.zeros((halo, xbuf_ref.shape[1]), F32)
        h_ref[...] = jnp.zeros(h_ref.shape, F32)

    @pl.when(c > 0)
    def _():
        xbuf_ref[0:halo, :] = xbuf_ref[tc:tc + halo, :]

    xbuf_ref[halo:halo + tc, :] = xr_ref[0]
    xfull = xbuf_ref[...]
    u = xfull * cw_ref[0:1, :]
    for j in range(1, CONV_W):
        u = xfull * cw_ref[j:j + 1, :] + pltpu.roll(u, 1, 0)
    xc = (u + cb_ref[...])[halo:, :]

    xcb = xc.astype(BF16)
    tr = jnp.tanh(jnp.dot(xcb, wa_ref[0], preferred_element_type=F32) + ba_ref[...])
    ti = jnp.tanh(jnp.dot(xcb, wx_ref[0], preferred_element_type=F32) + bx_ref[...])
    half_c = (-0.5 * RG_C * LOG2E) * _softplus(-lam_ref[...])
    a = jnp.exp2(tr * half_c + half_c)
    one_m_a2 = 1.0 - a * a
    mult = jnp.where(one_m_a2 > 0.0, one_m_a2 * lax.rsqrt(one_m_a2), 0.0)
    bt = (mult * xc) * (0.5 * ti + 0.5)

    nseg = SUBLANES
    seg = tc // nseg
    pitch = seg + RG_SEG_PAD
    nslab = a_ref.shape[0]
    for sl in range(nslab):
        for sg in range(nseg):
            a_ref[sl, sg * pitch:sg * pitch + seg, :] = a[sg * seg:(sg + 1) * seg, sl * LANES:(sl + 1) * LANES]
            b_ref[sl, sg * pitch:sg * pitch + seg, :] = bt[sg * seg:(sg + 1) * seg, sl * LANES:(sl + 1) * LANES]

    def seg_step(j, carry):
        out = []
        for sl in range(nslab):
            hl, pr = carry[2 * sl], carry[2 * sl + 1]
            rows = pl.ds(j, nseg, stride=pitch)
            a8 = a_ref[sl, rows, :]
            hl = a8 * hl + b_ref[sl, rows, :]
            pr = a8 * pr
            b_ref[sl, rows, :] = hl
            a_ref[sl, rows, :] = pr
            out += [hl, pr]
        return tuple(out)

    init = tuple(v for _ in range(nslab)
                 for v in (jnp.zeros((nseg, LANES), F32), jnp.ones((nseg, LANES), F32)))
    ends = lax.fori_loop(0, seg, seg_step, init, unroll=RG_SCAN_UNROLL)

    gate = gz_ref[0].astype(F32)
    for sl in range(nslab):
        h_end, p_end = ends[2 * sl], ends[2 * sl + 1]
        h_in = h_ref[:, sl * LANES:(sl + 1) * LANES]
        for sg in range(nseg):
            rows = slice(sg * pitch, sg * pitch + seg)
            hs = b_ref[sl, rows, :] + a_ref[sl, rows, :] * h_in
            y_ref[0, sg * seg:(sg + 1) * seg, sl * LANES:(sl + 1) * LANES] = (
                hs * gate[sg * seg:(sg + 1) * seg, sl * LANES:(sl + 1) * LANES]).astype(BF16)
            h_in = p_end[sg:sg + 1, :] * h_in + h_end[sg:sg + 1, :]
        h_ref[:, sl * LANES:(sl + 1) * LANES] = h_in


def _rglru(xr, gz, conv_w, conv_b, wa_g, b_a, wx_g, b_x, lam):
    b, s, d = xr.shape
    g = wa_g.shape[0]
    cg = d // g
    tc = min(RG_TC, s)
    assert ((tc // SUBLANES + RG_SEG_PAD) // SUBLANES) % 2 == 1 and cg % LANES == 0
    row = lambda v: v.reshape(1, d).astype(F32)
    vec_spec = pl.BlockSpec((1, cg), lambda bi, gi, c: (0, gi))
    return pl.pallas_call(
        _rglru_kernel,
        grid=(b, g, s // tc),
        in_specs=[pl.BlockSpec((1, tc, cg), lambda bi, gi, c: (bi, c, gi)),
                  pl.BlockSpec((1, tc, cg), lambda bi, gi, c: (bi, c, gi)),
                  pl.BlockSpec((CONV_W, cg), lambda bi, gi, c: (0, gi)),
                  vec_spec,
                  pl.BlockSpec((1, cg, cg), lambda bi, gi, c: (gi, 0, 0)),
                  vec_spec,
                  pl.BlockSpec((1, cg, cg), lambda bi, gi, c: (gi, 0, 0)),
                  vec_spec,
                  vec_spec],
        out_specs=pl.BlockSpec((1, tc, cg), lambda bi, gi, c: (bi, c, gi)),
        out_shape=jax.ShapeDtypeStruct((b, s, d), BF16),
        scratch_shapes=[pltpu.VMEM((tc + SUBLANES, cg), F32),
                        pltpu.VMEM((cg // LANES, tc + SUBLANES * RG_SEG_PAD, LANES), F32),
                        pltpu.VMEM((cg // LANES, tc + SUBLANES * RG_SEG_PAD, LANES), F32),
                        pltpu.VMEM((1, cg), F32)],
        compiler_params=_cparams(("parallel", "parallel", "arbitrary")),
        name="rglru",
    )(xr, gz, conv_w.astype(F32), row(conv_b), wa_g * 0.5, row(b_a) * 0.5, wx_g * 0.5, row(b_x) * 0.5,
      row(lam))


def _group_block_diag(w, pair):
    nb, k, _ = w.shape
    out = None
    for p in range(pair):
        lo, hi = p * k, (pair - 1 - p) * k
        blk = jnp.pad(w[p::pair], ((0, 0), (lo, hi), (lo, hi)))
        out = blk if out is None else out + blk
    return out


def _attention_layer(x2d, b, s, norm_g, w_in, kv_g, w_uk, w_uv, idx_g, idx_b, w_out, rel_bias):
    d_model = x2d.shape[1]
    n_heads = w_uk.shape[0]
    wq = n_heads * HEAD_DIM
    assert N_IDX_HEADS * IDX_DIM == wq
    sizes = (wq, KV_LORA, wq, IDX_DIM, N_IDX_HEADS, wq, wq, wq, 2 * wq)
    offs = [int(o) for o in np.concatenate([[0], np.cumsum(sizes)])]
    wt = w_in.T.astype(BF16)
    zrows = lambda n: jnp.zeros((n, d_model), BF16)
    wt_small = jnp.concatenate([wt[offs[1]:offs[2]], wt[offs[3]:offs[4]], zrows(LANES - IDX_DIM),
                                wt[offs[4]:offs[5]], zrows(LANES - N_IDX_HEADS)], axis=0)

    h = _rmsnorm(x2d, norm_g, BF16, "rmsnorm0")
    pqa = _matmul_nt(h, wt, offs[0], wq, BF16, "proj_attn_qa").reshape(b, s, wq)
    pqi = _matmul_nt(h, wt, offs[2], wq, BF16, "proj_attn_qi").reshape(b, s, wq)
    ptail = _matmul_nt(h, wt, offs[5], 5 * wq, BF16, "proj_attn_tail").reshape(b, s, 5 * wq)
    psmall = _matmul_nt(h, wt_small, 0, wt_small.shape[0], F32, "proj_attn_small").reshape(b, s, -1)

    ckv, kbd, widx = _dsa_prep(psmall, kv_g, idx_g, idx_b)
    mask = _dsa_index(pqi, 0, kbd, widx)
    w_kv = jnp.concatenate([jnp.transpose(w_uk, (1, 0, 2)).reshape(KV_LORA, wq),
                            jnp.transpose(w_uv, (1, 0, 2)).reshape(KV_LORA, wq)], axis=1).astype(BF16)
    kv = _matmul([ckv.reshape(b * s, KV_LORA)], [w_kv], BF16, "dsa_kv_up", bm=2 * MM_BM).reshape(b, s, 2 * wq)

    dtab_a = _bias_blocks(rel_bias[:, :n_heads])
    dtab_b = _bias_blocks(rel_bias[:, n_heads:])
    gpw = n_heads // ATT_HB
    ya = _flash_attention((pqa, 0), (kv, 0), (kv, gpw), (ptail, 3 * gpw), dtab_a, n_heads,
                          "dsa_attention", mask=mask)
    yb = _flash_attention((ptail, 0), (ptail, gpw), (ptail, 2 * gpw), (ptail, 4 * gpw), dtab_b,
                          n_heads, "moba_attention")

    w_out_b = w_out.astype(BF16)
    return _matmul([ya.reshape(b * s, wq), yb.reshape(b * s, wq)], [(w_out_b, 0, 0), (w_out_b, 1, 0)],
                   F32, "proj_attn_out", res=x2d)


def _recurrent_layer(x2d, b, s, norm_g, w_in, conv_w, conv_b, w_a, b_a, w_x, b_x, lam, w_out):
    d_rnn = conv_b.shape[0]
    h = _rmsnorm(x2d, norm_g, BF16, "rmsnorm1")
    w_in_b = w_in.astype(BF16)
    assert d_rnn % MM_BN == 0
    xr = _matmul([h], [(w_in_b, 0, 0)], F32, "proj_rec_x", n=d_rnn)
    gz = _matmul([h], [(w_in_b, 0, d_rnn // MM_BN)], BF16, "proj_rec_z", n=d_rnn, epilogue=_silu)
    y = _rglru(xr.reshape(b, s, d_rnn), gz.reshape(b, s, d_rnn), conv_w, conv_b,
               _group_block_diag(w_a.astype(BF16), RG_PAIR), b_a,
               _group_block_diag(w_x.astype(BF16), RG_PAIR), b_x, lam)
    return _matmul([y.reshape(b * s, d_rnn)], [w_out.astype(BF16)], F32, "proj_rec_out",
                   res=x2d, bn=MM_BN // 2)


def kernel(x, norm_g, final_g, rel_bias, attn_w_in, attn_kv_g, attn_w_uk, attn_w_uv, idx_k_g, idx_k_b,
           attn_w_out, rec_w_in, rec_conv_w, rec_conv_b, rec_w_a, rec_b_a, rec_w_x, rec_b_x,
           rec_lambda, rec_w_out):
    b, s, d = x.shape
    depth = norm_g.shape[0]
    x2d = x.reshape(b * s, d)
    for layer in range(depth):
        li = layer // 2
        if layer % 2 == 0:
            x2d = _attention_layer(x2d, b, s, norm_g[layer], attn_w_in[li], attn_kv_g[li],
                                   attn_w_uk[li], attn_w_uv[li], idx_k_g[li], idx_k_b[li],
                                   attn_w_out[li], rel_bias)
        else:
            x2d = _recurrent_layer(x2d, b, s, norm_g[layer], rec_w_in[li], rec_conv_w[li],
                                   rec_conv_b[li], rec_w_a[li], rec_b_a[li], rec_w_x[li],
                                   rec_b_x[li], rec_lambda[li], rec_w_out[li])
    return _rmsnorm(x2d, final_g, x.dtype, "rmsnorm_final").reshape(b, s, d)
```
